```python
import math
import jax, jax.numpy as jnp
from jax import lax
import numpy as np

D_MODEL = 2048
BATCH = 4
SEQ = 2048
DEPTH = 1
DEC_BATCH = 128
DEC_SEQ = 4
PAST_LEN = 2048
PAGE_SIZE = 128

MIX = D_MODEL
GLA_HEADS = 4
GLA_WIDTH = MIX // 2
GLA_DV = GLA_WIDTH // GLA_HEADS
GLA_DK = GLA_DV // 2
GLA_RANK = 16
GLA_TAU = 16.0
GLA_CHUNK = 16
NSA_HEADS = 8
NSA_KV = 2
NSA_REP = NSA_HEADS // NSA_KV
NSA_HD = (MIX - GLA_WIDTH) // NSA_HEADS
CMP_LEN = 32
CMP_STRIDE = 16
SLC_LEN = 64
N_SELECT = 16
WINDOW = 512
WIN_BLOCK = 128
SLC_QBLOCK_TOKENS = 128
D_FF = ((8 * D_MODEL // 3 + 127) // 128) * 128
CONV_W = 3
PLE_DIM = 256
EPS = 1e-6
NEG = -1e30
FORCED = 1e6
IN_SIZES = (GLA_HEADS * GLA_DK, GLA_HEADS * GLA_DK, GLA_WIDTH, GLA_WIDTH, GLA_RANK,
            NSA_HEADS * NSA_HD,
            NSA_KV * NSA_HD, NSA_KV * NSA_HD, NSA_KV * NSA_HD,
            NSA_KV * NSA_HD, NSA_KV * NSA_HD, NSA_KV * NSA_HD,
            3 * NSA_HEADS)
N_IN = sum(IN_SIZES)

kernel_name = "hymba_gla_nsa_convffn_ple_step"


def rmsnorm(x, g):
    xf = x.astype(jnp.float32)
    y = xf * lax.rsqrt(jnp.mean(xf * xf, axis=-1, keepdims=True) + EPS)
    return (y * g.astype(jnp.float32)).astype(x.dtype)


def masked_softmax(s, mask):
    s = jnp.where(mask, s.astype(jnp.float32), NEG)
    m = jnp.max(s, axis=-1, keepdims=True)
    e = jnp.where(mask, jnp.exp(s - m), 0.0)
    return e / jnp.maximum(jnp.sum(e, axis=-1, keepdims=True), 1e-30)


def gla_scan(q, k, v, log_a, s0):
    B, L, H, DK = q.shape
    DV = v.shape[-1]
    C = math.gcd(L, GLA_CHUNK)
    n = L // C

    def chunks(t):
        return t.astype(jnp.float32).reshape(B, n, C, H, t.shape[-1]).transpose(1, 0, 3, 2, 4)

    tril = jnp.tril(jnp.ones((C, C), dtype=bool))

    def step(S, inp):
        qc, kc, vc, gc = inp
        b = jnp.cumsum(gc, axis=2)
        qe = qc * jnp.exp(b)
        a = jnp.einsum('bhid,bhjd->bhij', qe, kc * jnp.exp(-b))
        a = jnp.where(tril, a, 0.0)
        o = jnp.einsum('bhij,bhjv->bhiv', a, vc) + jnp.einsum('bhid,bhdv->bhiv', qe, S)
        b_last = b[:, :, -1:, :]
        S = jnp.exp(b_last[:, :, 0, :])[..., None] * S + jnp.einsum(
            'bhjd,bhjv->bhdv', kc * jnp.exp(b_last - b), vc)
        return S, o

    S, o = lax.scan(step, s0.astype(jnp.float32),
                    (chunks(q) * (DK ** -0.5), chunks(k), chunks(v), chunks(log_a)))
    o = o.transpose(1, 0, 3, 2, 4).reshape(B, L, H, DV)
    return o, S


def compress(rows, w1, w2):
    B, T = rows.shape[:2]
    nseg = T // CMP_STRIDE
    seg = rows[:, :nseg * CMP_STRIDE].reshape(B, nseg, CMP_STRIDE, NSA_KV, NSA_HD).astype(jnp.float32)
    first = jnp.einsum('bnjgd,jde->bnge', seg, w1[:CMP_STRIDE])
    second = jnp.einsum('bnjgd,jde->bnge', seg, w1[CMP_STRIDE:])
    h = jax.nn.gelu(first[:, :-1] + second[:, 1:])
    return jnp.einsum('bnge,ef->bngf', h, w2)


def nsa_mixer(q, kc, vc, ks, vs, kw, vw, gates, past_ck, past_cv, past_sk, past_sv,
              win_k, win_v, start, keep, wk1, wk2, wv1, wv2):
    B, L = q.shape[:2]
    scale = NSA_HD ** -0.5
    qg = q.reshape(B, L, NSA_KV, NSA_REP, NSA_HD)
    kv = lambda t: t.reshape(B, L, NSA_KV, NSA_HD)
    kc, vc, ks, vs, kw, vw = kv(kc), kv(vc), kv(ks), kv(vs), kv(kw), kv(vw)
    qpos = start + jnp.arange(L)

    kc_all = jnp.concatenate([past_ck, kc], axis=1)
    vc_all = jnp.concatenate([past_cv, vc], axis=1)
    T = kc_all.shape[1]
    ck = compress(kc_all, wk1, wk2)
    cv = compress(vc_all, wv1, wv2)
    NB = ck.shape[1]
    cend = jnp.arange(NB) * CMP_STRIDE + CMP_LEN - 1
    s_c = jnp.einsum('blgrd,bngd->bgrln', qg, ck) * scale
    p_c = masked_softmax(s_c, cend[None, :] <= qpos[:, None])
    o_c = jnp.einsum('bgrln,bngd->blgrd', p_c, cv)

    ratio = SLC_LEN // CMP_STRIDE
    nslc = -(-T // SLC_LEN)
    pg = p_c.sum(axis=2)
    pad = jnp.pad(pg, ((0, 0), (0, 0), (0, 0), (1, ratio * nslc - NB)))
    pair = pad[..., :-1] + pad[..., 1:]
    imp = pair.reshape(B, NSA_KV, L, nslc, ratio).sum(-1)
    blk = jnp.arange(nslc)[None, :]
    cur = (qpos // SLC_LEN)[:, None]
    valid = blk <= cur
    forced = (blk == 0) | (blk == cur) | (blk == cur - 1)
    score = jnp.where(valid & forced, FORCED, jnp.where(valid, imp, -1.0))
    nsel = min(N_SELECT, nslc)
    _, idx = lax.top_k(score, nsel)

    ks_all = jnp.concatenate([past_sk, ks], axis=1)
    vs_all = jnp.concatenate([past_sv, vs], axis=1)
    padT = nslc * SLC_LEN - T
    to_blocks = lambda t: jnp.pad(t, ((0, 0), (0, padT), (0, 0), (0, 0))).reshape(
        B, nslc, SLC_LEN, NSA_KV, NSA_HD).transpose(0, 3, 1, 2, 4)
    kb, vb = to_blocks(ks_all), to_blocks(vs_all)
    QB = math.gcd(L, max(1, SLC_QBLOCK_TOKENS // B))
    nqb = L // QB
    q_blocks = qg.reshape(B, nqb, QB, NSA_KV, NSA_REP, NSA_HD).transpose(1, 0, 2, 3, 4, 5)
    idx_blocks = idx.reshape(B, NSA_KV, nqb, QB, nsel).transpose(2, 0, 1, 3, 4)
    pos_blocks = qpos.reshape(nqb, QB)
    bi = jnp.arange(B)[:, None, None, None]
    gi = jnp.arange(NSA_KV)[None, :, None, None]

    def sel_block(args):
        qb, ib, pb = args
        kg = kb[bi, gi, ib]
        vg = vb[bi, gi, ib]
        kpos = ib[..., None] * SLC_LEN + jnp.arange(SLC_LEN)
        mask = (kpos <= pb[None, None, :, None, None]).reshape(B, NSA_KV, 1, QB, nsel * SLC_LEN)
        s = jnp.einsum('bqgrd,bgqksd->bgrqks', qb, kg) * scale
        p = masked_softmax(s.reshape(B, NSA_KV, NSA_REP, QB, nsel * SLC_LEN), mask)
        p = p.reshape(B, NSA_KV, NSA_REP, QB, nsel, SLC_LEN)
        return jnp.einsum('bgrqks,bgqksd->bqgrd', p, vg)

    o_s = lax.map(sel_block, (q_blocks, idx_blocks, pos_blocks))
    o_s = o_s.transpose(1, 0, 2, 3, 4, 5).reshape(B, L, NSA_KV, NSA_REP, NSA_HD)

    WB = win_k.shape[1]
    kw_all = jnp.concatenate([win_k, kw], axis=1)
    vw_all = jnp.concatenate([win_v, vw], axis=1)
    kpos_all = start - WB + jnp.arange(WB + L)
    BQ = math.gcd(L, WIN_BLOCK)
    nwb = L // BQ
    key_idx = jnp.arange(nwb)[:, None] * BQ + jnp.arange(WB + BQ)[None, :]
    kwg, vwg = kw_all[:, key_idx], vw_all[:, key_idx]
    kp = kpos_all[key_idx][:, None, :]
    qp = qpos.reshape(nwb, BQ)[:, :, None]
    mask_w = (kp <= qp) & (kp > qp - WINDOW) & (kp >= 0)
    qw = qg.reshape(B, nwb, BQ, NSA_KV, NSA_REP, NSA_HD)
    s_w = jnp.einsum('bnqgrd,bnkgd->bgrnqk', qw, kwg) * scale
    p_w = masked_softmax(s_w, mask_w[None, None, None])
    o_w = jnp.einsum('bgrnqk,bnkgd->bnqgrd', p_w, vwg).reshape(B, L, NSA_KV, NSA_REP, NSA_HD)

    g = jax.nn.sigmoid(gates.astype(jnp.float32)).reshape(B, L, 3, NSA_KV, NSA_REP)[..., None]
    o = g[:, :, 0] * o_c + g[:, :, 1] * o_s + g[:, :, 2] * o_w
    o = o.reshape(B, L, NSA_HEADS * NSA_HD).astype(q.dtype)
    return o, (kc, vc, ks, vs, kw_all[:, -keep:], vw_all[:, -keep:])


def conv_ffn(xn, prefix, w_up, conv_w, conv_b, w_down):
    L = xn.shape[1]
    a, u = jnp.split(xn @ w_up, 2, axis=-1)
    a_all = jnp.concatenate([prefix.astype(a.dtype), a], axis=1)
    c = a_all[:, 0:L] * conv_w[0] + a_all[:, 1:L + 1] * conv_w[1] + a_all[:, 2:L + 2] * conv_w[2] + conv_b
    h = jax.nn.gelu(c) * u
    return h @ w_down, a_all[:, -(CONV_W - 1):]


def run_group(x, p, past_ck, past_cv, past_sk, past_sv, win_k, win_v, gla_s, conv_s, start, keep, params):
    (attn_norm_g, w_in, gla_w_a2, gla_b_a, gla_norm_g, cmp_wk1, cmp_wk2, cmp_wv1, cmp_wv2, w_out,
     ffn_norm_g, ffn_w_up, ffn_conv_w, ffn_conv_b, ffn_w_down, ple_w_proj, ple_w_gate, ple_b_gate,
     final_norm_g) = params
    B, L, _ = x.shape
    offsets = np.cumsum(IN_SIZES)[:-1].tolist()
    outs = [[] for _ in range(8)]
    h = x
    for i in range(DEPTH):
        xn = rmsnorm(h, attn_norm_g[i])
        proj = xn @ w_in[i]
        gq, gk, gv, gr, ga, nq, kc, vc, ks, vs, kw, vw, ng = jnp.split(proj, offsets, axis=-1)
        log_a = jax.nn.log_sigmoid((ga @ gla_w_a2[i] + gla_b_a[i]).astype(jnp.float32)) / GLA_TAU
        hd = lambda t, d: t.reshape(B, L, GLA_HEADS, d)
        o_gla, S = gla_scan(hd(gq, GLA_DK), hd(gk, GLA_DK), hd(gv, GLA_DV),
                            hd(log_a, GLA_DK), gla_s[i])
        o_gla = rmsnorm(o_gla, gla_norm_g[i]).reshape(B, L, GLA_WIDTH).astype(x.dtype)
        o_gla = o_gla * jax.nn.silu(gr)
        o_nsa, nsa_new = nsa_mixer(nq, kc, vc, ks, vs, kw, vw, ng, past_ck[i], past_cv[i],
                                   past_sk[i], past_sv[i], win_k[i], win_v[i], start, keep,
                                   cmp_wk1[i], cmp_wk2[i], cmp_wv1[i], cmp_wv2[i])
        h = h + jnp.concatenate([o_gla, o_nsa], axis=-1) @ w_out[i]
        f, conv_new = conv_ffn(rmsnorm(h, ffn_norm_g[i]), conv_s[i], ffn_w_up[i], ffn_conv_w[i],
                               ffn_conv_b[i], ffn_w_down[i])
        h = h + f
        h = h + jax.nn.sigmoid(h @ ple_w_gate[i] + ple_b_gate[i]) * (p[i] @ ple_w_proj[i])
        for j, t in enumerate(nsa_new + (S.astype(x.dtype), conv_new)):
            outs[j].append(t)
    y = rmsnorm(h, final_norm_g)
    return y, tuple(jnp.stack(o, axis=0) for o in outs)


def gather_pages(cache, page_table):
    g = cache[:, page_table]
    return g.reshape(g.shape[0], g.shape[1], g.shape[2] * g.shape[3], g.shape[4], g.shape[5])


def setup_inputs(seed: int = 0) -> dict:
    key = jax.random.key(seed)
    ks = iter(jax.random.split(key, 48))
    nrm = lambda shape, s: jax.random.normal(next(ks), shape, jnp.float32) * s
    gain = lambda shape: 1.0 + nrm(shape, 0.01)
    n_pages = PAST_LEN // PAGE_SIZE
    n_pool = (DEC_BATCH * n_pages * 5) // 4
    win_len = min(WINDOW, PAST_LEN)
    page_shape = (DEPTH, n_pool, PAGE_SIZE, NSA_KV, NSA_HD)
    d = {}
    d["x_prompt"] = nrm((BATCH, SEQ, D_MODEL), 1.0)
    d["x_sample"] = nrm((DEC_BATCH, DEC_SEQ, D_MODEL), 1.0)
    d["p_prompt"] = nrm((DEPTH, BATCH, SEQ, PLE_DIM), 1.0)
    d["p_sample"] = nrm((DEPTH, DEC_BATCH, DEC_SEQ, PLE_DIM), 1.0)
    d["cache_cmp_k"] = nrm(page_shape, 1.0)
    d["cache_cmp_v"] = nrm(page_shape, 1.0)
    d["cache_slc_k"] = nrm(page_shape, 1.0)
    d["cache_slc_v"] = nrm(page_shape, 1.0)
    perm = jax.random.permutation(next(ks), n_pool)[:DEC_BATCH * n_pages]
    d["page_table"] = perm.reshape(DEC_BATCH, n_pages).astype(jnp.int32)
    d["state_win_k"] = nrm((DEPTH, DEC_BATCH, win_len, NSA_KV, NSA_HD), 1.0)
    d["state_win_v"] = nrm((DEPTH, DEC_BATCH, win_len, NSA_KV, NSA_HD), 1.0)
    d["state_gla"] = nrm((DEPTH, DEC_BATCH, GLA_HEADS, GLA_DK, GLA_DV), 1.0)
    d["state_ffn_conv"] = nrm((DEPTH, DEC_BATCH, CONV_W - 1, D_FF), 1.0)
    d["attn_norm_g"] = gain((DEPTH, D_MODEL))
    d["w_in"] = nrm((DEPTH, D_MODEL, N_IN), D_MODEL ** -0.5)
    d["gla_w_a2"] = nrm((DEPTH, GLA_RANK, GLA_HEADS * GLA_DK), GLA_RANK ** -0.5)
    d["gla_b_a"] = nrm((DEPTH, GLA_HEADS * GLA_DK), 0.1)
    d["gla_norm_g"] = gain((DEPTH, GLA_DV))
    d["cmp_wk1"] = nrm((DEPTH, CMP_LEN, NSA_HD, NSA_HD), (CMP_LEN * NSA_HD) ** -0.5)
    d["cmp_wk2"] = nrm((DEPTH, NSA_HD, NSA_HD), NSA_HD ** -0.5)
    d["cmp_wv1"] = nrm((DEPTH, CMP_LEN, NSA_HD, NSA_HD), (CMP_LEN * NSA_HD) ** -0.5)
    d["cmp_wv2"] = nrm((DEPTH, NSA_HD, NSA_HD), NSA_HD ** -0.5)
    d["w_out"] = nrm((DEPTH, MIX, D_MODEL), MIX ** -0.5)
    d["ffn_norm_g"] = gain((DEPTH, D_MODEL))
    d["ffn_w_up"] = nrm((DEPTH, D_MODEL, 2 * D_FF), D_MODEL ** -0.5)
    d["ffn_conv_w"] = nrm((DEPTH, CONV_W, D_FF), CONV_W ** -0.5)
    d["ffn_conv_b"] = nrm((DEPTH, D_FF), 0.01)
    d["ffn_w_down"] = nrm((DEPTH, D_FF, D_MODEL), D_FF ** -0.5)
    d["ple_w_proj"] = nrm((DEPTH, PLE_DIM, D_MODEL), PLE_DIM ** -0.5)
    d["ple_w_gate"] = nrm((DEPTH, D_MODEL, D_MODEL), D_MODEL ** -0.5)
    d["ple_b_gate"] = nrm((DEPTH, D_MODEL), 0.01)
    d["final_norm_g"] = gain((D_MODEL,))
    return d


def reference(x_prompt, x_sample, p_prompt, p_sample, cache_cmp_k, cache_cmp_v, cache_slc_k,
              cache_slc_v, page_table, state_win_k, state_win_v, state_gla, state_ffn_conv,
              attn_norm_g, w_in, gla_w_a2, gla_b_a, gla_norm_g, cmp_wk1, cmp_wk2, cmp_wv1, cmp_wv2,
              w_out, ffn_norm_g, ffn_w_up, ffn_conv_w, ffn_conv_b, ffn_w_down, ple_w_proj,
              ple_w_gate, ple_b_gate, final_norm_g):
    params = (attn_norm_g, w_in, gla_w_a2, gla_b_a, gla_norm_g, cmp_wk1, cmp_wk2, cmp_wv1, cmp_wv2,
              w_out, ffn_norm_g, ffn_w_up, ffn_conv_w, ffn_conv_b, ffn_w_down, ple_w_proj,
              ple_w_gate, ple_b_gate, final_norm_g)
    Bp, Lp = x_prompt.shape[:2]
    dt = x_prompt.dtype
    empty = jnp.zeros((DEPTH, Bp, 0, NSA_KV, NSA_HD), dt)
    win0 = jnp.zeros((DEPTH, Bp, WINDOW, NSA_KV, NSA_HD), dt)
    gla0 = jnp.zeros((DEPTH, Bp, GLA_HEADS, GLA_DK, GLA_DV), jnp.float32)
    conv0 = jnp.zeros((DEPTH, Bp, CONV_W - 1, D_FF), dt)
    y_prompt, (ck_p, cv_p, sk_p, sv_p, wk_p, wv_p, gla_p, conv_p) = run_group(
        x_prompt, p_prompt, empty, empty, empty, empty, win0, win0, gla0, conv0,
        0, min(WINDOW, Lp), params)

    past_ck = gather_pages(cache_cmp_k, page_table)
    past_cv = gather_pages(cache_cmp_v, page_table)
    past_sk = gather_pages(cache_slc_k, page_table)
    past_sv = gather_pages(cache_slc_v, page_table)
    y_sample, (ck_s, cv_s, sk_s, sv_s, wk_s, wv_s, gla_s, conv_s) = run_group(
        x_sample, p_sample, past_ck, past_cv, past_sk, past_sv, state_win_k, state_win_v,
        state_gla, state_ffn_conv, past_ck.shape[2], state_win_k.shape[2], params)
    return (y_prompt, y_sample, ck_p, cv_p, sk_p, sv_p, wk_p, wv_p, gla_p, conv_p,
            ck_s, cv_s, sk_s, sv_s, wk_s, wv_s, gla_s, conv_s)
```

```python
import functools
import math

import numpy as np
import jax
import jax.numpy as jnp
from jax import lax
from jax.experimental import pallas as pl
from jax.experimental.pallas import tpu as pltpu

F32 = jnp.float32
BF16 = jnp.bfloat16

GLA_HEADS = 4
GLA_DK = 128
GLA_DV = 256
GLA_RANK = 16
GLA_TAU = 16.0
GLA_SUB = 16
NSA_HEADS = 8
NSA_KV = 2
NSA_REP = NSA_HEADS // NSA_KV
NSA_HD = 128
CMP_LEN = 32
CMP_STRIDE = 16
SLC_LEN = 64
N_SELECT = 16
WINDOW = 512
CONV_W = 3
EPS = 1e-6
NEG = -1e30
FORCED = 1e6

LANE = 128
SUBLANE = 8
KCHUNK = 128
VMEM_LIMIT = 56 * 1024 * 1024

GQ0 = 0
GK0 = GQ0 + GLA_HEADS * GLA_DK
GV0 = GK0 + GLA_HEADS * GLA_DK
GR0 = GV0 + GLA_HEADS * GLA_DV
NQ0 = GR0 + GLA_HEADS * GLA_DV
KVW = NSA_KV * NSA_HD
KC0 = NQ0 + NSA_HEADS * NSA_HD
VC0 = KC0 + KVW
KS0 = VC0 + KVW
VS0 = KS0 + KVW
KW0 = VS0 + KVW
VW0 = KW0 + KVW
NMAIN = VW0 + KVW
GATE0 = GLA_RANK


def _cp(sem):
    return pltpu.CompilerParams(dimension_semantics=sem, vmem_limit_bytes=VMEM_LIMIT)


def _dot(a, b):
    return jnp.dot(a, b, preferred_element_type=F32)


def _dot_nt(a, b):
    return lax.dot_general(a, b, (((1,), (1,)), ((), ())), preferred_element_type=F32)


def _rms(x, g):
    return x * lax.rsqrt(jnp.mean(x * x, axis=-1, keepdims=True) + EPS) * g


def _norm_proj_kernel(x_ref, g_ref, wm_ref, ws_ref, om_ref, os_ref, xn_ref):
    @pl.when(pl.program_id(1) == 0)
    def _():
        xn = _rms(x_ref[...], g_ref[...]).astype(BF16)
        xn_ref[...] = xn
        os_ref[...] = _dot(xn, ws_ref[...])

    om_ref[...] = _dot(xn_ref[...], wm_ref[...])


def norm_proj(x2, g, w_main, w_small, tm, tn):
    M, Dm = x2.shape
    N = w_main.shape[1]
    return pl.pallas_call(
        _norm_proj_kernel,
        grid=(M // tm, N // tn),
        in_specs=[pl.BlockSpec((tm, Dm), lambda i, j: (i, 0)),
                  pl.BlockSpec((1, Dm), lambda i, j: (0, 0)),
                  pl.BlockSpec((Dm, tn), lambda i, j: (0, j)),
                  pl.BlockSpec((Dm, LANE), lambda i, j: (0, 0))],
        out_specs=[pl.BlockSpec((tm, tn), lambda i, j: (i, j)),
                   pl.BlockSpec((tm, LANE), lambda i, j: (i, 0))],
        out_shape=[jax.ShapeDtypeStruct((M, N), F32), jax.ShapeDtypeStruct((M, LANE), F32)],
        scratch_shapes=[pltpu.VMEM((tm, Dm), BF16)],
        compiler_params=_cp(("parallel", "arbitrary")),
        name="norm_proj")(x2, g, w_main, w_small)


def _gla_kernel(q_ref, k_ref, v_ref, r_ref, sm_ref, wa_ref, ba_ref, gn_ref, s0_ref,
                o_ref, s_ref, st_ref, *, cb, sub, valid):
    c = pl.program_id(2)

    @pl.when(c == 0)
    def _():
        st_ref[...] = s0_ref[0, 0]

    q = q_ref[0] * (GLA_DK ** -0.5)
    k = k_ref[0]
    v = v_ref[0]
    pre = _dot(sm_ref[0].astype(BF16), wa_ref[...]) + ba_ref[...]
    log_a = jax.nn.log_sigmoid(pre) / GLA_TAU
    if valid < cb:
        rowv = lax.broadcasted_iota(jnp.int32, (cb, 1), 0)
        log_a = jnp.where(rowv < valid, log_a, 0.0)
        k = jnp.where(rowv < valid, k, 0.0)
        v = jnp.where(rowv < valid, v, 0.0)
    ri = lax.broadcasted_iota(jnp.int32, (cb, cb), 0)
    ci = lax.broadcasted_iota(jnp.int32, (cb, cb), 1)
    tri = jnp.where(ci <= ri, 1.0, 0.0)
    hi = log_a.astype(BF16).astype(F32)
    lo = (log_a - hi).astype(BF16).astype(F32)
    b = _dot(tri, hi) + _dot(tri, lo)
    b_last = b[cb - 1:cb, :]
    S = st_ref[...]
    vb = v.astype(BF16)

    o_inter = _dot((q * jnp.exp(b)).astype(BF16), S.astype(BF16))
    rows = []
    for blk in range(cb // sub):
        lo_r, hi_r = blk * sub, (blk + 1) * sub
        r_ref_dec = jnp.zeros((1, GLA_DK), F32) if blk == 0 else b[lo_r - 1:lo_r, :]
        q_i = q[lo_r:hi_r] * jnp.exp(b[lo_r:hi_r] - r_ref_dec)
        k_i = k[:hi_r] * jnp.exp(r_ref_dec - b[:hi_r])
        a = _dot_nt(q_i, k_i)
        rr = lax.broadcasted_iota(jnp.int32, (sub, hi_r), 0) + lo_r
        cc = lax.broadcasted_iota(jnp.int32, (sub, hi_r), 1)
        a = jnp.where(cc <= rr, a, 0.0)
        rows.append(_dot(a, v[:hi_r]))
    o = o_inter + (rows[0] if len(rows) == 1 else jnp.concatenate(rows, axis=0))

    kh = k * jnp.exp(b_last - b)
    if cb < LANE:
        kh = jnp.concatenate([kh, jnp.zeros((LANE - cb, GLA_DK), F32)], axis=0)
        vpad = jnp.concatenate([vb, jnp.zeros((LANE - cb, GLA_DV), BF16)], axis=0)
    else:
        vpad = vb
    dec_col = jnp.transpose(jnp.broadcast_to(jnp.exp(b_last), (GLA_DK, GLA_DK)))[:, 0:1]
    s_new = dec_col * S + _dot(jnp.transpose(kh).astype(BF16), vpad)
    st_ref[...] = s_new

    @pl.when(c == pl.num_programs(2) - 1)
    def _():
        s_ref[0, 0] = s_new

    r = r_ref[0]
    o_ref[0] = (_rms(o, gn_ref[...]) * (r * jax.nn.sigmoid(r))).astype(o_ref.dtype)


def gla(proj3, small3, wa_pad, b_a, gnorm, s0, cb, sub, valid):
    B, L, _ = proj3.shape
    H = GLA_HEADS
    kern = functools.partial(_gla_kernel, cb=cb, sub=sub, valid=valid)
    return pl.pallas_call(
        kern,
        grid=(B, H, L // cb),
        in_specs=[pl.BlockSpec((1, cb, GLA_DK), lambda b, h, c: (b, c, GQ0 // GLA_DK + h)),
                  pl.BlockSpec((1, cb, GLA_DK), lambda b, h, c: (b, c, GK0 // GLA_DK + h)),
                  pl.BlockSpec((1, cb, GLA_DV), lambda b, h, c: (b, c, GV0 // GLA_DV + h)),
                  pl.BlockSpec((1, cb, GLA_DV), lambda b, h, c: (b, c, GR0 // GLA_DV + h)),
                  pl.BlockSpec((1, cb, LANE), lambda b, h, c: (b, c, 0)),
                  pl.BlockSpec((LANE, GLA_DK), lambda b, h, c: (0, h)),
                  pl.BlockSpec((1, GLA_DK), lambda b, h, c: (0, h)),
                  pl.BlockSpec((1, GLA_DV), lambda b, h, c: (0, 0)),
                  pl.BlockSpec((1, 1, GLA_DK, GLA_DV), lambda b, h, c: (b, h, 0, 0))],
        out_specs=[pl.BlockSpec((1, cb, GLA_DV), lambda b, h, c: (b, c, h)),
                   pl.BlockSpec((1, 1, GLA_DK, GLA_DV), lambda b, h, c: (b, h, 0, 0))],
        out_shape=[jax.ShapeDtypeStruct((B, L, H * GLA_DV), BF16),
                   jax.ShapeDtypeStruct((B, H, GLA_DK, GLA_DV), F32)],
        scratch_shapes=[pltpu.VMEM((GLA_DK, GLA_DV), F32)],
        compiler_params=_cp(("parallel", "parallel", "arbitrary")),
        name="gla")(proj3, proj3, proj3, proj3, small3, wa_pad, b_a, gnorm, s0)


def _compress_group(load_rows, w1_ref, w2, nseg):
    acc = jnp.zeros((nseg, 2 * NSA_HD), F32)
    for j in range(CMP_STRIDE):
        acc = acc + _dot(load_rows(j).astype(BF16), w1_ref[j])
    first = acc[:, :NSA_HD]
    second = acc[:, NSA_HD:]
    h = jax.nn.gelu(first + pltpu.roll(second, nseg - 1, axis=0))
    return _dot(h.astype(BF16), w2)


def _cmp_branch(q4, ck, cv, qpos4, nb):
    s = _dot_nt(q4, ck.astype(BF16))
    n = lax.broadcasted_iota(jnp.int32, s.shape, 1)
    mask = jnp.where(n * CMP_STRIDE + (CMP_LEN - 1) <= qpos4, n, nb) < nb
    s = jnp.where(mask, s, NEG)
    m = jnp.max(s, axis=-1, keepdims=True)
    e = jnp.where(mask, jnp.exp(s - m), 0.0)
    p = e / jnp.maximum(jnp.sum(e, axis=-1, keepdims=True), 1e-30)
    return p, _dot(p.astype(BF16), cv.astype(BF16))


def _select(pg, mmat, qpos_tok, nslc, nsel):
    hi = pg.astype(BF16)
    lo = (pg - hi.astype(F32)).astype(BF16)
    imp = _dot(hi, mmat) + _dot(lo, mmat)
    blk = lax.broadcasted_iota(jnp.int32, pg.shape, 1)
    cur = qpos_tok // SLC_LEN
    forced = jnp.where(blk == 0, 1, 0) + jnp.where(blk == cur, 1, 0) + jnp.where(blk == cur - 1, 1, 0)
    score = jnp.where(blk <= cur, jnp.where(forced > 0, FORCED, imp), -1.0)
    score = jnp.where(blk < nslc, score, -2.0)
    rank = jnp.zeros(pg.shape, F32)
    for j in range(nslc):
        sj = score[:, j:j + 1]
        tie = jnp.where(blk > j, sj, NEG)
        rank = rank + jnp.where(sj > score, 1.0, 0.0) + jnp.where(tie == score, 1.0, 0.0)
    return jnp.where(rank < nsel, jnp.where(blk < nslc, 1.0, 0.0), 0.0)


def _flash_init(m_ref, l_ref, acc_ref):
    m_ref[...] = jnp.full(m_ref.shape, NEG, F32)
    l_ref[...] = jnp.zeros(l_ref.shape, F32)
    acc_ref[...] = jnp.zeros(acc_ref.shape, F32)


def _flash_step(q4, k, v, mask, m_ref, l_ref, acc_ref):
    s = jnp.where(mask, _dot_nt(q4, k), NEG)
    m_old = m_ref[...]
    m_new = jnp.maximum(m_old, jnp.max(s, axis=-1, keepdims=True))
    alpha = jnp.exp(m_old - m_new)
    p = jnp.where(mask, jnp.exp(s - m_new), 0.0)
    l_ref[...] = alpha * l_ref[...] + jnp.sum(p, axis=-1, keepdims=True)
    acc_ref[...] = alpha * acc_ref[...] + _dot(p.astype(BF16), v)
    m_ref[...] = m_new


def _flash_out(l_ref, acc_ref):
    return acc_ref[...] / jnp.maximum(l_ref[...], 1e-30)


def _tile_rows(x, reps):
    return jnp.concatenate([x] * reps, axis=0)


def _compress_kernel(x_ref, w1_ref, w2_ref, o_ref, *, nseg):
    load = lambda j: x_ref[0, pl.ds(j, nseg, stride=CMP_STRIDE), :]
    o_ref[0, 0, 0] = _compress_group(load, w1_ref.at[0], w2_ref[0].astype(BF16), nseg)


def compress(proj3, w1cat, w2s):
    B, L, _ = proj3.shape
    nseg = L // CMP_STRIDE
    G = NSA_KV
    return pl.pallas_call(
        functools.partial(_compress_kernel, nseg=nseg),
        grid=(B, 2, G),
        in_specs=[pl.BlockSpec((1, L, NSA_HD), lambda b, w, g: (b, 0, KC0 // NSA_HD + w * G + g)),
                  pl.BlockSpec((1, CMP_STRIDE, NSA_HD, 2 * NSA_HD), lambda b, w, g: (w, 0, 0, 0)),
                  pl.BlockSpec((1, NSA_HD, NSA_HD), lambda b, w, g: (w, 0, 0))],
        out_specs=pl.BlockSpec((1, 1, 1, nseg, NSA_HD), lambda b, w, g: (w, b, g, 0, 0)),
        out_shape=jax.ShapeDtypeStruct((2, B, G, nseg, NSA_HD), F32),
        compiler_params=_cp(("parallel", "parallel", "parallel")),
        name="compress")(proj3, w1cat, w2s)


def _nsa_prompt_kernel(q_ref, sm_ref, ck_ref, cv_ref, ks_ref, vs_ref, kw_ref, vw_ref, e_ref, mm_ref,
                       o_ref, selm_ref, m_ref, l_ref, acc_ref, *, tq, nslc, nsel, nb):
    g = pl.program_id(1)
    qb = pl.program_id(2)
    R = NSA_REP
    scale = NSA_HD ** -0.5
    q4 = (jnp.concatenate([q_ref[0, :, r * NSA_HD:(r + 1) * NSA_HD] for r in range(R)], axis=0)
          * scale).astype(BF16)
    qpos_tok = qb * tq + lax.broadcasted_iota(jnp.int32, (tq, 1), 0)
    qpos4 = _tile_rows(qpos_tok, R)

    p, o_c = _cmp_branch(q4, ck_ref[0, 0, 0], cv_ref[0, 0, 0], qpos4, nb)
    pg = p[0:tq]
    for r in range(1, R):
        pg = pg + p[r * tq:(r + 1) * tq]
    sel = _select(pg, mm_ref[...], qpos_tok, nslc, nsel)
    selm_ref[...] = _dot(sel.astype(BF16), e_ref[...])

    lane = lax.broadcasted_iota(jnp.int32, (R * tq, KCHUNK), 1)

    _flash_init(m_ref, l_ref, acc_ref)

    def body_s(c, carry):
        off = pl.multiple_of(c * KCHUNK, KCHUNK)
        k = ks_ref[0, pl.ds(off, KCHUNK), :].astype(BF16)
        v = vs_ref[0, pl.ds(off, KCHUNK), :].astype(BF16)
        selc = _tile_rows(selm_ref[:, pl.ds(off, KCHUNK)], R)
        kpos = jnp.where(selc > 0.5, off + lane, qpos4 + 1)
        _flash_step(q4, k, v, kpos <= qpos4, m_ref, l_ref, acc_ref)
        return carry

    lax.fori_loop(0, qb + 1, body_s, 0)
    o_s = _flash_out(l_ref, acc_ref)

    _flash_init(m_ref, l_ref, acc_ref)

    def body_w(c, carry):
        off = pl.multiple_of(c * KCHUNK, KCHUNK)
        k = kw_ref[0, pl.ds(off, KCHUNK), :].astype(BF16)
        v = vw_ref[0, pl.ds(off, KCHUNK), :].astype(BF16)
        kpos = off + lane
        kpos = jnp.where(kpos > qpos4 - WINDOW, kpos, qpos4 + 1)
        _flash_step(q4, k, v, kpos <= qpos4, m_ref, l_ref, acc_ref)
        return carry

    lax.fori_loop(jnp.maximum(qb - WINDOW // KCHUNK, 0), qb + 1, body_w, 0)
    o_w = _flash_out(l_ref, acc_ref)

    gts = jax.nn.sigmoid(sm_ref[0])
    for r in range(R):
        def gate(br, r=r):
            c0 = GATE0 + br * NSA_HEADS + r
            return jnp.where(g == 0, gts[:, c0:c0 + 1], gts[:, c0 + R:c0 + R + 1])
        rs = slice(r * tq, (r + 1) * tq)
        o = gate(0) * o_c[rs] + gate(1) * o_s[rs] + gate(2) * o_w[rs]
        o_ref[0, :, r * NSA_HD:(r + 1) * NSA_HD] = o.astype(o_ref.dtype)


def nsa_prompt(proj3, small3, cmp, emat, mmat):
    B, L, _ = proj3.shape
    tq = KCHUNK
    nseg = L // CMP_STRIDE
    nb = nseg - 1
    nslc = -(-L // SLC_LEN)
    nsel = min(N_SELECT, nslc)
    G, R = NSA_KV, NSA_REP
    kv_spec = lambda c0: pl.BlockSpec((1, L, NSA_HD), lambda b, g, qb: (b, 0, c0 // NSA_HD + g))
    kern = functools.partial(_nsa_prompt_kernel, tq=tq, nslc=nslc, nsel=nsel, nb=nb)
    return pl.pallas_call(
        kern,
        grid=(B, G, L // tq),
        in_specs=[pl.BlockSpec((1, tq, R * NSA_HD), lambda b, g, qb: (b, qb, NQ0 // (R * NSA_HD) + g)),
                  pl.BlockSpec((1, tq, LANE), lambda b, g, qb: (b, qb, 0)),
                  pl.BlockSpec((1, 1, 1, nseg, NSA_HD), lambda b, g, qb: (0, b, g, 0, 0)),
                  pl.BlockSpec((1, 1, 1, nseg, NSA_HD), lambda b, g, qb: (1, b, g, 0, 0)),
                  kv_spec(KS0), kv_spec(VS0), kv_spec(KW0), kv_spec(VW0),
                  pl.BlockSpec((LANE, L), lambda b, g, qb: (0, 0)),
                  pl.BlockSpec((LANE, LANE), lambda b, g, qb: (0, 0))],
        out_specs=pl.BlockSpec((1, tq, R * NSA_HD), lambda b, g, qb: (b, qb, g)),
        out_shape=jax.ShapeDtypeStruct((B, L, NSA_HEADS * NSA_HD), BF16),
        scratch_shapes=[pltpu.VMEM((tq, L), F32),
                        pltpu.VMEM((R * tq, 1), F32),
                        pltpu.VMEM((R * tq, 1), F32),
                        pltpu.VMEM((R * tq, NSA_HD), F32)],
        compiler_params=_cp(("parallel", "parallel", "arbitrary")),
        name="nsa_prompt")(proj3, small3, cmp, cmp, proj3, proj3, proj3, proj3, emat, mmat)


def _nsa_decode_kernel(pt_ref, q_ref, sm_ref, ksn_ref, vsn_ref, kwn_ref, vwn_ref,
                       pck_ref, pcv_ref, psk_ref, psv_ref, wk_ref, wv_ref,
                       w1_ref, w2_ref, e_ref, mm_ref, o_ref,
                       past_ref, selm_ref, m_ref, l_ref, acc_ref,
                       *, lp, lreal, plen, wlen, nslc, nsel, nb):
    p_id = pl.program_id(1)
    psz = pck_ref.shape[1]
    off_p = pl.multiple_of(p_id * psz, psz)
    G = NSA_KV
    for w, page in enumerate((pck_ref, pcv_ref, psk_ref, psv_ref)):
        for g in range(G):
            past_ref[w * G + g, pl.ds(off_p, psz), :] = page[0, :, g * NSA_HD:(g + 1) * NSA_HD]

    @pl.when(p_id == pl.num_programs(1) - 1)
    def _():
        R = NSA_REP
        T = R * lp
        scale = NSA_HD ** -0.5
        nseg = plen // CMP_STRIDE
        tok = lax.broadcasted_iota(jnp.int32, (lp, 1), 0)
        qpos_tok = plen + tok
        qpos4 = _tile_rows(qpos_tok, R)
        lane = lax.broadcasted_iota(jnp.int32, (T, KCHUNK), 1)
        gts = jax.nn.sigmoid(sm_ref[0])
        zpad_f = jnp.zeros((KCHUNK - lp, NSA_HD), F32)
        for g in range(NSA_KV):
            ls = slice(g * NSA_HD, (g + 1) * NSA_HD)
            q4 = (jnp.concatenate([q_ref[0, :, (g * R + r) * NSA_HD:(g * R + r + 1) * NSA_HD]
                                   for r in range(R)], axis=0) * scale).astype(BF16)

            cmp = []
            for w in range(2):
                load = lambda j, w=w, g=g: past_ref[w * G + g, pl.ds(j, nseg, stride=CMP_STRIDE), :]
                cmp.append(_compress_group(load, w1_ref.at[w], w2_ref[w].astype(BF16), nseg))
            p, o_c = _cmp_branch(q4, cmp[0], cmp[1], qpos4, nb)
            pg = p[0:lp]
            for r in range(1, R):
                pg = pg + p[r * lp:(r + 1) * lp]
            sel = _select(pg, mm_ref[...], qpos_tok, nslc, nsel)
            selm_ref[...] = _dot(sel.astype(BF16), e_ref[...])

            new_pos = jnp.where(lane < lreal, plen + lane, qpos4 + 1)

            _flash_init(m_ref, l_ref, acc_ref)

            def body_s(c, carry):
                off = pl.multiple_of(c * KCHUNK, KCHUNK)
                k = past_ref[2 * G + g, pl.ds(off, KCHUNK), :].astype(BF16)
                v = past_ref[3 * G + g, pl.ds(off, KCHUNK), :].astype(BF16)
                selc = _tile_rows(selm_ref[:, pl.ds(off, KCHUNK)], R)
                kpos = jnp.where(selc > 0.5, off + lane, qpos4 + 1)
                _flash_step(q4, k, v, kpos <= qpos4, m_ref, l_ref, acc_ref)
                return carry

            lax.fori_loop(0, plen // KCHUNK, body_s, 0)
            k = jnp.concatenate([ksn_ref[0, :, ls], zpad_f], axis=0).astype(BF16)
            v = jnp.concatenate([vsn_ref[0, :, ls], zpad_f], axis=0).astype(BF16)
            selc = _tile_rows(selm_ref[:, plen:plen + KCHUNK], R)
            kpos = jnp.where(selc > 0.5, new_pos, qpos4 + 1)
            _flash_step(q4, k, v, kpos <= qpos4, m_ref, l_ref, acc_ref)
            o_s = _flash_out(l_ref, acc_ref)

            _flash_init(m_ref, l_ref, acc_ref)
            for c in range(wlen // KCHUNK):
                k = wk_ref[0, c * KCHUNK:(c + 1) * KCHUNK, ls].astype(BF16)
                v = wv_ref[0, c * KCHUNK:(c + 1) * KCHUNK, ls].astype(BF16)
                kpos = plen - wlen + c * KCHUNK + lane
                kpos = jnp.where(kpos > qpos4 - WINDOW, kpos, qpos4 + 1)
                kpos = jnp.where(kpos >= 0, kpos, qpos4 + 1)
                _flash_step(q4, k, v, kpos <= qpos4, m_ref, l_ref, acc_ref)
            k = jnp.concatenate([kwn_ref[0, :, ls], zpad_f], axis=0).astype(BF16)
            v = jnp.concatenate([vwn_ref[0, :, ls], zpad_f], axis=0).astype(BF16)
            kpos = jnp.where(new_pos > qpos4 - WINDOW, new_pos, qpos4 + 1)
            _flash_step(q4, k, v, kpos <= qpos4, m_ref, l_ref, acc_ref)
            o_w = _flash_out(l_ref, acc_ref)

            for r in range(R):
                c0 = GATE0 + g * R + r
                rs = slice(r * lp, (r + 1) * lp)
                o = (gts[:, c0:c0 + 1] * o_c[rs]
                     + gts[:, c0 + NSA_HEADS:c0 + NSA_HEADS + 1] * o_s[rs]
                     + gts[:, c0 + 2 * NSA_HEADS:c0 + 2 * NSA_HEADS + 1] * o_w[rs])
                hq = (g * R + r) * NSA_HD
                o_ref[0, :, hq:hq + NSA_HD] = o.astype(o_ref.dtype)


def nsa_decode(proj3, small3, caches, page_table, win_k, win_v, w1cat, w2s, emat, mmat, lreal):
    B, lp, _ = proj3.shape
    n_pages = page_table.shape[1]
    psz = caches[0].shape[1]
    plen = n_pages * psz
    wlen = win_k.shape[1]
    nseg = plen // CMP_STRIDE
    nb = (plen + lreal) // CMP_STRIDE - 1
    nslc = -(-(plen + lreal) // SLC_LEN)
    nsel = min(N_SELECT, nslc)
    R = NSA_REP
    T = R * lp
    new_spec = lambda c0: pl.BlockSpec((1, lp, KVW), lambda b, p, pt: (b, 0, c0 // KVW))
    page_spec = pl.BlockSpec((1, psz, KVW), lambda b, p, pt: (pt[b, p], 0, 0))
    const = lambda shape: pl.BlockSpec(shape, lambda b, p, pt: (0,) * len(shape))
    kern = functools.partial(_nsa_decode_kernel, lp=lp, lreal=lreal, plen=plen, wlen=wlen,
                             nslc=nslc, nsel=nsel, nb=nb)
    grid_spec = pltpu.PrefetchScalarGridSpec(
        num_scalar_prefetch=1,
        grid=(B, n_pages),
        in_specs=[pl.BlockSpec((1, lp, NSA_HEADS * NSA_HD), lambda b, p, pt: (b, 0, NQ0 // (NSA_HEADS * NSA_HD))),
                  pl.BlockSpec((1, lp, LANE), lambda b, p, pt: (b, 0, 0)),
                  new_spec(KS0), new_spec(VS0), new_spec(KW0), new_spec(VW0),
                  page_spec, page_spec, page_spec, page_spec,
                  pl.BlockSpec((1, wlen, KVW), lambda b, p, pt: (b, 0, 0)),
                  pl.BlockSpec((1, wlen, KVW), lambda b, p, pt: (b, 0, 0)),
                  const((2, CMP_STRIDE, NSA_HD, 2 * NSA_HD)),
                  const((2, NSA_HD, NSA_HD)),
                  const((LANE, plen + KCHUNK)),
                  const((LANE, LANE))],
        out_specs=pl.BlockSpec((1, lp, NSA_HEADS * NSA_HD), lambda b, p, pt: (b, 0, 0)),
        scratch_shapes=[pltpu.VMEM((4 * NSA_KV, plen, NSA_HD), F32),
                        pltpu.VMEM((lp, plen + KCHUNK), F32),
                        pltpu.VMEM((T, 1), F32),
                        pltpu.VMEM((T, 1), F32),
                        pltpu.VMEM((T, NSA_HD), F32)])
    return pl.pallas_call(
        kern,
        grid_spec=grid_spec,
        out_shape=jax.ShapeDtypeStruct((B, lp, NSA_HEADS * NSA_HD), BF16),
        compiler_params=_cp(("parallel", "arbitrary")),
        name="nsa_decode")(page_table, proj3, small3, proj3, proj3, proj3, proj3,
                           caches[0], caches[1], caches[2], caches[3], win_k, win_v,
                           w1cat, w2s, emat, mmat)


def _out_proj_kernel(og_ref, on_ref, x_ref, w0_ref, w1_ref, g_ref, h_ref, hn_ref):
    h = x_ref[...] + _dot(og_ref[...], w0_ref[...]) + _dot(on_ref[...], w1_ref[...])
    h_ref[...] = h
    hn_ref[...] = _rms(h, g_ref[...]).astype(BF16)


def out_proj(og, on, x2, w_out, g, tm):
    M, Dm = x2.shape
    Kh = og.shape[1]
    return pl.pallas_call(
        _out_proj_kernel,
        grid=(M // tm,),
        in_specs=[pl.BlockSpec((tm, Kh), lambda i: (i, 0)),
                  pl.BlockSpec((tm, Kh), lambda i: (i, 0)),
                  pl.BlockSpec((tm, Dm), lambda i: (i, 0)),
                  pl.BlockSpec((Kh, Dm), lambda i: (0, 0)),
                  pl.BlockSpec((Kh, Dm), lambda i: (1, 0)),
                  pl.BlockSpec((1, Dm), lambda i: (0, 0))],
        out_specs=[pl.BlockSpec((tm, Dm), lambda i: (i, 0)),
                   pl.BlockSpec((tm, Dm), lambda i: (i, 0))],
        out_shape=[jax.ShapeDtypeStruct((M, Dm), F32), jax.ShapeDtypeStruct((M, Dm), BF16)],
        compiler_params=_cp(("parallel",)),
        name="out_proj")(og, on, x2, w_out, w_out, g)


def _ffn_up_seq_kernel(hn_ref, wa_ref, wu_ref, cw_ref, cb_ref, pre_ref, hm_ref, tail_ref, aext_ref,
                       *, tm, tiles_per_seq):
    m = pl.program_id(1)

    @pl.when(m % tiles_per_seq == 0)
    def _():
        aext_ref[0:SUBLANE, :] = pre_ref[0]

    hn = hn_ref[...]
    a = _dot(hn, wa_ref[...])
    u = _dot(hn, wu_ref[...])
    aext_ref[SUBLANE:SUBLANE + tm, :] = a
    p1 = aext_ref[pl.ds(SUBLANE - 1, tm), :]
    p2 = aext_ref[pl.ds(SUBLANE - 2, tm), :]
    c = p2 * cw_ref[0:1, :] + p1 * cw_ref[1:2, :] + a * cw_ref[2:3, :] + cb_ref[...]
    hm_ref[...] = (jax.nn.gelu(c) * u).astype(hm_ref.dtype)
    tail = a[tm - SUBLANE:tm, :]
    tail_ref[0] = tail
    aext_ref[0:SUBLANE, :] = tail


def ffn_up_seq(hn, w_up, conv_w, conv_b, prefix, seq_len, tm, tf):
    M, Dm = hn.shape
    Fp = conv_w.shape[1]
    nf = Fp // tf
    B = M // seq_len
    tps = seq_len // tm
    kern = functools.partial(_ffn_up_seq_kernel, tm=tm, tiles_per_seq=tps)
    return pl.pallas_call(
        kern,
        grid=(nf, M // tm),
        in_specs=[pl.BlockSpec((tm, Dm), lambda f, m: (m, 0)),
                  pl.BlockSpec((Dm, tf), lambda f, m: (0, f)),
                  pl.BlockSpec((Dm, tf), lambda f, m: (0, f + nf)),
                  pl.BlockSpec((SUBLANE, tf), lambda f, m: (0, f)),
                  pl.BlockSpec((1, tf), lambda f, m: (0, f)),
                  pl.BlockSpec((1, SUBLANE, tf), lambda f, m: (m // tps, 0, f))],
        out_specs=[pl.BlockSpec((tm, tf), lambda f, m: (m, f)),
                   pl.BlockSpec((1, SUBLANE, tf), lambda f, m: (m // tps, 0, f))],
        out_shape=[jax.ShapeDtypeStruct((M, Fp), BF16), jax.ShapeDtypeStruct((B, SUBLANE, Fp), F32)],
        scratch_shapes=[pltpu.VMEM((SUBLANE + tm, tf), F32)],
        compiler_params=_cp(("parallel", "arbitrary")),
        name="ffn_up_seq")(hn, w_up, w_up, conv_w, conv_b, prefix)


def _ffn_up_tm_kernel(hn_ref, wa_ref, wu_ref, cw_ref, cb_ref, pre_ref, hm_ref, tail_ref, *, steps, nb):
    hn = hn_ref[...]
    a = _dot(hn, wa_ref[...])
    u = _dot(hn, wu_ref[...])
    slabs = [pre_ref[i] for i in range(CONV_W - 1)] + [a[t * nb:(t + 1) * nb] for t in range(steps)]
    for t in range(steps):
        c = (slabs[t] * cw_ref[0:1, :] + slabs[t + 1] * cw_ref[1:2, :] + slabs[t + 2] * cw_ref[2:3, :]
             + cb_ref[...])
        hm_ref[t * nb:(t + 1) * nb, :] = (jax.nn.gelu(c) * u[t * nb:(t + 1) * nb]).astype(hm_ref.dtype)
    for i in range(CONV_W - 1):
        tail_ref[i] = slabs[steps + i]


def ffn_up_tm(hn, w_up, conv_w, conv_b, prefix, steps, tf):
    M, Dm = hn.shape
    Fp = conv_w.shape[1]
    nf = Fp // tf
    nb = M // steps
    kern = functools.partial(_ffn_up_tm_kernel, steps=steps, nb=nb)
    return pl.pallas_call(
        kern,
        grid=(nf,),
        in_specs=[pl.BlockSpec((M, Dm), lambda f: (0, 0)),
                  pl.BlockSpec((Dm, tf), lambda f: (0, f)),
                  pl.BlockSpec((Dm, tf), lambda f: (0, f + nf)),
                  pl.BlockSpec((SUBLANE, tf), lambda f: (0, f)),
                  pl.BlockSpec((1, tf), lambda f: (0, f)),
                  pl.BlockSpec((CONV_W - 1, nb, tf), lambda f: (0, 0, f))],
        out_specs=[pl.BlockSpec((M, tf), lambda f: (0, f)),
                   pl.BlockSpec((CONV_W - 1, nb, tf), lambda f: (0, 0, f))],
        out_shape=[jax.ShapeDtypeStruct((M, Fp), BF16), jax.ShapeDtypeStruct((CONV_W - 1, nb, Fp), F32)],
        compiler_params=_cp(("parallel",)),
        name="ffn_up_tm")(hn, w_up, w_up, conv_w, conv_b, prefix)


def _ffn_down_kernel(hm_ref, w_ref, h_ref, o_ref):
    o_ref[...] = h_ref[...] + _dot(hm_ref[...], w_ref[...])


def ffn_down(hmid, w_down, h1, tm, tn):
    M, Fp = hmid.shape
    Dm = w_down.shape[1]
    return pl.pallas_call(
        _ffn_down_kernel,
        grid=(M // tm, Dm // tn),
        in_specs=[pl.BlockSpec((tm, Fp), lambda i, j: (i, 0)),
                  pl.BlockSpec((Fp, tn), lambda i, j: (0, j)),
                  pl.BlockSpec((tm, tn), lambda i, j: (i, j))],
        out_specs=pl.BlockSpec((tm, tn), lambda i, j: (i, j)),
        out_shape=jax.ShapeDtypeStruct((M, Dm), F32),
        compiler_params=_cp(("parallel", "arbitrary")),
        name="ffn_down")(hmid, w_down, h1)


def _ple_final_kernel(h_ref, p_ref, wg_ref, bg_ref, wp_ref, g_ref, o_ref):
    h = h_ref[...]
    gate = jax.nn.sigmoid(_dot(h.astype(BF16), wg_ref[...]) + bg_ref[...])
    h = h + gate * _dot(p_ref[...].astype(BF16), wp_ref[...])
    o_ref[...] = _rms(h, g_ref[...])


def ple_final(h2, p2, w_gate, b_gate, w_proj, g, tm):
    M, Dm = h2.shape
    Pd = p2.shape[1]
    return pl.pallas_call(
        _ple_final_kernel,
        grid=(M // tm,),
        in_specs=[pl.BlockSpec((tm, Dm), lambda i: (i, 0)),
                  pl.BlockSpec((tm, Pd), lambda i: (i, 0)),
                  pl.BlockSpec((Dm, Dm), lambda i: (0, 0)),
                  pl.BlockSpec((1, Dm), lambda i: (0, 0)),
                  pl.BlockSpec((Pd, Dm), lambda i: (0, 0)),
                  pl.BlockSpec((1, Dm), lambda i: (0, 0))],
        out_specs=pl.BlockSpec((tm, Dm), lambda i: (i, 0)),
        out_shape=jax.ShapeDtypeStruct((M, Dm), F32),
        compiler_params=_cp(("parallel",)),
        name="ple_final")(h2, p2, w_gate, b_gate, w_proj, g)


def _pick_tile(n, prefs):
    for t in prefs:
        if n % t == 0:
            return t
    raise ValueError(f"no tile for {n}")


def _block_matrices(total_keys, nb):
    e = np.zeros((LANE, total_keys), np.float32)
    t = np.arange(total_keys)
    e[t // SLC_LEN, t] = 1.0
    ratio = SLC_LEN // CMP_STRIDE
    m = np.zeros((LANE, LANE), np.float32)
    for n in range(nb):
        for s in range(LANE):
            m[n, s] = float(ratio * s <= n <= ratio * s + ratio - 1) + float(ratio * s - 1 <= n <= ratio * s + ratio - 2)
    return jnp.asarray(e, BF16), jnp.asarray(m, BF16)


def _prep_weights(i, attn_norm_g, w_in, gla_w_a2, gla_b_a, gla_norm_g, cmp_wk1, cmp_wk2, cmp_wv1, cmp_wv2,
                  w_out, ffn_norm_g, ffn_w_up, ffn_conv_w, ffn_conv_b, ffn_w_down, ple_w_proj, ple_w_gate,
                  ple_b_gate):
    Dm = w_in.shape[1]
    F = ffn_w_down.shape[1]
    Fp = -(-F // 512) * 512
    win = w_in[i]
    ga0 = NQ0
    ng0 = ga0 + GLA_RANK + (NMAIN - NQ0)
    ngw = 3 * NSA_HEADS
    w_main = jnp.concatenate([win[:, :ga0], win[:, ga0 + GLA_RANK:ng0]], axis=1).astype(BF16)
    w_small = jnp.concatenate([win[:, ga0:ga0 + GLA_RANK], win[:, ng0:ng0 + ngw],
                               jnp.zeros((Dm, LANE - GLA_RANK - ngw), F32)], axis=1).astype(BF16)
    wa_pad = jnp.concatenate([gla_w_a2[i], jnp.zeros((LANE - GLA_RANK, GLA_HEADS * GLA_DK), F32)],
                             axis=0).astype(BF16)
    cat = lambda w1: jnp.concatenate([w1[:CMP_STRIDE], w1[CMP_STRIDE:]], axis=-1)
    w1cat = jnp.stack([cat(cmp_wk1[i]), cat(cmp_wv1[i])]).astype(BF16)
    w2s = jnp.stack([cmp_wk2[i], cmp_wv2[i]])
    wup = ffn_w_up[i]
    zc = jnp.zeros((Dm, Fp - F), F32)
    w_up = jnp.concatenate([wup[:, :F], zc, wup[:, F:], zc], axis=1).astype(BF16)
    conv_w = jnp.pad(ffn_conv_w[i], ((0, SUBLANE - CONV_W), (0, Fp - F)))
    conv_b = jnp.pad(ffn_conv_b[i], (0, Fp - F)).reshape(1, Fp)
    w_down = jnp.pad(ffn_w_down[i], ((0, Fp - F), (0, 0))).astype(BF16)
    return dict(
        g_attn=attn_norm_g[i].reshape(1, Dm), w_main=w_main, w_small=w_small, wa_pad=wa_pad,
        b_a=gla_b_a[i].reshape(1, -1), gnorm=gla_norm_g[i].reshape(1, -1), w1cat=w1cat, w2s=w2s,
        w_out=w_out[i].astype(BF16), g_ffn=ffn_norm_g[i].reshape(1, Dm), w_up=w_up, conv_w=conv_w,
        conv_b=conv_b, w_down=w_down, w_proj=ple_w_proj[i].astype(BF16), w_gate=ple_w_gate[i].astype(BF16),
        b_gate=ple_b_gate[i].reshape(1, Dm), F=F, Fp=Fp)


def _kv_rows(proj3, c0, rows):
    B = proj3.shape[0]
    t = proj3[:, rows, c0:c0 + KVW]
    return t.reshape(B, t.shape[1], NSA_KV, NSA_HD)


def _prompt_layer(h2d, p2d, B, L, W):
    Dm = h2d.shape[1]
    M = B * L
    proj, small = norm_proj(h2d, W["g_attn"], W["w_main"], W["w_small"],
                            _pick_tile(M, (1024, 512, 256, 128)), 512)
    proj3 = proj.reshape(B, L, NMAIN)
    small3 = small.reshape(B, L, LANE)
    cb = _pick_tile(L, (128, 64, 32, 16))
    s0 = jnp.zeros((B, GLA_HEADS, GLA_DK, GLA_DV), F32)
    og, gla_state = gla(proj3, small3, W["wa_pad"], W["b_a"], W["gnorm"], s0, cb, GLA_SUB, cb)
    cmp = compress(proj3, W["w1cat"], W["w2s"])
    emat, mmat = _block_matrices(L, L // CMP_STRIDE - 1)
    on = nsa_prompt(proj3, small3, cmp, emat, mmat)
    h1, hn = out_proj(og.reshape(M, -1), on.reshape(M, -1), h2d, W["w_out"], W["g_ffn"],
                      _pick_tile(M, (256, 128)))
    prefix = jnp.zeros((B, SUBLANE, W["Fp"]), F32)
    hmid, tail = ffn_up_seq(hn, W["w_up"], W["conv_w"], W["conv_b"], prefix, L,
                            _pick_tile(L, (512, 256, 128)), 512)
    h2 = ffn_down(hmid, W["w_down"], h1, _pick_tile(M, (512, 256, 128)), 512)
    keep = min(WINDOW, L)
    rows = slice(0, L)
    outs = (_kv_rows(proj3, KC0, rows), _kv_rows(proj3, VC0, rows), _kv_rows(proj3, KS0, rows),
            _kv_rows(proj3, VS0, rows), _kv_rows(proj3, KW0, slice(L - keep, L)),
            _kv_rows(proj3, VW0, slice(L - keep, L)), gla_state,
            tail[:, SUBLANE - (CONV_W - 1):, :W["F"]])
    return h2, outs


def _decode_layer(h3d, p2d_tm, B, L, lp, caches, page_table, win_k, win_v, gla_s, conv_s, W):
    Dm = h3d.shape[2]
    Mp = B * lp
    x2 = h3d.reshape(Mp, Dm)
    proj, small = norm_proj(x2, W["g_attn"], W["w_main"], W["w_small"],
                            _pick_tile(Mp, (1024, 512, 256, 128, 64, 32, 16, 8)), 512)
    proj3 = proj.reshape(B, lp, NMAIN)
    small3 = small.reshape(B, lp, LANE)
    og, gla_state = gla(proj3, small3, W["wa_pad"], W["b_a"], W["gnorm"], gla_s, lp, lp, L)
    plen = page_table.shape[1] * caches[0].shape[1]
    emat, mmat = _block_matrices(plen + KCHUNK, (plen + L) // CMP_STRIDE - 1)
    on = nsa_decode(proj3, small3, caches, page_table, win_k, win_v, W["w1cat"], W["w2s"], emat, mmat, L)
    h1, hn = out_proj(og.reshape(Mp, -1), on.reshape(Mp, -1), x2, W["w_out"], W["g_ffn"],
                      _pick_tile(Mp, (256, 128, 64, 32, 16, 8)))
    to_tm = lambda t: t.reshape(B, lp, Dm)[:, :L].transpose(1, 0, 2).reshape(L * B, Dm)
    h1_tm, hn_tm = to_tm(h1), to_tm(hn)
    prefix = jnp.pad(conv_s.transpose(1, 0, 2), ((0, 0), (0, 0), (0, W["Fp"] - W["F"])))
    hmid, tail = ffn_up_tm(hn_tm, W["w_up"], W["conv_w"], W["conv_b"], prefix, L, 512)
    M = L * B
    h2 = ffn_down(hmid, W["w_down"], h1_tm, _pick_tile(M, (512, 256, 128, 64, 32, 16, 8)), 512)
    rows = slice(0, L)
    kw_new, vw_new = _kv_rows(proj3, KW0, rows), _kv_rows(proj3, VW0, rows)
    wshape = (B, win_k.shape[1], NSA_KV, NSA_HD)
    keep = win_k.shape[1]
    outs = (_kv_rows(proj3, KC0, rows), _kv_rows(proj3, VC0, rows), _kv_rows(proj3, KS0, rows),
            _kv_rows(proj3, VS0, rows),
            jnp.concatenate([win_k.reshape(wshape), kw_new], axis=1)[:, -keep:],
            jnp.concatenate([win_v.reshape(wshape), vw_new], axis=1)[:, -keep:],
            gla_state, tail[:, :, :W["F"]].transpose(1, 0, 2))
    return h2, outs


def kernel(x_prompt, x_sample, p_prompt, p_sample, cache_cmp_k, cache_cmp_v, cache_slc_k, cache_slc_v,
           page_table, state_win_k, state_win_v, state_gla, state_ffn_conv, attn_norm_g, w_in, gla_w_a2,
           gla_b_a, gla_norm_g, cmp_wk1, cmp_wk2, cmp_wv1, cmp_wv2, w_out, ffn_norm_g, ffn_w_up,
           ffn_conv_w, ffn_conv_b, ffn_w_down, ple_w_proj, ple_w_gate, ple_b_gate, final_norm_g):
    depth = w_in.shape[0]
    assert depth == 1, "layers are chained through HBM one at a time; only depth 1 is wired"
    Bp, Lp, Dm = x_prompt.shape
    Bs, Ls, _ = x_sample.shape
    W = _prep_weights(0, attn_norm_g, w_in, gla_w_a2, gla_b_a, gla_norm_g, cmp_wk1, cmp_wk2, cmp_wv1,
                      cmp_wv2, w_out, ffn_norm_g, ffn_w_up, ffn_conv_w, ffn_conv_b, ffn_w_down,
                      ple_w_proj, ple_w_gate, ple_b_gate)
    g_final = final_norm_g.reshape(1, Dm)

    Mp = Bp * Lp
    h2, outs_p = _prompt_layer(x_prompt.reshape(Mp, Dm), None, Bp, Lp, W)
    y_prompt = ple_final(h2, p_prompt[0].reshape(Mp, -1), W["w_gate"], W["b_gate"], W["w_proj"], g_final,
                         _pick_tile(Mp, (256, 128))).reshape(Bp, Lp, Dm)

    lp = -(-Ls // SUBLANE) * SUBLANE
    xs = jnp.pad(x_sample, ((0, 0), (0, lp - Ls), (0, 0)))
    n_pool, psz = cache_cmp_k.shape[1], cache_cmp_k.shape[2]
    caches = [c[0].reshape(n_pool, psz, KVW) for c in (cache_cmp_k, cache_cmp_v, cache_slc_k, cache_slc_v)]
    wlen = state_win_k.shape[2]
    h2s, outs_s = _decode_layer(xs, None, Bs, Ls, lp, caches, page_table,
                                state_win_k[0].reshape(Bs, wlen, KVW), state_win_v[0].reshape(Bs, wlen, KVW),
                                state_gla[0], state_ffn_conv[0], W)
    p_tm = p_sample[0].transpose(1, 0, 2).reshape(Ls * Bs, -1)
    Ms = Ls * Bs
    y_tm = ple_final(h2s, p_tm, W["w_gate"], W["b_gate"], W["w_proj"], g_final,
                     _pick_tile(Ms, (256, 128, 64, 32, 16, 8)))
    y_sample = y_tm.reshape(Ls, Bs, Dm).transpose(1, 0, 2)

    lead = lambda t: t[None]
    return (y_prompt, y_sample) + tuple(lead(t) for t in outs_p) + tuple(lead(t) for t in outs_s)
```

```python
import functools
import math

import numpy as np
import jax
import jax.numpy as jnp
from jax import lax
from jax.experimental import pallas as pl
from jax.experimental.pallas import tpu as pltpu

F32 = jnp.float32
BF16 = jnp.bfloat16

GLA_HEADS = 4
GLA_DK = 128
GLA_DV = 256
GLA_RANK = 16
GLA_TAU = 16.0
GLA_SUB = 16
NSA_HEADS = 8
NSA_KV = 2
NSA_REP = NSA_HEADS // NSA_KV
NSA_HD = 128
CMP_LEN = 32
CMP_STRIDE = 16
SLC_LEN = 64
N_SELECT = 16
WINDOW = 512
CONV_W = 3
EPS = 1e-6
NEG = -1e30
FORCED = 1e6

LANE = 128
SUBLANE = 8
KCHUNK = 128
VMEM_LIMIT = 56 * 1024 * 1024

GQ0 = 0
GK0 = GQ0 + GLA_HEADS * GLA_DK
GV0 = GK0 + GLA_HEADS * GLA_DK
GR0 = GV0 + GLA_HEADS * GLA_DV
NQ0 = GR0 + GLA_HEADS * GLA_DV
KVW = NSA_KV * NSA_HD
KC0 = NQ0 + NSA_HEADS * NSA_HD
VC0 = KC0 + KVW
KS0 = VC0 + KVW
VS0 = KS0 + KVW
KW0 = VS0 + KVW
VW0 = KW0 + KVW
NMAIN = VW0 + KVW
GATE0 = GLA_RANK


def _cp(sem):
    return pltpu.CompilerParams(dimension_semantics=sem, vmem_limit_bytes=VMEM_LIMIT)


def _dot(a, b):
    return jnp.dot(a, b, preferred_element_type=F32)


def _dot_nt(a, b):
    return lax.dot_general(a, b, (((1,), (1,)), ((), ())), preferred_element_type=F32)


def _rms(x, g):
    return x * lax.rsqrt(jnp.mean(x * x, axis=-1, keepdims=True) + EPS) * g


def _norm_proj_kernel(x_ref, g_ref, wm_ref, ws_ref, om_ref, os_ref, xn_ref):
    @pl.when(pl.program_id(1) == 0)
    def _():
        xn = _rms(x_ref[...], g_ref[...]).astype(BF16)
        xn_ref[...] = xn
        os_ref[...] = _dot(xn, ws_ref[...])

    om_ref[...] = _dot(xn_ref[...], wm_ref[...])


def norm_proj(x2, g, w_main, w_small, tm, tn):
    M, Dm = x2.shape
    N = w_main.shape[1]
    return pl.pallas_call(
        _norm_proj_kernel,
        grid=(M // tm, N // tn),
        in_specs=[pl.BlockSpec((tm, Dm), lambda i, j: (i, 0)),
                  pl.BlockSpec((1, Dm), lambda i, j: (0, 0)),
                  pl.BlockSpec((Dm, tn), lambda i, j: (0, j)),
                  pl.BlockSpec((Dm, LANE), lambda i, j: (0, 0))],
        out_specs=[pl.BlockSpec((tm, tn), lambda i, j: (i, j)),
                   pl.BlockSpec((tm, LANE), lambda i, j: (i, 0))],
        out_shape=[jax.ShapeDtypeStruct((M, N), F32), jax.ShapeDtypeStruct((M, LANE), F32)],
        scratch_shapes=[pltpu.VMEM((tm, Dm), BF16)],
        compiler_params=_cp(("parallel", "arbitrary")),
        name="norm_proj")(x2, g, w_main, w_small)


def _gla_kernel(q_ref, k_ref, v_ref, r_ref, sm_ref, wa_ref, ba_ref, gn_ref, s0_ref,
                o_ref, s_ref, st_ref, *, cb, sub, valid):
    c = pl.program_id(1)

    @pl.when(c == 0)
    def _():
        st_ref[...] = s0_ref[0]

    smb = sm_ref[0].astype(BF16)
    ri = lax.broadcasted_iota(jnp.int32, (cb, cb), 0)
    ci = lax.broadcasted_iota(jnp.int32, (cb, cb), 1)
    tri = jnp.where(ci <= ri, 1.0, 0.0)
    rowv = lax.broadcasted_iota(jnp.int32, (cb, 1), 0)
    for h in range(GLA_HEADS):
        ks = slice(h * GLA_DK, (h + 1) * GLA_DK)
        vs = slice(h * GLA_DV, (h + 1) * GLA_DV)
        q = q_ref[0, :, ks] * (GLA_DK ** -0.5)
        k = k_ref[0, :, ks]
        v = v_ref[0, :, vs]
        pre = _dot(smb, wa_ref[:, ks]) + ba_ref[:, ks]
        log_a = jax.nn.log_sigmoid(pre) / GLA_TAU
        if valid < cb:
            log_a = jnp.where(rowv < valid, log_a, 0.0)
            k = jnp.where(rowv < valid, k, 0.0)
            v = jnp.where(rowv < valid, v, 0.0)
        hi = log_a.astype(BF16).astype(F32)
        lo = (log_a - hi).astype(BF16).astype(F32)
        b = _dot(tri, hi) + _dot(tri, lo)
        b_last = b[cb - 1:cb, :]
        S = st_ref[h]
        vb = v.astype(BF16)

        o_inter = _dot((q * jnp.exp(b)).astype(BF16), S.astype(BF16))
        rows = []
        for blk in range(cb // sub):
            lo_r, hi_r = blk * sub, (blk + 1) * sub
            r_dec = jnp.zeros((1, GLA_DK), F32) if blk == 0 else b[lo_r - 1:lo_r, :]
            q_i = q[lo_r:hi_r] * jnp.exp(b[lo_r:hi_r] - r_dec)
            k_i = k[:hi_r] * jnp.exp(r_dec - b[:hi_r])
            a = _dot_nt(q_i, k_i)
            rr = lax.broadcasted_iota(jnp.int32, (sub, hi_r), 0) + lo_r
            cc = lax.broadcasted_iota(jnp.int32, (sub, hi_r), 1)
            a = jnp.where(cc <= rr, a, 0.0)
            rows.append(_dot(a, v[:hi_r]))
        o = o_inter + (rows[0] if len(rows) == 1 else jnp.concatenate(rows, axis=0))

        kh = k * jnp.exp(b_last - b)
        if cb < LANE:
            kh = jnp.concatenate([kh, jnp.zeros((LANE - cb, GLA_DK), F32)], axis=0)
            vpad = jnp.concatenate([vb, jnp.zeros((LANE - cb, GLA_DV), BF16)], axis=0)
        else:
            vpad = vb
        dec_col = jnp.transpose(jnp.broadcast_to(jnp.exp(b_last), (GLA_DK, GLA_DK)))[:, 0:1]
        s_new = dec_col * S + _dot(jnp.transpose(kh).astype(BF16), vpad)
        st_ref[h] = s_new
        s_ref[0, h] = s_new

        r = r_ref[0, :, vs]
        o_ref[0, :, vs] = (_rms(o, gn_ref[...]) * (r * jax.nn.sigmoid(r))).astype(o_ref.dtype)


def gla(proj3, small3, wa_pad, b_a, gnorm, s0, cb, sub, valid):
    B, L, _ = proj3.shape
    H = GLA_HEADS
    qw, vw = H * GLA_DK, H * GLA_DV
    kern = functools.partial(_gla_kernel, cb=cb, sub=sub, valid=valid)
    return pl.pallas_call(
        kern,
        grid=(B, L // cb),
        in_specs=[pl.BlockSpec((1, cb, qw), lambda b, c: (b, c, GQ0 // qw)),
                  pl.BlockSpec((1, cb, qw), lambda b, c: (b, c, GK0 // qw)),
                  pl.BlockSpec((1, cb, vw), lambda b, c: (b, c, GV0 // vw)),
                  pl.BlockSpec((1, cb, vw), lambda b, c: (b, c, GR0 // vw)),
                  pl.BlockSpec((1, cb, LANE), lambda b, c: (b, c, 0)),
                  pl.BlockSpec((LANE, qw), lambda b, c: (0, 0)),
                  pl.BlockSpec((1, qw), lambda b, c: (0, 0)),
                  pl.BlockSpec((1, GLA_DV), lambda b, c: (0, 0)),
                  pl.BlockSpec((1, H, GLA_DK, GLA_DV), lambda b, c: (b, 0, 0, 0))],
        out_specs=[pl.BlockSpec((1, cb, vw), lambda b, c: (b, c, 0)),
                   pl.BlockSpec((1, H, GLA_DK, GLA_DV), lambda b, c: (b, 0, 0, 0))],
        out_shape=[jax.ShapeDtypeStruct((B, L, vw), BF16),
                   jax.ShapeDtypeStruct((B, H, GLA_DK, GLA_DV), F32)],
        scratch_shapes=[pltpu.VMEM((H, GLA_DK, GLA_DV), F32)],
        compiler_params=_cp(("parallel", "arbitrary")),
        name="gla")(proj3, proj3, proj3, proj3, small3, wa_pad, b_a, gnorm, s0)


def _compress_group(load_rows, w1_ref, w2, nseg):
    acc = jnp.zeros((nseg, 2 * NSA_HD), F32)
    for j in range(CMP_STRIDE):
        acc = acc + _dot(load_rows(j).astype(BF16), w1_ref[j])
    first = acc[:, :NSA_HD]
    second = acc[:, NSA_HD:]
    h = jax.nn.gelu(first + pltpu.roll(second, nseg - 1, axis=0))
    return _dot(h.astype(BF16), w2)


def _cmp_branch(q4, ck, cv, qpos4, nb):
    s = _dot_nt(q4, ck.astype(BF16))
    n = lax.broadcasted_iota(jnp.int32, s.shape, 1)
    mask = jnp.where(n * CMP_STRIDE + (CMP_LEN - 1) <= qpos4, n, nb) < nb
    s = jnp.where(mask, s, NEG)
    m = jnp.max(s, axis=-1, keepdims=True)
    e = jnp.where(mask, jnp.exp(s - m), 0.0)
    p = e / jnp.maximum(jnp.sum(e, axis=-1, keepdims=True), 1e-30)
    return p, _dot(p.astype(BF16), cv.astype(BF16))


def _select(pg, mmat, qpos_tok, nslc, nsel):
    hi = pg.astype(BF16)
    lo = (pg - hi.astype(F32)).astype(BF16)
    imp = _dot(hi, mmat) + _dot(lo, mmat)
    blk = lax.broadcasted_iota(jnp.int32, pg.shape, 1)
    cur = qpos_tok // SLC_LEN
    forced = jnp.where(blk == 0, 1, 0) + jnp.where(blk == cur, 1, 0) + jnp.where(blk == cur - 1, 1, 0)
    score = jnp.where(blk <= cur, jnp.where(forced > 0, FORCED, imp), -1.0)
    score = jnp.where(blk < nslc, score, -2.0)
    rank = jnp.zeros(pg.shape, F32)
    for j in range(nslc):
        sj = score[:, j:j + 1]
        tie = jnp.where(blk > j, sj, NEG)
        rank = rank + jnp.where(sj > score, 1.0, 0.0) + jnp.where(tie == score, 1.0, 0.0)
    return jnp.where(rank < nsel, jnp.where(blk < nslc, 1.0, 0.0), 0.0)


def _flash_init(m_ref, l_ref, acc_ref):
    m_ref[...] = jnp.full(m_ref.shape, NEG, F32)
    l_ref[...] = jnp.zeros(l_ref.shape, F32)
    acc_ref[...] = jnp.zeros(acc_ref.shape, F32)


def _flash_step(q4, k, v, mask, m_ref, l_ref, acc_ref):
    s = jnp.where(mask, _dot_nt(q4, k), NEG)
    m_old = m_ref[...]
    m_new = jnp.maximum(m_old, jnp.max(s, axis=-1, keepdims=True))
    alpha = jnp.exp(m_old - m_new)
    p = jnp.where(mask, jnp.exp(s - m_new), 0.0)
    l_ref[...] = alpha * l_ref[...] + jnp.sum(p, axis=-1, keepdims=True)
    acc_ref[...] = alpha * acc_ref[...] + _dot(p.astype(BF16), v)
    m_ref[...] = m_new


def _flash_out(l_ref, acc_ref):
    return acc_ref[...] / jnp.maximum(l_ref[...], 1e-30)


def _tile_rows(x, reps):
    return jnp.concatenate([x] * reps, axis=0)


def _attend(q4, keys, vals, masks):
    s = [jnp.where(mk, _dot_nt(q4, k), NEG) for k, mk in zip(keys, masks)]
    m = s[0].max(axis=-1, keepdims=True)
    for si in s[1:]:
        m = jnp.maximum(m, si.max(axis=-1, keepdims=True))
    e = [jnp.where(mk, jnp.exp(si - m), 0.0) for si, mk in zip(s, masks)]
    l = sum(ei.sum(axis=-1, keepdims=True) for ei in e)
    o = sum(_dot(ei.astype(BF16), v) for ei, v in zip(e, vals))
    return o / jnp.maximum(l, 1e-30)


def _compress_kernel(x_ref, w1_ref, w2_ref, o_ref, *, nseg):
    load = lambda j: x_ref[0, pl.ds(j, nseg, stride=CMP_STRIDE), :]
    o_ref[0, 0, 0] = _compress_group(load, w1_ref.at[0], w2_ref[0].astype(BF16), nseg)


def compress(proj3, w1cat, w2s):
    B, L, _ = proj3.shape
    nseg = L // CMP_STRIDE
    G = NSA_KV
    return pl.pallas_call(
        functools.partial(_compress_kernel, nseg=nseg),
        grid=(B, 2, G),
        in_specs=[pl.BlockSpec((1, L, NSA_HD), lambda b, w, g: (b, 0, KC0 // NSA_HD + w * G + g)),
                  pl.BlockSpec((1, CMP_STRIDE, NSA_HD, 2 * NSA_HD), lambda b, w, g: (w, 0, 0, 0)),
                  pl.BlockSpec((1, NSA_HD, NSA_HD), lambda b, w, g: (w, 0, 0))],
        out_specs=pl.BlockSpec((1, 1, 1, nseg, NSA_HD), lambda b, w, g: (w, b, g, 0, 0)),
        out_shape=jax.ShapeDtypeStruct((2, B, G, nseg, NSA_HD), F32),
        compiler_params=_cp(("parallel", "parallel", "parallel")),
        name="compress")(proj3, w1cat, w2s)


def _nsa_prompt_kernel(q_ref, sm_ref, ck_ref, cv_ref, ks_ref, vs_ref, kw_ref, vw_ref, e_ref, mm_ref,
                       o_ref, selm_ref, m_ref, l_ref, acc_ref, *, tq, nslc, nsel, nb):
    g = pl.program_id(1)
    qb = pl.program_id(2)
    R = NSA_REP
    scale = NSA_HD ** -0.5
    q4 = (jnp.concatenate([q_ref[0, :, r * NSA_HD:(r + 1) * NSA_HD] for r in range(R)], axis=0)
          * scale).astype(BF16)
    qpos_tok = qb * tq + lax.broadcasted_iota(jnp.int32, (tq, 1), 0)
    qpos4 = _tile_rows(qpos_tok, R)

    p, o_c = _cmp_branch(q4, ck_ref[0, 0, 0], cv_ref[0, 0, 0], qpos4, nb)
    pg = p[0:tq]
    for r in range(1, R):
        pg = pg + p[r * tq:(r + 1) * tq]
    sel = _select(pg, mm_ref[...], qpos_tok, nslc, nsel)
    selm_ref[...] = _dot(sel.astype(BF16), e_ref[...])

    lane = lax.broadcasted_iota(jnp.int32, (R * tq, KCHUNK), 1)

    _flash_init(m_ref, l_ref, acc_ref)

    def body_s(c, carry):
        off = pl.multiple_of(c * KCHUNK, KCHUNK)
        k = ks_ref[0, pl.ds(off, KCHUNK), :].astype(BF16)
        v = vs_ref[0, pl.ds(off, KCHUNK), :].astype(BF16)
        selc = _tile_rows(selm_ref[:, pl.ds(off, KCHUNK)], R)
        kpos = jnp.where(selc > 0.5, off + lane, qpos4 + 1)
        _flash_step(q4, k, v, kpos <= qpos4, m_ref, l_ref, acc_ref)
        return carry

    lax.fori_loop(0, qb + 1, body_s, 0)
    o_s = _flash_out(l_ref, acc_ref)

    _flash_init(m_ref, l_ref, acc_ref)

    def body_w(c, carry):
        off = pl.multiple_of(c * KCHUNK, KCHUNK)
        k = kw_ref[0, pl.ds(off, KCHUNK), :].astype(BF16)
        v = vw_ref[0, pl.ds(off, KCHUNK), :].astype(BF16)
        kpos = off + lane
        kpos = jnp.where(kpos > qpos4 - WINDOW, kpos, qpos4 + 1)
        _flash_step(q4, k, v, kpos <= qpos4, m_ref, l_ref, acc_ref)
        return carry

    lax.fori_loop(jnp.maximum(qb - WINDOW // KCHUNK, 0), qb + 1, body_w, 0)
    o_w = _flash_out(l_ref, acc_ref)

    gts = jax.nn.sigmoid(sm_ref[0])
    for r in range(R):
        def gate(br, r=r):
            c0 = GATE0 + br * NSA_HEADS + r
            return jnp.where(g == 0, gts[:, c0:c0 + 1], gts[:, c0 + R:c0 + R + 1])
        rs = slice(r * tq, (r + 1) * tq)
        o = gate(0) * o_c[rs] + gate(1) * o_s[rs] + gate(2) * o_w[rs]
        o_ref[0, :, r * NSA_HD:(r + 1) * NSA_HD] = o.astype(o_ref.dtype)


def nsa_prompt(proj3, small3, cmp, emat, mmat):
    B, L, _ = proj3.shape
    tq = KCHUNK
    nseg = L // CMP_STRIDE
    nb = nseg - 1
    nslc = -(-L // SLC_LEN)
    nsel = min(N_SELECT, nslc)
    G, R = NSA_KV, NSA_REP
    kv_spec = lambda c0: pl.BlockSpec((1, L, NSA_HD), lambda b, g, qb: (b, 0, c0 // NSA_HD + g))
    kern = functools.partial(_nsa_prompt_kernel, tq=tq, nslc=nslc, nsel=nsel, nb=nb)
    return pl.pallas_call(
        kern,
        grid=(B, G, L // tq),
        in_specs=[pl.BlockSpec((1, tq, R * NSA_HD), lambda b, g, qb: (b, qb, NQ0 // (R * NSA_HD) + g)),
                  pl.BlockSpec((1, tq, LANE), lambda b, g, qb: (b, qb, 0)),
                  pl.BlockSpec((1, 1, 1, nseg, NSA_HD), lambda b, g, qb: (0, b, g, 0, 0)),
                  pl.BlockSpec((1, 1, 1, nseg, NSA_HD), lambda b, g, qb: (1, b, g, 0, 0)),
                  kv_spec(KS0), kv_spec(VS0), kv_spec(KW0), kv_spec(VW0),
                  pl.BlockSpec((LANE, L), lambda b, g, qb: (0, 0)),
                  pl.BlockSpec((LANE, LANE), lambda b, g, qb: (0, 0))],
        out_specs=pl.BlockSpec((1, tq, R * NSA_HD), lambda b, g, qb: (b, qb, g)),
        out_shape=jax.ShapeDtypeStruct((B, L, NSA_HEADS * NSA_HD), BF16),
        scratch_shapes=[pltpu.VMEM((tq, L), F32),
                        pltpu.VMEM((R * tq, 1), F32),
                        pltpu.VMEM((R * tq, 1), F32),
                        pltpu.VMEM((R * tq, NSA_HD), F32)],
        compiler_params=_cp(("parallel", "parallel", "arbitrary")),
        name="nsa_prompt")(proj3, small3, cmp, cmp, proj3, proj3, proj3, proj3, emat, mmat)


def _nsa_decode_kernel(pt_ref, q_ref, sm_ref, ksn_ref, vsn_ref, kwn_ref, vwn_ref, wk_ref, wv_ref,
                       w1_ref, w2_ref, e_ref, mm_ref, c0_hbm, c1_hbm, c2_hbm, c3_hbm, o_ref,
                       buf_ref, sem_ref, *, lp, lreal, plen, rows_pp, n_pages, wlen, nslc, nsel, nb):
    b = pl.program_id(0)
    slot = b % 2
    caches = (c0_hbm, c1_hbm, c2_hbm, c3_hbm)
    G = NSA_KV
    R = NSA_REP
    T = R * lp

    def page_copy(bb, sl, w, p):
        return pltpu.make_async_copy(caches[w].at[pt_ref[bb, p]],
                                     buf_ref.at[sl, w, pl.ds(p * rows_pp, rows_pp)],
                                     sem_ref.at[sl, w])

    def fetch(bb, sl):
        for w in range(len(caches)):
            for p in range(n_pages):
                page_copy(bb, sl, w, p).start()

    @pl.when(b == 0)
    def _():
        fetch(0, 0)

    @pl.when(b + 1 < pl.num_programs(0))
    def _():
        fetch(b + 1, 1 - slot)

    for w in range(len(caches)):
        for p in range(n_pages):
            page_copy(b, slot, w, p).wait()

    scale = NSA_HD ** -0.5
    nseg = plen // CMP_STRIDE
    tok = lax.broadcasted_iota(jnp.int32, (lp, 1), 0)
    qpos_tok = plen + tok
    qpos4 = _tile_rows(qpos_tok, R)
    lane_n = lax.broadcasted_iota(jnp.int32, (T, KCHUNK), 1)
    new_pos = jnp.where(lane_n < lreal, plen + lane_n, qpos4 + 1)
    past_pos = lax.broadcasted_iota(jnp.int32, (T, plen), 1)
    win_pos = plen - wlen + lax.broadcasted_iota(jnp.int32, (T, wlen), 1)
    win_pos = jnp.where(win_pos > qpos4 - WINDOW, win_pos, qpos4 + 1)
    win_pos = jnp.where(win_pos >= 0, win_pos, qpos4 + 1)
    new_win_pos = jnp.where(new_pos > qpos4 - WINDOW, new_pos, qpos4 + 1)
    gts = jax.nn.sigmoid(sm_ref[0])
    zpad_f = jnp.zeros((KCHUNK - lp, NSA_HD), F32)
    pad_new = lambda ref, ls: jnp.concatenate([ref[0, :, ls], zpad_f], axis=0).astype(BF16)
    for g in range(G):
        ls = slice(g * NSA_HD, (g + 1) * NSA_HD)
        q4 = (jnp.concatenate([q_ref[0, :, (g * R + r) * NSA_HD:(g * R + r + 1) * NSA_HD]
                               for r in range(R)], axis=0) * scale).astype(BF16)
        cmp = []
        for w in range(2):
            load = lambda j, w=w, g=g: buf_ref[slot, w, pl.ds(G * j + g, nseg, stride=G * CMP_STRIDE), :]
            cmp.append(_compress_group(load, w1_ref.at[w], w2_ref[w].astype(BF16), nseg))
        p, o_c = _cmp_branch(q4, cmp[0], cmp[1], qpos4, nb)
        pg = p[0:lp]
        for r in range(1, R):
            pg = pg + p[r * lp:(r + 1) * lp]
        sel = _select(pg, mm_ref[...], qpos_tok, nslc, nsel)
        selm = _tile_rows(_dot(sel.astype(BF16), e_ref[...]), R)

        k_past = buf_ref[slot, 2, pl.ds(g, plen, stride=G), :].astype(BF16)
        v_past = buf_ref[slot, 3, pl.ds(g, plen, stride=G), :].astype(BF16)
        m_past = jnp.where(selm[:, :plen] > 0.5, past_pos, qpos4 + 1) <= qpos4
        m_new = jnp.where(selm[:, plen:] > 0.5, new_pos, qpos4 + 1) <= qpos4
        o_s = _attend(q4, [k_past, pad_new(ksn_ref, ls)], [v_past, pad_new(vsn_ref, ls)], [m_past, m_new])

        k_win = wk_ref[0, pl.ds(g, wlen, stride=G), :].astype(BF16)
        v_win = wv_ref[0, pl.ds(g, wlen, stride=G), :].astype(BF16)
        o_w = _attend(q4, [k_win, pad_new(kwn_ref, ls)], [v_win, pad_new(vwn_ref, ls)],
                      [win_pos <= qpos4, new_win_pos <= qpos4])

        for r in range(R):
            c0 = GATE0 + g * R + r
            rs = slice(r * lp, (r + 1) * lp)
            o = (gts[:, c0:c0 + 1] * o_c[rs]
                 + gts[:, c0 + NSA_HEADS:c0 + NSA_HEADS + 1] * o_s[rs]
                 + gts[:, c0 + 2 * NSA_HEADS:c0 + 2 * NSA_HEADS + 1] * o_w[rs])
            hq = (g * R + r) * NSA_HD
            o_ref[0, :, hq:hq + NSA_HD] = o.astype(o_ref.dtype)


def nsa_decode(proj3, small3, caches, page_table, win_k, win_v, w1cat, w2s, emat, mmat, lreal):
    B, lp, _ = proj3.shape
    G = NSA_KV
    n_pages = page_table.shape[1]
    rows_pp = caches[0].shape[1]
    plen = n_pages * rows_pp // G
    wlen = win_k.shape[1] // G
    assert plen % CMP_STRIDE == 0 and lreal < CMP_STRIDE and lreal <= lp
    nb = (plen + lreal) // CMP_STRIDE - 1
    nslc = -(-(plen + lreal) // SLC_LEN)
    nsel = min(N_SELECT, nslc)
    new_spec = lambda c0: pl.BlockSpec((1, lp, KVW), lambda b, pt: (b, 0, c0 // KVW))
    const = lambda shape: pl.BlockSpec(shape, lambda b, pt: (0,) * len(shape))
    hbm = pl.BlockSpec(memory_space=pl.ANY)
    kern = functools.partial(_nsa_decode_kernel, lp=lp, lreal=lreal, plen=plen, rows_pp=rows_pp,
                             n_pages=n_pages, wlen=wlen, nslc=nslc, nsel=nsel, nb=nb)
    grid_spec = pltpu.PrefetchScalarGridSpec(
        num_scalar_prefetch=1,
        grid=(B,),
        in_specs=[pl.BlockSpec((1, lp, NSA_HEADS * NSA_HD), lambda b, pt: (b, 0, NQ0 // (NSA_HEADS * NSA_HD))),
                  pl.BlockSpec((1, lp, LANE), lambda b, pt: (b, 0, 0)),
                  new_spec(KS0), new_spec(VS0), new_spec(KW0), new_spec(VW0),
                  pl.BlockSpec((1, wlen * G, NSA_HD), lambda b, pt: (b, 0, 0)),
                  pl.BlockSpec((1, wlen * G, NSA_HD), lambda b, pt: (b, 0, 0)),
                  const((2, CMP_STRIDE, NSA_HD, 2 * NSA_HD)),
                  const((2, NSA_HD, NSA_HD)),
                  const((LANE, plen + KCHUNK)),
                  const((LANE, LANE)),
                  hbm, hbm, hbm, hbm],
        out_specs=pl.BlockSpec((1, lp, NSA_HEADS * NSA_HD), lambda b, pt: (b, 0, 0)),
        scratch_shapes=[pltpu.VMEM((2, len(caches), n_pages * rows_pp, NSA_HD), F32),
                        pltpu.SemaphoreType.DMA((2, len(caches)))])
    return pl.pallas_call(
        kern,
        grid_spec=grid_spec,
        out_shape=jax.ShapeDtypeStruct((B, lp, NSA_HEADS * NSA_HD), BF16),
        compiler_params=_cp(("arbitrary",)),
        name="nsa_decode")(page_table, proj3, small3, proj3, proj3, proj3, proj3, win_k, win_v,
                           w1cat, w2s, emat, mmat, caches[0], caches[1], caches[2], caches[3])


def _out_proj_kernel(og_ref, on_ref, x_ref, w0_ref, w1_ref, g_ref, h_ref, hn_ref):
    h = x_ref[...] + _dot(og_ref[...], w0_ref[...]) + _dot(on_ref[...], w1_ref[...])
    h_ref[...] = h
    hn_ref[...] = _rms(h, g_ref[...]).astype(BF16)


def out_proj(og, on, x2, w_out, g, tm):
    M, Dm = x2.shape
    Kh = og.shape[1]
    return pl.pallas_call(
        _out_proj_kernel,
        grid=(M // tm,),
        in_specs=[pl.BlockSpec((tm, Kh), lambda i: (i, 0)),
                  pl.BlockSpec((tm, Kh), lambda i: (i, 0)),
                  pl.BlockSpec((tm, Dm), lambda i: (i, 0)),
                  pl.BlockSpec((Kh, Dm), lambda i: (0, 0)),
                  pl.BlockSpec((Kh, Dm), lambda i: (1, 0)),
                  pl.BlockSpec((1, Dm), lambda i: (0, 0))],
        out_specs=[pl.BlockSpec((tm, Dm), lambda i: (i, 0)),
                   pl.BlockSpec((tm, Dm), lambda i: (i, 0))],
        out_shape=[jax.ShapeDtypeStruct((M, Dm), F32), jax.ShapeDtypeStruct((M, Dm), BF16)],
        compiler_params=_cp(("parallel",)),
        name="out_proj")(og, on, x2, w_out, w_out, g)


def _ffn_up_seq_kernel(hn_ref, wa_ref, wu_ref, cw_ref, cb_ref, pre_ref, hm_ref, tail_ref, aext_ref,
                       *, tm, tiles_per_seq):
    m = pl.program_id(1)

    @pl.when(m % tiles_per_seq == 0)
    def _():
        aext_ref[0:SUBLANE, :] = pre_ref[0]

    hn = hn_ref[...]
    a = _dot(hn, wa_ref[...])
    u = _dot(hn, wu_ref[...])
    aext_ref[SUBLANE:SUBLANE + tm, :] = a
    p1 = aext_ref[pl.ds(SUBLANE - 1, tm), :]
    p2 = aext_ref[pl.ds(SUBLANE - 2, tm), :]
    c = p2 * cw_ref[0:1, :] + p1 * cw_ref[1:2, :] + a * cw_ref[2:3, :] + cb_ref[...]
    hm_ref[...] = (jax.nn.gelu(c) * u).astype(hm_ref.dtype)
    tail = a[tm - SUBLANE:tm, :]
    tail_ref[0] = tail
    aext_ref[0:SUBLANE, :] = tail


def ffn_up_seq(hn, w_up, conv_w, conv_b, prefix, seq_len, tm, tf):
    M, Dm = hn.shape
    Fp = conv_w.shape[1]
    nf = Fp // tf
    B = M // seq_len
    tps = seq_len // tm
    kern = functools.partial(_ffn_up_seq_kernel, tm=tm, tiles_per_seq=tps)
    return pl.pallas_call(
        kern,
        grid=(nf, M // tm),
        in_specs=[pl.BlockSpec((tm, Dm), lambda f, m: (m, 0)),
                  pl.BlockSpec((Dm, tf), lambda f, m: (0, f)),
                  pl.BlockSpec((Dm, tf), lambda f, m: (0, f + nf)),
                  pl.BlockSpec((SUBLANE, tf), lambda f, m: (0, f)),
                  pl.BlockSpec((1, tf), lambda f, m: (0, f)),
                  pl.BlockSpec((1, SUBLANE, tf), lambda f, m: (m // tps, 0, f))],
        out_specs=[pl.BlockSpec((tm, tf), lambda f, m: (m, f)),
                   pl.BlockSpec((1, SUBLANE, tf), lambda f, m: (m // tps, 0, f))],
        out_shape=[jax.ShapeDtypeStruct((M, Fp), BF16), jax.ShapeDtypeStruct((B, SUBLANE, Fp), F32)],
        scratch_shapes=[pltpu.VMEM((SUBLANE + tm, tf), F32)],
        compiler_params=_cp(("parallel", "arbitrary")),
        name="ffn_up_seq")(hn, w_up, w_up, conv_w, conv_b, prefix)


def _ffn_up_tm_kernel(hn_ref, wa_ref, wu_ref, cw_ref, cb_ref, pre_ref, hm_ref, tail_ref, *, steps, nb):
    hn = hn_ref[...]
    a = _dot(hn, wa_ref[...])
    u = _dot(hn, wu_ref[...])
    slabs = [pre_ref[i] for i in range(CONV_W - 1)] + [a[t * nb:(t + 1) * nb] for t in range(steps)]
    for t in range(steps):
        c = (slabs[t] * cw_ref[0:1, :] + slabs[t + 1] * cw_ref[1:2, :] + slabs[t + 2] * cw_ref[2:3, :]
             + cb_ref[...])
        hm_ref[t * nb:(t + 1) * nb, :] = (jax.nn.gelu(c) * u[t * nb:(t + 1) * nb]).astype(hm_ref.dtype)
    for i in range(CONV_W - 1):
        tail_ref[i] = slabs[steps + i]


def ffn_up_tm(hn, w_up, conv_w, conv_b, prefix, steps, tf):
    M, Dm = hn.shape
    Fp = conv_w.shape[1]
    nf = Fp // tf
    nb = M // steps
    kern = functools.partial(_ffn_up_tm_kernel, steps=steps, nb=nb)
    return pl.pallas_call(
        kern,
        grid=(nf,),
        in_specs=[pl.BlockSpec((M, Dm), lambda f: (0, 0)),
                  pl.BlockSpec((Dm, tf), lambda f: (0, f)),
                  pl.BlockSpec((Dm, tf), lambda f: (0, f + nf)),
                  pl.BlockSpec((SUBLANE, tf), lambda f: (0, f)),
                  pl.BlockSpec((1, tf), lambda f: (0, f)),
                  pl.BlockSpec((CONV_W - 1, nb, tf), lambda f: (0, 0, f))],
        out_specs=[pl.BlockSpec((M, tf), lambda f: (0, f)),
                   pl.BlockSpec((CONV_W - 1, nb, tf), lambda f: (0, 0, f))],
        out_shape=[jax.ShapeDtypeStruct((M, Fp), BF16), jax.ShapeDtypeStruct((CONV_W - 1, nb, Fp), F32)],
        compiler_params=_cp(("parallel",)),
        name="ffn_up_tm")(hn, w_up, w_up, conv_w, conv_b, prefix)


def _ffn_down_kernel(hm_ref, w_ref, h_ref, o_ref):
    o_ref[...] = h_ref[...] + _dot(hm_ref[...], w_ref[...])


def ffn_down(hmid, w_down, h1, tm, tn):
    M, Fp = hmid.shape
    Dm = w_down.shape[1]
    return pl.pallas_call(
        _ffn_down_kernel,
        grid=(M // tm, Dm // tn),
        in_specs=[pl.BlockSpec((tm, Fp), lambda i, j: (i, 0)),
                  pl.BlockSpec((Fp, tn), lambda i, j: (0, j)),
                  pl.BlockSpec((tm, tn), lambda i, j: (i, j))],
        out_specs=pl.BlockSpec((tm, tn), lambda i, j: (i, j)),
        out_shape=jax.ShapeDtypeStruct((M, Dm), F32),
        compiler_params=_cp(("parallel", "arbitrary")),
        name="ffn_down")(hmid, w_down, h1)


def _ple_final_kernel(h_ref, p_ref, wg_ref, bg_ref, wp_ref, g_ref, o_ref):
    h = h_ref[...]
    gate = jax.nn.sigmoid(_dot(h.astype(BF16), wg_ref[...]) + bg_ref[...])
    h = h + gate * _dot(p_ref[...].astype(BF16), wp_ref[...])
    o_ref[...] = _rms(h, g_ref[...])


def ple_final(h2, p2, w_gate, b_gate, w_proj, g, tm):
    M, Dm = h2.shape
    Pd = p2.shape[1]
    return pl.pallas_call(
        _ple_final_kernel,
        grid=(M // tm,),
        in_specs=[pl.BlockSpec((tm, Dm), lambda i: (i, 0)),
                  pl.BlockSpec((tm, Pd), lambda i: (i, 0)),
                  pl.BlockSpec((Dm, Dm), lambda i: (0, 0)),
                  pl.BlockSpec((1, Dm), lambda i: (0, 0)),
                  pl.BlockSpec((Pd, Dm), lambda i: (0, 0)),
                  pl.BlockSpec((1, Dm), lambda i: (0, 0))],
        out_specs=pl.BlockSpec((tm, Dm), lambda i: (i, 0)),
        out_shape=jax.ShapeDtypeStruct((M, Dm), F32),
        compiler_params=_cp(("parallel",)),
        name="ple_final")(h2, p2, w_gate, b_gate, w_proj, g)


def _pick_tile(n, prefs):
    for t in prefs:
        if n % t == 0:
            return t
    raise ValueError(f"no tile for {n}")


def _block_matrices(total_keys, nb):
    e = np.zeros((LANE, total_keys), np.float32)
    t = np.arange(total_keys)
    e[t // SLC_LEN, t] = 1.0
    ratio = SLC_LEN // CMP_STRIDE
    m = np.zeros((LANE, LANE), np.float32)
    for n in range(nb):
        for s in range(LANE):
            m[n, s] = float(ratio * s <= n <= ratio * s + ratio - 1) + float(ratio * s - 1 <= n <= ratio * s + ratio - 2)
    return jnp.asarray(e, BF16), jnp.asarray(m, BF16)


def _prep_weights(i, attn_norm_g, w_in, gla_w_a2, gla_b_a, gla_norm_g, cmp_wk1, cmp_wk2, cmp_wv1, cmp_wv2,
                  w_out, ffn_norm_g, ffn_w_up, ffn_conv_w, ffn_conv_b, ffn_w_down, ple_w_proj, ple_w_gate,
                  ple_b_gate):
    Dm = w_in.shape[1]
    F = ffn_w_down.shape[1]
    Fp = -(-F // 512) * 512
    win = w_in[i]
    ga0 = NQ0
    ng0 = ga0 + GLA_RANK + (NMAIN - NQ0)
    ngw = 3 * NSA_HEADS
    w_main = jnp.concatenate([win[:, :ga0], win[:, ga0 + GLA_RANK:ng0]], axis=1).astype(BF16)
    w_small = jnp.concatenate([win[:, ga0:ga0 + GLA_RANK], win[:, ng0:ng0 + ngw],
                               jnp.zeros((Dm, LANE - GLA_RANK - ngw), F32)], axis=1).astype(BF16)
    wa_pad = jnp.concatenate([gla_w_a2[i], jnp.zeros((LANE - GLA_RANK, GLA_HEADS * GLA_DK), F32)],
                             axis=0).astype(BF16)
    cat = lambda w1: jnp.concatenate([w1[:CMP_STRIDE], w1[CMP_STRIDE:]], axis=-1)
    w1cat = jnp.stack([cat(cmp_wk1[i]), cat(cmp_wv1[i])]).astype(BF16)
    w2s = jnp.stack([cmp_wk2[i], cmp_wv2[i]])
    wup = ffn_w_up[i]
    zc = jnp.zeros((Dm, Fp - F), F32)
    w_up = jnp.concatenate([wup[:, :F], zc, wup[:, F:], zc], axis=1).astype(BF16)
    conv_w = jnp.pad(ffn_conv_w[i], ((0, SUBLANE - CONV_W), (0, Fp - F)))
    conv_b = jnp.pad(ffn_conv_b[i], (0, Fp - F)).reshape(1, Fp)
    w_down = jnp.pad(ffn_w_down[i], ((0, Fp - F), (0, 0))).astype(BF16)
    return dict(
        g_attn=attn_norm_g[i].reshape(1, Dm), w_main=w_main, w_small=w_small, wa_pad=wa_pad,
        b_a=gla_b_a[i].reshape(1, -1), gnorm=gla_norm_g[i].reshape(1, -1), w1cat=w1cat, w2s=w2s,
        w_out=w_out[i].astype(BF16), g_ffn=ffn_norm_g[i].reshape(1, Dm), w_up=w_up, conv_w=conv_w,
        conv_b=conv_b, w_down=w_down, w_proj=ple_w_proj[i].astype(BF16), w_gate=ple_w_gate[i].astype(BF16),
        b_gate=ple_b_gate[i].reshape(1, Dm), F=F, Fp=Fp)


def _kv_rows(proj3, c0, rows):
    B = proj3.shape[0]
    t = proj3[:, rows, c0:c0 + KVW]
    return t.reshape(B, t.shape[1], NSA_KV, NSA_HD)


def _prompt_layer(h2d, p2d, B, L, W):
    Dm = h2d.shape[1]
    M = B * L
    proj, small = norm_proj(h2d, W["g_attn"], W["w_main"], W["w_small"],
                            _pick_tile(M, (1024, 512, 256, 128)), 512)
    proj3 = proj.reshape(B, L, NMAIN)
    small3 = small.reshape(B, L, LANE)
    cb = _pick_tile(L, (128, 64, 32, 16))
    s0 = jnp.zeros((B, GLA_HEADS, GLA_DK, GLA_DV), F32)
    og, gla_state = gla(proj3, small3, W["wa_pad"], W["b_a"], W["gnorm"], s0, cb, GLA_SUB, cb)
    cmp = compress(proj3, W["w1cat"], W["w2s"])
    emat, mmat = _block_matrices(L, L // CMP_STRIDE - 1)
    on = nsa_prompt(proj3, small3, cmp, emat, mmat)
    h1, hn = out_proj(og.reshape(M, -1), on.reshape(M, -1), h2d, W["w_out"], W["g_ffn"],
                      _pick_tile(M, (256, 128)))
    prefix = jnp.zeros((B, SUBLANE, W["Fp"]), F32)
    hmid, tail = ffn_up_seq(hn, W["w_up"], W["conv_w"], W["conv_b"], prefix, L,
                            _pick_tile(L, (512, 256, 128)), 512)
    h2 = ffn_down(hmid, W["w_down"], h1, _pick_tile(M, (512, 256, 128)), 512)
    keep = min(WINDOW, L)
    rows = slice(0, L)
    outs = (_kv_rows(proj3, KC0, rows), _kv_rows(proj3, VC0, rows), _kv_rows(proj3, KS0, rows),
            _kv_rows(proj3, VS0, rows), _kv_rows(proj3, KW0, slice(L - keep, L)),
            _kv_rows(proj3, VW0, slice(L - keep, L)), gla_state,
            tail[:, SUBLANE - (CONV_W - 1):, :W["F"]])
    return h2, outs


def _decode_layer(h3d, p2d_tm, B, L, lp, caches, page_table, win_k, win_v, gla_s, conv_s, W):
    Dm = h3d.shape[2]
    Mp = B * lp
    x2 = h3d.reshape(Mp, Dm)
    proj, small = norm_proj(x2, W["g_attn"], W["w_main"], W["w_small"],
                            _pick_tile(Mp, (1024, 512, 256, 128, 64, 32, 16, 8)), 512)
    proj3 = proj.reshape(B, lp, NMAIN)
    small3 = small.reshape(B, lp, LANE)
    og, gla_state = gla(proj3, small3, W["wa_pad"], W["b_a"], W["gnorm"], gla_s, lp, lp, L)
    plen = page_table.shape[1] * caches[0].shape[1] // NSA_KV
    emat, mmat = _block_matrices(plen + KCHUNK, (plen + L) // CMP_STRIDE - 1)
    wlen = win_k.shape[1]
    on = nsa_decode(proj3, small3, caches, page_table, win_k.reshape(B, wlen * NSA_KV, NSA_HD),
                    win_v.reshape(B, wlen * NSA_KV, NSA_HD), W["w1cat"], W["w2s"], emat, mmat, L)
    h1, hn = out_proj(og.reshape(Mp, -1), on.reshape(Mp, -1), x2, W["w_out"], W["g_ffn"],
                      _pick_tile(Mp, (256, 128, 64, 32, 16, 8)))
    to_tm = lambda t: t.reshape(B, lp, Dm)[:, :L].transpose(1, 0, 2).reshape(L * B, Dm)
    h1_tm, hn_tm = to_tm(h1), to_tm(hn)
    prefix = jnp.pad(conv_s.transpose(1, 0, 2), ((0, 0), (0, 0), (0, W["Fp"] - W["F"])))
    hmid, tail = ffn_up_tm(hn_tm, W["w_up"], W["conv_w"], W["conv_b"], prefix, L, 512)
    M = L * B
    h2 = ffn_down(hmid, W["w_down"], h1_tm, _pick_tile(M, (512, 256, 128, 64, 32, 16, 8)), 512)
    rows = slice(0, L)
    kw_new, vw_new = _kv_rows(proj3, KW0, rows), _kv_rows(proj3, VW0, rows)
    outs = (_kv_rows(proj3, KC0, rows), _kv_rows(proj3, VC0, rows), _kv_rows(proj3, KS0, rows),
            _kv_rows(proj3, VS0, rows),
            jnp.concatenate([win_k[:, L:], kw_new], axis=1),
            jnp.concatenate([win_v[:, L:], vw_new], axis=1),
            gla_state, tail[:, :, :W["F"]].transpose(1, 0, 2))
    return h2, outs


def kernel(x_prompt, x_sample, p_prompt, p_sample, cache_cmp_k, cache_cmp_v, cache_slc_k, cache_slc_v,
           page_table, state_win_k, state_win_v, state_gla, state_ffn_conv, attn_norm_g, w_in, gla_w_a2,
           gla_b_a, gla_norm_g, cmp_wk1, cmp_wk2, cmp_wv1, cmp_wv2, w_out, ffn_norm_g, ffn_w_up,
           ffn_conv_w, ffn_conv_b, ffn_w_down, ple_w_proj, ple_w_gate, ple_b_gate, final_norm_g):
    depth = w_in.shape[0]
    assert depth == 1, "layers are chained through HBM one at a time; only depth 1 is wired"
    Bp, Lp, Dm = x_prompt.shape
    Bs, Ls, _ = x_sample.shape
    W = _prep_weights(0, attn_norm_g, w_in, gla_w_a2, gla_b_a, gla_norm_g, cmp_wk1, cmp_wk2, cmp_wv1,
                      cmp_wv2, w_out, ffn_norm_g, ffn_w_up, ffn_conv_w, ffn_conv_b, ffn_w_down,
                      ple_w_proj, ple_w_gate, ple_b_gate)
    g_final = final_norm_g.reshape(1, Dm)

    Mp = Bp * Lp
    h2, outs_p = _prompt_layer(x_prompt.reshape(Mp, Dm), None, Bp, Lp, W)
    y_prompt = ple_final(h2, p_prompt[0].reshape(Mp, -1), W["w_gate"], W["b_gate"], W["w_proj"], g_final,
                         _pick_tile(Mp, (256, 128))).reshape(Bp, Lp, Dm)

    lp = -(-Ls // SUBLANE) * SUBLANE
    xs = jnp.pad(x_sample, ((0, 0), (0, lp - Ls), (0, 0)))
    n_pool, psz = cache_cmp_k.shape[1], cache_cmp_k.shape[2]
    caches = [c[0].reshape(n_pool, psz * NSA_KV, NSA_HD)
              for c in (cache_cmp_k, cache_cmp_v, cache_slc_k, cache_slc_v)]
    h2s, outs_s = _decode_layer(xs, None, Bs, Ls, lp, caches, page_table, state_win_k[0], state_win_v[0],
                                state_gla[0], state_ffn_conv[0], W)
    p_tm = p_sample[0].transpose(1, 0, 2).reshape(Ls * Bs, -1)
    Ms = Ls * Bs
    y_tm = ple_final(h2s, p_tm, W["w_gate"], W["b_gate"], W["w_proj"], g_final,
                     _pick_tile(Ms, (256, 128, 64, 32, 16, 8)))
    y_sample = y_tm.reshape(Ls, Bs, Dm).transpose(1, 0, 2)

    lead = lambda t: t[None]
    return (y_prompt, y_sample) + tuple(lead(t) for t in outs_p) + tuple(lead(t) for t in outs_s)
```

```python
import functools
import math

import numpy as np
import jax
import jax.numpy as jnp
from jax import lax
from jax.experimental import pallas as pl
from jax.experimental.pallas import tpu as pltpu

F32 = jnp.float32
BF16 = jnp.bfloat16

GLA_HEADS = 4
GLA_DK = 128
GLA_DV = 256
GLA_RANK = 16
GLA_TAU = 16.0
GLA_SUB = 16
NSA_HEADS = 8
NSA_KV = 2
NSA_REP = NSA_HEADS // NSA_KV
NSA_HD = 128
CMP_LEN = 32
CMP_STRIDE = 16
SLC_LEN = 64
N_SELECT = 16
WINDOW = 512
CONV_W = 3
EPS = 1e-6
NEG = -1e30
FORCED = 1e6

LANE = 128
SUBLANE = 8
KCHUNK = 128
CMP_SEG_ROWS = NSA_KV * CMP_STRIDE
CMP_PITCH = CMP_SEG_ROWS + SUBLANE
VMEM_LIMIT = 56 * 1024 * 1024

GQ0 = 0
GK0 = GQ0 + GLA_HEADS * GLA_DK
GV0 = GK0 + GLA_HEADS * GLA_DK
GR0 = GV0 + GLA_HEADS * GLA_DV
NQ0 = GR0 + GLA_HEADS * GLA_DV
KVW = NSA_KV * NSA_HD
KC0 = NQ0 + NSA_HEADS * NSA_HD
VC0 = KC0 + KVW
KS0 = VC0 + KVW
VS0 = KS0 + KVW
KW0 = VS0 + KVW
VW0 = KW0 + KVW
NMAIN = VW0 + KVW
GATE0 = GLA_RANK


def _cp(sem):
    return pltpu.CompilerParams(dimension_semantics=sem, vmem_limit_bytes=VMEM_LIMIT)


def _dot(a, b):
    return jnp.dot(a, b, preferred_element_type=F32)


def _dot_nt(a, b):
    return lax.dot_general(a, b, (((1,), (1,)), ((), ())), preferred_element_type=F32)


def _rms(x, g):
    return x * lax.rsqrt(jnp.mean(x * x, axis=-1, keepdims=True) + EPS) * g


def _norm_proj_kernel(x_ref, g_ref, wm_ref, ws_ref, om_ref, os_ref, xn_ref):
    @pl.when(pl.program_id(1) == 0)
    def _():
        xn = _rms(x_ref[...], g_ref[...]).astype(BF16)
        xn_ref[...] = xn
        os_ref[...] = _dot(xn, ws_ref[...])

    om_ref[...] = _dot(xn_ref[...], wm_ref[...])


def norm_proj(x2, g, w_main, w_small, tm, tn):
    M, Dm = x2.shape
    N = w_main.shape[1]
    return pl.pallas_call(
        _norm_proj_kernel,
        grid=(M // tm, N // tn),
        in_specs=[pl.BlockSpec((tm, Dm), lambda i, j: (i, 0)),
                  pl.BlockSpec((1, Dm), lambda i, j: (0, 0)),
                  pl.BlockSpec((Dm, tn), lambda i, j: (0, j)),
                  pl.BlockSpec((Dm, LANE), lambda i, j: (0, 0))],
        out_specs=[pl.BlockSpec((tm, tn), lambda i, j: (i, j)),
                   pl.BlockSpec((tm, LANE), lambda i, j: (i, 0))],
        out_shape=[jax.ShapeDtypeStruct((M, N), F32), jax.ShapeDtypeStruct((M, LANE), F32)],
        scratch_shapes=[pltpu.VMEM((tm, Dm), BF16)],
        compiler_params=_cp(("parallel", "arbitrary")),
        name="norm_proj")(x2, g, w_main, w_small)


def _gla_kernel(q_ref, k_ref, v_ref, r_ref, sm_ref, wa_ref, ba_ref, gn_ref, s0_ref,
                o_ref, s_ref, st_ref, *, cb, sub, valid):
    c = pl.program_id(1)

    @pl.when(c == 0)
    def _():
        st_ref[...] = s0_ref[0]

    smb = sm_ref[0].astype(BF16)
    ri = lax.broadcasted_iota(jnp.int32, (cb, cb), 0)
    ci = lax.broadcasted_iota(jnp.int32, (cb, cb), 1)
    tri = jnp.where(ci <= ri, 1.0, 0.0)
    rowv = lax.broadcasted_iota(jnp.int32, (cb, 1), 0)
    for h in range(GLA_HEADS):
        ks = slice(h * GLA_DK, (h + 1) * GLA_DK)
        vs = slice(h * GLA_DV, (h + 1) * GLA_DV)
        q = q_ref[0, :, ks] * (GLA_DK ** -0.5)
        k = k_ref[0, :, ks]
        v = v_ref[0, :, vs]
        pre = _dot(smb, wa_ref[:, ks]) + ba_ref[:, ks]
        log_a = jax.nn.log_sigmoid(pre) / GLA_TAU
        if valid < cb:
            log_a = jnp.where(rowv < valid, log_a, 0.0)
            k = jnp.where(rowv < valid, k, 0.0)
            v = jnp.where(rowv < valid, v, 0.0)
        hi = log_a.astype(BF16).astype(F32)
        lo = (log_a - hi).astype(BF16).astype(F32)
        b = _dot(tri, hi) + _dot(tri, lo)
        b_last = b[cb - 1:cb, :]
        S = st_ref[h]
        vb = v.astype(BF16)

        o_inter = _dot((q * jnp.exp(b)).astype(BF16), S.astype(BF16))
        rows = []
        for blk in range(cb // sub):
            lo_r, hi_r = blk * sub, (blk + 1) * sub
            r_dec = jnp.zeros((1, GLA_DK), F32) if blk == 0 else b[lo_r - 1:lo_r, :]
            q_i = q[lo_r:hi_r] * jnp.exp(b[lo_r:hi_r] - r_dec)
            k_i = k[:hi_r] * jnp.exp(r_dec - b[:hi_r])
            a = _dot_nt(q_i, k_i)
            rr = lax.broadcasted_iota(jnp.int32, (sub, hi_r), 0) + lo_r
            cc = lax.broadcasted_iota(jnp.int32, (sub, hi_r), 1)
            a = jnp.where(cc <= rr, a, 0.0)
            rows.append(_dot(a, v[:hi_r]))
        o = o_inter + (rows[0] if len(rows) == 1 else jnp.concatenate(rows, axis=0))

        kh = k * jnp.exp(b_last - b)
        if cb < LANE:
            kh = jnp.concatenate([kh, jnp.zeros((LANE - cb, GLA_DK), F32)], axis=0)
            vpad = jnp.concatenate([vb, jnp.zeros((LANE - cb, GLA_DV), BF16)], axis=0)
        else:
            vpad = vb
        dec_col = jnp.transpose(jnp.broadcast_to(jnp.exp(b_last), (GLA_DK, GLA_DK)))[:, 0:1]
        s_new = dec_col * S + _dot(jnp.transpose(kh).astype(BF16), vpad)
        st_ref[h] = s_new
        s_ref[0, h] = s_new

        r = r_ref[0, :, vs]
        o_ref[0, :, vs] = (_rms(o, gn_ref[...]) * (r * jax.nn.sigmoid(r))).astype(o_ref.dtype)


def gla(proj3, small3, wa_pad, b_a, gnorm, s0, cb, sub, valid):
    B, L, _ = proj3.shape
    H = GLA_HEADS
    qw, vw = H * GLA_DK, H * GLA_DV
    kern = functools.partial(_gla_kernel, cb=cb, sub=sub, valid=valid)
    return pl.pallas_call(
        kern,
        grid=(B, L // cb),
        in_specs=[pl.BlockSpec((1, cb, qw), lambda b, c: (b, c, GQ0 // qw)),
                  pl.BlockSpec((1, cb, qw), lambda b, c: (b, c, GK0 // qw)),
                  pl.BlockSpec((1, cb, vw), lambda b, c: (b, c, GV0 // vw)),
                  pl.BlockSpec((1, cb, vw), lambda b, c: (b, c, GR0 // vw)),
                  pl.BlockSpec((1, cb, LANE), lambda b, c: (b, c, 0)),
                  pl.BlockSpec((LANE, qw), lambda b, c: (0, 0)),
                  pl.BlockSpec((1, qw), lambda b, c: (0, 0)),
                  pl.BlockSpec((1, GLA_DV), lambda b, c: (0, 0)),
                  pl.BlockSpec((1, H, GLA_DK, GLA_DV), lambda b, c: (b, 0, 0, 0))],
        out_specs=[pl.BlockSpec((1, cb, vw), lambda b, c: (b, c, 0)),
                   pl.BlockSpec((1, H, GLA_DK, GLA_DV), lambda b, c: (b, 0, 0, 0))],
        out_shape=[jax.ShapeDtypeStruct((B, L, vw), BF16),
                   jax.ShapeDtypeStruct((B, H, GLA_DK, GLA_DV), F32)],
        scratch_shapes=[pltpu.VMEM((H, GLA_DK, GLA_DV), F32)],
        compiler_params=_cp(("parallel", "arbitrary")),
        name="gla")(proj3, proj3, proj3, proj3, small3, wa_pad, b_a, gnorm, s0)


def _compress_group(load_rows, w1_ref, w2, nseg):
    acc = jnp.zeros((nseg, 2 * NSA_HD), F32)
    for j in range(CMP_STRIDE):
        acc = acc + _dot(load_rows(j).astype(BF16), w1_ref[j])
    first = acc[:, :NSA_HD]
    second = acc[:, NSA_HD:]
    h = jax.nn.gelu(first + pltpu.roll(second, nseg - 1, axis=0))
    return _dot(h.astype(BF16), w2)


def _cmp_branch(q4, ck, cv, qpos4, nb):
    s = _dot_nt(q4, ck.astype(BF16))
    n = lax.broadcasted_iota(jnp.int32, s.shape, 1)
    mask = jnp.where(n * CMP_STRIDE + (CMP_LEN - 1) <= qpos4, n, nb) < nb
    s = jnp.where(mask, s, NEG)
    m = jnp.max(s, axis=-1, keepdims=True)
    e = jnp.where(mask, jnp.exp(s - m), 0.0)
    p = e / jnp.maximum(jnp.sum(e, axis=-1, keepdims=True), 1e-30)
    return p, _dot(p.astype(BF16), cv.astype(BF16))


def _select(pg, mmat, qpos_tok, nslc, nsel):
    hi = pg.astype(BF16)
    lo = (pg - hi.astype(F32)).astype(BF16)
    imp = _dot(hi, mmat) + _dot(lo, mmat)
    blk = lax.broadcasted_iota(jnp.int32, pg.shape, 1)
    cur = qpos_tok // SLC_LEN
    forced = jnp.where(blk == 0, 1, 0) + jnp.where(blk == cur, 1, 0) + jnp.where(blk == cur - 1, 1, 0)
    score = jnp.where(blk <= cur, jnp.where(forced > 0, FORCED, imp), -1.0)
    score = jnp.where(blk < nslc, score, -2.0)
    rank = jnp.zeros(pg.shape, F32)
    for j in range(nslc):
        sj = score[:, j:j + 1]
        tie = jnp.where(blk > j, sj, NEG)
        rank = rank + jnp.where(sj > score, 1.0, 0.0) + jnp.where(tie == score, 1.0, 0.0)
    return jnp.where(rank < nsel, jnp.where(blk < nslc, 1.0, 0.0), 0.0)


def _flash_init(m_ref, l_ref, acc_ref):
    m_ref[...] = jnp.full(m_ref.shape, NEG, F32)
    l_ref[...] = jnp.zeros(l_ref.shape, F32)
    acc_ref[...] = jnp.zeros(acc_ref.shape, F32)


def _flash_step(q4, k, v, mask, m_ref, l_ref, acc_ref):
    s = jnp.where(mask, _dot_nt(q4, k), NEG)
    m_old = m_ref[...]
    m_new = jnp.maximum(m_old, jnp.max(s, axis=-1, keepdims=True))
    alpha = jnp.exp(m_old - m_new)
    p = jnp.where(mask, jnp.exp(s - m_new), 0.0)
    l_ref[...] = alpha * l_ref[...] + jnp.sum(p, axis=-1, keepdims=True)
    acc_ref[...] = alpha * acc_ref[...] + _dot(p.astype(BF16), v)
    m_ref[...] = m_new


def _flash_out(l_ref, acc_ref):
    return acc_ref[...] / jnp.maximum(l_ref[...], 1e-30)


def _tile_rows(x, reps):
    return jnp.concatenate([x] * reps, axis=0)


def _attend(q4, keys, vals, masks):
    s = [jnp.where(mk, _dot_nt(q4, k), NEG) for k, mk in zip(keys, masks)]
    m = s[0].max(axis=-1, keepdims=True)
    for si in s[1:]:
        m = jnp.maximum(m, si.max(axis=-1, keepdims=True))
    e = [jnp.where(mk, jnp.exp(si - m), 0.0) for si, mk in zip(s, masks)]
    l = sum(ei.sum(axis=-1, keepdims=True) for ei in e)
    o = sum(_dot(ei.astype(BF16), v) for ei, v in zip(e, vals))
    return o / jnp.maximum(l, 1e-30)


def _compress_kernel(x_ref, w1_ref, w2_ref, o_ref, *, nseg):
    load = lambda j: x_ref[0, pl.ds(j, nseg, stride=CMP_STRIDE), :]
    o_ref[0, 0, 0] = _compress_group(load, w1_ref.at[0], w2_ref[0].astype(BF16), nseg)


def compress(proj3, w1cat, w2s):
    B, L, _ = proj3.shape
    nseg = L // CMP_STRIDE
    G = NSA_KV
    return pl.pallas_call(
        functools.partial(_compress_kernel, nseg=nseg),
        grid=(B, 2, G),
        in_specs=[pl.BlockSpec((1, L, NSA_HD), lambda b, w, g: (b, 0, KC0 // NSA_HD + w * G + g)),
                  pl.BlockSpec((1, CMP_STRIDE, NSA_HD, 2 * NSA_HD), lambda b, w, g: (w, 0, 0, 0)),
                  pl.BlockSpec((1, NSA_HD, NSA_HD), lambda b, w, g: (w, 0, 0))],
        out_specs=pl.BlockSpec((1, 1, 1, nseg, NSA_HD), lambda b, w, g: (w, b, g, 0, 0)),
        out_shape=jax.ShapeDtypeStruct((2, B, G, nseg, NSA_HD), F32),
        compiler_params=_cp(("parallel", "parallel", "parallel")),
        name="compress")(proj3, w1cat, w2s)


def _select_t(pg_t, mm_t, qpos_row, nslc, nsel):
    nsp = -(-nslc // SUBLANE) * SUBLANE
    hi = pg_t.astype(BF16)
    lo = (pg_t - hi.astype(F32)).astype(BF16)
    imp = (_dot(mm_t, hi) + _dot(mm_t, lo))[:nsp]
    blk = lax.broadcasted_iota(jnp.int32, imp.shape, 0)
    cur = qpos_row // SLC_LEN
    forced = jnp.where(blk == 0, 1, 0) + jnp.where(blk == cur, 1, 0) + jnp.where(blk == cur - 1, 1, 0)
    score = jnp.where(blk <= cur, jnp.where(forced > 0, FORCED, imp), -1.0)
    score = jnp.where(blk < nslc, score, -2.0)
    rank = jnp.zeros(imp.shape, F32)
    for j in range(nslc):
        sj = score[j:j + 1, :]
        tie = jnp.where(blk > j, sj, NEG)
        rank = rank + jnp.where(sj > score, 1.0, 0.0) + jnp.where(tie == score, 1.0, 0.0)
    return jnp.where(rank < nsel, jnp.where(blk < nslc, 1.0, 0.0), 0.0)


def _flash_step_t(q_t, k, v_t, mask, m_ref, l_ref, acc_ref):
    s = jnp.where(mask, _dot(k, q_t), NEG)
    m_old = m_ref[...]
    m_new = jnp.maximum(m_old, jnp.max(s, axis=0, keepdims=True))
    alpha = jnp.exp(m_old - m_new)
    p = jnp.where(mask, jnp.exp(s - m_new), 0.0)
    l_ref[...] = alpha * l_ref[...] + jnp.sum(p, axis=0, keepdims=True)
    acc_ref[...] = alpha * acc_ref[...] + _dot(v_t, p.astype(BF16))
    m_ref[...] = m_new


def _nsa_prompt_kernel(q_ref, sm_ref, ck_ref, cv_ref, ks_ref, vs_ref, kw_ref, vw_ref, mm_ref,
                       o_ref, vst_ref, vwt_ref, sel_ref, m_ref, l_ref, acc_ref, *, tq, kstep, nslc, nsel, nb):
    g = pl.program_id(1)
    qb = pl.program_id(2)
    R = NSA_REP
    T = R * tq
    L = ks_ref.shape[1]
    scale = NSA_HD ** -0.5

    @pl.when(qb == 0)
    def _():
        for c in range(L // KCHUNK):
            cs = slice(c * KCHUNK, (c + 1) * KCHUNK)
            vst_ref[:, cs] = jnp.transpose(vs_ref[0, cs, :]).astype(BF16)
            vwt_ref[:, cs] = jnp.transpose(vw_ref[0, cs, :]).astype(BF16)

    q_t = (jnp.concatenate([jnp.transpose(q_ref[0, :, r * NSA_HD:(r + 1) * NSA_HD]) for r in range(R)],
                           axis=1) * scale).astype(BF16)
    qpos_row = qb * tq + lax.broadcasted_iota(jnp.int32, (1, tq), 1)
    qpos4 = jnp.concatenate([qpos_row] * R, axis=1)

    s = _dot(ck_ref[0, 0, 0].astype(BF16), q_t)
    n = lax.broadcasted_iota(jnp.int32, s.shape, 0)
    mask = jnp.where(n * CMP_STRIDE + (CMP_LEN - 1) <= qpos4, n, nb) < nb
    s = jnp.where(mask, s, NEG)
    e = jnp.where(mask, jnp.exp(s - jnp.max(s, axis=0, keepdims=True)), 0.0)
    p = e / jnp.maximum(jnp.sum(e, axis=0, keepdims=True), 1e-30)
    o_c = _dot(jnp.transpose(cv_ref[0, 0, 0]).astype(BF16), p.astype(BF16))
    pg = p[:, 0:tq]
    for r in range(1, R):
        pg = pg + p[:, r * tq:(r + 1) * tq]
    sel_ref[...] = _select_t(pg, mm_ref[...], qpos_row, nslc, nsel)

    key = lax.broadcasted_iota(jnp.int32, (kstep, T), 0)
    hi_step = (qb * tq + tq - 1) // kstep + 1
    lo_step = jnp.maximum(qb * tq - (WINDOW - 1), 0) // kstep

    def init():
        m_ref[...] = jnp.full(m_ref.shape, NEG, F32)
        l_ref[...] = jnp.zeros(l_ref.shape, F32)
        acc_ref[...] = jnp.zeros(acc_ref.shape, F32)

    def out():
        return acc_ref[...] / jnp.maximum(l_ref[...], 1e-30)

    init()

    def body_s(c, carry):
        off = pl.multiple_of(c * kstep, kstep)
        k = ks_ref[0, pl.ds(off, kstep), :].astype(BF16)
        v_t = vst_ref[:, pl.ds(off, kstep)]
        blocks = [jnp.broadcast_to(sel_ref[pl.ds((kstep // SLC_LEN) * c + i, 1), :], (SLC_LEN, tq))
                  for i in range(kstep // SLC_LEN)]
        selc = jnp.concatenate(blocks, axis=0)
        selc = jnp.concatenate([selc] * R, axis=1)
        kpos = jnp.where(selc > 0.5, off + key, qpos4 + 1)
        _flash_step_t(q_t, k, v_t, kpos <= qpos4, m_ref, l_ref, acc_ref)
        return carry

    lax.fori_loop(0, hi_step, body_s, 0)
    o_s = out()

    init()

    def body_w(c, carry):
        off = pl.multiple_of(c * kstep, kstep)
        k = kw_ref[0, pl.ds(off, kstep), :].astype(BF16)
        v_t = vwt_ref[:, pl.ds(off, kstep)]
        kpos = off + key
        kpos = jnp.where(kpos > qpos4 - WINDOW, kpos, qpos4 + 1)
        _flash_step_t(q_t, k, v_t, kpos <= qpos4, m_ref, l_ref, acc_ref)
        return carry

    lax.fori_loop(lo_step, hi_step, body_w, 0)
    o_w = out()

    g_t = jnp.transpose(jax.nn.sigmoid(sm_ref[0]))
    for r in range(R):
        def gate(br, r=r):
            c0 = GATE0 + br * NSA_HEADS + r
            return jnp.where(g == 0, g_t[c0:c0 + 1, :], g_t[c0 + R:c0 + R + 1, :])
        cs = slice(r * tq, (r + 1) * tq)
        o = gate(0) * o_c[:, cs] + gate(1) * o_s[:, cs] + gate(2) * o_w[:, cs]
        o_ref[0, :, r * NSA_HD:(r + 1) * NSA_HD] = jnp.transpose(o).astype(o_ref.dtype)


def nsa_prompt(proj3, small3, cmp, mmat_t):
    B, L, _ = proj3.shape
    tq = KCHUNK
    nseg = L // CMP_STRIDE
    nb = nseg - 1
    nslc = -(-L // SLC_LEN)
    nsel = min(N_SELECT, nslc)
    nsp = -(-nslc // SUBLANE) * SUBLANE
    G, R = NSA_KV, NSA_REP
    kstep = _pick_tile(L, (2 * KCHUNK, KCHUNK))
    assert nseg == LANE and L % KCHUNK == 0 and kstep % SLC_LEN == 0
    kv_spec = lambda c0: pl.BlockSpec((1, L, NSA_HD), lambda b, g, qb: (b, 0, c0 // NSA_HD + g))
    kern = functools.partial(_nsa_prompt_kernel, tq=tq, kstep=kstep, nslc=nslc, nsel=nsel, nb=nb)
    return pl.pallas_call(
        kern,
        grid=(B, G, L // tq),
        in_specs=[pl.BlockSpec((1, tq, R * NSA_HD), lambda b, g, qb: (b, qb, NQ0 // (R * NSA_HD) + g)),
                  pl.BlockSpec((1, tq, LANE), lambda b, g, qb: (b, qb, 0)),
                  pl.BlockSpec((1, 1, 1, nseg, NSA_HD), lambda b, g, qb: (0, b, g, 0, 0)),
                  pl.BlockSpec((1, 1, 1, nseg, NSA_HD), lambda b, g, qb: (1, b, g, 0, 0)),
                  kv_spec(KS0), kv_spec(VS0), kv_spec(KW0), kv_spec(VW0),
                  pl.BlockSpec((LANE, LANE), lambda b, g, qb: (0, 0))],
        out_specs=pl.BlockSpec((1, tq, R * NSA_HD), lambda b, g, qb: (b, qb, g)),
        out_shape=jax.ShapeDtypeStruct((B, L, NSA_HEADS * NSA_HD), BF16),
        scratch_shapes=[pltpu.VMEM((NSA_HD, L), BF16),
                        pltpu.VMEM((NSA_HD, L), BF16),
                        pltpu.VMEM((nsp, tq), F32),
                        pltpu.VMEM((1, R * tq), F32),
                        pltpu.VMEM((1, R * tq), F32),
                        pltpu.VMEM((NSA_HD, R * tq), F32)],
        compiler_params=_cp(("parallel", "parallel", "arbitrary")),
        name="nsa_prompt")(proj3, small3, cmp, cmp, proj3, proj3, proj3, proj3, mmat_t)


def _nsa_decode_kernel(pt_ref, q_ref, sm_ref, ksn_ref, vsn_ref, kwn_ref, vwn_ref, wk_ref, wv_ref,
                       w1_ref, w2_ref, e_ref, mm_ref, c0_hbm, c1_hbm, c2_hbm, c3_hbm, o_ref,
                       cbuf_ref, sbuf_ref, sem_ref,
                       *, lp, lreal, plen, rows_pp, n_pages, wlen, nslc, nsel, nb):
    b = pl.program_id(0)
    slot = b % 2
    caches = (c0_hbm, c1_hbm, c2_hbm, c3_hbm)
    G = NSA_KV
    R = NSA_REP
    T = R * lp
    segs_pp = rows_pp // CMP_SEG_ROWS

    def copies(bb, sl):
        out = []
        for p in range(n_pages):
            page = pt_ref[bb, p]
            for w in range(2):
                for s in range(segs_pp):
                    out.append(pltpu.make_async_copy(
                        caches[w].at[page, pl.ds(s * CMP_SEG_ROWS, CMP_SEG_ROWS)],
                        cbuf_ref.at[sl, w, pl.ds((p * segs_pp + s) * CMP_PITCH, CMP_SEG_ROWS)],
                        sem_ref.at[sl, w]))
                out.append(pltpu.make_async_copy(
                    caches[2 + w].at[page], sbuf_ref.at[sl, w, pl.ds(p * rows_pp, rows_pp)],
                    sem_ref.at[sl, 2 + w]))
        return out

    @pl.when(b == 0)
    def _():
        for cp in copies(0, 0):
            cp.start()

    @pl.when(b + 1 < pl.num_programs(0))
    def _():
        for cp in copies(b + 1, 1 - slot):
            cp.start()

    for w in range(len(caches)):
        whole = sbuf_ref.at[slot, 0]
        pltpu.make_async_copy(whole, whole, sem_ref.at[slot, w]).wait()

    scale = NSA_HD ** -0.5
    nseg = plen // CMP_STRIDE
    tok = lax.broadcasted_iota(jnp.int32, (lp, 1), 0)
    qpos_tok = plen + tok
    qpos4 = _tile_rows(qpos_tok, R)
    lane_n = lax.broadcasted_iota(jnp.int32, (T, KCHUNK), 1)
    new_pos = jnp.where(lane_n < lreal, plen + lane_n, qpos4 + 1)
    past_pos = lax.broadcasted_iota(jnp.int32, (T, plen), 1)
    win_pos = plen - wlen + lax.broadcasted_iota(jnp.int32, (T, wlen), 1)
    win_pos = jnp.where(win_pos > qpos4 - WINDOW, win_pos, qpos4 + 1)
    win_pos = jnp.where(win_pos >= 0, win_pos, qpos4 + 1)
    new_win_pos = jnp.where(new_pos > qpos4 - WINDOW, new_pos, qpos4 + 1)
    gts = jax.nn.sigmoid(sm_ref[0])
    zpad_f = jnp.zeros((KCHUNK - lp, NSA_HD), F32)
    pad_new = lambda ref, ls: jnp.concatenate([ref[0, :, ls], zpad_f], axis=0).astype(BF16)
    for g in range(G):
        ls = slice(g * NSA_HD, (g + 1) * NSA_HD)
        q4 = (jnp.concatenate([q_ref[0, :, (g * R + r) * NSA_HD:(g * R + r + 1) * NSA_HD]
                               for r in range(R)], axis=0) * scale).astype(BF16)
        cmp = []
        for w in range(2):
            load = lambda j, w=w, g=g: cbuf_ref[slot, w, pl.ds(G * j + g, nseg, stride=CMP_PITCH), :]
            cmp.append(_compress_group(load, w1_ref.at[w], w2_ref[w].astype(BF16), nseg))
        p, o_c = _cmp_branch(q4, cmp[0], cmp[1], qpos4, nb)
        pg = p[0:lp]
        for r in range(1, R):
            pg = pg + p[r * lp:(r + 1) * lp]
        sel = _select(pg, mm_ref[...], qpos_tok, nslc, nsel)
        selm = _tile_rows(_dot(sel.astype(BF16), e_ref[...]), R)

        k_past = sbuf_ref[slot, 0, pl.ds(g, plen, stride=G), :].astype(BF16)
        v_past = sbuf_ref[slot, 1, pl.ds(g, plen, stride=G), :].astype(BF16)
        m_past = jnp.where(selm[:, :plen] > 0.5, past_pos, qpos4 + 1) <= qpos4
        m_new = jnp.where(selm[:, plen:] > 0.5, new_pos, qpos4 + 1) <= qpos4
        o_s = _attend(q4, [k_past, pad_new(ksn_ref, ls)], [v_past, pad_new(vsn_ref, ls)], [m_past, m_new])

        k_win = wk_ref[0, pl.ds(g, wlen, stride=G), :].astype(BF16)
        v_win = wv_ref[0, pl.ds(g, wlen, stride=G), :].astype(BF16)
        o_w = _attend(q4, [k_win, pad_new(kwn_ref, ls)], [v_win, pad_new(vwn_ref, ls)],
                      [win_pos <= qpos4, new_win_pos <= qpos4])

        for r in range(R):
            c0 = GATE0 + g * R + r
            rs = slice(r * lp, (r + 1) * lp)
            o = (gts[:, c0:c0 + 1] * o_c[rs]
                 + gts[:, c0 + NSA_HEADS:c0 + NSA_HEADS + 1] * o_s[rs]
                 + gts[:, c0 + 2 * NSA_HEADS:c0 + 2 * NSA_HEADS + 1] * o_w[rs])
            hq = (g * R + r) * NSA_HD
            o_ref[0, :, hq:hq + NSA_HD] = o.astype(o_ref.dtype)


def nsa_decode(proj3, small3, caches, page_table, win_k, win_v, w1cat, w2s, emat, mmat, lreal):
    B, lp, _ = proj3.shape
    G = NSA_KV
    n_pages = page_table.shape[1]
    rows_pp = caches[0].shape[1]
    plen = n_pages * rows_pp // G
    wlen = win_k.shape[1] // G
    assert plen % CMP_STRIDE == 0 and lreal < CMP_STRIDE and lreal <= lp
    nb = (plen + lreal) // CMP_STRIDE - 1
    nslc = -(-(plen + lreal) // SLC_LEN)
    nsel = min(N_SELECT, nslc)
    new_spec = lambda c0: pl.BlockSpec((1, lp, KVW), lambda b, pt: (b, 0, c0 // KVW))
    const = lambda shape: pl.BlockSpec(shape, lambda b, pt: (0,) * len(shape))
    hbm = pl.BlockSpec(memory_space=pl.ANY)
    kern = functools.partial(_nsa_decode_kernel, lp=lp, lreal=lreal, plen=plen, rows_pp=rows_pp,
                             n_pages=n_pages, wlen=wlen, nslc=nslc, nsel=nsel, nb=nb)
    grid_spec = pltpu.PrefetchScalarGridSpec(
        num_scalar_prefetch=1,
        grid=(B,),
        in_specs=[pl.BlockSpec((1, lp, NSA_HEADS * NSA_HD), lambda b, pt: (b, 0, NQ0 // (NSA_HEADS * NSA_HD))),
                  pl.BlockSpec((1, lp, LANE), lambda b, pt: (b, 0, 0)),
                  new_spec(KS0), new_spec(VS0), new_spec(KW0), new_spec(VW0),
                  pl.BlockSpec((1, wlen * G, NSA_HD), lambda b, pt: (b, 0, 0)),
                  pl.BlockSpec((1, wlen * G, NSA_HD), lambda b, pt: (b, 0, 0)),
                  const((2, CMP_STRIDE, NSA_HD, 2 * NSA_HD)),
                  const((2, NSA_HD, NSA_HD)),
                  const((LANE, plen + KCHUNK)),
                  const((LANE, LANE)),
                  hbm, hbm, hbm, hbm],
        out_specs=pl.BlockSpec((1, lp, NSA_HEADS * NSA_HD), lambda b, pt: (b, 0, 0)),
        scratch_shapes=[pltpu.VMEM((2, 2, n_pages * rows_pp // CMP_SEG_ROWS * CMP_PITCH, NSA_HD), F32),
                        pltpu.VMEM((2, 2, n_pages * rows_pp, NSA_HD), F32),
                        pltpu.SemaphoreType.DMA((2, len(caches)))])
    return pl.pallas_call(
        kern,
        grid_spec=grid_spec,
        out_shape=jax.ShapeDtypeStruct((B, lp, NSA_HEADS * NSA_HD), BF16),
        compiler_params=_cp(("arbitrary",)),
        name="nsa_decode")(page_table, proj3, small3, proj3, proj3, proj3, proj3, win_k, win_v,
                           w1cat, w2s, emat, mmat, caches[0], caches[1], caches[2], caches[3])


def _out_proj_kernel(og_ref, on_ref, x_ref, w0_ref, w1_ref, g_ref, h_ref, hn_ref):
    h = x_ref[...] + _dot(og_ref[...], w0_ref[...]) + _dot(on_ref[...], w1_ref[...])
    h_ref[...] = h
    hn_ref[...] = _rms(h, g_ref[...]).astype(BF16)


def out_proj(og, on, x2, w_out, g, tm):
    M, Dm = x2.shape
    Kh = og.shape[1]
    return pl.pallas_call(
        _out_proj_kernel,
        grid=(M // tm,),
        in_specs=[pl.BlockSpec((tm, Kh), lambda i: (i, 0)),
                  pl.BlockSpec((tm, Kh), lambda i: (i, 0)),
                  pl.BlockSpec((tm, Dm), lambda i: (i, 0)),
                  pl.BlockSpec((Kh, Dm), lambda i: (0, 0)),
                  pl.BlockSpec((Kh, Dm), lambda i: (1, 0)),
                  pl.BlockSpec((1, Dm), lambda i: (0, 0))],
        out_specs=[pl.BlockSpec((tm, Dm), lambda i: (i, 0)),
                   pl.BlockSpec((tm, Dm), lambda i: (i, 0))],
        out_shape=[jax.ShapeDtypeStruct((M, Dm), F32), jax.ShapeDtypeStruct((M, Dm), BF16)],
        compiler_params=_cp(("parallel",)),
        name="out_proj")(og, on, x2, w_out, w_out, g)


def _ffn_up_seq_kernel(hn_ref, wa_ref, wu_ref, cw_ref, cb_ref, pre_ref, hm_ref, tail_ref, aext_ref,
                       *, tm, tiles_per_seq):
    m = pl.program_id(1)

    @pl.when(m % tiles_per_seq == 0)
    def _():
        aext_ref[0:SUBLANE, :] = pre_ref[0]

    hn = hn_ref[...]
    a = _dot(hn, wa_ref[...])
    u = _dot(hn, wu_ref[...])
    aext_ref[SUBLANE:SUBLANE + tm, :] = a
    p1 = aext_ref[pl.ds(SUBLANE - 1, tm), :]
    p2 = aext_ref[pl.ds(SUBLANE - 2, tm), :]
    c = p2 * cw_ref[0:1, :] + p1 * cw_ref[1:2, :] + a * cw_ref[2:3, :] + cb_ref[...]
    hm_ref[...] = (jax.nn.gelu(c) * u).astype(hm_ref.dtype)
    tail = a[tm - SUBLANE:tm, :]
    tail_ref[0] = tail
    aext_ref[0:SUBLANE, :] = tail


def ffn_up_seq(hn, w_up, conv_w, conv_b, prefix, seq_len, tm, tf):
    M, Dm = hn.shape
    Fp = conv_w.shape[1]
    nf = Fp // tf
    B = M // seq_len
    tps = seq_len // tm
    kern = functools.partial(_ffn_up_seq_kernel, tm=tm, tiles_per_seq=tps)
    return pl.pallas_call(
        kern,
        grid=(nf, M // tm),
        in_specs=[pl.BlockSpec((tm, Dm), lambda f, m: (m, 0)),
                  pl.BlockSpec((Dm, tf), lambda f, m: (0, f)),
                  pl.BlockSpec((Dm, tf), lambda f, m: (0, f + nf)),
                  pl.BlockSpec((SUBLANE, tf), lambda f, m: (0, f)),
                  pl.BlockSpec((1, tf), lambda f, m: (0, f)),
                  pl.BlockSpec((1, SUBLANE, tf), lambda f, m: (m // tps, 0, f))],
        out_specs=[pl.BlockSpec((tm, tf), lambda f, m: (m, f)),
                   pl.BlockSpec((1, SUBLANE, tf), lambda f, m: (m // tps, 0, f))],
        out_shape=[jax.ShapeDtypeStruct((M, Fp), BF16), jax.ShapeDtypeStruct((B, SUBLANE, Fp), F32)],
        scratch_shapes=[pltpu.VMEM((SUBLANE + tm, tf), F32)],
        compiler_params=_cp(("parallel", "arbitrary")),
        name="ffn_up_seq")(hn, w_up, w_up, conv_w, conv_b, prefix)


def _ffn_up_tm_kernel(hn_ref, wa_ref, wu_ref, cw_ref, cb_ref, pre_ref, hm_ref, tail_ref, *, steps, nb):
    hn = hn_ref[...]
    a = _dot(hn, wa_ref[...])
    u = _dot(hn, wu_ref[...])
    slabs = [pre_ref[i] for i in range(CONV_W - 1)] + [a[t * nb:(t + 1) * nb] for t in range(steps)]
    for t in range(steps):
        c = (slabs[t] * cw_ref[0:1, :] + slabs[t + 1] * cw_ref[1:2, :] + slabs[t + 2] * cw_ref[2:3, :]
             + cb_ref[...])
        hm_ref[t * nb:(t + 1) * nb, :] = (jax.nn.gelu(c) * u[t * nb:(t + 1) * nb]).astype(hm_ref.dtype)
    for i in range(CONV_W - 1):
        tail_ref[i] = slabs[steps + i]


def ffn_up_tm(hn, w_up, conv_w, conv_b, prefix, steps, tf):
    M, Dm = hn.shape
    Fp = conv_w.shape[1]
    nf = Fp // tf
    nb = M // steps
    kern = functools.partial(_ffn_up_tm_kernel, steps=steps, nb=nb)
    return pl.pallas_call(
        kern,
        grid=(nf,),
        in_specs=[pl.BlockSpec((M, Dm), lambda f: (0, 0)),
                  pl.BlockSpec((Dm, tf), lambda f: (0, f)),
                  pl.BlockSpec((Dm, tf), lambda f: (0, f + nf)),
                  pl.BlockSpec((SUBLANE, tf), lambda f: (0, f)),
                  pl.BlockSpec((1, tf), lambda f: (0, f)),
                  pl.BlockSpec((CONV_W - 1, nb, tf), lambda f: (0, 0, f))],
        out_specs=[pl.BlockSpec((M, tf), lambda f: (0, f)),
                   pl.BlockSpec((CONV_W - 1, nb, tf), lambda f: (0, 0, f))],
        out_shape=[jax.ShapeDtypeStruct((M, Fp), BF16), jax.ShapeDtypeStruct((CONV_W - 1, nb, Fp), F32)],
        compiler_params=_cp(("parallel",)),
        name="ffn_up_tm")(hn, w_up, w_up, conv_w, conv_b, prefix)


def _ffn_down_kernel(hm_ref, w_ref, h_ref, o_ref):
    o_ref[...] = h_ref[...] + _dot(hm_ref[...], w_ref[...])


def ffn_down(hmid, w_down, h1, tm, tn):
    M, Fp = hmid.shape
    Dm = w_down.shape[1]
    return pl.pallas_call(
        _ffn_down_kernel,
        grid=(M // tm, Dm // tn),
        in_specs=[pl.BlockSpec((tm, Fp), lambda i, j: (i, 0)),
                  pl.BlockSpec((Fp, tn), lambda i, j: (0, j)),
                  pl.BlockSpec((tm, tn), lambda i, j: (i, j))],
        out_specs=pl.BlockSpec((tm, tn), lambda i, j: (i, j)),
        out_shape=jax.ShapeDtypeStruct((M, Dm), F32),
        compiler_params=_cp(("parallel", "arbitrary")),
        name="ffn_down")(hmid, w_down, h1)


def _ple_final_kernel(h_ref, p_ref, wg_ref, bg_ref, wp_ref, g_ref, o_ref):
    h = h_ref[...]
    gate = jax.nn.sigmoid(_dot(h.astype(BF16), wg_ref[...]) + bg_ref[...])
    h = h + gate * _dot(p_ref[...].astype(BF16), wp_ref[...])
    o_ref[...] = _rms(h, g_ref[...])


def ple_final(h2, p2, w_gate, b_gate, w_proj, g, tm):
    M, Dm = h2.shape
    Pd = p2.shape[1]
    return pl.pallas_call(
        _ple_final_kernel,
        grid=(M // tm,),
        in_specs=[pl.BlockSpec((tm, Dm), lambda i: (i, 0)),
                  pl.BlockSpec((tm, Pd), lambda i: (i, 0)),
                  pl.BlockSpec((Dm, Dm), lambda i: (0, 0)),
                  pl.BlockSpec((1, Dm), lambda i: (0, 0)),
                  pl.BlockSpec((Pd, Dm), lambda i: (0, 0)),
                  pl.BlockSpec((1, Dm), lambda i: (0, 0))],
        out_specs=pl.BlockSpec((tm, Dm), lambda i: (i, 0)),
        out_shape=jax.ShapeDtypeStruct((M, Dm), F32),
        compiler_params=_cp(("parallel",)),
        name="ple_final")(h2, p2, w_gate, b_gate, w_proj, g)


def _pick_tile(n, prefs):
    for t in prefs:
        if n % t == 0:
            return t
    raise ValueError(f"no tile for {n}")


def _block_matrices(total_keys, nb):
    e = np.zeros((LANE, total_keys), np.float32)
    t = np.arange(total_keys)
    e[t // SLC_LEN, t] = 1.0
    ratio = SLC_LEN // CMP_STRIDE
    m = np.zeros((LANE, LANE), np.float32)
    for n in range(nb):
        for s in range(LANE):
            m[n, s] = float(ratio * s <= n <= ratio * s + ratio - 1) + float(ratio * s - 1 <= n <= ratio * s + ratio - 2)
    return jnp.asarray(e, BF16), jnp.asarray(m, BF16)


def _prep_weights(i, attn_norm_g, w_in, gla_w_a2, gla_b_a, gla_norm_g, cmp_wk1, cmp_wk2, cmp_wv1, cmp_wv2,
                  w_out, ffn_norm_g, ffn_w_up, ffn_conv_w, ffn_conv_b, ffn_w_down, ple_w_proj, ple_w_gate,
                  ple_b_gate):
    Dm = w_in.shape[1]
    F = ffn_w_down.shape[1]
    Fp = -(-F // 512) * 512
    win = w_in[i]
    ga0 = NQ0
    ng0 = ga0 + GLA_RANK + (NMAIN - NQ0)
    ngw = 3 * NSA_HEADS
    w_main = jnp.concatenate([win[:, :ga0], win[:, ga0 + GLA_RANK:ng0]], axis=1).astype(BF16)
    w_small = jnp.concatenate([win[:, ga0:ga0 + GLA_RANK], win[:, ng0:ng0 + ngw],
                               jnp.zeros((Dm, LANE - GLA_RANK - ngw), F32)], axis=1).astype(BF16)
    wa_pad = jnp.concatenate([gla_w_a2[i], jnp.zeros((LANE - GLA_RANK, GLA_HEADS * GLA_DK), F32)],
                             axis=0).astype(BF16)
    cat = lambda w1: jnp.concatenate([w1[:CMP_STRIDE], w1[CMP_STRIDE:]], axis=-1)
    w1cat = jnp.stack([cat(cmp_wk1[i]), cat(cmp_wv1[i])]).astype(BF16)
    w2s = jnp.stack([cmp_wk2[i], cmp_wv2[i]])
    wup = ffn_w_up[i]
    zc = jnp.zeros((Dm, Fp - F), F32)
    w_up = jnp.concatenate([wup[:, :F], zc, wup[:, F:], zc], axis=1).astype(BF16)
    conv_w = jnp.pad(ffn_conv_w[i], ((0, SUBLANE - CONV_W), (0, Fp - F)))
    conv_b = jnp.pad(ffn_conv_b[i], (0, Fp - F)).reshape(1, Fp)
    w_down = jnp.pad(ffn_w_down[i], ((0, Fp - F), (0, 0))).astype(BF16)
    return dict(
        g_attn=attn_norm_g[i].reshape(1, Dm), w_main=w_main, w_small=w_small, wa_pad=wa_pad,
        b_a=gla_b_a[i].reshape(1, -1), gnorm=gla_norm_g[i].reshape(1, -1), w1cat=w1cat, w2s=w2s,
        w_out=w_out[i].astype(BF16), g_ffn=ffn_norm_g[i].reshape(1, Dm), w_up=w_up, conv_w=conv_w,
        conv_b=conv_b, w_down=w_down, w_proj=ple_w_proj[i].astype(BF16), w_gate=ple_w_gate[i].astype(BF16),
        b_gate=ple_b_gate[i].reshape(1, Dm), F=F, Fp=Fp)


def _kv_rows(proj3, c0, rows):
    B = proj3.shape[0]
    t = proj3[:, rows, c0:c0 + KVW]
    return t.reshape(B, t.shape[1], NSA_KV, NSA_HD)


def _prompt_layer(h2d, p2d, B, L, W):
    Dm = h2d.shape[1]
    M = B * L
    proj, small = norm_proj(h2d, W["g_attn"], W["w_main"], W["w_small"],
                            _pick_tile(M, (1024, 512, 256, 128)), 512)
    proj3 = proj.reshape(B, L, NMAIN)
    small3 = small.reshape(B, L, LANE)
    cb = _pick_tile(L, (128, 64, 32, 16))
    s0 = jnp.zeros((B, GLA_HEADS, GLA_DK, GLA_DV), F32)
    og, gla_state = gla(proj3, small3, W["wa_pad"], W["b_a"], W["gnorm"], s0, cb, GLA_SUB, cb)
    cmp = compress(proj3, W["w1cat"], W["w2s"])
    _, mmat = _block_matrices(L, L // CMP_STRIDE - 1)
    on = nsa_prompt(proj3, small3, cmp, mmat.T)
    h1, hn = out_proj(og.reshape(M, -1), on.reshape(M, -1), h2d, W["w_out"], W["g_ffn"],
                      _pick_tile(M, (256, 128)))
    prefix = jnp.zeros((B, SUBLANE, W["Fp"]), F32)
    hmid, tail = ffn_up_seq(hn, W["w_up"], W["conv_w"], W["conv_b"], prefix, L,
                            _pick_tile(L, (512, 256, 128)), 512)
    h2 = ffn_down(hmid, W["w_down"], h1, _pick_tile(M, (512, 256, 128)), 512)
    keep = min(WINDOW, L)
    rows = slice(0, L)
    outs = (_kv_rows(proj3, KC0, rows), _kv_rows(proj3, VC0, rows), _kv_rows(proj3, KS0, rows),
            _kv_rows(proj3, VS0, rows), _kv_rows(proj3, KW0, slice(L - keep, L)),
            _kv_rows(proj3, VW0, slice(L - keep, L)), gla_state,
            tail[:, SUBLANE - (CONV_W - 1):, :W["F"]])
    return h2, outs


def _decode_layer(h3d, p2d_tm, B, L, lp, caches, page_table, win_k, win_v, gla_s, conv_s, W):
    Dm = h3d.shape[2]
    Mp = B * lp
    x2 = h3d.reshape(Mp, Dm)
    proj, small = norm_proj(x2, W["g_attn"], W["w_main"], W["w_small"],
                            _pick_tile(Mp, (1024, 512, 256, 128, 64, 32, 16, 8)), 512)
    proj3 = proj.reshape(B, lp, NMAIN)
    small3 = small.reshape(B, lp, LANE)
    og, gla_state = gla(proj3, small3, W["wa_pad"], W["b_a"], W["gnorm"], gla_s, lp, lp, L)
    plen = page_table.shape[1] * caches[0].shape[1] // NSA_KV
    emat, mmat = _block_matrices(plen + KCHUNK, (plen + L) // CMP_STRIDE - 1)
    wlen = win_k.shape[1]
    on = nsa_decode(proj3, small3, caches, page_table, win_k.reshape(B, wlen * NSA_KV, NSA_HD),
                    win_v.reshape(B, wlen * NSA_KV, NSA_HD), W["w1cat"], W["w2s"], emat, mmat, L)
    h1, hn = out_proj(og.reshape(Mp, -1), on.reshape(Mp, -1), x2, W["w_out"], W["g_ffn"],
                      _pick_tile(Mp, (256, 128, 64, 32, 16, 8)))
    to_tm = lambda t: t.reshape(B, lp, Dm)[:, :L].transpose(1, 0, 2).reshape(L * B, Dm)
    h1_tm, hn_tm = to_tm(h1), to_tm(hn)
    prefix = jnp.pad(conv_s.transpose(1, 0, 2), ((0, 0), (0, 0), (0, W["Fp"] - W["F"])))
    hmid, tail = ffn_up_tm(hn_tm, W["w_up"], W["conv_w"], W["conv_b"], prefix, L, 512)
    M = L * B
    h2 = ffn_down(hmid, W["w_down"], h1_tm, _pick_tile(M, (512, 256, 128, 64, 32, 16, 8)), 512)
    rows = slice(0, L)
    kw_new, vw_new = _kv_rows(proj3, KW0, rows), _kv_rows(proj3, VW0, rows)
    outs = (_kv_rows(proj3, KC0, rows), _kv_rows(proj3, VC0, rows), _kv_rows(proj3, KS0, rows),
            _kv_rows(proj3, VS0, rows),
            jnp.concatenate([win_k[:, L:], kw_new], axis=1),
            jnp.concatenate([win_v[:, L:], vw_new], axis=1),
            gla_state, tail[:, :, :W["F"]].transpose(1, 0, 2))
    return h2, outs


def kernel(x_prompt, x_sample, p_prompt, p_sample, cache_cmp_k, cache_cmp_v, cache_slc_k, cache_slc_v,
           page_table, state_win_k, state_win_v, state_gla, state_ffn_conv, attn_norm_g, w_in, gla_w_a2,
           gla_b_a, gla_norm_g, cmp_wk1, cmp_wk2, cmp_wv1, cmp_wv2, w_out, ffn_norm_g, ffn_w_up,
           ffn_conv_w, ffn_conv_b, ffn_w_down, ple_w_proj, ple_w_gate, ple_b_gate, final_norm_g):
    depth = w_in.shape[0]
    assert depth == 1, "layers are chained through HBM one at a time; only depth 1 is wired"
    Bp, Lp, Dm = x_prompt.shape
    Bs, Ls, _ = x_sample.shape
    W = _prep_weights(0, attn_norm_g, w_in, gla_w_a2, gla_b_a, gla_norm_g, cmp_wk1, cmp_wk2, cmp_wv1,
                      cmp_wv2, w_out, ffn_norm_g, ffn_w_up, ffn_conv_w, ffn_conv_b, ffn_w_down,
                      ple_w_proj, ple_w_gate, ple_b_gate)
    g_final = final_norm_g.reshape(1, Dm)

    Mp = Bp * Lp
    h2, outs_p = _prompt_layer(x_prompt.reshape(Mp, Dm), None, Bp, Lp, W)
    y_prompt = ple_final(h2, p_prompt[0].reshape(Mp, -1), W["w_gate"], W["b_gate"], W["w_proj"], g_final,
                         _pick_tile(Mp, (256, 128))).reshape(Bp, Lp, Dm)

    lp = -(-Ls // SUBLANE) * SUBLANE
    xs = jnp.pad(x_sample, ((0, 0), (0, lp - Ls), (0, 0)))
    n_pool, psz = cache_cmp_k.shape[1], cache_cmp_k.shape[2]
    caches = [c[0].reshape(n_pool, psz * NSA_KV, NSA_HD)
              for c in (cache_cmp_k, cache_cmp_v, cache_slc_k, cache_slc_v)]
    h2s, outs_s = _decode_layer(xs, None, Bs, Ls, lp, caches, page_table, state_win_k[0], state_win_v[0],
                                state_gla[0], state_ffn_conv[0], W)
    p_tm = p_sample[0].transpose(1, 0, 2).reshape(Ls * Bs, -1)
    Ms = Ls * Bs
    y_tm = ple_final(h2s, p_tm, W["w_gate"], W["b_gate"], W["w_proj"], g_final,
                     _pick_tile(Ms, (256, 128, 64, 32, 16, 8)))
    y_sample = y_tm.reshape(Ls, Bs, Dm).transpose(1, 0, 2)

    lead = lambda t: t[None]
    return (y_prompt, y_sample) + tuple(lead(t) for t in outs_p) + tuple(lead(t) for t in outs_s)
```

```python
import functools
import math

import numpy as np
import jax
import jax.numpy as jnp
from jax import lax
from jax.experimental import pallas as pl
from jax.experimental.pallas import tpu as pltpu

F32 = jnp.float32
BF16 = jnp.bfloat16

GLA_HEADS = 4
GLA_DK = 128
GLA_DV = 256
GLA_RANK = 16
GLA_TAU = 16.0
GLA_SUB = 16
NSA_HEADS = 8
NSA_KV = 2
NSA_REP = NSA_HEADS // NSA_KV
NSA_HD = 128
CMP_LEN = 32
CMP_STRIDE = 16
SLC_LEN = 64
N_SELECT = 16
WINDOW = 512
CONV_W = 3
EPS = 1e-6
NEG = -1e30
FORCED = 1e6

LANE = 128
SUBLANE = 8
KCHUNK = 128
CMP_SEG_ROWS = NSA_KV * CMP_STRIDE
CMP_PITCH = CMP_SEG_ROWS + SUBLANE
VMEM_LIMIT = 56 * 1024 * 1024

GQ0 = 0
GK0 = GQ0 + GLA_HEADS * GLA_DK
GV0 = GK0 + GLA_HEADS * GLA_DK
GR0 = GV0 + GLA_HEADS * GLA_DV
NQ0 = GR0 + GLA_HEADS * GLA_DV
KVW = NSA_KV * NSA_HD
KC0 = NQ0 + NSA_HEADS * NSA_HD
VC0 = KC0 + KVW
KS0 = VC0 + KVW
VS0 = KS0 + KVW
KW0 = VS0 + KVW
VW0 = KW0 + KVW
NMAIN = VW0 + KVW
GATE0 = GLA_RANK


def _cp(sem):
    return pltpu.CompilerParams(dimension_semantics=sem, vmem_limit_bytes=VMEM_LIMIT)


def _dot(a, b):
    return jnp.dot(a, b, preferred_element_type=F32)


def _dot_nt(a, b):
    return lax.dot_general(a, b, (((1,), (1,)), ((), ())), preferred_element_type=F32)


def _rms(x, g):
    return x * lax.rsqrt(jnp.mean(x * x, axis=-1, keepdims=True) + EPS) * g


def _norm_proj_kernel(x_ref, g_ref, wm_ref, ws_ref, om_ref, os_ref, xn_ref):
    @pl.when(pl.program_id(1) == 0)
    def _():
        xn = _rms(x_ref[...], g_ref[...]).astype(BF16)
        xn_ref[...] = xn
        os_ref[...] = _dot(xn, ws_ref[...])

    om_ref[...] = _dot(xn_ref[...], wm_ref[...])


def norm_proj(x2, g, w_main, w_small, tm, tn):
    M, Dm = x2.shape
    N = w_main.shape[1]
    return pl.pallas_call(
        _norm_proj_kernel,
        grid=(M // tm, N // tn),
        in_specs=[pl.BlockSpec((tm, Dm), lambda i, j: (i, 0)),
                  pl.BlockSpec((1, Dm), lambda i, j: (0, 0)),
                  pl.BlockSpec((Dm, tn), lambda i, j: (0, j)),
                  pl.BlockSpec((Dm, LANE), lambda i, j: (0, 0))],
        out_specs=[pl.BlockSpec((tm, tn), lambda i, j: (i, j)),
                   pl.BlockSpec((tm, LANE), lambda i, j: (i, 0))],
        out_shape=[jax.ShapeDtypeStruct((M, N), F32), jax.ShapeDtypeStruct((M, LANE), F32)],
        scratch_shapes=[pltpu.VMEM((tm, Dm), BF16)],
        compiler_params=_cp(("parallel", "arbitrary")),
        name="norm_proj")(x2, g, w_main, w_small)


def _gla_kernel(q_ref, k_ref, v_ref, r_ref, sm_ref, wa_ref, ba_ref, gn_ref, s0_ref,
                o_ref, s_ref, st_ref, *, cb, sub, valid):
    c = pl.program_id(1)

    @pl.when(c == 0)
    def _():
        st_ref[...] = s0_ref[0]

    H = GLA_HEADS
    hk = lambda h: slice(h * GLA_DK, (h + 1) * GLA_DK)
    hv = lambda h: slice(h * GLA_DV, (h + 1) * GLA_DV)
    ri = lax.broadcasted_iota(jnp.int32, (cb, cb), 0)
    ci = lax.broadcasted_iota(jnp.int32, (cb, cb), 1)
    tri = jnp.where(ci <= ri, 1.0, 0.0)
    rowv = lax.broadcasted_iota(jnp.int32, (cb, 1), 0)
    q = q_ref[0] * (GLA_DK ** -0.5)
    k = k_ref[0]
    v = v_ref[0]
    pre = _dot(sm_ref[0].astype(BF16), wa_ref[...]) + ba_ref[...]
    log_a = jax.nn.log_sigmoid(pre) / GLA_TAU
    if valid < cb:
        log_a = jnp.where(rowv < valid, log_a, 0.0)
        k = jnp.where(rowv < valid, k, 0.0)
        v = jnp.where(rowv < valid, v, 0.0)
    hi = log_a.astype(BF16).astype(F32)
    lo = (log_a - hi).astype(BF16).astype(F32)
    b = _dot(tri, hi) + _dot(tri, lo)
    b_last = b[cb - 1:cb, :]
    S = [st_ref[h] for h in range(H)]
    vb = v.astype(BF16)

    qe = (q * jnp.exp(b)).astype(BF16)
    o = [_dot(qe[:, hk(h)], S[h].astype(BF16)) for h in range(H)]
    a_rows = [[] for _ in range(H)]
    cc = lax.broadcasted_iota(jnp.int32, (sub, cb), 1)
    for blk in range(cb // sub):
        lo_r, hi_r = blk * sub, (blk + 1) * sub
        r_dec = jnp.zeros((1, H * GLA_DK), F32) if blk == 0 else b[lo_r - 1:lo_r, :]
        q_i = q[lo_r:hi_r] * jnp.exp(b[lo_r:hi_r] - r_dec)
        k_i = k * jnp.exp(jnp.where(rowv < hi_r, r_dec - b, 0.0))
        rr = lax.broadcasted_iota(jnp.int32, (sub, cb), 0) + lo_r
        for h in range(H):
            a = _dot_nt(q_i[:, hk(h)], k_i[:, hk(h)])
            a_rows[h].append(jnp.where(cc <= rr, a, 0.0))
    for h in range(H):
        a = a_rows[h][0] if len(a_rows[h]) == 1 else jnp.concatenate(a_rows[h], axis=0)
        o[h] = o[h] + _dot(a, v[:, hv(h)])

    kh = k * jnp.exp(b_last - b)
    dec = jnp.exp(b_last)
    if cb < LANE:
        kh = jnp.concatenate([kh, jnp.zeros((LANE - cb, H * GLA_DK), F32)], axis=0)
        vpad = jnp.concatenate([vb, jnp.zeros((LANE - cb, H * GLA_DV), BF16)], axis=0)
    else:
        vpad = vb
    for h in range(H):
        dec_col = jnp.transpose(jnp.broadcast_to(dec[:, hk(h)], (GLA_DK, GLA_DK)))[:, 0:1]
        s_new = dec_col * S[h] + _dot(jnp.transpose(kh[:, hk(h)]).astype(BF16), vpad[:, hv(h)])
        st_ref[h] = s_new
        s_ref[0, h] = s_new

    r = r_ref[0]
    gate = r * jax.nn.sigmoid(r)
    for h in range(H):
        o_ref[0, :, hv(h)] = (_rms(o[h], gn_ref[...]) * gate[:, hv(h)]).astype(o_ref.dtype)


def gla(proj3, small3, wa_pad, b_a, gnorm, s0, cb, sub, valid):
    B, L, _ = proj3.shape
    H = GLA_HEADS
    qw, vw = H * GLA_DK, H * GLA_DV
    kern = functools.partial(_gla_kernel, cb=cb, sub=sub, valid=valid)
    return pl.pallas_call(
        kern,
        grid=(B, L // cb),
        in_specs=[pl.BlockSpec((1, cb, qw), lambda b, c: (b, c, GQ0 // qw)),
                  pl.BlockSpec((1, cb, qw), lambda b, c: (b, c, GK0 // qw)),
                  pl.BlockSpec((1, cb, vw), lambda b, c: (b, c, GV0 // vw)),
                  pl.BlockSpec((1, cb, vw), lambda b, c: (b, c, GR0 // vw)),
                  pl.BlockSpec((1, cb, LANE), lambda b, c: (b, c, 0)),
                  pl.BlockSpec((LANE, qw), lambda b, c: (0, 0)),
                  pl.BlockSpec((1, qw), lambda b, c: (0, 0)),
                  pl.BlockSpec((1, GLA_DV), lambda b, c: (0, 0)),
                  pl.BlockSpec((1, H, GLA_DK, GLA_DV), lambda b, c: (b, 0, 0, 0))],
        out_specs=[pl.BlockSpec((1, cb, vw), lambda b, c: (b, c, 0)),
                   pl.BlockSpec((1, H, GLA_DK, GLA_DV), lambda b, c: (b, 0, 0, 0))],
        out_shape=[jax.ShapeDtypeStruct((B, L, vw), BF16),
                   jax.ShapeDtypeStruct((B, H, GLA_DK, GLA_DV), F32)],
        scratch_shapes=[pltpu.VMEM((H, GLA_DK, GLA_DV), F32)],
        compiler_params=_cp(("parallel", "arbitrary")),
        name="gla")(proj3, proj3, proj3, proj3, small3, wa_pad, b_a, gnorm, s0)


def _compress_group(load_rows, w1_ref, w2, nseg):
    acc = jnp.zeros((nseg, 2 * NSA_HD), F32)
    for j in range(CMP_STRIDE):
        acc = acc + _dot(load_rows(j).astype(BF16), w1_ref[j])
    first = acc[:, :NSA_HD]
    second = acc[:, NSA_HD:]
    h = jax.nn.gelu(first + pltpu.roll(second, nseg - 1, axis=0))
    return _dot(h.astype(BF16), w2)


def _compress_groups(loads, w1_refs, w2s, nrow):
    n = len(loads)
    acc = [jnp.zeros((nrow, 2 * NSA_HD), F32) for _ in range(n)]
    for j in range(CMP_STRIDE):
        for i in range(n):
            acc[i] = acc[i] + _dot(loads[i](j).astype(BF16), w1_refs[i][j])
    hs = [jax.nn.gelu(a[:, :NSA_HD] + pltpu.roll(a[:, NSA_HD:], nrow - 1, axis=0)) for a in acc]
    return [_dot(h.astype(BF16), w2) for h, w2 in zip(hs, w2s)]


def _cmp_branch(q4, ck, cv, qpos4, nb):
    s = _dot_nt(q4, ck.astype(BF16))
    n = lax.broadcasted_iota(jnp.int32, s.shape, 1)
    mask = jnp.where(n * CMP_STRIDE + (CMP_LEN - 1) <= qpos4, n, nb) < nb
    s = jnp.where(mask, s, NEG)
    m = jnp.max(s, axis=-1, keepdims=True)
    e = jnp.where(mask, jnp.exp(s - m), 0.0)
    p = e / jnp.maximum(jnp.sum(e, axis=-1, keepdims=True), 1e-30)
    return p, _dot(p.astype(BF16), cv.astype(BF16))


def _select(pg, mmat, qpos_tok, nslc, nsel):
    hi = pg.astype(BF16)
    lo = (pg - hi.astype(F32)).astype(BF16)
    imp = _dot(hi, mmat) + _dot(lo, mmat)
    blk = lax.broadcasted_iota(jnp.int32, pg.shape, 1)
    cur = qpos_tok // SLC_LEN
    forced = jnp.where(blk == 0, 1, 0) + jnp.where(blk == cur, 1, 0) + jnp.where(blk == cur - 1, 1, 0)
    score = jnp.where(blk <= cur, jnp.where(forced > 0, FORCED, imp), -1.0)
    score = jnp.where(blk < nslc, score, -2.0)
    rank = jnp.zeros(pg.shape, F32)
    for j in range(nslc):
        sj = score[:, j:j + 1]
        tie = jnp.where(blk > j, sj, NEG)
        rank = rank + jnp.where(sj > score, 1.0, 0.0) + jnp.where(tie == score, 1.0, 0.0)
    return jnp.where(rank < nsel, jnp.where(blk < nslc, 1.0, 0.0), 0.0)


def _flash_init(m_ref, l_ref, acc_ref):
    m_ref[...] = jnp.full(m_ref.shape, NEG, F32)
    l_ref[...] = jnp.zeros(l_ref.shape, F32)
    acc_ref[...] = jnp.zeros(acc_ref.shape, F32)


def _flash_step(q4, k, v, mask, m_ref, l_ref, acc_ref):
    s = jnp.where(mask, _dot_nt(q4, k), NEG)
    m_old = m_ref[...]
    m_new = jnp.maximum(m_old, jnp.max(s, axis=-1, keepdims=True))
    alpha = jnp.exp(m_old - m_new)
    p = jnp.where(mask, jnp.exp(s - m_new), 0.0)
    l_ref[...] = alpha * l_ref[...] + jnp.sum(p, axis=-1, keepdims=True)
    acc_ref[...] = alpha * acc_ref[...] + _dot(p.astype(BF16), v)
    m_ref[...] = m_new


def _flash_out(l_ref, acc_ref):
    return acc_ref[...] / jnp.maximum(l_ref[...], 1e-30)


def _tile_rows(x, reps):
    return jnp.concatenate([x] * reps, axis=0)


def _attend(q4, keys, vals, masks):
    s = [jnp.where(mk, _dot_nt(q4, k), NEG) for k, mk in zip(keys, masks)]
    m = s[0].max(axis=-1, keepdims=True)
    for si in s[1:]:
        m = jnp.maximum(m, si.max(axis=-1, keepdims=True))
    e = [jnp.where(mk, jnp.exp(si - m), 0.0) for si, mk in zip(s, masks)]
    l = sum(ei.sum(axis=-1, keepdims=True) for ei in e)
    o = sum(_dot(ei.astype(BF16), v) for ei, v in zip(e, vals))
    return o / jnp.maximum(l, 1e-30)


def _compress_kernel(x_ref, w1_ref, w2_ref, o_ref, *, nseg):
    load = lambda j: x_ref[0, pl.ds(j, nseg, stride=CMP_STRIDE), :]
    o_ref[0, 0, 0] = _compress_group(load, w1_ref.at[0], w2_ref[0].astype(BF16), nseg)


def compress(proj3, w1cat, w2s):
    B, L, _ = proj3.shape
    nseg = L // CMP_STRIDE
    G = NSA_KV
    return pl.pallas_call(
        functools.partial(_compress_kernel, nseg=nseg),
        grid=(B, 2, G),
        in_specs=[pl.BlockSpec((1, L, NSA_HD), lambda b, w, g: (b, 0, KC0 // NSA_HD + w * G + g)),
                  pl.BlockSpec((1, CMP_STRIDE, NSA_HD, 2 * NSA_HD), lambda b, w, g: (w, 0, 0, 0)),
                  pl.BlockSpec((1, NSA_HD, NSA_HD), lambda b, w, g: (w, 0, 0))],
        out_specs=pl.BlockSpec((1, 1, 1, nseg, NSA_HD), lambda b, w, g: (w, b, g, 0, 0)),
        out_shape=jax.ShapeDtypeStruct((2, B, G, nseg, NSA_HD), F32),
        compiler_params=_cp(("parallel", "parallel", "parallel")),
        name="compress")(proj3, w1cat, w2s)


def _select_t(pg_t, mm_t, qpos_row, nslc, nsel):
    nsp = -(-nslc // SUBLANE) * SUBLANE
    hi = pg_t.astype(BF16)
    lo = (pg_t - hi.astype(F32)).astype(BF16)
    imp = (_dot(mm_t, hi) + _dot(mm_t, lo))[:nsp]
    blk = lax.broadcasted_iota(jnp.int32, imp.shape, 0)
    cur = qpos_row // SLC_LEN
    forced = jnp.where(blk == 0, 1, 0) + jnp.where(blk == cur, 1, 0) + jnp.where(blk == cur - 1, 1, 0)
    score = jnp.where(blk <= cur, jnp.where(forced > 0, FORCED, imp), -1.0)
    score = jnp.where(blk < nslc, score, -2.0)
    rank = jnp.zeros(imp.shape, F32)
    for j in range(nslc):
        sj = score[j:j + 1, :]
        tie = jnp.where(blk > j, sj, NEG)
        rank = rank + jnp.where(sj > score, 1.0, 0.0) + jnp.where(tie == score, 1.0, 0.0)
    return jnp.where(rank < nsel, jnp.where(blk < nslc, 1.0, 0.0), 0.0)


def _flash_step_t(q_t, k, v_t, mask, m_ref, l_ref, acc_ref):
    s = jnp.where(mask, _dot(k, q_t), NEG)
    m_old = m_ref[...]
    m_new = jnp.maximum(m_old, jnp.max(s, axis=0, keepdims=True))
    alpha = jnp.exp(m_old - m_new)
    p = jnp.where(mask, jnp.exp(s - m_new), 0.0)
    l_ref[...] = alpha * l_ref[...] + jnp.sum(p, axis=0, keepdims=True)
    acc_ref[...] = alpha * acc_ref[...] + _dot(v_t, p.astype(BF16))
    m_ref[...] = m_new


def _nsa_prompt_kernel(q_ref, sm_ref, ck_ref, cv_ref, ks_ref, vs_ref, kw_ref, vw_ref, mm_ref,
                       o_ref, vst_ref, vwt_ref, sel_ref, m_ref, l_ref, acc_ref, *, tq, kstep, nslc, nsel, nb):
    g = pl.program_id(1)
    qb = pl.program_id(2)
    R = NSA_REP
    T = R * tq
    L = ks_ref.shape[1]
    scale = NSA_HD ** -0.5

    @pl.when(qb == 0)
    def _():
        for c in range(L // KCHUNK):
            cs = slice(c * KCHUNK, (c + 1) * KCHUNK)
            vst_ref[:, cs] = jnp.transpose(vs_ref[0, cs, :]).astype(BF16)
            vwt_ref[:, cs] = jnp.transpose(vw_ref[0, cs, :]).astype(BF16)

    q_t = (jnp.concatenate([jnp.transpose(q_ref[0, :, r * NSA_HD:(r + 1) * NSA_HD]) for r in range(R)],
                           axis=1) * scale).astype(BF16)
    qpos_row = qb * tq + lax.broadcasted_iota(jnp.int32, (1, tq), 1)
    qpos4 = jnp.concatenate([qpos_row] * R, axis=1)

    s = _dot(ck_ref[0, 0, 0].astype(BF16), q_t)
    n = lax.broadcasted_iota(jnp.int32, s.shape, 0)
    mask = jnp.where(n * CMP_STRIDE + (CMP_LEN - 1) <= qpos4, n, nb) < nb
    s = jnp.where(mask, s, NEG)
    e = jnp.where(mask, jnp.exp(s - jnp.max(s, axis=0, keepdims=True)), 0.0)
    p = e / jnp.maximum(jnp.sum(e, axis=0, keepdims=True), 1e-30)
    o_c = _dot(jnp.transpose(cv_ref[0, 0, 0]).astype(BF16), p.astype(BF16))
    pg = p[:, 0:tq]
    for r in range(1, R):
        pg = pg + p[:, r * tq:(r + 1) * tq]
    sel_ref[...] = _select_t(pg, mm_ref[...], qpos_row, nslc, nsel)

    key = lax.broadcasted_iota(jnp.int32, (kstep, T), 0)
    hi_step = (qb * tq + tq - 1) // kstep + 1
    lo_step = jnp.maximum(qb * tq - (WINDOW - 1), 0) // kstep

    def init():
        m_ref[...] = jnp.full(m_ref.shape, NEG, F32)
        l_ref[...] = jnp.zeros(l_ref.shape, F32)
        acc_ref[...] = jnp.zeros(acc_ref.shape, F32)

    def out():
        return acc_ref[...] / jnp.maximum(l_ref[...], 1e-30)

    init()

    def body_s(c, carry):
        off = pl.multiple_of(c * kstep, kstep)
        k = ks_ref[0, pl.ds(off, kstep), :].astype(BF16)
        v_t = vst_ref[:, pl.ds(off, kstep)]
        blocks = [jnp.broadcast_to(sel_ref[pl.ds((kstep // SLC_LEN) * c + i, 1), :], (SLC_LEN, tq))
                  for i in range(kstep // SLC_LEN)]
        selc = jnp.concatenate(blocks, axis=0)
        selc = jnp.concatenate([selc] * R, axis=1)
        kpos = jnp.where(selc > 0.5, off + key, qpos4 + 1)
        _flash_step_t(q_t, k, v_t, kpos <= qpos4, m_ref, l_ref, acc_ref)
        return carry

    lax.fori_loop(0, hi_step, body_s, 0)
    o_s = out()

    init()

    def body_w(c, carry):
        off = pl.multiple_of(c * kstep, kstep)
        k = kw_ref[0, pl.ds(off, kstep), :].astype(BF16)
        v_t = vwt_ref[:, pl.ds(off, kstep)]
        kpos = off + key
        kpos = jnp.where(kpos > qpos4 - WINDOW, kpos, qpos4 + 1)
        _flash_step_t(q_t, k, v_t, kpos <= qpos4, m_ref, l_ref, acc_ref)
        return carry

    lax.fori_loop(lo_step, hi_step, body_w, 0)
    o_w = out()

    g_t = jnp.transpose(jax.nn.sigmoid(sm_ref[0]))
    for r in range(R):
        def gate(br, r=r):
            c0 = GATE0 + br * NSA_HEADS + r
            return jnp.where(g == 0, g_t[c0:c0 + 1, :], g_t[c0 + R:c0 + R + 1, :])
        cs = slice(r * tq, (r + 1) * tq)
        o = gate(0) * o_c[:, cs] + gate(1) * o_s[:, cs] + gate(2) * o_w[:, cs]
        o_ref[0, :, r * NSA_HD:(r + 1) * NSA_HD] = jnp.transpose(o).astype(o_ref.dtype)


def nsa_prompt(proj3, small3, cmp, mmat_t):
    B, L, _ = proj3.shape
    tq = KCHUNK
    nseg = L // CMP_STRIDE
    nb = nseg - 1
    nslc = -(-L // SLC_LEN)
    nsel = min(N_SELECT, nslc)
    nsp = -(-nslc // SUBLANE) * SUBLANE
    G, R = NSA_KV, NSA_REP
    kstep = _pick_tile(L, (2 * KCHUNK, KCHUNK))
    assert nseg == LANE and L % KCHUNK == 0 and kstep % SLC_LEN == 0
    kv_spec = lambda c0: pl.BlockSpec((1, L, NSA_HD), lambda b, g, qb: (b, 0, c0 // NSA_HD + g))
    kern = functools.partial(_nsa_prompt_kernel, tq=tq, kstep=kstep, nslc=nslc, nsel=nsel, nb=nb)
    return pl.pallas_call(
        kern,
        grid=(B, G, L // tq),
        in_specs=[pl.BlockSpec((1, tq, R * NSA_HD), lambda b, g, qb: (b, qb, NQ0 // (R * NSA_HD) + g)),
                  pl.BlockSpec((1, tq, LANE), lambda b, g, qb: (b, qb, 0)),
                  pl.BlockSpec((1, 1, 1, nseg, NSA_HD), lambda b, g, qb: (0, b, g, 0, 0)),
                  pl.BlockSpec((1, 1, 1, nseg, NSA_HD), lambda b, g, qb: (1, b, g, 0, 0)),
                  kv_spec(KS0), kv_spec(VS0), kv_spec(KW0), kv_spec(VW0),
                  pl.BlockSpec((LANE, LANE), lambda b, g, qb: (0, 0))],
        out_specs=pl.BlockSpec((1, tq, R * NSA_HD), lambda b, g, qb: (b, qb, g)),
        out_shape=jax.ShapeDtypeStruct((B, L, NSA_HEADS * NSA_HD), BF16),
        scratch_shapes=[pltpu.VMEM((NSA_HD, L), BF16),
                        pltpu.VMEM((NSA_HD, L), BF16),
                        pltpu.VMEM((nsp, tq), F32),
                        pltpu.VMEM((1, R * tq), F32),
                        pltpu.VMEM((1, R * tq), F32),
                        pltpu.VMEM((NSA_HD, R * tq), F32)],
        compiler_params=_cp(("parallel", "parallel", "arbitrary")),
        name="nsa_prompt")(proj3, small3, cmp, cmp, proj3, proj3, proj3, proj3, mmat_t)


def _nsa_decode_kernel(pt_ref, q_ref, sm_ref, ksn_ref, vsn_ref, kwn_ref, vwn_ref, wk_ref, wv_ref,
                       w1_ref, w2_ref, e_ref, mm_ref, c0_hbm, c1_hbm, c2_hbm, c3_hbm,
                       o_ref, wko_ref, wvo_ref, cbuf_ref, sbuf_ref, sem_ref,
                       *, lp, lreal, plen, rows_pp, n_pages, wlen, nslc, nsel, nb):
    b = pl.program_id(0)
    slot = b % 2
    caches = (c0_hbm, c1_hbm, c2_hbm, c3_hbm)
    G = NSA_KV
    R = NSA_REP
    T = R * lp
    segs_pp = rows_pp // CMP_SEG_ROWS

    def copies(bb, sl):
        out = []
        for p in range(n_pages):
            page = pt_ref[bb, p]
            for w in range(2):
                for s in range(segs_pp):
                    out.append(pltpu.make_async_copy(
                        caches[w].at[page, pl.ds(s * CMP_SEG_ROWS, CMP_SEG_ROWS)],
                        cbuf_ref.at[sl, w, pl.ds((p * segs_pp + s) * CMP_PITCH, CMP_SEG_ROWS)],
                        sem_ref.at[sl, w]))
                out.append(pltpu.make_async_copy(
                    caches[2 + w].at[page], sbuf_ref.at[sl, w, pl.ds(p * rows_pp, rows_pp)],
                    sem_ref.at[sl, 2 + w]))
        return out

    @pl.when(b == 0)
    def _():
        for cp in copies(0, 0):
            cp.start()

    @pl.when(b + 1 < pl.num_programs(0))
    def _():
        for cp in copies(b + 1, 1 - slot):
            cp.start()

    for w in range(len(caches)):
        whole = sbuf_ref.at[slot, 0]
        pltpu.make_async_copy(whole, whole, sem_ref.at[slot, w]).wait()

    scale = NSA_HD ** -0.5
    nseg = plen // CMP_STRIDE
    tok = lax.broadcasted_iota(jnp.int32, (lp, 1), 0)
    qpos_tok = plen + tok
    qpos4 = _tile_rows(qpos_tok, R)
    lane_n = lax.broadcasted_iota(jnp.int32, (T, KCHUNK), 1)
    new_pos = jnp.where(lane_n < lreal, plen + lane_n, qpos4 + 1)
    past_pos = lax.broadcasted_iota(jnp.int32, (T, plen), 1)
    win_pos = plen - wlen + lax.broadcasted_iota(jnp.int32, (T, wlen), 1)
    win_pos = jnp.where(win_pos > qpos4 - WINDOW, win_pos, qpos4 + 1)
    win_pos = jnp.where(win_pos >= 0, win_pos, qpos4 + 1)
    new_win_pos = jnp.where(new_pos > qpos4 - WINDOW, new_pos, qpos4 + 1)
    gts = jax.nn.sigmoid(sm_ref[0])
    zpad_f = jnp.zeros((KCHUNK - lp, NSA_HD), F32)
    pad_new = lambda ref, ls: jnp.concatenate([ref[0, :, ls], zpad_f], axis=0).astype(BF16)
    gls = lambda g: slice(g * NSA_HD, (g + 1) * NSA_HD)
    q4 = [(jnp.concatenate([q_ref[0, :, (g * R + r) * NSA_HD:(g * R + r + 1) * NSA_HD]
                            for r in range(R)], axis=0) * scale).astype(BF16) for g in range(G)]

    def seg_rows(w, j):
        return jnp.concatenate([cbuf_ref[slot, w, pl.ds(G * j + g, nseg, stride=CMP_PITCH), :]
                                for g in range(G)], axis=0)
    cmp = _compress_groups([functools.partial(seg_rows, w) for w in range(2)],
                           [w1_ref.at[w] for w in range(2)],
                           [w2_ref[w].astype(BF16) for w in range(2)], G * nseg)
    grow = lambda g: slice(g * nseg, (g + 1) * nseg)
    branch = [_cmp_branch(q4[g], cmp[0][grow(g)], cmp[1][grow(g)], qpos4, nb) for g in range(G)]
    pgs = []
    for g in range(G):
        p = branch[g][0]
        pg = p[0:lp]
        for r in range(1, R):
            pg = pg + p[r * lp:(r + 1) * lp]
        pgs.append(pg)
    sel = _select(jnp.concatenate(pgs, axis=0), mm_ref[...], _tile_rows(qpos_tok, G), nslc, nsel)
    selm_all = _dot(sel.astype(BF16), e_ref[...])

    o_s, o_w = [], []
    for g in range(G):
        selm = _tile_rows(selm_all[g * lp:(g + 1) * lp], R)
        k_past = sbuf_ref[slot, 0, pl.ds(g, plen, stride=G), :].astype(BF16)
        v_past = sbuf_ref[slot, 1, pl.ds(g, plen, stride=G), :].astype(BF16)
        m_past = jnp.where(selm[:, :plen] > 0.5, past_pos, qpos4 + 1) <= qpos4
        m_new = jnp.where(selm[:, plen:] > 0.5, new_pos, qpos4 + 1) <= qpos4
        o_s.append(_attend(q4[g], [k_past, pad_new(ksn_ref, gls(g))], [v_past, pad_new(vsn_ref, gls(g))],
                           [m_past, m_new]))
    for g in range(G):
        k_win = wk_ref[0, pl.ds(g, wlen, stride=G), :].astype(BF16)
        v_win = wv_ref[0, pl.ds(g, wlen, stride=G), :].astype(BF16)
        o_w.append(_attend(q4[g], [k_win, pad_new(kwn_ref, gls(g))], [v_win, pad_new(vwn_ref, gls(g))],
                           [win_pos <= qpos4, new_win_pos <= qpos4]))

    for g in range(G):
        for r in range(R):
            c0 = GATE0 + g * R + r
            rs = slice(r * lp, (r + 1) * lp)
            o = (gts[:, c0:c0 + 1] * branch[g][1][rs]
                 + gts[:, c0 + NSA_HEADS:c0 + NSA_HEADS + 1] * o_s[g][rs]
                 + gts[:, c0 + 2 * NSA_HEADS:c0 + 2 * NSA_HEADS + 1] * o_w[g][rs])
            hq = (g * R + r) * NSA_HD
            o_ref[0, :, hq:hq + NSA_HD] = o.astype(o_ref.dtype)

    wrows = wlen * G
    shift = lreal * G
    for src, new, dst in ((wk_ref, kwn_ref, wko_ref), (wv_ref, vwn_ref, wvo_ref)):
        dst[0, 0:wrows - shift, :] = src[0, shift:wrows, :]
        for t in range(lreal):
            for g in range(G):
                row = wrows - shift + t * G + g
                dst[0, row:row + 1, :] = new[0, t:t + 1, g * NSA_HD:(g + 1) * NSA_HD]


def nsa_decode(proj3, small3, caches, page_table, win_k, win_v, w1cat, w2s, emat, mmat, lreal):
    B, lp, _ = proj3.shape
    G = NSA_KV
    n_pages = page_table.shape[1]
    rows_pp = caches[0].shape[1]
    plen = n_pages * rows_pp // G
    wlen = win_k.shape[1] // G
    assert plen % CMP_STRIDE == 0 and lreal < CMP_STRIDE and lreal <= lp
    nb = (plen + lreal) // CMP_STRIDE - 1
    nslc = -(-(plen + lreal) // SLC_LEN)
    nsel = min(N_SELECT, nslc)
    new_spec = lambda c0: pl.BlockSpec((1, lp, KVW), lambda b, pt: (b, 0, c0 // KVW))
    const = lambda shape: pl.BlockSpec(shape, lambda b, pt: (0,) * len(shape))
    hbm = pl.BlockSpec(memory_space=pl.ANY)
    kern = functools.partial(_nsa_decode_kernel, lp=lp, lreal=lreal, plen=plen, rows_pp=rows_pp,
                             n_pages=n_pages, wlen=wlen, nslc=nslc, nsel=nsel, nb=nb)
    grid_spec = pltpu.PrefetchScalarGridSpec(
        num_scalar_prefetch=1,
        grid=(B,),
        in_specs=[pl.BlockSpec((1, lp, NSA_HEADS * NSA_HD), lambda b, pt: (b, 0, NQ0 // (NSA_HEADS * NSA_HD))),
                  pl.BlockSpec((1, lp, LANE), lambda b, pt: (b, 0, 0)),
                  new_spec(KS0), new_spec(VS0), new_spec(KW0), new_spec(VW0),
                  pl.BlockSpec((1, wlen * G, NSA_HD), lambda b, pt: (b, 0, 0)),
                  pl.BlockSpec((1, wlen * G, NSA_HD), lambda b, pt: (b, 0, 0)),
                  const((2, CMP_STRIDE, NSA_HD, 2 * NSA_HD)),
                  const((2, NSA_HD, NSA_HD)),
                  const((LANE, plen + KCHUNK)),
                  const((LANE, LANE)),
                  hbm, hbm, hbm, hbm],
        out_specs=[pl.BlockSpec((1, lp, NSA_HEADS * NSA_HD), lambda b, pt: (b, 0, 0)),
                   pl.BlockSpec((1, wlen * G, NSA_HD), lambda b, pt: (b, 0, 0)),
                   pl.BlockSpec((1, wlen * G, NSA_HD), lambda b, pt: (b, 0, 0))],
        scratch_shapes=[pltpu.VMEM((2, 2, n_pages * rows_pp // CMP_SEG_ROWS * CMP_PITCH, NSA_HD), F32),
                        pltpu.VMEM((2, 2, n_pages * rows_pp, NSA_HD), F32),
                        pltpu.SemaphoreType.DMA((2, len(caches)))])
    return pl.pallas_call(
        kern,
        grid_spec=grid_spec,
        out_shape=[jax.ShapeDtypeStruct((B, lp, NSA_HEADS * NSA_HD), BF16),
                   jax.ShapeDtypeStruct(win_k.shape, F32), jax.ShapeDtypeStruct(win_v.shape, F32)],
        compiler_params=_cp(("arbitrary",)),
        name="nsa_decode")(page_table, proj3, small3, proj3, proj3, proj3, proj3, win_k, win_v,
                           w1cat, w2s, emat, mmat, caches[0], caches[1], caches[2], caches[3])


def _out_proj_kernel(og_ref, on_ref, x_ref, w0_ref, w1_ref, g_ref, h_ref, hn_ref):
    h = x_ref[...] + _dot(og_ref[...], w0_ref[...]) + _dot(on_ref[...], w1_ref[...])
    h_ref[...] = h
    hn_ref[...] = _rms(h, g_ref[...]).astype(BF16)


def out_proj(og, on, x2, w_out, g, tm):
    M, Dm = x2.shape
    Kh = og.shape[1]
    return pl.pallas_call(
        _out_proj_kernel,
        grid=(M // tm,),
        in_specs=[pl.BlockSpec((tm, Kh), lambda i: (i, 0)),
                  pl.BlockSpec((tm, Kh), lambda i: (i, 0)),
                  pl.BlockSpec((tm, Dm), lambda i: (i, 0)),
                  pl.BlockSpec((Kh, Dm), lambda i: (0, 0)),
                  pl.BlockSpec((Kh, Dm), lambda i: (1, 0)),
                  pl.BlockSpec((1, Dm), lambda i: (0, 0))],
        out_specs=[pl.BlockSpec((tm, Dm), lambda i: (i, 0)),
                   pl.BlockSpec((tm, Dm), lambda i: (i, 0))],
        out_shape=[jax.ShapeDtypeStruct((M, Dm), F32), jax.ShapeDtypeStruct((M, Dm), BF16)],
        compiler_params=_cp(("parallel",)),
        name="out_proj")(og, on, x2, w_out, w_out, g)


def _ffn_up_seq_kernel(hn_ref, wa_ref, wu_ref, cw_ref, cb_ref, pre_ref, hm_ref, tail_ref, aext_ref,
                       *, tm, tiles_per_seq):
    m = pl.program_id(1)

    @pl.when(m % tiles_per_seq == 0)
    def _():
        aext_ref[0:SUBLANE, :] = pre_ref[0]

    hn = hn_ref[...]
    a = _dot(hn, wa_ref[...])
    u = _dot(hn, wu_ref[...])
    aext_ref[SUBLANE:SUBLANE + tm, :] = a
    p1 = aext_ref[pl.ds(SUBLANE - 1, tm), :]
    p2 = aext_ref[pl.ds(SUBLANE - 2, tm), :]
    c = p2 * cw_ref[0:1, :] + p1 * cw_ref[1:2, :] + a * cw_ref[2:3, :] + cb_ref[...]
    hm_ref[...] = (jax.nn.gelu(c) * u).astype(hm_ref.dtype)
    tail = a[tm - SUBLANE:tm, :]
    tail_ref[0] = tail
    aext_ref[0:SUBLANE, :] = tail


def ffn_up_seq(hn, w_up, conv_w, conv_b, prefix, seq_len, tm, tf):
    M, Dm = hn.shape
    Fp = conv_w.shape[1]
    nf = Fp // tf
    B = M // seq_len
    tps = seq_len // tm
    kern = functools.partial(_ffn_up_seq_kernel, tm=tm, tiles_per_seq=tps)
    return pl.pallas_call(
        kern,
        grid=(nf, M // tm),
        in_specs=[pl.BlockSpec((tm, Dm), lambda f, m: (m, 0)),
                  pl.BlockSpec((Dm, tf), lambda f, m: (0, f)),
                  pl.BlockSpec((Dm, tf), lambda f, m: (0, f + nf)),
                  pl.BlockSpec((SUBLANE, tf), lambda f, m: (0, f)),
                  pl.BlockSpec((1, tf), lambda f, m: (0, f)),
                  pl.BlockSpec((1, SUBLANE, tf), lambda f, m: (m // tps, 0, f))],
        out_specs=[pl.BlockSpec((tm, tf), lambda f, m: (m, f)),
                   pl.BlockSpec((1, SUBLANE, tf), lambda f, m: (m // tps, 0, f))],
        out_shape=[jax.ShapeDtypeStruct((M, Fp), BF16), jax.ShapeDtypeStruct((B, SUBLANE, Fp), F32)],
        scratch_shapes=[pltpu.VMEM((SUBLANE + tm, tf), F32)],
        compiler_params=_cp(("parallel", "arbitrary")),
        name="ffn_up_seq")(hn, w_up, w_up, conv_w, conv_b, prefix)


def _ffn_up_tm_kernel(hn_ref, wa_ref, wu_ref, cw_ref, cb_ref, pre_ref, hm_ref, tail_ref, *, steps, nb):
    hn = hn_ref[...]
    a = _dot(hn, wa_ref[...])
    u = _dot(hn, wu_ref[...])
    slabs = [pre_ref[i] for i in range(CONV_W - 1)] + [a[t * nb:(t + 1) * nb] for t in range(steps)]
    for t in range(steps):
        c = (slabs[t] * cw_ref[0:1, :] + slabs[t + 1] * cw_ref[1:2, :] + slabs[t + 2] * cw_ref[2:3, :]
             + cb_ref[...])
        hm_ref[t * nb:(t + 1) * nb, :] = (jax.nn.gelu(c) * u[t * nb:(t + 1) * nb]).astype(hm_ref.dtype)
    for i in range(CONV_W - 1):
        tail_ref[i] = slabs[steps + i]


def ffn_up_tm(hn, w_up, conv_w, conv_b, prefix, steps, tf):
    M, Dm = hn.shape
    Fp = conv_w.shape[1]
    nf = Fp // tf
    nb = M // steps
    kern = functools.partial(_ffn_up_tm_kernel, steps=steps, nb=nb)
    return pl.pallas_call(
        kern,
        grid=(nf,),
        in_specs=[pl.BlockSpec((M, Dm), lambda f: (0, 0)),
                  pl.BlockSpec((Dm, tf), lambda f: (0, f)),
                  pl.BlockSpec((Dm, tf), lambda f: (0, f + nf)),
                  pl.BlockSpec((SUBLANE, tf), lambda f: (0, f)),
                  pl.BlockSpec((1, tf), lambda f: (0, f)),
                  pl.BlockSpec((CONV_W - 1, nb, tf), lambda f: (0, 0, f))],
        out_specs=[pl.BlockSpec((M, tf), lambda f: (0, f)),
                   pl.BlockSpec((CONV_W - 1, nb, tf), lambda f: (0, 0, f))],
        out_shape=[jax.ShapeDtypeStruct((M, Fp), BF16), jax.ShapeDtypeStruct((CONV_W - 1, nb, Fp), F32)],
        compiler_params=_cp(("parallel",)),
        name="ffn_up_tm")(hn, w_up, w_up, conv_w, conv_b, prefix)


def _ffn_down_kernel(hm_ref, w_ref, h_ref, o_ref):
    o_ref[...] = h_ref[...] + _dot(hm_ref[...], w_ref[...])


def ffn_down(hmid, w_down, h1, tm, tn):
    M, Fp = hmid.shape
    Dm = w_down.shape[1]
    return pl.pallas_call(
        _ffn_down_kernel,
        grid=(M // tm, Dm // tn),
        in_specs=[pl.BlockSpec((tm, Fp), lambda i, j: (i, 0)),
                  pl.BlockSpec((Fp, tn), lambda i, j: (0, j)),
                  pl.BlockSpec((tm, tn), lambda i, j: (i, j))],
        out_specs=pl.BlockSpec((tm, tn), lambda i, j: (i, j)),
        out_shape=jax.ShapeDtypeStruct((M, Dm), F32),
        compiler_params=_cp(("parallel", "arbitrary")),
        name="ffn_down")(hmid, w_down, h1)


def _ple_final_kernel(h_ref, p_ref, wg_ref, bg_ref, wp_ref, g_ref, o_ref):
    h = h_ref[...]
    gate = jax.nn.sigmoid(_dot(h.astype(BF16), wg_ref[...]) + bg_ref[...])
    h = h + gate * _dot(p_ref[...].astype(BF16), wp_ref[...])
    o_ref[...] = _rms(h, g_ref[...])


def ple_final(h2, p2, w_gate, b_gate, w_proj, g, tm):
    M, Dm = h2.shape
    Pd = p2.shape[1]
    return pl.pallas_call(
        _ple_final_kernel,
        grid=(M // tm,),
        in_specs=[pl.BlockSpec((tm, Dm), lambda i: (i, 0)),
                  pl.BlockSpec((tm, Pd), lambda i: (i, 0)),
                  pl.BlockSpec((Dm, Dm), lambda i: (0, 0)),
                  pl.BlockSpec((1, Dm), lambda i: (0, 0)),
                  pl.BlockSpec((Pd, Dm), lambda i: (0, 0)),
                  pl.BlockSpec((1, Dm), lambda i: (0, 0))],
        out_specs=pl.BlockSpec((tm, Dm), lambda i: (i, 0)),
        out_shape=jax.ShapeDtypeStruct((M, Dm), F32),
        compiler_params=_cp(("parallel",)),
        name="ple_final")(h2, p2, w_gate, b_gate, w_proj, g)


def _pick_tile(n, prefs):
    for t in prefs:
        if n % t == 0:
            return t
    raise ValueError(f"no tile for {n}")


def _block_matrices(total_keys, nb):
    e = np.zeros((LANE, total_keys), np.float32)
    t = np.arange(total_keys)
    e[t // SLC_LEN, t] = 1.0
    ratio = SLC_LEN // CMP_STRIDE
    m = np.zeros((LANE, LANE), np.float32)
    for n in range(nb):
        for s in range(LANE):
            m[n, s] = float(ratio * s <= n <= ratio * s + ratio - 1) + float(ratio * s - 1 <= n <= ratio * s + ratio - 2)
    return jnp.asarray(e, BF16), jnp.asarray(m, BF16)


def _prep_weights(i, attn_norm_g, w_in, gla_w_a2, gla_b_a, gla_norm_g, cmp_wk1, cmp_wk2, cmp_wv1, cmp_wv2,
                  w_out, ffn_norm_g, ffn_w_up, ffn_conv_w, ffn_conv_b, ffn_w_down, ple_w_proj, ple_w_gate,
                  ple_b_gate):
    Dm = w_in.shape[1]
    F = ffn_w_down.shape[1]
    Fp = -(-F // 512) * 512
    win = w_in[i]
    ga0 = NQ0
    ng0 = ga0 + GLA_RANK + (NMAIN - NQ0)
    ngw = 3 * NSA_HEADS
    w_main = jnp.concatenate([win[:, :ga0], win[:, ga0 + GLA_RANK:ng0]], axis=1).astype(BF16)
    w_small = jnp.concatenate([win[:, ga0:ga0 + GLA_RANK], win[:, ng0:ng0 + ngw],
                               jnp.zeros((Dm, LANE - GLA_RANK - ngw), F32)], axis=1).astype(BF16)
    wa_pad = jnp.concatenate([gla_w_a2[i], jnp.zeros((LANE - GLA_RANK, GLA_HEADS * GLA_DK), F32)],
                             axis=0).astype(BF16)
    cat = lambda w1: jnp.concatenate([w1[:CMP_STRIDE], w1[CMP_STRIDE:]], axis=-1)
    w1cat = jnp.stack([cat(cmp_wk1[i]), cat(cmp_wv1[i])]).astype(BF16)
    w2s = jnp.stack([cmp_wk2[i], cmp_wv2[i]])
    wup = ffn_w_up[i]
    zc = jnp.zeros((Dm, Fp - F), F32)
    w_up = jnp.concatenate([wup[:, :F], zc, wup[:, F:], zc], axis=1).astype(BF16)
    conv_w = jnp.pad(ffn_conv_w[i], ((0, SUBLANE - CONV_W), (0, Fp - F)))
    conv_b = jnp.pad(ffn_conv_b[i], (0, Fp - F)).reshape(1, Fp)
    w_down = jnp.pad(ffn_w_down[i], ((0, Fp - F), (0, 0))).astype(BF16)
    return dict(
        g_attn=attn_norm_g[i].reshape(1, Dm), w_main=w_main, w_small=w_small, wa_pad=wa_pad,
        b_a=gla_b_a[i].reshape(1, -1), gnorm=gla_norm_g[i].reshape(1, -1), w1cat=w1cat, w2s=w2s,
        w_out=w_out[i].astype(BF16), g_ffn=ffn_norm_g[i].reshape(1, Dm), w_up=w_up, conv_w=conv_w,
        conv_b=conv_b, w_down=w_down, w_proj=ple_w_proj[i].astype(BF16), w_gate=ple_w_gate[i].astype(BF16),
        b_gate=ple_b_gate[i].reshape(1, Dm), F=F, Fp=Fp)


def _kv_rows(proj3, c0, rows):
    B = proj3.shape[0]
    t = proj3[:, rows, c0:c0 + KVW]
    return t.reshape(B, t.shape[1], NSA_KV, NSA_HD)


def _prompt_layer(h2d, p2d, B, L, W):
    Dm = h2d.shape[1]
    M = B * L
    proj, small = norm_proj(h2d, W["g_attn"], W["w_main"], W["w_small"],
                            _pick_tile(M, (1024, 512, 256, 128)), 512)
    proj3 = proj.reshape(B, L, NMAIN)
    small3 = small.reshape(B, L, LANE)
    cb = _pick_tile(L, (128, 64, 32, 16))
    s0 = jnp.zeros((B, GLA_HEADS, GLA_DK, GLA_DV), F32)
    og, gla_state = gla(proj3, small3, W["wa_pad"], W["b_a"], W["gnorm"], s0, cb, GLA_SUB, cb)
    cmp = compress(proj3, W["w1cat"], W["w2s"])
    _, mmat = _block_matrices(L, L // CMP_STRIDE - 1)
    on = nsa_prompt(proj3, small3, cmp, mmat.T)
    h1, hn = out_proj(og.reshape(M, -1), on.reshape(M, -1), h2d, W["w_out"], W["g_ffn"],
                      _pick_tile(M, (256, 128)))
    prefix = jnp.zeros((B, SUBLANE, W["Fp"]), F32)
    hmid, tail = ffn_up_seq(hn, W["w_up"], W["conv_w"], W["conv_b"], prefix, L,
                            _pick_tile(L, (512, 256, 128)), 512)
    h2 = ffn_down(hmid, W["w_down"], h1, _pick_tile(M, (512, 256, 128)), 512)
    keep = min(WINDOW, L)
    rows = slice(0, L)
    outs = (_kv_rows(proj3, KC0, rows), _kv_rows(proj3, VC0, rows), _kv_rows(proj3, KS0, rows),
            _kv_rows(proj3, VS0, rows), _kv_rows(proj3, KW0, slice(L - keep, L)),
            _kv_rows(proj3, VW0, slice(L - keep, L)), gla_state,
            tail[:, SUBLANE - (CONV_W - 1):, :W["F"]])
    return h2, outs


def _decode_layer(h3d, p2d_tm, B, L, lp, caches, page_table, win_k, win_v, gla_s, conv_s, W):
    Dm = h3d.shape[2]
    Mp = B * lp
    x2 = h3d.reshape(Mp, Dm)
    proj, small = norm_proj(x2, W["g_attn"], W["w_main"], W["w_small"],
                            _pick_tile(Mp, (1024, 512, 256, 128, 64, 32, 16, 8)), 512)
    proj3 = proj.reshape(B, lp, NMAIN)
    small3 = small.reshape(B, lp, LANE)
    og, gla_state = gla(proj3, small3, W["wa_pad"], W["b_a"], W["gnorm"], gla_s, lp, lp, L)
    plen = page_table.shape[1] * caches[0].shape[1] // NSA_KV
    emat, mmat = _block_matrices(plen + KCHUNK, (plen + L) // CMP_STRIDE - 1)
    wlen = win_k.shape[1]
    on, wk_new, wv_new = nsa_decode(proj3, small3, caches, page_table,
                                    win_k.reshape(B, wlen * NSA_KV, NSA_HD),
                                    win_v.reshape(B, wlen * NSA_KV, NSA_HD),
                                    W["w1cat"], W["w2s"], emat, mmat, L)
    h1, hn = out_proj(og.reshape(Mp, -1), on.reshape(Mp, -1), x2, W["w_out"], W["g_ffn"],
                      _pick_tile(Mp, (256, 128, 64, 32, 16, 8)))
    to_tm = lambda t: t.reshape(B, lp, Dm)[:, :L].transpose(1, 0, 2).reshape(L * B, Dm)
    h1_tm, hn_tm = to_tm(h1), to_tm(hn)
    prefix = jnp.pad(conv_s.transpose(1, 0, 2), ((0, 0), (0, 0), (0, W["Fp"] - W["F"])))
    hmid, tail = ffn_up_tm(hn_tm, W["w_up"], W["conv_w"], W["conv_b"], prefix, L, 512)
    M = L * B
    h2 = ffn_down(hmid, W["w_down"], h1_tm, _pick_tile(M, (512, 256, 128, 64, 32, 16, 8)), 512)
    rows = slice(0, L)
    outs = (_kv_rows(proj3, KC0, rows), _kv_rows(proj3, VC0, rows), _kv_rows(proj3, KS0, rows),
            _kv_rows(proj3, VS0, rows), wk_new.reshape(win_k.shape), wv_new.reshape(win_v.shape),
            gla_state, tail[:, :, :W["F"]].transpose(1, 0, 2))
    return h2, outs


def kernel(x_prompt, x_sample, p_prompt, p_sample, cache_cmp_k, cache_cmp_v, cache_slc_k, cache_slc_v,
           page_table, state_win_k, state_win_v, state_gla, state_ffn_conv, attn_norm_g, w_in, gla_w_a2,
           gla_b_a, gla_norm_g, cmp_wk1, cmp_wk2, cmp_wv1, cmp_wv2, w_out, ffn_norm_g, ffn_w_up,
           ffn_conv_w, ffn_conv_b, ffn_w_down, ple_w_proj, ple_w_gate, ple_b_gate, final_norm_g):
    depth = w_in.shape[0]
    assert depth == 1, "layers are chained through HBM one at a time; only depth 1 is wired"
    Bp, Lp, Dm = x_prompt.shape
    Bs, Ls, _ = x_sample.shape
    W = _prep_weights(0, attn_norm_g, w_in, gla_w_a2, gla_b_a, gla_norm_g, cmp_wk1, cmp_wk2, cmp_wv1,
                      cmp_wv2, w_out, ffn_norm_g, ffn_w_up, ffn_conv_w, ffn_conv_b, ffn_w_down,
                      ple_w_proj, ple_w_gate, ple_b_gate)
    g_final = final_norm_g.reshape(1, Dm)

    Mp = Bp * Lp
    h2, outs_p = _prompt_layer(x_prompt.reshape(Mp, Dm), None, Bp, Lp, W)
    y_prompt = ple_final(h2, p_prompt[0].reshape(Mp, -1), W["w_gate"], W["b_gate"], W["w_proj"], g_final,
                         _pick_tile(Mp, (256, 128))).reshape(Bp, Lp, Dm)

    lp = -(-Ls // SUBLANE) * SUBLANE
    xs = jnp.pad(x_sample, ((0, 0), (0, lp - Ls), (0, 0)))
    n_pool, psz = cache_cmp_k.shape[1], cache_cmp_k.shape[2]
    caches = [c[0].reshape(n_pool, psz * NSA_KV, NSA_HD)
              for c in (cache_cmp_k, cache_cmp_v, cache_slc_k, cache_slc_v)]
    h2s, outs_s = _decode_layer(xs, None, Bs, Ls, lp, caches, page_table, state_win_k[0], state_win_v[0],
                                state_gla[0], state_ffn_conv[0], W)
    p_tm = p_sample[0].transpose(1, 0, 2).reshape(Ls * Bs, -1)
    Ms = Ls * Bs
    y_tm = ple_final(h2s, p_tm, W["w_gate"], W["b_gate"], W["w_proj"], g_final,
                     _pick_tile(Ms, (256, 128, 64, 32, 16, 8)))
    y_sample = y_tm.reshape(Ls, Bs, Dm).transpose(1, 0, 2)

    lead = lambda t: t[None]
    return (y_prompt, y_sample) + tuple(lead(t) for t in outs_p) + tuple(lead(t) for t in outs_s)
```

```python
import functools
import math

import numpy as np
import jax
import jax.numpy as jnp
from jax import lax
from jax.experimental import pallas as pl
from jax.experimental.pallas import tpu as pltpu

F32 = jnp.float32
BF16 = jnp.bfloat16

GLA_HEADS = 4
GLA_DK = 128
GLA_DV = 256
GLA_RANK = 16
GLA_TAU = 16.0
GLA_SUB = 16
NSA_HEADS = 8
NSA_KV = 2
NSA_REP = NSA_HEADS // NSA_KV
NSA_HD = 128
CMP_LEN = 32
CMP_STRIDE = 16
SLC_LEN = 64
N_SELECT = 16
WINDOW = 512
CONV_W = 3
EPS = 1e-6
NEG = -1e30
M_FLOOR = 0.1 * NEG
FORCED = 1e6

LANE = 128
SUBLANE = 8
KCHUNK = 128
CMP_SEG_ROWS = NSA_KV * CMP_STRIDE
CMP_PITCH = CMP_SEG_ROWS + SUBLANE
VMEM_LIMIT = 56 * 1024 * 1024

GQ0 = 0
GK0 = GQ0 + GLA_HEADS * GLA_DK
GV0 = GK0 + GLA_HEADS * GLA_DK
GR0 = GV0 + GLA_HEADS * GLA_DV
NQ0 = GR0 + GLA_HEADS * GLA_DV
KVW = NSA_KV * NSA_HD
KC0 = NQ0 + NSA_HEADS * NSA_HD
VC0 = KC0 + KVW
KS0 = VC0 + KVW
VS0 = KS0 + KVW
KW0 = VS0 + KVW
VW0 = KW0 + KVW
NMAIN = VW0 + KVW
GATE0 = GLA_RANK


def _cp(sem):
    return pltpu.CompilerParams(dimension_semantics=sem, vmem_limit_bytes=VMEM_LIMIT)


def _dot(a, b):
    return jnp.dot(a, b, preferred_element_type=F32)


def _dot_nt(a, b):
    return lax.dot_general(a, b, (((1,), (1,)), ((), ())), preferred_element_type=F32)


def _rms(x, g):
    return x * lax.rsqrt(jnp.mean(x * x, axis=-1, keepdims=True) + EPS) * g


def _norm_proj_kernel(x_ref, g_ref, wm_ref, ws_ref, om_ref, os_ref, kv_ref, xn_ref, *, j_kv0):
    j = pl.program_id(1)

    @pl.when(j == 0)
    def _():
        xn = _rms(x_ref[...], g_ref[...]).astype(BF16)
        xn_ref[...] = xn
        os_ref[...] = _dot(xn, ws_ref[...])

    tile = _dot(xn_ref[...], wm_ref[...])
    om_ref[...] = tile

    @pl.when(j >= j_kv0)
    def _():
        tm = tile.shape[0]
        for a in range(tile.shape[1] // KVW):
            for g in range(NSA_KV):
                c0 = a * KVW + g * NSA_HD
                kv_ref[a, pl.ds(g, tm, stride=NSA_KV), :] = tile[:, c0:c0 + NSA_HD]


def norm_proj(x2, g, w_main, w_small, tm, tn):
    M, Dm = x2.shape
    N = w_main.shape[1]
    assert KC0 % tn == 0 and tn % KVW == 0
    j_kv0 = KC0 // tn
    per_tile = tn // KVW
    n_kv = (N - KC0) // KVW
    return pl.pallas_call(
        functools.partial(_norm_proj_kernel, j_kv0=j_kv0),
        grid=(M // tm, N // tn),
        in_specs=[pl.BlockSpec((tm, Dm), lambda i, j: (i, 0)),
                  pl.BlockSpec((1, Dm), lambda i, j: (0, 0)),
                  pl.BlockSpec((Dm, tn), lambda i, j: (0, j)),
                  pl.BlockSpec((Dm, LANE), lambda i, j: (0, 0))],
        out_specs=[pl.BlockSpec((tm, tn), lambda i, j: (i, j)),
                   pl.BlockSpec((tm, LANE), lambda i, j: (i, 0)),
                   pl.BlockSpec((per_tile, NSA_KV * tm, NSA_HD),
                                lambda i, j: (jnp.maximum(j - j_kv0, 0), i, 0))],
        out_shape=[jax.ShapeDtypeStruct((M, N), F32), jax.ShapeDtypeStruct((M, LANE), F32),
                   jax.ShapeDtypeStruct((n_kv, NSA_KV * M, NSA_HD), F32)],
        scratch_shapes=[pltpu.VMEM((tm, Dm), BF16)],
        compiler_params=_cp(("parallel", "arbitrary")),
        name="norm_proj")(x2, g, w_main, w_small)


def _gla_kernel(q_ref, k_ref, v_ref, r_ref, sm_ref, wa_ref, ba_ref, gn_ref, s0_ref,
                o_ref, s_ref, st_ref, *, cb, sub, valid):
    c = pl.program_id(1)

    @pl.when(c == 0)
    def _():
        st_ref[...] = s0_ref[0]

    H = GLA_HEADS
    hk = lambda h: slice(h * GLA_DK, (h + 1) * GLA_DK)
    hv = lambda h: slice(h * GLA_DV, (h + 1) * GLA_DV)
    ri = lax.broadcasted_iota(jnp.int32, (cb, cb), 0)
    ci = lax.broadcasted_iota(jnp.int32, (cb, cb), 1)
    tri = jnp.where(ci <= ri, 1.0, 0.0)
    rowv = lax.broadcasted_iota(jnp.int32, (cb, 1), 0)
    q = q_ref[0] * (GLA_DK ** -0.5)
    k = k_ref[0]
    v = v_ref[0]
    pre = _dot(sm_ref[0].astype(BF16), wa_ref[...]) + ba_ref[...]
    log_a = jax.nn.log_sigmoid(pre) / GLA_TAU
    if valid < cb:
        log_a = jnp.where(rowv < valid, log_a, 0.0)
        k = jnp.where(rowv < valid, k, 0.0)
        v = jnp.where(rowv < valid, v, 0.0)
    hi = log_a.astype(BF16).astype(F32)
    lo = (log_a - hi).astype(BF16).astype(F32)
    b = _dot(tri, hi) + _dot(tri, lo)
    b_last = b[cb - 1:cb, :]
    S = [st_ref[h] for h in range(H)]
    vb = v.astype(BF16)

    qe = (q * jnp.exp(b)).astype(BF16)
    o = [_dot(qe[:, hk(h)], S[h].astype(BF16)) for h in range(H)]
    a_rows = [[] for _ in range(H)]
    cc = lax.broadcasted_iota(jnp.int32, (sub, cb), 1)
    for blk in range(cb // sub):
        lo_r, hi_r = blk * sub, (blk + 1) * sub
        r_dec = jnp.zeros((1, H * GLA_DK), F32) if blk == 0 else b[lo_r - 1:lo_r, :]
        q_i = q[lo_r:hi_r] * jnp.exp(b[lo_r:hi_r] - r_dec)
        k_i = k * jnp.exp(jnp.where(rowv < hi_r, r_dec - b, 0.0))
        rr = lax.broadcasted_iota(jnp.int32, (sub, cb), 0) + lo_r
        for h in range(H):
            a = _dot_nt(q_i[:, hk(h)], k_i[:, hk(h)])
            a_rows[h].append(jnp.where(cc <= rr, a, 0.0))
    for h in range(H):
        a = a_rows[h][0] if len(a_rows[h]) == 1 else jnp.concatenate(a_rows[h], axis=0)
        o[h] = o[h] + _dot(a, v[:, hv(h)])

    kh = k * jnp.exp(b_last - b)
    dec = jnp.exp(b_last)
    if cb < LANE:
        kh = jnp.concatenate([kh, jnp.zeros((LANE - cb, H * GLA_DK), F32)], axis=0)
        vpad = jnp.concatenate([vb, jnp.zeros((LANE - cb, H * GLA_DV), BF16)], axis=0)
    else:
        vpad = vb
    for h in range(H):
        dec_col = jnp.transpose(jnp.broadcast_to(dec[:, hk(h)], (GLA_DK, GLA_DK)))[:, 0:1]
        s_new = dec_col * S[h] + _dot(jnp.transpose(kh[:, hk(h)]).astype(BF16), vpad[:, hv(h)])
        st_ref[h] = s_new
        s_ref[0, h] = s_new

    r = r_ref[0]
    gate = r * jax.nn.sigmoid(r)
    for h in range(H):
        o_ref[0, :, hv(h)] = (_rms(o[h], gn_ref[...]) * gate[:, hv(h)]).astype(o_ref.dtype)


def gla(proj3, small3, wa_pad, b_a, gnorm, s0, cb, sub, valid):
    B, L, _ = proj3.shape
    H = GLA_HEADS
    qw, vw = H * GLA_DK, H * GLA_DV
    kern = functools.partial(_gla_kernel, cb=cb, sub=sub, valid=valid)
    return pl.pallas_call(
        kern,
        grid=(B, L // cb),
        in_specs=[pl.BlockSpec((1, cb, qw), lambda b, c: (b, c, GQ0 // qw)),
                  pl.BlockSpec((1, cb, qw), lambda b, c: (b, c, GK0 // qw)),
                  pl.BlockSpec((1, cb, vw), lambda b, c: (b, c, GV0 // vw)),
                  pl.BlockSpec((1, cb, vw), lambda b, c: (b, c, GR0 // vw)),
                  pl.BlockSpec((1, cb, LANE), lambda b, c: (b, c, 0)),
                  pl.BlockSpec((LANE, qw), lambda b, c: (0, 0)),
                  pl.BlockSpec((1, qw), lambda b, c: (0, 0)),
                  pl.BlockSpec((1, GLA_DV), lambda b, c: (0, 0)),
                  pl.BlockSpec((1, H, GLA_DK, GLA_DV), lambda b, c: (b, 0, 0, 0))],
        out_specs=[pl.BlockSpec((1, cb, vw), lambda b, c: (b, c, 0)),
                   pl.BlockSpec((1, H, GLA_DK, GLA_DV), lambda b, c: (b, 0, 0, 0))],
        out_shape=[jax.ShapeDtypeStruct((B, L, vw), BF16),
                   jax.ShapeDtypeStruct((B, H, GLA_DK, GLA_DV), F32)],
        scratch_shapes=[pltpu.VMEM((H, GLA_DK, GLA_DV), F32)],
        compiler_params=_cp(("parallel", "arbitrary")),
        name="gla")(proj3, proj3, proj3, proj3, small3, wa_pad, b_a, gnorm, s0)


def _compress_group(load_rows, w1_ref, w2, nseg):
    acc = jnp.zeros((nseg, 2 * NSA_HD), F32)
    for j in range(CMP_STRIDE):
        acc = acc + _dot(load_rows(j).astype(BF16), w1_ref[j])
    first = acc[:, :NSA_HD]
    second = acc[:, NSA_HD:]
    h = jax.nn.gelu(first + pltpu.roll(second, nseg - 1, axis=0))
    return _dot(h.astype(BF16), w2)


def _compress_groups(loads, w1_refs, w2s, nrow):
    n = len(loads)
    acc = [jnp.zeros((nrow, 2 * NSA_HD), F32) for _ in range(n)]
    for j in range(CMP_STRIDE):
        for i in range(n):
            acc[i] = acc[i] + _dot(loads[i](j).astype(BF16), w1_refs[i][j])
    hs = [jax.nn.gelu(a[:, :NSA_HD] + pltpu.roll(a[:, NSA_HD:], nrow - 1, axis=0)) for a in acc]
    return [_dot(h.astype(BF16), w2) for h, w2 in zip(hs, w2s)]


def _cmp_branch(q4, ck, cv, qpos4, nb):
    s = _dot_nt(q4, ck.astype(BF16))
    n = lax.broadcasted_iota(jnp.int32, s.shape, 1)
    mask = jnp.where(n * CMP_STRIDE + (CMP_LEN - 1) <= qpos4, n, nb) < nb
    s = jnp.where(mask, s, NEG)
    m = jnp.max(s, axis=-1, keepdims=True)
    e = jnp.where(mask, jnp.exp(s - m), 0.0)
    p = e / jnp.maximum(jnp.sum(e, axis=-1, keepdims=True), 1e-30)
    return p, _dot(p.astype(BF16), cv.astype(BF16))


def _select(pg, mmat, qpos_tok, nslc, nsel):
    hi = pg.astype(BF16)
    lo = (pg - hi.astype(F32)).astype(BF16)
    imp = _dot(hi, mmat) + _dot(lo, mmat)
    blk = lax.broadcasted_iota(jnp.int32, pg.shape, 1)
    cur = qpos_tok // SLC_LEN
    forced = jnp.where(blk == 0, 1, 0) + jnp.where(blk == cur, 1, 0) + jnp.where(blk == cur - 1, 1, 0)
    score = jnp.where(blk <= cur, jnp.where(forced > 0, FORCED, imp), -1.0)
    score = jnp.where(blk < nslc, score, -2.0)
    rank = jnp.zeros(pg.shape, F32)
    for j in range(nslc):
        sj = score[:, j:j + 1]
        tie = jnp.where(blk > j, sj, NEG)
        rank = rank + jnp.where(sj > score, 1.0, 0.0) + jnp.where(tie == score, 1.0, 0.0)
    return jnp.where(rank < nsel, jnp.where(blk < nslc, 1.0, 0.0), 0.0)


def _flash_init(m_ref, l_ref, acc_ref):
    m_ref[...] = jnp.full(m_ref.shape, NEG, F32)
    l_ref[...] = jnp.zeros(l_ref.shape, F32)
    acc_ref[...] = jnp.zeros(acc_ref.shape, F32)


def _flash_step(q4, k, v, mask, m_ref, l_ref, acc_ref):
    s = jnp.where(mask, _dot_nt(q4, k), NEG)
    m_old = m_ref[...]
    m_new = jnp.maximum(m_old, jnp.max(s, axis=-1, keepdims=True))
    alpha = jnp.exp(m_old - m_new)
    p = jnp.where(mask, jnp.exp(s - m_new), 0.0)
    l_ref[...] = alpha * l_ref[...] + jnp.sum(p, axis=-1, keepdims=True)
    acc_ref[...] = alpha * acc_ref[...] + _dot(p.astype(BF16), v)
    m_ref[...] = m_new


def _flash_out(l_ref, acc_ref):
    return acc_ref[...] / jnp.maximum(l_ref[...], 1e-30)


def _tile_rows(x, reps):
    return jnp.concatenate([x] * reps, axis=0)


def _attend(q4, keys, vals, masks):
    s = [jnp.where(mk, _dot_nt(q4, k), NEG) for k, mk in zip(keys, masks)]
    m = s[0].max(axis=-1, keepdims=True)
    for si in s[1:]:
        m = jnp.maximum(m, si.max(axis=-1, keepdims=True))
    e = [jnp.where(mk, jnp.exp(si - m), 0.0) for si, mk in zip(s, masks)]
    l = sum(ei.sum(axis=-1, keepdims=True) for ei in e)
    o = sum(_dot(ei.astype(BF16), v) for ei, v in zip(e, vals))
    return o / jnp.maximum(l, 1e-30)


def _compress_kernel(x_ref, w1_ref, w2_ref, o_ref, *, nseg):
    load = lambda j: x_ref[0, pl.ds(j, nseg, stride=CMP_STRIDE), :]
    o_ref[0, 0, 0] = _compress_group(load, w1_ref.at[0], w2_ref[0].astype(BF16), nseg)


def compress(proj3, w1cat, w2s):
    B, L, _ = proj3.shape
    nseg = L // CMP_STRIDE
    G = NSA_KV
    return pl.pallas_call(
        functools.partial(_compress_kernel, nseg=nseg),
        grid=(B, 2, G),
        in_specs=[pl.BlockSpec((1, L, NSA_HD), lambda b, w, g: (b, 0, KC0 // NSA_HD + w * G + g)),
                  pl.BlockSpec((1, CMP_STRIDE, NSA_HD, 2 * NSA_HD), lambda b, w, g: (w, 0, 0, 0)),
                  pl.BlockSpec((1, NSA_HD, NSA_HD), lambda b, w, g: (w, 0, 0))],
        out_specs=pl.BlockSpec((1, 1, 1, nseg, NSA_HD), lambda b, w, g: (w, b, g, 0, 0)),
        out_shape=jax.ShapeDtypeStruct((2, B, G, nseg, NSA_HD), F32),
        compiler_params=_cp(("parallel", "parallel", "parallel")),
        name="compress")(proj3, w1cat, w2s)


def _select_t(pg_t, mm_t, qpos_row, nslc, nsel):
    nsp = -(-nslc // SUBLANE) * SUBLANE
    hi = pg_t.astype(BF16)
    lo = (pg_t - hi.astype(F32)).astype(BF16)
    imp = (_dot(mm_t, hi) + _dot(mm_t, lo))[:nsp]
    blk = lax.broadcasted_iota(jnp.int32, imp.shape, 0)
    cur = qpos_row // SLC_LEN
    forced = jnp.where(blk == 0, 1, 0) + jnp.where(blk == cur, 1, 0) + jnp.where(blk == cur - 1, 1, 0)
    score = jnp.where(blk <= cur, jnp.where(forced > 0, FORCED, imp), -1.0)
    score = jnp.where(blk < nslc, score, -2.0)
    rank = jnp.zeros(imp.shape, F32)
    for j in range(nslc):
        sj = score[j:j + 1, :]
        tie = jnp.where(blk > j, sj, NEG)
        rank = rank + jnp.where(sj > score, 1.0, 0.0) + jnp.where(tie == score, 1.0, 0.0)
    return jnp.where(rank < nsel, jnp.where(blk < nslc, 1.0, 0.0), 0.0)


def _flash_step_t(q_t, k, v_t, bias, m_ref, l_ref, acc_ref):
    s = _dot(k, q_t) + bias
    m_old = m_ref[...]
    m_new = jnp.maximum(m_old, jnp.max(s, axis=0, keepdims=True))
    alpha = jnp.exp(m_old - m_new)
    p = jnp.exp(s - m_new)
    l_ref[...] = alpha * l_ref[...] + jnp.sum(p, axis=0, keepdims=True)
    acc_ref[...] = alpha * acc_ref[...] + _dot(v_t, p.astype(BF16))
    m_ref[...] = m_new


def _nsa_prompt_kernel(q_ref, sm_ref, ck_ref, cv_ref, ks_ref, vs_ref, kw_ref, vw_ref, mm_ref,
                       o_ref, vst_ref, vwt_ref, sel_ref, m_ref, l_ref, acc_ref, mw_ref, lw_ref, accw_ref,
                       *, tq, kstep, nslc, nsel, nb):
    g = pl.program_id(1)
    qb = pl.program_id(2)
    R = NSA_REP
    T = R * tq
    L = ks_ref.shape[1]
    scale = NSA_HD ** -0.5

    @pl.when(qb == 0)
    def _():
        for c in range(L // KCHUNK):
            cs = slice(c * KCHUNK, (c + 1) * KCHUNK)
            vst_ref[:, cs] = jnp.transpose(vs_ref[0, cs, :]).astype(BF16)
            vwt_ref[:, cs] = jnp.transpose(vw_ref[0, cs, :]).astype(BF16)

    q_t = (jnp.concatenate([jnp.transpose(q_ref[0, :, r * NSA_HD:(r + 1) * NSA_HD]) for r in range(R)],
                           axis=1) * scale).astype(BF16)
    qpos_row = qb * tq + lax.broadcasted_iota(jnp.int32, (1, tq), 1)
    qpos4 = jnp.concatenate([qpos_row] * R, axis=1)

    s = _dot(ck_ref[0, 0, 0].astype(BF16), q_t)
    n = lax.broadcasted_iota(jnp.int32, s.shape, 0)
    mask = jnp.where(n * CMP_STRIDE + (CMP_LEN - 1) <= qpos4, n, nb) < nb
    s = jnp.where(mask, s, NEG)
    e = jnp.where(mask, jnp.exp(s - jnp.max(s, axis=0, keepdims=True)), 0.0)
    p = e / jnp.maximum(jnp.sum(e, axis=0, keepdims=True), 1e-30)
    o_c = _dot(jnp.transpose(cv_ref[0, 0, 0]).astype(BF16), p.astype(BF16))
    pg = p[:, 0:tq]
    for r in range(1, R):
        pg = pg + p[:, r * tq:(r + 1) * tq]
    sel_ref[...] = (_select_t(pg, mm_ref[...], qpos_row, nslc, nsel) - 1.0) * (-NEG)

    key = lax.broadcasted_iota(jnp.int32, (kstep, T), 0)
    hi_step = (qb * tq + tq - 1) // kstep + 1
    lo_step = jnp.maximum(qb * tq - (WINDOW - 1), 0) // kstep

    sel_state = (m_ref, l_ref, acc_ref)
    win_state = (mw_ref, lw_ref, accw_ref)
    for m_r, l_r, acc_r in (sel_state, win_state):
        m_r[...] = jnp.full(m_r.shape, M_FLOOR, F32)
        l_r[...] = jnp.zeros(l_r.shape, F32)
        acc_r[...] = jnp.zeros(acc_r.shape, F32)

    def sel_step(c, causal):
        off = pl.multiple_of(c * kstep, kstep)
        k = ks_ref[0, pl.ds(off, kstep), :].astype(BF16)
        v_t = vst_ref[:, pl.ds(off, kstep)]
        blocks = [jnp.broadcast_to(sel_ref[pl.ds((kstep // SLC_LEN) * c + i, 1), :], (SLC_LEN, tq))
                  for i in range(kstep // SLC_LEN)]
        bias = jnp.concatenate(blocks, axis=0)
        bias = jnp.concatenate([bias] * R, axis=1)
        if causal:
            bias = jnp.where(off + key <= qpos4, bias, NEG)
        _flash_step_t(q_t, k, v_t, bias, *sel_state)

    def win_step(c, causal):
        off = pl.multiple_of(c * kstep, kstep)
        k = kw_ref[0, pl.ds(off, kstep), :].astype(BF16)
        v_t = vwt_ref[:, pl.ds(off, kstep)]
        kpos = off + key
        if causal:
            bias = jnp.where(kpos <= qpos4, 0.0, NEG)
        else:
            bias = jnp.where(kpos > qpos4 - WINDOW, 0.0, NEG)
        _flash_step_t(q_t, k, v_t, bias, *win_state)

    def body_sel(c, carry):
        sel_step(c, False)
        return carry

    def body_both(c, carry):
        sel_step(c, False)
        win_step(c, False)
        return carry

    last = hi_step - 1
    lax.fori_loop(0, jnp.minimum(lo_step, last), body_sel, 0)
    lax.fori_loop(lo_step, last, body_both, 0)
    sel_step(last, True)
    win_step(last, True)
    o_s = acc_ref[...] / jnp.maximum(l_ref[...], 1e-30)
    o_w = accw_ref[...] / jnp.maximum(lw_ref[...], 1e-30)

    g_t = jnp.transpose(jax.nn.sigmoid(sm_ref[0]))
    for r in range(R):
        def gate(br, r=r):
            c0 = GATE0 + br * NSA_HEADS + r
            return jnp.where(g == 0, g_t[c0:c0 + 1, :], g_t[c0 + R:c0 + R + 1, :])
        cs = slice(r * tq, (r + 1) * tq)
        o = gate(0) * o_c[:, cs] + gate(1) * o_s[:, cs] + gate(2) * o_w[:, cs]
        o_ref[0, :, r * NSA_HD:(r + 1) * NSA_HD] = jnp.transpose(o).astype(o_ref.dtype)


def nsa_prompt(proj3, small3, cmp, mmat_t):
    B, L, _ = proj3.shape
    tq = KCHUNK
    nseg = L // CMP_STRIDE
    nb = nseg - 1
    nslc = -(-L // SLC_LEN)
    nsel = min(N_SELECT, nslc)
    nsp = -(-nslc // SUBLANE) * SUBLANE
    G, R = NSA_KV, NSA_REP
    kstep = _pick_tile(L, (2 * KCHUNK, KCHUNK))
    assert nseg == LANE and L % KCHUNK == 0 and kstep % SLC_LEN == 0
    kv_spec = lambda c0: pl.BlockSpec((1, L, NSA_HD), lambda b, g, qb: (b, 0, c0 // NSA_HD + g))
    kern = functools.partial(_nsa_prompt_kernel, tq=tq, kstep=kstep, nslc=nslc, nsel=nsel, nb=nb)
    return pl.pallas_call(
        kern,
        grid=(B, G, L // tq),
        in_specs=[pl.BlockSpec((1, tq, R * NSA_HD), lambda b, g, qb: (b, qb, NQ0 // (R * NSA_HD) + g)),
                  pl.BlockSpec((1, tq, LANE), lambda b, g, qb: (b, qb, 0)),
                  pl.BlockSpec((1, 1, 1, nseg, NSA_HD), lambda b, g, qb: (0, b, g, 0, 0)),
                  pl.BlockSpec((1, 1, 1, nseg, NSA_HD), lambda b, g, qb: (1, b, g, 0, 0)),
                  kv_spec(KS0), kv_spec(VS0), kv_spec(KW0), kv_spec(VW0),
                  pl.BlockSpec((LANE, LANE), lambda b, g, qb: (0, 0))],
        out_specs=pl.BlockSpec((1, tq, R * NSA_HD), lambda b, g, qb: (b, qb, g)),
        out_shape=jax.ShapeDtypeStruct((B, L, NSA_HEADS * NSA_HD), BF16),
        scratch_shapes=[pltpu.VMEM((NSA_HD, L), BF16),
                        pltpu.VMEM((NSA_HD, L), BF16),
                        pltpu.VMEM((nsp, tq), F32)]
                       + 2 * [pltpu.VMEM((1, R * tq), F32),
                              pltpu.VMEM((1, R * tq), F32),
                              pltpu.VMEM((NSA_HD, R * tq), F32)],
        compiler_params=_cp(("parallel", "parallel", "arbitrary")),
        name="nsa_prompt")(proj3, small3, cmp, cmp, proj3, proj3, proj3, proj3, mmat_t)


def _nsa_decode_kernel(pt_ref, q_ref, sm_ref, ksn_ref, vsn_ref, kwn_ref, vwn_ref, wk_ref, wv_ref,
                       w1_ref, w2_ref, e_ref, mm_ref, c0_hbm, c1_hbm, c2_hbm, c3_hbm,
                       o_ref, wko_ref, wvo_ref, cbuf_ref, sbuf_ref, sem_ref,
                       *, lp, lreal, plen, rows_pp, n_pages, wlen, nslc, nsel, nb):
    b = pl.program_id(0)
    slot = b % 2
    caches = (c0_hbm, c1_hbm, c2_hbm, c3_hbm)
    G = NSA_KV
    R = NSA_REP
    T = R * lp
    segs_pp = rows_pp // CMP_SEG_ROWS

    def copies(bb, sl):
        out = []
        for p in range(n_pages):
            page = pt_ref[bb, p]
            for w in range(2):
                for s in range(segs_pp):
                    out.append(pltpu.make_async_copy(
                        caches[w].at[page, pl.ds(s * CMP_SEG_ROWS, CMP_SEG_ROWS)],
                        cbuf_ref.at[sl, w, pl.ds((p * segs_pp + s) * CMP_PITCH, CMP_SEG_ROWS)],
                        sem_ref.at[sl, w]))
                out.append(pltpu.make_async_copy(
                    caches[2 + w].at[page], sbuf_ref.at[sl, w, pl.ds(p * rows_pp, rows_pp)],
                    sem_ref.at[sl, 2 + w]))
        return out

    @pl.when(b == 0)
    def _():
        for cp in copies(0, 0):
            cp.start()

    @pl.when(b + 1 < pl.num_programs(0))
    def _():
        for cp in copies(b + 1, 1 - slot):
            cp.start()

    for w in range(len(caches)):
        whole = sbuf_ref.at[slot, 0]
        pltpu.make_async_copy(whole, whole, sem_ref.at[slot, w]).wait()

    scale = NSA_HD ** -0.5
    nseg = plen // CMP_STRIDE
    tok = lax.broadcasted_iota(jnp.int32, (lp, 1), 0)
    qpos_tok = plen + tok
    qpos4 = _tile_rows(qpos_tok, R)
    lane_n = lax.broadcasted_iota(jnp.int32, (T, KCHUNK), 1)
    new_pos = jnp.where(lane_n < lreal, plen + lane_n, qpos4 + 1)
    past_pos = lax.broadcasted_iota(jnp.int32, (T, plen), 1)
    win_pos = plen - wlen + lax.broadcasted_iota(jnp.int32, (T, wlen), 1)
    win_pos = jnp.where(win_pos > qpos4 - WINDOW, win_pos, qpos4 + 1)
    win_pos = jnp.where(win_pos >= 0, win_pos, qpos4 + 1)
    new_win_pos = jnp.where(new_pos > qpos4 - WINDOW, new_pos, qpos4 + 1)
    gts = jax.nn.sigmoid(sm_ref[0])
    zpad_f = jnp.zeros((KCHUNK - lp, NSA_HD), F32)
    pad_new = lambda ref, ls: jnp.concatenate([ref[0, :, ls], zpad_f], axis=0).astype(BF16)
    gls = lambda g: slice(g * NSA_HD, (g + 1) * NSA_HD)
    q4 = [(jnp.concatenate([q_ref[0, :, (g * R + r) * NSA_HD:(g * R + r + 1) * NSA_HD]
                            for r in range(R)], axis=0) * scale).astype(BF16) for g in range(G)]

    def seg_rows(w, j):
        return jnp.concatenate([cbuf_ref[slot, w, pl.ds(G * j + g, nseg, stride=CMP_PITCH), :]
                                for g in range(G)], axis=0)
    cmp = _compress_groups([functools.partial(seg_rows, w) for w in range(2)],
                           [w1_ref.at[w] for w in range(2)],
                           [w2_ref[w].astype(BF16) for w in range(2)], G * nseg)
    grow = lambda g: slice(g * nseg, (g + 1) * nseg)
    branch = [_cmp_branch(q4[g], cmp[0][grow(g)], cmp[1][grow(g)], qpos4, nb) for g in range(G)]
    pgs = []
    for g in range(G):
        p = branch[g][0]
        pg = p[0:lp]
        for r in range(1, R):
            pg = pg + p[r * lp:(r + 1) * lp]
        pgs.append(pg)
    sel = _select(jnp.concatenate(pgs, axis=0), mm_ref[...], _tile_rows(qpos_tok, G), nslc, nsel)
    selm_all = _dot(sel.astype(BF16), e_ref[...])

    o_s, o_w = [], []
    for g in range(G):
        selm = _tile_rows(selm_all[g * lp:(g + 1) * lp], R)
        k_past = sbuf_ref[slot, 0, pl.ds(g, plen, stride=G), :].astype(BF16)
        v_past = sbuf_ref[slot, 1, pl.ds(g, plen, stride=G), :].astype(BF16)
        m_past = jnp.where(selm[:, :plen] > 0.5, past_pos, qpos4 + 1) <= qpos4
        m_new = jnp.where(selm[:, plen:] > 0.5, new_pos, qpos4 + 1) <= qpos4
        o_s.append(_attend(q4[g], [k_past, pad_new(ksn_ref, gls(g))], [v_past, pad_new(vsn_ref, gls(g))],
                           [m_past, m_new]))
    for g in range(G):
        k_win = wk_ref[0, pl.ds(g, wlen, stride=G), :].astype(BF16)
        v_win = wv_ref[0, pl.ds(g, wlen, stride=G), :].astype(BF16)
        o_w.append(_attend(q4[g], [k_win, pad_new(kwn_ref, gls(g))], [v_win, pad_new(vwn_ref, gls(g))],
                           [win_pos <= qpos4, new_win_pos <= qpos4]))

    for g in range(G):
        for r in range(R):
            c0 = GATE0 + g * R + r
            rs = slice(r * lp, (r + 1) * lp)
            o = (gts[:, c0:c0 + 1] * branch[g][1][rs]
                 + gts[:, c0 + NSA_HEADS:c0 + NSA_HEADS + 1] * o_s[g][rs]
                 + gts[:, c0 + 2 * NSA_HEADS:c0 + 2 * NSA_HEADS + 1] * o_w[g][rs])
            hq = (g * R + r) * NSA_HD
            o_ref[0, :, hq:hq + NSA_HD] = o.astype(o_ref.dtype)

    wrows = wlen * G
    shift = lreal * G
    for src, new, dst in ((wk_ref, kwn_ref, wko_ref), (wv_ref, vwn_ref, wvo_ref)):
        dst[0, 0:wrows - shift, :] = src[0, shift:wrows, :]
        for t in range(lreal):
            for g in range(G):
                row = wrows - shift + t * G + g
                dst[0, row:row + 1, :] = new[0, t:t + 1, g * NSA_HD:(g + 1) * NSA_HD]


def nsa_decode(proj3, small3, caches, page_table, win_k, win_v, w1cat, w2s, emat, mmat, lreal):
    B, lp, _ = proj3.shape
    G = NSA_KV
    n_pages = page_table.shape[1]
    rows_pp = caches[0].shape[1]
    plen = n_pages * rows_pp // G
    wlen = win_k.shape[1] // G
    assert plen % CMP_STRIDE == 0 and lreal < CMP_STRIDE and lreal <= lp
    nb = (plen + lreal) // CMP_STRIDE - 1
    nslc = -(-(plen + lreal) // SLC_LEN)
    nsel = min(N_SELECT, nslc)
    new_spec = lambda c0: pl.BlockSpec((1, lp, KVW), lambda b, pt: (b, 0, c0 // KVW))
    const = lambda shape: pl.BlockSpec(shape, lambda b, pt: (0,) * len(shape))
    hbm = pl.BlockSpec(memory_space=pl.ANY)
    kern = functools.partial(_nsa_decode_kernel, lp=lp, lreal=lreal, plen=plen, rows_pp=rows_pp,
                             n_pages=n_pages, wlen=wlen, nslc=nslc, nsel=nsel, nb=nb)
    grid_spec = pltpu.PrefetchScalarGridSpec(
        num_scalar_prefetch=1,
        grid=(B,),
        in_specs=[pl.BlockSpec((1, lp, NSA_HEADS * NSA_HD), lambda b, pt: (b, 0, NQ0 // (NSA_HEADS * NSA_HD))),
                  pl.BlockSpec((1, lp, LANE), lambda b, pt: (b, 0, 0)),
                  new_spec(KS0), new_spec(VS0), new_spec(KW0), new_spec(VW0),
                  pl.BlockSpec((1, wlen * G, NSA_HD), lambda b, pt: (b, 0, 0)),
                  pl.BlockSpec((1, wlen * G, NSA_HD), lambda b, pt: (b, 0, 0)),
                  const((2, CMP_STRIDE, NSA_HD, 2 * NSA_HD)),
                  const((2, NSA_HD, NSA_HD)),
                  const((LANE, plen + KCHUNK)),
                  const((LANE, LANE)),
                  hbm, hbm, hbm, hbm],
        out_specs=[pl.BlockSpec((1, lp, NSA_HEADS * NSA_HD), lambda b, pt: (b, 0, 0)),
                   pl.BlockSpec((1, wlen * G, NSA_HD), lambda b, pt: (b, 0, 0)),
                   pl.BlockSpec((1, wlen * G, NSA_HD), lambda b, pt: (b, 0, 0))],
        scratch_shapes=[pltpu.VMEM((2, 2, n_pages * rows_pp // CMP_SEG_ROWS * CMP_PITCH, NSA_HD), F32),
                        pltpu.VMEM((2, 2, n_pages * rows_pp, NSA_HD), F32),
                        pltpu.SemaphoreType.DMA((2, len(caches)))])
    return pl.pallas_call(
        kern,
        grid_spec=grid_spec,
        out_shape=[jax.ShapeDtypeStruct((B, lp, NSA_HEADS * NSA_HD), BF16),
                   jax.ShapeDtypeStruct(win_k.shape, F32), jax.ShapeDtypeStruct(win_v.shape, F32)],
        compiler_params=_cp(("arbitrary",)),
        name="nsa_decode")(page_table, proj3, small3, proj3, proj3, proj3, proj3, win_k, win_v,
                           w1cat, w2s, emat, mmat, caches[0], caches[1], caches[2], caches[3])


def _out_proj_kernel(og_ref, on_ref, x_ref, w0_ref, w1_ref, g_ref, h_ref, hn_ref):
    h = x_ref[...] + _dot(og_ref[...], w0_ref[...]) + _dot(on_ref[...], w1_ref[...])
    h_ref[...] = h
    hn_ref[...] = _rms(h, g_ref[...]).astype(BF16)


def out_proj(og, on, x2, w_out, g, tm):
    M, Dm = x2.shape
    Kh = og.shape[1]
    return pl.pallas_call(
        _out_proj_kernel,
        grid=(M // tm,),
        in_specs=[pl.BlockSpec((tm, Kh), lambda i: (i, 0)),
                  pl.BlockSpec((tm, Kh), lambda i: (i, 0)),
                  pl.BlockSpec((tm, Dm), lambda i: (i, 0)),
                  pl.BlockSpec((Kh, Dm), lambda i: (0, 0)),
                  pl.BlockSpec((Kh, Dm), lambda i: (1, 0)),
                  pl.BlockSpec((1, Dm), lambda i: (0, 0))],
        out_specs=[pl.BlockSpec((tm, Dm), lambda i: (i, 0)),
                   pl.BlockSpec((tm, Dm), lambda i: (i, 0))],
        out_shape=[jax.ShapeDtypeStruct((M, Dm), F32), jax.ShapeDtypeStruct((M, Dm), BF16)],
        compiler_params=_cp(("parallel",)),
        name="out_proj")(og, on, x2, w_out, w_out, g)


def _ffn_up_seq_kernel(hn_ref, wa_ref, wu_ref, cw_ref, cb_ref, pre_ref, hm_ref, tail_ref, aext_ref,
                       *, tm, tiles_per_seq):
    m = pl.program_id(1)

    @pl.when(m % tiles_per_seq == 0)
    def _():
        aext_ref[0:SUBLANE, :] = pre_ref[0]

    hn = hn_ref[...]
    a = _dot(hn, wa_ref[...])
    u = _dot(hn, wu_ref[...])
    aext_ref[SUBLANE:SUBLANE + tm, :] = a
    p1 = aext_ref[pl.ds(SUBLANE - 1, tm), :]
    p2 = aext_ref[pl.ds(SUBLANE - 2, tm), :]
    c = p2 * cw_ref[0:1, :] + p1 * cw_ref[1:2, :] + a * cw_ref[2:3, :] + cb_ref[...]
    hm_ref[...] = (jax.nn.gelu(c) * u).astype(hm_ref.dtype)
    tail = a[tm - SUBLANE:tm, :]
    tail_ref[0] = tail
    aext_ref[0:SUBLANE, :] = tail


def ffn_up_seq(hn, w_a, w_u, conv_w, conv_b, prefix, seq_len, tm, tf):
    M, Dm = hn.shape
    Fp = conv_w.shape[1]
    nf = Fp // tf
    B = M // seq_len
    tps = seq_len // tm
    kern = functools.partial(_ffn_up_seq_kernel, tm=tm, tiles_per_seq=tps)
    return pl.pallas_call(
        kern,
        grid=(nf, M // tm),
        in_specs=[pl.BlockSpec((tm, Dm), lambda f, m: (m, 0)),
                  pl.BlockSpec((Dm, tf), lambda f, m: (0, f)),
                  pl.BlockSpec((Dm, tf), lambda f, m: (0, f)),
                  pl.BlockSpec((SUBLANE, tf), lambda f, m: (0, f)),
                  pl.BlockSpec((1, tf), lambda f, m: (0, f)),
                  pl.BlockSpec((1, SUBLANE, tf), lambda f, m: (m // tps, 0, f))],
        out_specs=[pl.BlockSpec((tm, tf), lambda f, m: (m, f)),
                   pl.BlockSpec((1, SUBLANE, tf), lambda f, m: (m // tps, 0, f))],
        out_shape=[jax.ShapeDtypeStruct((M, Fp), BF16), jax.ShapeDtypeStruct((B, SUBLANE, Fp), F32)],
        scratch_shapes=[pltpu.VMEM((SUBLANE + tm, tf), F32)],
        compiler_params=_cp(("parallel", "arbitrary")),
        name="ffn_up_seq")(hn, w_a, w_u, conv_w, conv_b, prefix)


def _ffn_up_tm_kernel(hn_ref, wa_ref, wu_ref, cw_ref, cb_ref, pre_ref, hm_ref, tail_ref, *, steps, nb):
    hn = hn_ref[...]
    a = _dot(hn, wa_ref[...])
    u = _dot(hn, wu_ref[...])
    slabs = [pre_ref[i] for i in range(CONV_W - 1)] + [a[t * nb:(t + 1) * nb] for t in range(steps)]
    for t in range(steps):
        c = (slabs[t] * cw_ref[0:1, :] + slabs[t + 1] * cw_ref[1:2, :] + slabs[t + 2] * cw_ref[2:3, :]
             + cb_ref[...])
        hm_ref[t * nb:(t + 1) * nb, :] = (jax.nn.gelu(c) * u[t * nb:(t + 1) * nb]).astype(hm_ref.dtype)
    for i in range(CONV_W - 1):
        tail_ref[i] = slabs[steps + i]


def ffn_up_tm(hn, w_a, w_u, conv_w, conv_b, prefix, steps, tf):
    M, Dm = hn.shape
    Fp = conv_w.shape[1]
    nf = Fp // tf
    nb = M // steps
    kern = functools.partial(_ffn_up_tm_kernel, steps=steps, nb=nb)
    return pl.pallas_call(
        kern,
        grid=(nf,),
        in_specs=[pl.BlockSpec((M, Dm), lambda f: (0, 0)),
                  pl.BlockSpec((Dm, tf), lambda f: (0, f)),
                  pl.BlockSpec((Dm, tf), lambda f: (0, f)),
                  pl.BlockSpec((SUBLANE, tf), lambda f: (0, f)),
                  pl.BlockSpec((1, tf), lambda f: (0, f)),
                  pl.BlockSpec((CONV_W - 1, nb, tf), lambda f: (0, 0, f))],
        out_specs=[pl.BlockSpec((M, tf), lambda f: (0, f)),
                   pl.BlockSpec((CONV_W - 1, nb, tf), lambda f: (0, 0, f))],
        out_shape=[jax.ShapeDtypeStruct((M, Fp), BF16), jax.ShapeDtypeStruct((CONV_W - 1, nb, Fp), F32)],
        compiler_params=_cp(("parallel",)),
        name="ffn_up_tm")(hn, w_a, w_u, conv_w, conv_b, prefix)


def _ffn_down_kernel(hm_ref, w_ref, h_ref, o_ref):
    o_ref[...] = h_ref[...] + _dot(hm_ref[...], w_ref[...])


def ffn_down(hmid, w_down, h1, tm, tn):
    M = hmid.shape[0]
    F, Dm = w_down.shape
    assert F % LANE == 0 and F <= hmid.shape[1]
    return pl.pallas_call(
        _ffn_down_kernel,
        grid=(M // tm, Dm // tn),
        in_specs=[pl.BlockSpec((tm, F), lambda i, j: (i, 0)),
                  pl.BlockSpec((F, tn), lambda i, j: (0, j)),
                  pl.BlockSpec((tm, tn), lambda i, j: (i, j))],
        out_specs=pl.BlockSpec((tm, tn), lambda i, j: (i, j)),
        out_shape=jax.ShapeDtypeStruct((M, Dm), F32),
        compiler_params=_cp(("parallel", "arbitrary")),
        name="ffn_down")(hmid, w_down, h1)


def _ple_final_kernel(h_ref, p_ref, wg_ref, bg_ref, wp_ref, g_ref, o_ref):
    h = h_ref[...]
    gate = jax.nn.sigmoid(_dot(h.astype(BF16), wg_ref[...]) + bg_ref[...])
    h = h + gate * _dot(p_ref[...].astype(BF16), wp_ref[...])
    o_ref[...] = _rms(h, g_ref[...])


def ple_final(h2, p2, w_gate, b_gate, w_proj, g, tm):
    M, Dm = h2.shape
    Pd = p2.shape[1]
    return pl.pallas_call(
        _ple_final_kernel,
        grid=(M // tm,),
        in_specs=[pl.BlockSpec((tm, Dm), lambda i: (i, 0)),
                  pl.BlockSpec((tm, Pd), lambda i: (i, 0)),
                  pl.BlockSpec((Dm, Dm), lambda i: (0, 0)),
                  pl.BlockSpec((1, Dm), lambda i: (0, 0)),
                  pl.BlockSpec((Pd, Dm), lambda i: (0, 0)),
                  pl.BlockSpec((1, Dm), lambda i: (0, 0))],
        out_specs=pl.BlockSpec((tm, Dm), lambda i: (i, 0)),
        out_shape=jax.ShapeDtypeStruct((M, Dm), F32),
        compiler_params=_cp(("parallel",)),
        name="ple_final")(h2, p2, w_gate, b_gate, w_proj, g)


def _pick_tile(n, prefs):
    for t in prefs:
        if n % t == 0:
            return t
    raise ValueError(f"no tile for {n}")


def _block_matrices(total_keys, nb):
    e = np.zeros((LANE, total_keys), np.float32)
    t = np.arange(total_keys)
    e[t // SLC_LEN, t] = 1.0
    ratio = SLC_LEN // CMP_STRIDE
    m = np.zeros((LANE, LANE), np.float32)
    for n in range(nb):
        for s in range(LANE):
            m[n, s] = float(ratio * s <= n <= ratio * s + ratio - 1) + float(ratio * s - 1 <= n <= ratio * s + ratio - 2)
    return jnp.asarray(e, BF16), jnp.asarray(m, BF16)


def _prep_weights(i, attn_norm_g, w_in, gla_w_a2, gla_b_a, gla_norm_g, cmp_wk1, cmp_wk2, cmp_wv1, cmp_wv2,
                  w_out, ffn_norm_g, ffn_w_up, ffn_conv_w, ffn_conv_b, ffn_w_down, ple_w_proj, ple_w_gate,
                  ple_b_gate):
    Dm = w_in.shape[1]
    F = ffn_w_down.shape[1]
    Fp = -(-F // 512) * 512
    win = w_in[i]
    ga0 = NQ0
    ng0 = ga0 + GLA_RANK + (NMAIN - NQ0)
    ngw = 3 * NSA_HEADS
    before_ga = lax.broadcasted_iota(jnp.int32, (1, NMAIN), 1) < ga0
    w_main = jnp.where(before_ga, win[:, :NMAIN], win[:, GLA_RANK:GLA_RANK + NMAIN]).astype(BF16)
    w_small = jnp.concatenate([win[:, ga0:ga0 + GLA_RANK], win[:, ng0:ng0 + ngw],
                               jnp.zeros((Dm, LANE - GLA_RANK - ngw), F32)], axis=1).astype(BF16)
    wa_pad = jnp.concatenate([gla_w_a2[i], jnp.zeros((LANE - GLA_RANK, GLA_HEADS * GLA_DK), F32)],
                             axis=0).astype(BF16)
    cat = lambda w1: jnp.concatenate([w1[:CMP_STRIDE], w1[CMP_STRIDE:]], axis=-1)
    w1cat = jnp.stack([cat(cmp_wk1[i]), cat(cmp_wv1[i])]).astype(BF16)
    w2s = jnp.stack([cmp_wk2[i], cmp_wv2[i]])
    wup = ffn_w_up[i]
    w_up_a = jnp.pad(wup[:, :F], ((0, 0), (0, Fp - F))).astype(BF16)
    w_up_u = jnp.pad(wup[:, F:], ((0, 0), (0, Fp - F))).astype(BF16)
    conv_w = jnp.pad(ffn_conv_w[i], ((0, SUBLANE - CONV_W), (0, Fp - F)))
    conv_b = jnp.pad(ffn_conv_b[i], (0, Fp - F)).reshape(1, Fp)
    w_down = ffn_w_down[i].astype(BF16)
    return dict(
        g_attn=attn_norm_g[i].reshape(1, Dm), w_main=w_main, w_small=w_small, wa_pad=wa_pad,
        b_a=gla_b_a[i].reshape(1, -1), gnorm=gla_norm_g[i].reshape(1, -1), w1cat=w1cat, w2s=w2s,
        w_out=w_out[i].astype(BF16), g_ffn=ffn_norm_g[i].reshape(1, Dm), w_up_a=w_up_a, w_up_u=w_up_u,
        conv_w=conv_w,
        conv_b=conv_b, w_down=w_down, w_proj=ple_w_proj[i].astype(BF16), w_gate=ple_w_gate[i].astype(BF16),
        b_gate=ple_b_gate[i].reshape(1, Dm), F=F, Fp=Fp)


def _kv_rows(kv, B, rows):
    t = kv.reshape(kv.shape[0], B, -1, NSA_KV, NSA_HD)[:, :, rows]
    return [t[i] for i in range(kv.shape[0])]


def _prompt_layer(h2d, p2d, B, L, W):
    Dm = h2d.shape[1]
    M = B * L
    proj, small, kv = norm_proj(h2d, W["g_attn"], W["w_main"], W["w_small"],
                                _pick_tile(M, (1024, 512, 256, 128)), 512)
    proj3 = proj.reshape(B, L, NMAIN)
    small3 = small.reshape(B, L, LANE)
    cb = _pick_tile(L, (128, 64, 32, 16))
    s0 = jnp.zeros((B, GLA_HEADS, GLA_DK, GLA_DV), F32)
    og, gla_state = gla(proj3, small3, W["wa_pad"], W["b_a"], W["gnorm"], s0, cb, GLA_SUB, cb)
    cmp = compress(proj3, W["w1cat"], W["w2s"])
    _, mmat = _block_matrices(L, L // CMP_STRIDE - 1)
    on = nsa_prompt(proj3, small3, cmp, mmat.T)
    h1, hn = out_proj(og.reshape(M, -1), on.reshape(M, -1), h2d, W["w_out"], W["g_ffn"],
                      _pick_tile(M, (256, 128)))
    prefix = jnp.zeros((B, SUBLANE, W["Fp"]), F32)
    hmid, tail = ffn_up_seq(hn, W["w_up_a"], W["w_up_u"], W["conv_w"], W["conv_b"], prefix, L,
                            _pick_tile(L, (512, 256, 128)), 512)
    h2 = ffn_down(hmid, W["w_down"], h1, _pick_tile(M, (512, 256, 128)), 512)
    keep = min(WINDOW, L)
    outs = tuple(_kv_rows(kv[:4], B, slice(0, L)) + _kv_rows(kv[4:], B, slice(L - keep, L))
                 + [gla_state, tail[:, SUBLANE - (CONV_W - 1):, :W["F"]]])
    return h2, outs


def _decode_layer(h3d, p2d_tm, B, L, lp, caches, page_table, win_k, win_v, gla_s, conv_s, W):
    Dm = h3d.shape[2]
    Mp = B * lp
    x2 = h3d.reshape(Mp, Dm)
    proj, small, kv = norm_proj(x2, W["g_attn"], W["w_main"], W["w_small"],
                                _pick_tile(Mp, (1024, 512, 256, 128, 64, 32, 16, 8)), 512)
    proj3 = proj.reshape(B, lp, NMAIN)
    small3 = small.reshape(B, lp, LANE)
    og, gla_state = gla(proj3, small3, W["wa_pad"], W["b_a"], W["gnorm"], gla_s, lp, lp, L)
    plen = page_table.shape[1] * caches[0].shape[1] // NSA_KV
    emat, mmat = _block_matrices(plen + KCHUNK, (plen + L) // CMP_STRIDE - 1)
    wlen = win_k.shape[1]
    on, wk_new, wv_new = nsa_decode(proj3, small3, caches, page_table,
                                    win_k.reshape(B, wlen * NSA_KV, NSA_HD),
                                    win_v.reshape(B, wlen * NSA_KV, NSA_HD),
                                    W["w1cat"], W["w2s"], emat, mmat, L)
    h1, hn = out_proj(og.reshape(Mp, -1), on.reshape(Mp, -1), x2, W["w_out"], W["g_ffn"],
                      _pick_tile(Mp, (256, 128, 64, 32, 16, 8)))
    to_tm = lambda t: t.reshape(B, lp, Dm)[:, :L].transpose(1, 0, 2).reshape(L * B, Dm)
    h1_tm, hn_tm = to_tm(h1), to_tm(hn)
    prefix = jnp.pad(conv_s.transpose(1, 0, 2), ((0, 0), (0, 0), (0, W["Fp"] - W["F"])))
    hmid, tail = ffn_up_tm(hn_tm, W["w_up_a"], W["w_up_u"], W["conv_w"], W["conv_b"], prefix, L, 512)
    M = L * B
    h2 = ffn_down(hmid, W["w_down"], h1_tm, _pick_tile(M, (512, 256, 128, 64, 32, 16, 8)), 512)
    outs = tuple(_kv_rows(kv[:4], B, slice(0, L))
                 + [wk_new.reshape(win_k.shape), wv_new.reshape(win_v.shape),
                    gla_state, tail[:, :, :W["F"]].transpose(1, 0, 2)])
    return h2, outs


def kernel(x_prompt, x_sample, p_prompt, p_sample, cache_cmp_k, cache_cmp_v, cache_slc_k, cache_slc_v,
           page_table, state_win_k, state_win_v, state_gla, state_ffn_conv, attn_norm_g, w_in, gla_w_a2,
           gla_b_a, gla_norm_g, cmp_wk1, cmp_wk2, cmp_wv1, cmp_wv2, w_out, ffn_norm_g, ffn_w_up,
           ffn_conv_w, ffn_conv_b, ffn_w_down, ple_w_proj, ple_w_gate, ple_b_gate, final_norm_g):
    depth = w_in.shape[0]
    assert depth == 1, "layers are chained through HBM one at a time; only depth 1 is wired"
    Bp, Lp, Dm = x_prompt.shape
    Bs, Ls, _ = x_sample.shape
    W = _prep_weights(0, attn_norm_g, w_in, gla_w_a2, gla_b_a, gla_norm_g, cmp_wk1, cmp_wk2, cmp_wv1,
                      cmp_wv2, w_out, ffn_norm_g, ffn_w_up, ffn_conv_w, ffn_conv_b, ffn_w_down,
                      ple_w_proj, ple_w_gate, ple_b_gate)
    g_final = final_norm_g.reshape(1, Dm)

    Mp = Bp * Lp
    h2, outs_p = _prompt_layer(x_prompt.reshape(Mp, Dm), None, Bp, Lp, W)
    y_prompt = ple_final(h2, p_prompt[0].reshape(Mp, -1), W["w_gate"], W["b_gate"], W["w_proj"], g_final,
                         _pick_tile(Mp, (256, 128))).reshape(Bp, Lp, Dm)

    lp = -(-Ls // SUBLANE) * SUBLANE
    xs = jnp.pad(x_sample, ((0, 0), (0, lp - Ls), (0, 0)))
    n_pool, psz = cache_cmp_k.shape[1], cache_cmp_k.shape[2]
    caches = [c[0].reshape(n_pool, psz * NSA_KV, NSA_HD)
              for c in (cache_cmp_k, cache_cmp_v, cache_slc_k, cache_slc_v)]
    h2s, outs_s = _decode_layer(xs, None, Bs, Ls, lp, caches, page_table, state_win_k[0], state_win_v[0],
                                state_gla[0], state_ffn_conv[0], W)
    p_tm = p_sample[0].transpose(1, 0, 2).reshape(Ls * Bs, -1)
    Ms = Ls * Bs
    y_tm = ple_final(h2s, p_tm, W["w_gate"], W["b_gate"], W["w_proj"], g_final,
                     _pick_tile(Ms, (256, 128, 64, 32, 16, 8)))
    y_sample = y_tm.reshape(Ls, Bs, Dm).transpose(1, 0, 2)

    lead = lambda t: t[None]
    return (y_prompt, y_sample) + tuple(lead(t) for t in outs_p) + tuple(lead(t) for t in outs_s)
```

```python
import functools
import math

import numpy as np
import jax
import jax.numpy as jnp
from jax import lax
from jax.experimental import pallas as pl
from jax.experimental.pallas import tpu as pltpu

F32 = jnp.float32
BF16 = jnp.bfloat16

GLA_HEADS = 4
GLA_DK = 128
GLA_DV = 256
GLA_RANK = 16
GLA_TAU = 16.0
GLA_SUB = 16
NSA_HEADS = 8
NSA_KV = 2
NSA_REP = NSA_HEADS // NSA_KV
NSA_HD = 128
CMP_LEN = 32
CMP_STRIDE = 16
SLC_LEN = 64
N_SELECT = 16
WINDOW = 512
CONV_W = 3
EPS = 1e-6
NEG = -1e30
M_FLOOR = 0.1 * NEG
FORCED = 1e6

LANE = 128
SUBLANE = 8
KCHUNK = 128
CMP_SEG_ROWS = NSA_KV * CMP_STRIDE
CMP_PITCH = CMP_SEG_ROWS + SUBLANE
VMEM_LIMIT = 56 * 1024 * 1024

GQ0 = 0
GK0 = GQ0 + GLA_HEADS * GLA_DK
GV0 = GK0 + GLA_HEADS * GLA_DK
GR0 = GV0 + GLA_HEADS * GLA_DV
NQ0 = GR0 + GLA_HEADS * GLA_DV
KVW = NSA_KV * NSA_HD
KC0 = NQ0 + NSA_HEADS * NSA_HD
VC0 = KC0 + KVW
KS0 = VC0 + KVW
VS0 = KS0 + KVW
KW0 = VS0 + KVW
VW0 = KW0 + KVW
NMAIN = VW0 + KVW
GATE0 = GLA_RANK


def _cp(sem):
    return pltpu.CompilerParams(dimension_semantics=sem, vmem_limit_bytes=VMEM_LIMIT)


def _dot(a, b):
    return jnp.dot(a, b, preferred_element_type=F32)


def _dot_nt(a, b):
    return lax.dot_general(a, b, (((1,), (1,)), ((), ())), preferred_element_type=F32)


def _rms(x, g):
    return x * lax.rsqrt(jnp.mean(x * x, axis=-1, keepdims=True) + EPS) * g


def _norm_proj_kernel(x_ref, g_ref, wm_ref, ws_ref, om_ref, os_ref, kv_ref, xn_ref, *, j_kv0):
    j = pl.program_id(1)

    @pl.when(j == 0)
    def _():
        xn = _rms(x_ref[...], g_ref[...]).astype(BF16)
        xn_ref[...] = xn
        os_ref[...] = _dot(xn, ws_ref[...])

    tile = _dot(xn_ref[...], wm_ref[...])
    om_ref[...] = tile

    @pl.when(j >= j_kv0)
    def _():
        tm = tile.shape[0]
        for a in range(tile.shape[1] // KVW):
            for g in range(NSA_KV):
                c0 = a * KVW + g * NSA_HD
                kv_ref[a, pl.ds(g, tm, stride=NSA_KV), :] = tile[:, c0:c0 + NSA_HD]


def norm_proj(x2, g, w_main, w_small, tm, tn):
    M, Dm = x2.shape
    N = w_main.shape[1]
    assert KC0 % tn == 0 and tn % KVW == 0
    j_kv0 = KC0 // tn
    per_tile = tn // KVW
    n_kv = (N - KC0) // KVW
    return pl.pallas_call(
        functools.partial(_norm_proj_kernel, j_kv0=j_kv0),
        grid=(M // tm, N // tn),
        in_specs=[pl.BlockSpec((tm, Dm), lambda i, j: (i, 0)),
                  pl.BlockSpec((1, Dm), lambda i, j: (0, 0)),
                  pl.BlockSpec((Dm, tn), lambda i, j: (0, j)),
                  pl.BlockSpec((Dm, LANE), lambda i, j: (0, 0))],
        out_specs=[pl.BlockSpec((tm, tn), lambda i, j: (i, j)),
                   pl.BlockSpec((tm, LANE), lambda i, j: (i, 0)),
                   pl.BlockSpec((per_tile, NSA_KV * tm, NSA_HD),
                                lambda i, j: (jnp.maximum(j - j_kv0, 0), i, 0))],
        out_shape=[jax.ShapeDtypeStruct((M, N), F32), jax.ShapeDtypeStruct((M, LANE), F32),
                   jax.ShapeDtypeStruct((n_kv, NSA_KV * M, NSA_HD), F32)],
        scratch_shapes=[pltpu.VMEM((tm, Dm), BF16)],
        compiler_params=_cp(("parallel", "arbitrary")),
        name="norm_proj")(x2, g, w_main, w_small)


def _gla_kernel(q_ref, k_ref, v_ref, r_ref, sm_ref, wa_ref, ba_ref, gn_ref, s0_ref,
                o_ref, s_ref, st_ref, *, cb, sub, valid, nbb):
    c = pl.program_id(1)

    @pl.when(c == 0)
    def _():
        st_ref[...] = s0_ref[...]

    GH = GLA_HEADS
    H = nbb * GH
    side = lambda ref: ref[0] if nbb == 1 else jnp.concatenate([ref[s] for s in range(nbb)], axis=1)
    hk = lambda h: slice(h * GLA_DK, (h + 1) * GLA_DK)
    hv = lambda h: slice(h * GLA_DV, (h + 1) * GLA_DV)
    ri = lax.broadcasted_iota(jnp.int32, (cb, cb), 0)
    ci = lax.broadcasted_iota(jnp.int32, (cb, cb), 1)
    tri = jnp.where(ci <= ri, 1.0, 0.0)
    rowv = lax.broadcasted_iota(jnp.int32, (cb, 1), 0)
    q = side(q_ref) * (GLA_DK ** -0.5)
    k = side(k_ref)
    v = side(v_ref)
    pre = [_dot(sm_ref[s].astype(BF16), wa_ref[...]) + ba_ref[...] for s in range(nbb)]
    pre = pre[0] if nbb == 1 else jnp.concatenate(pre, axis=1)
    log_a = jax.nn.log_sigmoid(pre) / GLA_TAU
    if valid < cb:
        log_a = jnp.where(rowv < valid, log_a, 0.0)
        k = jnp.where(rowv < valid, k, 0.0)
        v = jnp.where(rowv < valid, v, 0.0)
    hi = log_a.astype(BF16).astype(F32)
    lo = (log_a - hi).astype(BF16).astype(F32)
    b = _dot(tri, hi) + _dot(tri, lo)
    b_last = b[cb - 1:cb, :]
    S = [st_ref[h // GH, h % GH] for h in range(H)]
    vb = v.astype(BF16)

    qe = (q * jnp.exp(b)).astype(BF16)
    o = [_dot(qe[:, hk(h)], S[h].astype(BF16)) for h in range(H)]
    a_rows = [[] for _ in range(H)]
    cc = lax.broadcasted_iota(jnp.int32, (sub, cb), 1)
    for blk in range(cb // sub):
        lo_r, hi_r = blk * sub, (blk + 1) * sub
        r_dec = jnp.zeros((1, H * GLA_DK), F32) if blk == 0 else b[lo_r - 1:lo_r, :]
        q_i = q[lo_r:hi_r] * jnp.exp(b[lo_r:hi_r] - r_dec)
        k_i = k * jnp.exp(jnp.where(rowv < hi_r, r_dec - b, 0.0))
        rr = lax.broadcasted_iota(jnp.int32, (sub, cb), 0) + lo_r
        for h in range(H):
            a = _dot_nt(q_i[:, hk(h)], k_i[:, hk(h)])
            a_rows[h].append(jnp.where(cc <= rr, a, 0.0))
    for h in range(H):
        a = a_rows[h][0] if len(a_rows[h]) == 1 else jnp.concatenate(a_rows[h], axis=0)
        o[h] = o[h] + _dot(a, v[:, hv(h)])

    kh = k * jnp.exp(b_last - b)
    dec = jnp.exp(b_last)
    if cb < LANE:
        kh = jnp.concatenate([kh, jnp.zeros((LANE - cb, H * GLA_DK), F32)], axis=0)
        vpad = jnp.concatenate([vb, jnp.zeros((LANE - cb, H * GLA_DV), BF16)], axis=0)
    else:
        vpad = vb
    for h in range(H):
        dec_col = jnp.transpose(jnp.broadcast_to(dec[:, hk(h)], (GLA_DK, GLA_DK)))[:, 0:1]
        s_new = dec_col * S[h] + _dot(jnp.transpose(kh[:, hk(h)]).astype(BF16), vpad[:, hv(h)])
        st_ref[h // GH, h % GH] = s_new
        s_ref[h // GH, h % GH] = s_new

    r = side(r_ref)
    gate = r * jax.nn.sigmoid(r)
    for h in range(H):
        o_ref[h // GH, :, hv(h % GH)] = (_rms(o[h], gn_ref[...]) * gate[:, hv(h)]).astype(o_ref.dtype)


def gla(proj3, small3, wa_pad, b_a, gnorm, s0, cb, sub, valid, nbb):
    B, L, _ = proj3.shape
    H = GLA_HEADS
    qw, vw = H * GLA_DK, H * GLA_DV
    kern = functools.partial(_gla_kernel, cb=cb, sub=sub, valid=valid, nbb=nbb)
    return pl.pallas_call(
        kern,
        grid=(B // nbb, L // cb),
        in_specs=[pl.BlockSpec((nbb, cb, qw), lambda b, c: (b, c, GQ0 // qw)),
                  pl.BlockSpec((nbb, cb, qw), lambda b, c: (b, c, GK0 // qw)),
                  pl.BlockSpec((nbb, cb, vw), lambda b, c: (b, c, GV0 // vw)),
                  pl.BlockSpec((nbb, cb, vw), lambda b, c: (b, c, GR0 // vw)),
                  pl.BlockSpec((nbb, cb, LANE), lambda b, c: (b, c, 0)),
                  pl.BlockSpec((LANE, qw), lambda b, c: (0, 0)),
                  pl.BlockSpec((1, qw), lambda b, c: (0, 0)),
                  pl.BlockSpec((1, GLA_DV), lambda b, c: (0, 0)),
                  pl.BlockSpec((nbb, H, GLA_DK, GLA_DV), lambda b, c: (b, 0, 0, 0))],
        out_specs=[pl.BlockSpec((nbb, cb, vw), lambda b, c: (b, c, 0)),
                   pl.BlockSpec((nbb, H, GLA_DK, GLA_DV), lambda b, c: (b, 0, 0, 0))],
        out_shape=[jax.ShapeDtypeStruct((B, L, vw), BF16),
                   jax.ShapeDtypeStruct((B, H, GLA_DK, GLA_DV), F32)],
        scratch_shapes=[pltpu.VMEM((nbb, H, GLA_DK, GLA_DV), F32)],
        compiler_params=_cp(("parallel", "arbitrary")),
        name="gla")(proj3, proj3, proj3, proj3, small3, wa_pad, b_a, gnorm, s0)


def _compress_group(load_rows, w1_ref, w2, nseg):
    acc = jnp.zeros((nseg, 2 * NSA_HD), F32)
    for j in range(CMP_STRIDE):
        acc = acc + _dot(load_rows(j).astype(BF16), w1_ref[j])
    first = acc[:, :NSA_HD]
    second = acc[:, NSA_HD:]
    h = jax.nn.gelu(first + pltpu.roll(second, nseg - 1, axis=0))
    return _dot(h.astype(BF16), w2)


def _compress_groups(loads, w1_refs, w2s, nrow):
    n = len(loads)
    acc = [jnp.zeros((nrow, 2 * NSA_HD), F32) for _ in range(n)]
    for j in range(CMP_STRIDE):
        for i in range(n):
            acc[i] = acc[i] + _dot(loads[i](j).astype(BF16), w1_refs[i][j])
    hs = [jax.nn.gelu(a[:, :NSA_HD] + pltpu.roll(a[:, NSA_HD:], nrow - 1, axis=0)) for a in acc]
    return [_dot(h.astype(BF16), w2) for h, w2 in zip(hs, w2s)]


def _cmp_branch(q4, ck, cv, qpos4, nb):
    s = _dot_nt(q4, ck.astype(BF16))
    n = lax.broadcasted_iota(jnp.int32, s.shape, 1)
    mask = jnp.where(n * CMP_STRIDE + (CMP_LEN - 1) <= qpos4, n, nb) < nb
    s = jnp.where(mask, s, NEG)
    m = jnp.max(s, axis=-1, keepdims=True)
    e = jnp.where(mask, jnp.exp(s - m), 0.0)
    p = e / jnp.maximum(jnp.sum(e, axis=-1, keepdims=True), 1e-30)
    return p, _dot(p.astype(BF16), cv.astype(BF16))


def _select(pg, mmat, qpos_tok, nslc, nsel):
    hi = pg.astype(BF16)
    lo = (pg - hi.astype(F32)).astype(BF16)
    imp = _dot(hi, mmat) + _dot(lo, mmat)
    blk = lax.broadcasted_iota(jnp.int32, pg.shape, 1)
    cur = qpos_tok // SLC_LEN
    forced = jnp.where(blk == 0, 1, 0) + jnp.where(blk == cur, 1, 0) + jnp.where(blk == cur - 1, 1, 0)
    score = jnp.where(blk <= cur, jnp.where(forced > 0, FORCED, imp), -1.0)
    score = jnp.where(blk < nslc, score, -2.0)
    rank = jnp.zeros(pg.shape, F32)
    for j in range(nslc):
        sj = score[:, j:j + 1]
        tie = jnp.where(blk > j, sj, NEG)
        rank = rank + jnp.where(sj > score, 1.0, 0.0) + jnp.where(tie == score, 1.0, 0.0)
    return jnp.where(rank < nsel, jnp.where(blk < nslc, 1.0, 0.0), 0.0)


def _flash_init(m_ref, l_ref, acc_ref):
    m_ref[...] = jnp.full(m_ref.shape, NEG, F32)
    l_ref[...] = jnp.zeros(l_ref.shape, F32)
    acc_ref[...] = jnp.zeros(acc_ref.shape, F32)


def _flash_step(q4, k, v, mask, m_ref, l_ref, acc_ref):
    s = jnp.where(mask, _dot_nt(q4, k), NEG)
    m_old = m_ref[...]
    m_new = jnp.maximum(m_old, jnp.max(s, axis=-1, keepdims=True))
    alpha = jnp.exp(m_old - m_new)
    p = jnp.where(mask, jnp.exp(s - m_new), 0.0)
    l_ref[...] = alpha * l_ref[...] + jnp.sum(p, axis=-1, keepdims=True)
    acc_ref[...] = alpha * acc_ref[...] + _dot(p.astype(BF16), v)
    m_ref[...] = m_new


def _flash_out(l_ref, acc_ref):
    return acc_ref[...] / jnp.maximum(l_ref[...], 1e-30)


def _tile_rows(x, reps):
    return jnp.concatenate([x] * reps, axis=0)


def _attend(q4, keys, vals, masks):
    s = [jnp.where(mk, _dot_nt(q4, k), NEG) for k, mk in zip(keys, masks)]
    m = s[0].max(axis=-1, keepdims=True)
    for si in s[1:]:
        m = jnp.maximum(m, si.max(axis=-1, keepdims=True))
    e = [jnp.where(mk, jnp.exp(si - m), 0.0) for si, mk in zip(s, masks)]
    l = sum(ei.sum(axis=-1, keepdims=True) for ei in e)
    o = sum(_dot(ei.astype(BF16), v) for ei, v in zip(e, vals))
    return o / jnp.maximum(l, 1e-30)


def _compress_kernel(x_ref, w1_ref, w2_ref, o_ref, *, nseg):
    load = lambda j: x_ref[0, pl.ds(j, nseg, stride=CMP_STRIDE), :]
    o_ref[0, 0, 0] = _compress_group(load, w1_ref.at[0], w2_ref[0].astype(BF16), nseg)


def compress(proj3, w1cat, w2s):
    B, L, _ = proj3.shape
    nseg = L // CMP_STRIDE
    G = NSA_KV
    return pl.pallas_call(
        functools.partial(_compress_kernel, nseg=nseg),
        grid=(B, 2, G),
        in_specs=[pl.BlockSpec((1, L, NSA_HD), lambda b, w, g: (b, 0, KC0 // NSA_HD + w * G + g)),
                  pl.BlockSpec((1, CMP_STRIDE, NSA_HD, 2 * NSA_HD), lambda b, w, g: (w, 0, 0, 0)),
                  pl.BlockSpec((1, NSA_HD, NSA_HD), lambda b, w, g: (w, 0, 0))],
        out_specs=pl.BlockSpec((1, 1, 1, nseg, NSA_HD), lambda b, w, g: (w, b, g, 0, 0)),
        out_shape=jax.ShapeDtypeStruct((2, B, G, nseg, NSA_HD), F32),
        compiler_params=_cp(("parallel", "parallel", "parallel")),
        name="compress")(proj3, w1cat, w2s)


def _select_t(pg_t, mm_t, qpos_row, nslc, nsel):
    nsp = -(-nslc // SUBLANE) * SUBLANE
    hi = pg_t.astype(BF16)
    lo = (pg_t - hi.astype(F32)).astype(BF16)
    imp = (_dot(mm_t, hi) + _dot(mm_t, lo))[:nsp]
    blk = lax.broadcasted_iota(jnp.int32, imp.shape, 0)
    cur = qpos_row // SLC_LEN
    forced = jnp.where(blk == 0, 1, 0) + jnp.where(blk == cur, 1, 0) + jnp.where(blk == cur - 1, 1, 0)
    score = jnp.where(blk <= cur, jnp.where(forced > 0, FORCED, imp), -1.0)
    score = jnp.where(blk < nslc, score, -2.0)
    rank = jnp.zeros(imp.shape, F32)
    for j in range(nslc):
        sj = score[j:j + 1, :]
        tie = jnp.where(blk > j, sj, NEG)
        rank = rank + jnp.where(sj > score, 1.0, 0.0) + jnp.where(tie == score, 1.0, 0.0)
    return jnp.where(rank < nsel, jnp.where(blk < nslc, 1.0, 0.0), 0.0)


def _flash_step_t(q_t, k, v_t, bias, m_ref, l_ref, acc_ref):
    s = _dot(k, q_t) + bias
    m_old = m_ref[...]
    m_new = jnp.maximum(m_old, jnp.max(s, axis=0, keepdims=True))
    alpha = jnp.exp(m_old - m_new)
    p = jnp.exp(s - m_new)
    l_ref[...] = alpha * l_ref[...] + jnp.sum(p, axis=0, keepdims=True)
    acc_ref[...] = alpha * acc_ref[...] + _dot(v_t, p.astype(BF16))
    m_ref[...] = m_new


def _nsa_prompt_kernel(q_ref, sm_ref, ck_ref, cv_ref, ks_ref, vs_ref, kw_ref, vw_ref, mm_ref,
                       o_ref, vst_ref, vwt_ref, sel_ref, m_ref, l_ref, acc_ref, mw_ref, lw_ref, accw_ref,
                       *, tq, kstep, nslc, nsel, nb):
    g = pl.program_id(1)
    qb = pl.program_id(2)
    R = NSA_REP
    T = R * tq
    L = ks_ref.shape[1]
    scale = NSA_HD ** -0.5

    @pl.when(qb == 0)
    def _():
        for c in range(L // KCHUNK):
            cs = slice(c * KCHUNK, (c + 1) * KCHUNK)
            vst_ref[:, cs] = jnp.transpose(vs_ref[0, cs, :]).astype(BF16)
            vwt_ref[:, cs] = jnp.transpose(vw_ref[0, cs, :]).astype(BF16)

    q_t = (jnp.concatenate([jnp.transpose(q_ref[0, :, r * NSA_HD:(r + 1) * NSA_HD]) for r in range(R)],
                           axis=1) * scale).astype(BF16)
    qpos_row = qb * tq + lax.broadcasted_iota(jnp.int32, (1, tq), 1)
    qpos4 = jnp.concatenate([qpos_row] * R, axis=1)

    s = _dot(ck_ref[0, 0, 0].astype(BF16), q_t)
    n = lax.broadcasted_iota(jnp.int32, s.shape, 0)
    mask = jnp.where(n * CMP_STRIDE + (CMP_LEN - 1) <= qpos4, n, nb) < nb
    s = jnp.where(mask, s, NEG)
    e = jnp.where(mask, jnp.exp(s - jnp.max(s, axis=0, keepdims=True)), 0.0)
    p = e / jnp.maximum(jnp.sum(e, axis=0, keepdims=True), 1e-30)
    o_c = _dot(jnp.transpose(cv_ref[0, 0, 0]).astype(BF16), p.astype(BF16))
    pg = p[:, 0:tq]
    for r in range(1, R):
        pg = pg + p[:, r * tq:(r + 1) * tq]
    sel_ref[...] = (_select_t(pg, mm_ref[...], qpos_row, nslc, nsel) - 1.0) * (-NEG)

    key = lax.broadcasted_iota(jnp.int32, (kstep, T), 0)
    hi_step = (qb * tq + tq - 1) // kstep + 1
    lo_step = jnp.maximum(qb * tq - (WINDOW - 1), 0) // kstep

    sel_state = (m_ref, l_ref, acc_ref)
    win_state = (mw_ref, lw_ref, accw_ref)
    for m_r, l_r, acc_r in (sel_state, win_state):
        m_r[...] = jnp.full(m_r.shape, M_FLOOR, F32)
        l_r[...] = jnp.zeros(l_r.shape, F32)
        acc_r[...] = jnp.zeros(acc_r.shape, F32)

    def sel_step(c, causal):
        off = pl.multiple_of(c * kstep, kstep)
        k = ks_ref[0, pl.ds(off, kstep), :].astype(BF16)
        v_t = vst_ref[:, pl.ds(off, kstep)]
        blocks = [jnp.broadcast_to(sel_ref[pl.ds((kstep // SLC_LEN) * c + i, 1), :], (SLC_LEN, tq))
                  for i in range(kstep // SLC_LEN)]
        bias = jnp.concatenate(blocks, axis=0)
        bias = jnp.concatenate([bias] * R, axis=1)
        if causal:
            bias = jnp.where(off + key <= qpos4, bias, NEG)
        _flash_step_t(q_t, k, v_t, bias, *sel_state)

    def win_step(c, causal):
        off = pl.multiple_of(c * kstep, kstep)
        k = kw_ref[0, pl.ds(off, kstep), :].astype(BF16)
        v_t = vwt_ref[:, pl.ds(off, kstep)]
        kpos = off + key
        if causal:
            bias = jnp.where(kpos <= qpos4, 0.0, NEG)
        else:
            bias = jnp.where(kpos > qpos4 - WINDOW, 0.0, NEG)
        _flash_step_t(q_t, k, v_t, bias, *win_state)

    def body_sel(c, carry):
        sel_step(c, False)
        return carry

    def body_both(c, carry):
        sel_step(c, False)
        win_step(c, False)
        return carry

    last = hi_step - 1
    lax.fori_loop(0, jnp.minimum(lo_step, last), body_sel, 0)
    lax.fori_loop(lo_step, last, body_both, 0)
    sel_step(last, True)
    win_step(last, True)
    o_s = acc_ref[...] / jnp.maximum(l_ref[...], 1e-30)
    o_w = accw_ref[...] / jnp.maximum(lw_ref[...], 1e-30)

    g_t = jnp.transpose(jax.nn.sigmoid(sm_ref[0]))
    for r in range(R):
        def gate(br, r=r):
            c0 = GATE0 + br * NSA_HEADS + r
            return jnp.where(g == 0, g_t[c0:c0 + 1, :], g_t[c0 + R:c0 + R + 1, :])
        cs = slice(r * tq, (r + 1) * tq)
        o = gate(0) * o_c[:, cs] + gate(1) * o_s[:, cs] + gate(2) * o_w[:, cs]
        o_ref[0, :, r * NSA_HD:(r + 1) * NSA_HD] = jnp.transpose(o).astype(o_ref.dtype)


def nsa_prompt(proj3, small3, cmp, mmat_t):
    B, L, _ = proj3.shape
    tq = KCHUNK
    nseg = L // CMP_STRIDE
    nb = nseg - 1
    nslc = -(-L // SLC_LEN)
    nsel = min(N_SELECT, nslc)
    nsp = -(-nslc // SUBLANE) * SUBLANE
    G, R = NSA_KV, NSA_REP
    kstep = _pick_tile(L, (2 * KCHUNK, KCHUNK))
    assert nseg == LANE and L % KCHUNK == 0 and kstep % SLC_LEN == 0
    kv_spec = lambda c0: pl.BlockSpec((1, L, NSA_HD), lambda b, g, qb: (b, 0, c0 // NSA_HD + g))
    kern = functools.partial(_nsa_prompt_kernel, tq=tq, kstep=kstep, nslc=nslc, nsel=nsel, nb=nb)
    return pl.pallas_call(
        kern,
        grid=(B, G, L // tq),
        in_specs=[pl.BlockSpec((1, tq, R * NSA_HD), lambda b, g, qb: (b, qb, NQ0 // (R * NSA_HD) + g)),
                  pl.BlockSpec((1, tq, LANE), lambda b, g, qb: (b, qb, 0)),
                  pl.BlockSpec((1, 1, 1, nseg, NSA_HD), lambda b, g, qb: (0, b, g, 0, 0)),
                  pl.BlockSpec((1, 1, 1, nseg, NSA_HD), lambda b, g, qb: (1, b, g, 0, 0)),
                  kv_spec(KS0), kv_spec(VS0), kv_spec(KW0), kv_spec(VW0),
                  pl.BlockSpec((LANE, LANE), lambda b, g, qb: (0, 0))],
        out_specs=pl.BlockSpec((1, tq, R * NSA_HD), lambda b, g, qb: (b, qb, g)),
        out_shape=jax.ShapeDtypeStruct((B, L, NSA_HEADS * NSA_HD), BF16),
        scratch_shapes=[pltpu.VMEM((NSA_HD, L), BF16),
                        pltpu.VMEM((NSA_HD, L), BF16),
                        pltpu.VMEM((nsp, tq), F32)]
                       + 2 * [pltpu.VMEM((1, R * tq), F32),
                              pltpu.VMEM((1, R * tq), F32),
                              pltpu.VMEM((NSA_HD, R * tq), F32)],
        compiler_params=_cp(("parallel", "parallel", "arbitrary")),
        name="nsa_prompt")(proj3, small3, cmp, cmp, proj3, proj3, proj3, proj3, mmat_t)


def _nsa_decode_kernel(pt_ref, q_ref, sm_ref, ksn_ref, vsn_ref, kwn_ref, vwn_ref, wk_ref, wv_ref,
                       w1_ref, w2_ref, e_ref, mm_ref, c0_hbm, c1_hbm, c2_hbm, c3_hbm,
                       o_ref, wko_ref, wvo_ref, cbuf_ref, sbuf_ref, sem_ref,
                       *, lp, lreal, plen, rows_pp, n_pages, wlen, nslc, nsel, nb):
    b = pl.program_id(0)
    slot = b % 2
    caches = (c0_hbm, c1_hbm, c2_hbm, c3_hbm)
    G = NSA_KV
    R = NSA_REP
    T = R * lp
    segs_pp = rows_pp // CMP_SEG_ROWS

    def copies(bb, sl):
        out = []
        for p in range(n_pages):
            page = pt_ref[bb, p]
            for w in range(2):
                for s in range(segs_pp):
                    out.append(pltpu.make_async_copy(
                        caches[w].at[page, pl.ds(s * CMP_SEG_ROWS, CMP_SEG_ROWS)],
                        cbuf_ref.at[sl, w, pl.ds((p * segs_pp + s) * CMP_PITCH, CMP_SEG_ROWS)],
                        sem_ref.at[sl, w]))
                out.append(pltpu.make_async_copy(
                    caches[2 + w].at[page], sbuf_ref.at[sl, w, pl.ds(p * rows_pp, rows_pp)],
                    sem_ref.at[sl, 2 + w]))
        return out

    @pl.when(b == 0)
    def _():
        for cp in copies(0, 0):
            cp.start()

    @pl.when(b + 1 < pl.num_programs(0))
    def _():
        for cp in copies(b + 1, 1 - slot):
            cp.start()

    for w in range(len(caches)):
        whole = sbuf_ref.at[slot, 0]
        pltpu.make_async_copy(whole, whole, sem_ref.at[slot, w]).wait()

    scale = NSA_HD ** -0.5
    nseg = plen // CMP_STRIDE
    tok = lax.broadcasted_iota(jnp.int32, (lp, 1), 0)
    qpos_tok = plen + tok
    qpos4 = _tile_rows(qpos_tok, R)
    lane_n = lax.broadcasted_iota(jnp.int32, (T, KCHUNK), 1)
    new_pos = jnp.where(lane_n < lreal, plen + lane_n, qpos4 + 1)
    past_pos = lax.broadcasted_iota(jnp.int32, (T, plen), 1)
    win_pos = plen - wlen + lax.broadcasted_iota(jnp.int32, (T, wlen), 1)
    win_pos = jnp.where(win_pos > qpos4 - WINDOW, win_pos, qpos4 + 1)
    win_pos = jnp.where(win_pos >= 0, win_pos, qpos4 + 1)
    new_win_pos = jnp.where(new_pos > qpos4 - WINDOW, new_pos, qpos4 + 1)
    gts = jax.nn.sigmoid(sm_ref[0])
    zpad_f = jnp.zeros((KCHUNK - lp, NSA_HD), F32)
    pad_new = lambda ref, ls: jnp.concatenate([ref[0, :, ls], zpad_f], axis=0).astype(BF16)
    gls = lambda g: slice(g * NSA_HD, (g + 1) * NSA_HD)
    q4 = [(jnp.concatenate([q_ref[0, :, (g * R + r) * NSA_HD:(g * R + r + 1) * NSA_HD]
                            for r in range(R)], axis=0) * scale).astype(BF16) for g in range(G)]

    def seg_rows(w, j):
        return jnp.concatenate([cbuf_ref[slot, w, pl.ds(G * j + g, nseg, stride=CMP_PITCH), :]
                                for g in range(G)], axis=0)
    cmp = _compress_groups([functools.partial(seg_rows, w) for w in range(2)],
                           [w1_ref.at[w] for w in range(2)],
                           [w2_ref[w].astype(BF16) for w in range(2)], G * nseg)
    grow = lambda g: slice(g * nseg, (g + 1) * nseg)
    branch = [_cmp_branch(q4[g], cmp[0][grow(g)], cmp[1][grow(g)], qpos4, nb) for g in range(G)]
    pgs = []
    for g in range(G):
        p = branch[g][0]
        pg = p[0:lp]
        for r in range(1, R):
            pg = pg + p[r * lp:(r + 1) * lp]
        pgs.append(pg)
    sel = _select(jnp.concatenate(pgs, axis=0), mm_ref[...], _tile_rows(qpos_tok, G), nslc, nsel)
    selm_all = _dot(sel.astype(BF16), e_ref[...])

    o_s, o_w = [], []
    for g in range(G):
        selm = _tile_rows(selm_all[g * lp:(g + 1) * lp], R)
        k_past = sbuf_ref[slot, 0, pl.ds(g, plen, stride=G), :].astype(BF16)
        v_past = sbuf_ref[slot, 1, pl.ds(g, plen, stride=G), :].astype(BF16)
        m_past = jnp.where(selm[:, :plen] > 0.5, past_pos, qpos4 + 1) <= qpos4
        m_new = jnp.where(selm[:, plen:] > 0.5, new_pos, qpos4 + 1) <= qpos4
        o_s.append(_attend(q4[g], [k_past, pad_new(ksn_ref, gls(g))], [v_past, pad_new(vsn_ref, gls(g))],
                           [m_past, m_new]))
    for g in range(G):
        k_win = wk_ref[0, pl.ds(g, wlen, stride=G), :].astype(BF16)
        v_win = wv_ref[0, pl.ds(g, wlen, stride=G), :].astype(BF16)
        o_w.append(_attend(q4[g], [k_win, pad_new(kwn_ref, gls(g))], [v_win, pad_new(vwn_ref, gls(g))],
                           [win_pos <= qpos4, new_win_pos <= qpos4]))

    for g in range(G):
        for r in range(R):
            c0 = GATE0 + g * R + r
            rs = slice(r * lp, (r + 1) * lp)
            o = (gts[:, c0:c0 + 1] * branch[g][1][rs]
                 + gts[:, c0 + NSA_HEADS:c0 + NSA_HEADS + 1] * o_s[g][rs]
                 + gts[:, c0 + 2 * NSA_HEADS:c0 + 2 * NSA_HEADS + 1] * o_w[g][rs])
            hq = (g * R + r) * NSA_HD
            o_ref[0, :, hq:hq + NSA_HD] = o.astype(o_ref.dtype)

    wrows = wlen * G
    shift = lreal * G
    for src, new, dst in ((wk_ref, kwn_ref, wko_ref), (wv_ref, vwn_ref, wvo_ref)):
        dst[0, 0:wrows - shift, :] = src[0, shift:wrows, :]
        for t in range(lreal):
            for g in range(G):
                row = wrows - shift + t * G + g
                dst[0, row:row + 1, :] = new[0, t:t + 1, g * NSA_HD:(g + 1) * NSA_HD]


def nsa_decode(proj3, small3, caches, page_table, win_k, win_v, w1cat, w2s, emat, mmat, lreal):
    B, lp, _ = proj3.shape
    G = NSA_KV
    n_pages = page_table.shape[1]
    rows_pp = caches[0].shape[1]
    plen = n_pages * rows_pp // G
    wlen = win_k.shape[1] // G
    assert plen % CMP_STRIDE == 0 and lreal < CMP_STRIDE and lreal <= lp
    nb = (plen + lreal) // CMP_STRIDE - 1
    nslc = -(-(plen + lreal) // SLC_LEN)
    nsel = min(N_SELECT, nslc)
    new_spec = lambda c0: pl.BlockSpec((1, lp, KVW), lambda b, pt: (b, 0, c0 // KVW))
    const = lambda shape: pl.BlockSpec(shape, lambda b, pt: (0,) * len(shape))
    hbm = pl.BlockSpec(memory_space=pl.ANY)
    kern = functools.partial(_nsa_decode_kernel, lp=lp, lreal=lreal, plen=plen, rows_pp=rows_pp,
                             n_pages=n_pages, wlen=wlen, nslc=nslc, nsel=nsel, nb=nb)
    grid_spec = pltpu.PrefetchScalarGridSpec(
        num_scalar_prefetch=1,
        grid=(B,),
        in_specs=[pl.BlockSpec((1, lp, NSA_HEADS * NSA_HD), lambda b, pt: (b, 0, NQ0 // (NSA_HEADS * NSA_HD))),
                  pl.BlockSpec((1, lp, LANE), lambda b, pt: (b, 0, 0)),
                  new_spec(KS0), new_spec(VS0), new_spec(KW0), new_spec(VW0),
                  pl.BlockSpec((1, wlen * G, NSA_HD), lambda b, pt: (b, 0, 0)),
                  pl.BlockSpec((1, wlen * G, NSA_HD), lambda b, pt: (b, 0, 0)),
                  const((2, CMP_STRIDE, NSA_HD, 2 * NSA_HD)),
                  const((2, NSA_HD, NSA_HD)),
                  const((LANE, plen + KCHUNK)),
                  const((LANE, LANE)),
                  hbm, hbm, hbm, hbm],
        out_specs=[pl.BlockSpec((1, lp, NSA_HEADS * NSA_HD), lambda b, pt: (b, 0, 0)),
                   pl.BlockSpec((1, wlen * G, NSA_HD), lambda b, pt: (b, 0, 0)),
                   pl.BlockSpec((1, wlen * G, NSA_HD), lambda b, pt: (b, 0, 0))],
        scratch_shapes=[pltpu.VMEM((2, 2, n_pages * rows_pp // CMP_SEG_ROWS * CMP_PITCH, NSA_HD), F32),
                        pltpu.VMEM((2, 2, n_pages * rows_pp, NSA_HD), F32),
                        pltpu.SemaphoreType.DMA((2, len(caches)))])
    return pl.pallas_call(
        kern,
        grid_spec=grid_spec,
        out_shape=[jax.ShapeDtypeStruct((B, lp, NSA_HEADS * NSA_HD), BF16),
                   jax.ShapeDtypeStruct(win_k.shape, F32), jax.ShapeDtypeStruct(win_v.shape, F32)],
        compiler_params=_cp(("arbitrary",)),
        name="nsa_decode")(page_table, proj3, small3, proj3, proj3, proj3, proj3, win_k, win_v,
                           w1cat, w2s, emat, mmat, caches[0], caches[1], caches[2], caches[3])


def _out_proj_kernel(og_ref, on_ref, x_ref, w0_ref, w1_ref, g_ref, h_ref, hn_ref):
    h = x_ref[...] + _dot(og_ref[...], w0_ref[...]) + _dot(on_ref[...], w1_ref[...])
    h_ref[...] = h
    hn_ref[...] = _rms(h, g_ref[...]).astype(BF16)


def out_proj(og, on, x2, w_out, g, tm):
    M, Dm = x2.shape
    Kh = og.shape[1]
    return pl.pallas_call(
        _out_proj_kernel,
        grid=(M // tm,),
        in_specs=[pl.BlockSpec((tm, Kh), lambda i: (i, 0)),
                  pl.BlockSpec((tm, Kh), lambda i: (i, 0)),
                  pl.BlockSpec((tm, Dm), lambda i: (i, 0)),
                  pl.BlockSpec((Kh, Dm), lambda i: (0, 0)),
                  pl.BlockSpec((Kh, Dm), lambda i: (1, 0)),
                  pl.BlockSpec((1, Dm), lambda i: (0, 0))],
        out_specs=[pl.BlockSpec((tm, Dm), lambda i: (i, 0)),
                   pl.BlockSpec((tm, Dm), lambda i: (i, 0))],
        out_shape=[jax.ShapeDtypeStruct((M, Dm), F32), jax.ShapeDtypeStruct((M, Dm), BF16)],
        compiler_params=_cp(("parallel",)),
        name="out_proj")(og, on, x2, w_out, w_out, g)


def _ffn_up_seq_kernel(hn_ref, wa_ref, wu_ref, cw_ref, cb_ref, pre_ref, hm_ref, tail_ref, aext_ref,
                       *, tm, tiles_per_seq):
    m = pl.program_id(1)

    @pl.when(m % tiles_per_seq == 0)
    def _():
        aext_ref[0:SUBLANE, :] = pre_ref[0]

    hn = hn_ref[...]
    a = _dot(hn, wa_ref[...])
    u = _dot(hn, wu_ref[...])
    aext_ref[SUBLANE:SUBLANE + tm, :] = a
    p1 = aext_ref[pl.ds(SUBLANE - 1, tm), :]
    p2 = aext_ref[pl.ds(SUBLANE - 2, tm), :]
    c = p2 * cw_ref[0:1, :] + p1 * cw_ref[1:2, :] + a * cw_ref[2:3, :] + cb_ref[...]
    hm_ref[...] = (jax.nn.gelu(c) * u).astype(hm_ref.dtype)
    tail = a[tm - SUBLANE:tm, :]
    tail_ref[0] = tail
    aext_ref[0:SUBLANE, :] = tail


def ffn_up_seq(hn, w_a, w_u, conv_w, conv_b, prefix, seq_len, tm, tf):
    M, Dm = hn.shape
    Fp = conv_w.shape[1]
    nf = Fp // tf
    B = M // seq_len
    tps = seq_len // tm
    kern = functools.partial(_ffn_up_seq_kernel, tm=tm, tiles_per_seq=tps)
    return pl.pallas_call(
        kern,
        grid=(nf, M // tm),
        in_specs=[pl.BlockSpec((tm, Dm), lambda f, m: (m, 0)),
                  pl.BlockSpec((Dm, tf), lambda f, m: (0, f)),
                  pl.BlockSpec((Dm, tf), lambda f, m: (0, f)),
                  pl.BlockSpec((SUBLANE, tf), lambda f, m: (0, f)),
                  pl.BlockSpec((1, tf), lambda f, m: (0, f)),
                  pl.BlockSpec((1, SUBLANE, tf), lambda f, m: (m // tps, 0, f))],
        out_specs=[pl.BlockSpec((tm, tf), lambda f, m: (m, f)),
                   pl.BlockSpec((1, SUBLANE, tf), lambda f, m: (m // tps, 0, f))],
        out_shape=[jax.ShapeDtypeStruct((M, Fp), BF16), jax.ShapeDtypeStruct((B, SUBLANE, Fp), F32)],
        scratch_shapes=[pltpu.VMEM((SUBLANE + tm, tf), F32)],
        compiler_params=_cp(("parallel", "arbitrary")),
        name="ffn_up_seq")(hn, w_a, w_u, conv_w, conv_b, prefix)


def _ffn_up_tm_kernel(hn_ref, wa_ref, wu_ref, cw_ref, cb_ref, pre_ref, hm_ref, tail_ref, *, steps, nb):
    hn = hn_ref[...]
    a = _dot(hn, wa_ref[...])
    u = _dot(hn, wu_ref[...])
    slabs = [pre_ref[i] for i in range(CONV_W - 1)] + [a[t * nb:(t + 1) * nb] for t in range(steps)]
    for t in range(steps):
        c = (slabs[t] * cw_ref[0:1, :] + slabs[t + 1] * cw_ref[1:2, :] + slabs[t + 2] * cw_ref[2:3, :]
             + cb_ref[...])
        hm_ref[t * nb:(t + 1) * nb, :] = (jax.nn.gelu(c) * u[t * nb:(t + 1) * nb]).astype(hm_ref.dtype)
    for i in range(CONV_W - 1):
        tail_ref[i] = slabs[steps + i]


def ffn_up_tm(hn, w_a, w_u, conv_w, conv_b, prefix, steps, tf):
    M, Dm = hn.shape
    Fp = conv_w.shape[1]
    nf = Fp // tf
    nb = M // steps
    kern = functools.partial(_ffn_up_tm_kernel, steps=steps, nb=nb)
    return pl.pallas_call(
        kern,
        grid=(nf,),
        in_specs=[pl.BlockSpec((M, Dm), lambda f: (0, 0)),
                  pl.BlockSpec((Dm, tf), lambda f: (0, f)),
                  pl.BlockSpec((Dm, tf), lambda f: (0, f)),
                  pl.BlockSpec((SUBLANE, tf), lambda f: (0, f)),
                  pl.BlockSpec((1, tf), lambda f: (0, f)),
                  pl.BlockSpec((CONV_W - 1, nb, tf), lambda f: (0, 0, f))],
        out_specs=[pl.BlockSpec((M, tf), lambda f: (0, f)),
                   pl.BlockSpec((CONV_W - 1, nb, tf), lambda f: (0, 0, f))],
        out_shape=[jax.ShapeDtypeStruct((M, Fp), BF16), jax.ShapeDtypeStruct((CONV_W - 1, nb, Fp), F32)],
        compiler_params=_cp(("parallel",)),
        name="ffn_up_tm")(hn, w_a, w_u, conv_w, conv_b, prefix)


def _ffn_down_kernel(hm_ref, w_ref, h_ref, o_ref):
    o_ref[...] = h_ref[...] + _dot(hm_ref[...], w_ref[...])


def ffn_down(hmid, w_down, h1, tm, tn):
    M = hmid.shape[0]
    F, Dm = w_down.shape
    assert F % LANE == 0 and F <= hmid.shape[1]
    return pl.pallas_call(
        _ffn_down_kernel,
        grid=(M // tm, Dm // tn),
        in_specs=[pl.BlockSpec((tm, F), lambda i, j: (i, 0)),
                  pl.BlockSpec((F, tn), lambda i, j: (0, j)),
                  pl.BlockSpec((tm, tn), lambda i, j: (i, j))],
        out_specs=pl.BlockSpec((tm, tn), lambda i, j: (i, j)),
        out_shape=jax.ShapeDtypeStruct((M, Dm), F32),
        compiler_params=_cp(("parallel", "arbitrary")),
        name="ffn_down")(hmid, w_down, h1)


def _ple_final_kernel(h_ref, p_ref, wg_ref, bg_ref, wp_ref, g_ref, o_ref):
    h = h_ref[...]
    gate = jax.nn.sigmoid(_dot(h.astype(BF16), wg_ref[...]) + bg_ref[...])
    h = h + gate * _dot(p_ref[...].astype(BF16), wp_ref[...])
    o_ref[...] = _rms(h, g_ref[...])


def ple_final(h2, p2, w_gate, b_gate, w_proj, g, tm):
    M, Dm = h2.shape
    Pd = p2.shape[1]
    return pl.pallas_call(
        _ple_final_kernel,
        grid=(M // tm,),
        in_specs=[pl.BlockSpec((tm, Dm), lambda i: (i, 0)),
                  pl.BlockSpec((tm, Pd), lambda i: (i, 0)),
                  pl.BlockSpec((Dm, Dm), lambda i: (0, 0)),
                  pl.BlockSpec((1, Dm), lambda i: (0, 0)),
                  pl.BlockSpec((Pd, Dm), lambda i: (0, 0)),
                  pl.BlockSpec((1, Dm), lambda i: (0, 0))],
        out_specs=pl.BlockSpec((tm, Dm), lambda i: (i, 0)),
        out_shape=jax.ShapeDtypeStruct((M, Dm), F32),
        compiler_params=_cp(("parallel",)),
        name="ple_final")(h2, p2, w_gate, b_gate, w_proj, g)


def _pick_tile(n, prefs):
    for t in prefs:
        if n % t == 0:
            return t
    raise ValueError(f"no tile for {n}")


def _block_matrices(total_keys, nb):
    e = np.zeros((LANE, total_keys), np.float32)
    t = np.arange(total_keys)
    e[t // SLC_LEN, t] = 1.0
    ratio = SLC_LEN // CMP_STRIDE
    m = np.zeros((LANE, LANE), np.float32)
    for n in range(nb):
        for s in range(LANE):
            m[n, s] = float(ratio * s <= n <= ratio * s + ratio - 1) + float(ratio * s - 1 <= n <= ratio * s + ratio - 2)
    return jnp.asarray(e, BF16), jnp.asarray(m, BF16)


def _prep_weights(i, attn_norm_g, w_in, gla_w_a2, gla_b_a, gla_norm_g, cmp_wk1, cmp_wk2, cmp_wv1, cmp_wv2,
                  w_out, ffn_norm_g, ffn_w_up, ffn_conv_w, ffn_conv_b, ffn_w_down, ple_w_proj, ple_w_gate,
                  ple_b_gate):
    Dm = w_in.shape[1]
    F = ffn_w_down.shape[1]
    Fp = -(-F // 512) * 512
    win = w_in[i]
    ga0 = NQ0
    ng0 = ga0 + GLA_RANK + (NMAIN - NQ0)
    ngw = 3 * NSA_HEADS
    before_ga = lax.broadcasted_iota(jnp.int32, (1, NMAIN), 1) < ga0
    w_main = jnp.where(before_ga, win[:, :NMAIN], win[:, GLA_RANK:GLA_RANK + NMAIN]).astype(BF16)
    w_small = jnp.concatenate([win[:, ga0:ga0 + GLA_RANK], win[:, ng0:ng0 + ngw],
                               jnp.zeros((Dm, LANE - GLA_RANK - ngw), F32)], axis=1).astype(BF16)
    wa_pad = jnp.concatenate([gla_w_a2[i], jnp.zeros((LANE - GLA_RANK, GLA_HEADS * GLA_DK), F32)],
                             axis=0).astype(BF16)
    cat = lambda w1: jnp.concatenate([w1[:CMP_STRIDE], w1[CMP_STRIDE:]], axis=-1)
    w1cat = jnp.stack([cat(cmp_wk1[i]), cat(cmp_wv1[i])]).astype(BF16)
    w2s = jnp.stack([cmp_wk2[i], cmp_wv2[i]])
    wup = ffn_w_up[i]
    w_up_a = jnp.pad(wup[:, :F], ((0, 0), (0, Fp - F))).astype(BF16)
    w_up_u = jnp.pad(wup[:, F:], ((0, 0), (0, Fp - F))).astype(BF16)
    conv_w = jnp.pad(ffn_conv_w[i], ((0, SUBLANE - CONV_W), (0, Fp - F)))
    conv_b = jnp.pad(ffn_conv_b[i], (0, Fp - F)).reshape(1, Fp)
    w_down = ffn_w_down[i].astype(BF16)
    return dict(
        g_attn=attn_norm_g[i].reshape(1, Dm), w_main=w_main, w_small=w_small, wa_pad=wa_pad,
        b_a=gla_b_a[i].reshape(1, -1), gnorm=gla_norm_g[i].reshape(1, -1), w1cat=w1cat, w2s=w2s,
        w_out=w_out[i].astype(BF16), g_ffn=ffn_norm_g[i].reshape(1, Dm), w_up_a=w_up_a, w_up_u=w_up_u,
        conv_w=conv_w,
        conv_b=conv_b, w_down=w_down, w_proj=ple_w_proj[i].astype(BF16), w_gate=ple_w_gate[i].astype(BF16),
        b_gate=ple_b_gate[i].reshape(1, Dm), F=F, Fp=Fp)


def _kv_rows(kv, B, rows):
    t = kv.reshape(kv.shape[0], B, -1, NSA_KV, NSA_HD)[:, :, rows]
    return [t[i] for i in range(kv.shape[0])]


def _prompt_layer(h2d, p2d, B, L, W):
    Dm = h2d.shape[1]
    M = B * L
    proj, small, kv = norm_proj(h2d, W["g_attn"], W["w_main"], W["w_small"],
                                _pick_tile(M, (1024, 512, 256, 128)), 512)
    proj3 = proj.reshape(B, L, NMAIN)
    small3 = small.reshape(B, L, LANE)
    cb = _pick_tile(L, (128, 64, 32, 16))
    s0 = jnp.zeros((B, GLA_HEADS, GLA_DK, GLA_DV), F32)
    og, gla_state = gla(proj3, small3, W["wa_pad"], W["b_a"], W["gnorm"], s0, cb, GLA_SUB, cb, 1)
    cmp = compress(proj3, W["w1cat"], W["w2s"])
    _, mmat = _block_matrices(L, L // CMP_STRIDE - 1)
    on = nsa_prompt(proj3, small3, cmp, mmat.T)
    h1, hn = out_proj(og.reshape(M, -1), on.reshape(M, -1), h2d, W["w_out"], W["g_ffn"],
                      _pick_tile(M, (256, 128)))
    prefix = jnp.zeros((B, SUBLANE, W["Fp"]), F32)
    hmid, tail = ffn_up_seq(hn, W["w_up_a"], W["w_up_u"], W["conv_w"], W["conv_b"], prefix, L,
                            _pick_tile(L, (1024, 512, 256, 128)), 512)
    h2 = ffn_down(hmid, W["w_down"], h1, _pick_tile(M, (1024, 512, 256, 128)), 512)
    keep = min(WINDOW, L)
    outs = tuple(_kv_rows(kv[:4], B, slice(0, L)) + _kv_rows(kv[4:], B, slice(L - keep, L))
                 + [gla_state, tail[:, SUBLANE - (CONV_W - 1):, :W["F"]]])
    return h2, outs


def _decode_layer(h3d, p2d_tm, B, L, lp, caches, page_table, win_k, win_v, gla_s, conv_s, W):
    Dm = h3d.shape[2]
    Mp = B * lp
    x2 = h3d.reshape(Mp, Dm)
    proj, small, kv = norm_proj(x2, W["g_attn"], W["w_main"], W["w_small"],
                                _pick_tile(Mp, (1024, 512, 256, 128, 64, 32, 16, 8)), 512)
    proj3 = proj.reshape(B, lp, NMAIN)
    small3 = small.reshape(B, lp, LANE)
    og, gla_state = gla(proj3, small3, W["wa_pad"], W["b_a"], W["gnorm"], gla_s, lp, lp, L,
                        _pick_tile(B, (4, 2, 1)))
    plen = page_table.shape[1] * caches[0].shape[1] // NSA_KV
    emat, mmat = _block_matrices(plen + KCHUNK, (plen + L) // CMP_STRIDE - 1)
    wlen = win_k.shape[1]
    on, wk_new, wv_new = nsa_decode(proj3, small3, caches, page_table,
                                    win_k.reshape(B, wlen * NSA_KV, NSA_HD),
                                    win_v.reshape(B, wlen * NSA_KV, NSA_HD),
                                    W["w1cat"], W["w2s"], emat, mmat, L)
    h1, hn = out_proj(og.reshape(Mp, -1), on.reshape(Mp, -1), x2, W["w_out"], W["g_ffn"],
                      _pick_tile(Mp, (256, 128, 64, 32, 16, 8)))
    to_tm = lambda t: t.reshape(B, lp, Dm)[:, :L].transpose(1, 0, 2).reshape(L * B, Dm)
    h1_tm, hn_tm = to_tm(h1), to_tm(hn)
    prefix = jnp.pad(conv_s.transpose(1, 0, 2), ((0, 0), (0, 0), (0, W["Fp"] - W["F"])))
    hmid, tail = ffn_up_tm(hn_tm, W["w_up_a"], W["w_up_u"], W["conv_w"], W["conv_b"], prefix, L, 512)
    M = L * B
    h2 = ffn_down(hmid, W["w_down"], h1_tm, _pick_tile(M, (512, 256, 128, 64, 32, 16, 8)), 512)
    outs = tuple(_kv_rows(kv[:4], B, slice(0, L))
                 + [wk_new.reshape(win_k.shape), wv_new.reshape(win_v.shape),
                    gla_state, tail[:, :, :W["F"]].transpose(1, 0, 2)])
    return h2, outs


def kernel(x_prompt, x_sample, p_prompt, p_sample, cache_cmp_k, cache_cmp_v, cache_slc_k, cache_slc_v,
           page_table, state_win_k, state_win_v, state_gla, state_ffn_conv, attn_norm_g, w_in, gla_w_a2,
           gla_b_a, gla_norm_g, cmp_wk1, cmp_wk2, cmp_wv1, cmp_wv2, w_out, ffn_norm_g, ffn_w_up,
           ffn_conv_w, ffn_conv_b, ffn_w_down, ple_w_proj, ple_w_gate, ple_b_gate, final_norm_g):
    depth = w_in.shape[0]
    assert depth == 1, "layers are chained through HBM one at a time; only depth 1 is wired"
    Bp, Lp, Dm = x_prompt.shape
    Bs, Ls, _ = x_sample.shape
    W = _prep_weights(0, attn_norm_g, w_in, gla_w_a2, gla_b_a, gla_norm_g, cmp_wk1, cmp_wk2, cmp_wv1,
                      cmp_wv2, w_out, ffn_norm_g, ffn_w_up, ffn_conv_w, ffn_conv_b, ffn_w_down,
                      ple_w_proj, ple_w_gate, ple_b_gate)
    g_final = final_norm_g.reshape(1, Dm)

    Mp = Bp * Lp
    h2, outs_p = _prompt_layer(x_prompt.reshape(Mp, Dm), None, Bp, Lp, W)
    y_prompt = ple_final(h2, p_prompt[0].reshape(Mp, -1), W["w_gate"], W["b_gate"], W["w_proj"], g_final,
                         _pick_tile(Mp, (256, 128))).reshape(Bp, Lp, Dm)

    lp = -(-Ls // SUBLANE) * SUBLANE
    xs = jnp.pad(x_sample, ((0, 0), (0, lp - Ls), (0, 0)))
    n_pool, psz = cache_cmp_k.shape[1], cache_cmp_k.shape[2]
    caches = [c[0].reshape(n_pool, psz * NSA_KV, NSA_HD)
              for c in (cache_cmp_k, cache_cmp_v, cache_slc_k, cache_slc_v)]
    h2s, outs_s = _decode_layer(xs, None, Bs, Ls, lp, caches, page_table, state_win_k[0], state_win_v[0],
                                state_gla[0], state_ffn_conv[0], W)
    p_tm = p_sample[0].transpose(1, 0, 2).reshape(Ls * Bs, -1)
    Ms = Ls * Bs
    y_tm = ple_final(h2s, p_tm, W["w_gate"], W["b_gate"], W["w_proj"], g_final,
                     _pick_tile(Ms, (256, 128, 64, 32, 16, 8)))
    y_sample = y_tm.reshape(Ls, Bs, Dm).transpose(1, 0, 2)

    lead = lambda t: t[None]
    return (y_prompt, y_sample) + tuple(lead(t) for t in outs_p) + tuple(lead(t) for t in outs_s)
```

```python
import functools
import math

import numpy as np
import jax
import jax.numpy as jnp
from jax import lax
from jax.experimental import pallas as pl
from jax.experimental.pallas import tpu as pltpu

F32 = jnp.float32
BF16 = jnp.bfloat16

GLA_HEADS = 4
GLA_DK = 128
GLA_DV = 256
GLA_RANK = 16
GLA_TAU = 16.0
GLA_SUB = 16
NSA_HEADS = 8
NSA_KV = 2
NSA_REP = NSA_HEADS // NSA_KV
NSA_HD = 128
CMP_LEN = 32
CMP_STRIDE = 16
SLC_LEN = 64
N_SELECT = 16
WINDOW = 512
CONV_W = 3
EPS = 1e-6
NEG = -1e30
M_FLOOR = 0.1 * NEG
FORCED = 1e6

LANE = 128
SUBLANE = 8
KCHUNK = 128
CMP_SEG_ROWS = NSA_KV * CMP_STRIDE
CMP_PITCH = CMP_SEG_ROWS + SUBLANE
VMEM_LIMIT = 56 * 1024 * 1024

GQ0 = 0
GK0 = GQ0 + GLA_HEADS * GLA_DK
GV0 = GK0 + GLA_HEADS * GLA_DK
GR0 = GV0 + GLA_HEADS * GLA_DV
NQ0 = GR0 + GLA_HEADS * GLA_DV
KVW = NSA_KV * NSA_HD
KC0 = NQ0 + NSA_HEADS * NSA_HD
VC0 = KC0 + KVW
KS0 = VC0 + KVW
VS0 = KS0 + KVW
KW0 = VS0 + KVW
VW0 = KW0 + KVW
NMAIN = VW0 + KVW
GATE0 = GLA_RANK


def _cp(sem):
    return pltpu.CompilerParams(dimension_semantics=sem, vmem_limit_bytes=VMEM_LIMIT)


def _dot(a, b):
    return jnp.dot(a, b, preferred_element_type=F32)


def _dot_nt(a, b):
    return lax.dot_general(a, b, (((1,), (1,)), ((), ())), preferred_element_type=F32)


def _rms(x, g):
    return x * lax.rsqrt(jnp.mean(x * x, axis=-1, keepdims=True) + EPS) * g


def _norm_proj_kernel(x_ref, g_ref, wm_ref, ws_ref, om_ref, os_ref, kv_ref, xn_ref, *, j_kv0):
    j = pl.program_id(1)

    @pl.when(j == 0)
    def _():
        xn = _rms(x_ref[...], g_ref[...]).astype(BF16)
        xn_ref[...] = xn
        os_ref[...] = _dot(xn, ws_ref[...])

    tile = _dot(xn_ref[...], wm_ref[...])
    om_ref[...] = tile

    @pl.when(j >= j_kv0)
    def _():
        tm = tile.shape[0]
        for a in range(tile.shape[1] // KVW):
            for g in range(NSA_KV):
                c0 = a * KVW + g * NSA_HD
                kv_ref[a, pl.ds(g, tm, stride=NSA_KV), :] = tile[:, c0:c0 + NSA_HD]


def norm_proj(x2, g, w_main, w_small, tm, tn):
    M, Dm = x2.shape
    N = w_main.shape[1]
    assert KC0 % tn == 0 and tn % KVW == 0
    j_kv0 = KC0 // tn
    per_tile = tn // KVW
    n_kv = (N - KC0) // KVW
    return pl.pallas_call(
        functools.partial(_norm_proj_kernel, j_kv0=j_kv0),
        grid=(M // tm, N // tn),
        in_specs=[pl.BlockSpec((tm, Dm), lambda i, j: (i, 0)),
                  pl.BlockSpec((1, Dm), lambda i, j: (0, 0)),
                  pl.BlockSpec((Dm, tn), lambda i, j: (0, j)),
                  pl.BlockSpec((Dm, LANE), lambda i, j: (0, 0))],
        out_specs=[pl.BlockSpec((tm, tn), lambda i, j: (i, j)),
                   pl.BlockSpec((tm, LANE), lambda i, j: (i, 0)),
                   pl.BlockSpec((per_tile, NSA_KV * tm, NSA_HD),
                                lambda i, j: (jnp.maximum(j - j_kv0, 0), i, 0))],
        out_shape=[jax.ShapeDtypeStruct((M, N), F32), jax.ShapeDtypeStruct((M, LANE), F32),
                   jax.ShapeDtypeStruct((n_kv, NSA_KV * M, NSA_HD), F32)],
        scratch_shapes=[pltpu.VMEM((tm, Dm), BF16)],
        compiler_params=_cp(("parallel", "arbitrary")),
        name="norm_proj")(x2, g, w_main, w_small)


def _gla_kernel(q_ref, k_ref, v_ref, r_ref, sm_ref, wa_ref, ba_ref, gn_ref, s0_ref,
                o_ref, s_ref, st_ref, *, cb, sub, valid, nbb):
    c = pl.program_id(1)

    @pl.when(c == 0)
    def _():
        st_ref[...] = s0_ref[...]

    GH = GLA_HEADS
    H = nbb * GH
    side = lambda ref: ref[0] if nbb == 1 else jnp.concatenate([ref[s] for s in range(nbb)], axis=1)
    hk = lambda h: slice(h * GLA_DK, (h + 1) * GLA_DK)
    hv = lambda h: slice(h * GLA_DV, (h + 1) * GLA_DV)
    ri = lax.broadcasted_iota(jnp.int32, (cb, cb), 0)
    ci = lax.broadcasted_iota(jnp.int32, (cb, cb), 1)
    tri = jnp.where(ci <= ri, 1.0, 0.0)
    rowv = lax.broadcasted_iota(jnp.int32, (cb, 1), 0)
    q = side(q_ref) * (GLA_DK ** -0.5)
    k = side(k_ref)
    v = side(v_ref)
    pre = [_dot(sm_ref[s].astype(BF16), wa_ref[...]) + ba_ref[...] for s in range(nbb)]
    pre = pre[0] if nbb == 1 else jnp.concatenate(pre, axis=1)
    log_a = jax.nn.log_sigmoid(pre) / GLA_TAU
    if valid < cb:
        log_a = jnp.where(rowv < valid, log_a, 0.0)
        k = jnp.where(rowv < valid, k, 0.0)
        v = jnp.where(rowv < valid, v, 0.0)
    hi = log_a.astype(BF16).astype(F32)
    lo = (log_a - hi).astype(BF16).astype(F32)
    b = _dot(tri, hi) + _dot(tri, lo)
    b_last = b[cb - 1:cb, :]
    S = [st_ref[h // GH, h % GH] for h in range(H)]
    vb = v.astype(BF16)

    qe = (q * jnp.exp(b)).astype(BF16)
    o = [_dot(qe[:, hk(h)], S[h].astype(BF16)) for h in range(H)]
    a_rows = [[] for _ in range(H)]
    cc = lax.broadcasted_iota(jnp.int32, (sub, cb), 1)
    for blk in range(cb // sub):
        lo_r, hi_r = blk * sub, (blk + 1) * sub
        r_dec = jnp.zeros((1, H * GLA_DK), F32) if blk == 0 else b[lo_r - 1:lo_r, :]
        q_i = q[lo_r:hi_r] * jnp.exp(b[lo_r:hi_r] - r_dec)
        k_i = k * jnp.exp(jnp.where(rowv < hi_r, r_dec - b, 0.0))
        rr = lax.broadcasted_iota(jnp.int32, (sub, cb), 0) + lo_r
        for h in range(H):
            a = _dot_nt(q_i[:, hk(h)], k_i[:, hk(h)])
            a_rows[h].append(jnp.where(cc <= rr, a, 0.0))
    for h in range(H):
        a = a_rows[h][0] if len(a_rows[h]) == 1 else jnp.concatenate(a_rows[h], axis=0)
        o[h] = o[h] + _dot(a, v[:, hv(h)])

    kh = k * jnp.exp(b_last - b)
    dec = jnp.exp(b_last)
    if cb < LANE:
        kh = jnp.concatenate([kh, jnp.zeros((LANE - cb, H * GLA_DK), F32)], axis=0)
        vpad = jnp.concatenate([vb, jnp.zeros((LANE - cb, H * GLA_DV), BF16)], axis=0)
    else:
        vpad = vb
    for h in range(H):
        dec_col = jnp.transpose(jnp.broadcast_to(dec[:, hk(h)], (GLA_DK, GLA_DK)))[:, 0:1]
        s_new = dec_col * S[h] + _dot(jnp.transpose(kh[:, hk(h)]).astype(BF16), vpad[:, hv(h)])
        st_ref[h // GH, h % GH] = s_new
        s_ref[h // GH, h % GH] = s_new

    r = side(r_ref)
    gate = r * jax.nn.sigmoid(r)
    for h in range(H):
        o_ref[h // GH, :, hv(h % GH)] = (_rms(o[h], gn_ref[...]) * gate[:, hv(h)]).astype(o_ref.dtype)


def gla(proj3, small3, wa_pad, b_a, gnorm, s0, cb, sub, valid, nbb):
    B, L, _ = proj3.shape
    H = GLA_HEADS
    qw, vw = H * GLA_DK, H * GLA_DV
    kern = functools.partial(_gla_kernel, cb=cb, sub=sub, valid=valid, nbb=nbb)
    return pl.pallas_call(
        kern,
        grid=(B // nbb, L // cb),
        in_specs=[pl.BlockSpec((nbb, cb, qw), lambda b, c: (b, c, GQ0 // qw)),
                  pl.BlockSpec((nbb, cb, qw), lambda b, c: (b, c, GK0 // qw)),
                  pl.BlockSpec((nbb, cb, vw), lambda b, c: (b, c, GV0 // vw)),
                  pl.BlockSpec((nbb, cb, vw), lambda b, c: (b, c, GR0 // vw)),
                  pl.BlockSpec((nbb, cb, LANE), lambda b, c: (b, c, 0)),
                  pl.BlockSpec((LANE, qw), lambda b, c: (0, 0)),
                  pl.BlockSpec((1, qw), lambda b, c: (0, 0)),
                  pl.BlockSpec((1, GLA_DV), lambda b, c: (0, 0)),
                  pl.BlockSpec((nbb, H, GLA_DK, GLA_DV), lambda b, c: (b, 0, 0, 0))],
        out_specs=[pl.BlockSpec((nbb, cb, vw), lambda b, c: (b, c, 0)),
                   pl.BlockSpec((nbb, H, GLA_DK, GLA_DV), lambda b, c: (b, 0, 0, 0))],
        out_shape=[jax.ShapeDtypeStruct((B, L, vw), BF16),
                   jax.ShapeDtypeStruct((B, H, GLA_DK, GLA_DV), F32)],
        scratch_shapes=[pltpu.VMEM((nbb, H, GLA_DK, GLA_DV), F32)],
        compiler_params=_cp(("parallel", "arbitrary")),
        name="gla")(proj3, proj3, proj3, proj3, small3, wa_pad, b_a, gnorm, s0)


def _compress_group(load_rows, w1_ref, w2, nseg):
    acc = jnp.zeros((nseg, 2 * NSA_HD), F32)
    for j in range(CMP_STRIDE):
        acc = acc + _dot(load_rows(j).astype(BF16), w1_ref[j])
    first = acc[:, :NSA_HD]
    second = acc[:, NSA_HD:]
    h = jax.nn.gelu(first + pltpu.roll(second, nseg - 1, axis=0))
    return _dot(h.astype(BF16), w2)


def _compress_groups(loads, w1_refs, w2s, nrow):
    n = len(loads)
    acc = [jnp.zeros((nrow, 2 * NSA_HD), F32) for _ in range(n)]
    for j in range(CMP_STRIDE):
        for i in range(n):
            acc[i] = acc[i] + _dot(loads[i](j).astype(BF16), w1_refs[i][j])
    hs = [jax.nn.gelu(a[:, :NSA_HD] + pltpu.roll(a[:, NSA_HD:], nrow - 1, axis=0)) for a in acc]
    return [_dot(h.astype(BF16), w2) for h, w2 in zip(hs, w2s)]


def _cmp_branch(q4, ck, cv, qpos4, nb):
    s = _dot_nt(q4, ck.astype(BF16))
    n = lax.broadcasted_iota(jnp.int32, s.shape, 1)
    mask = jnp.where(n * CMP_STRIDE + (CMP_LEN - 1) <= qpos4, n, nb) < nb
    s = jnp.where(mask, s, NEG)
    m = jnp.max(s, axis=-1, keepdims=True)
    e = jnp.where(mask, jnp.exp(s - m), 0.0)
    p = e / jnp.maximum(jnp.sum(e, axis=-1, keepdims=True), 1e-30)
    return p, _dot(p.astype(BF16), cv.astype(BF16))


def _select(pg, mmat, qpos_tok, nslc, nsel):
    hi = pg.astype(BF16)
    lo = (pg - hi.astype(F32)).astype(BF16)
    imp = _dot(hi, mmat) + _dot(lo, mmat)
    blk = lax.broadcasted_iota(jnp.int32, pg.shape, 1)
    cur = qpos_tok // SLC_LEN
    forced = jnp.where(blk == 0, 1, 0) + jnp.where(blk == cur, 1, 0) + jnp.where(blk == cur - 1, 1, 0)
    score = jnp.where(blk <= cur, jnp.where(forced > 0, FORCED, imp), -1.0)
    score = jnp.where(blk < nslc, score, -2.0)
    rank = jnp.zeros(pg.shape, F32)
    for j in range(nslc):
        sj = score[:, j:j + 1]
        tie = jnp.where(blk > j, sj, NEG)
        rank = rank + jnp.where(sj > score, 1.0, 0.0) + jnp.where(tie == score, 1.0, 0.0)
    return jnp.where(rank < nsel, jnp.where(blk < nslc, 1.0, 0.0), 0.0)


def _flash_init(m_ref, l_ref, acc_ref):
    m_ref[...] = jnp.full(m_ref.shape, NEG, F32)
    l_ref[...] = jnp.zeros(l_ref.shape, F32)
    acc_ref[...] = jnp.zeros(acc_ref.shape, F32)


def _flash_step(q4, k, v, mask, m_ref, l_ref, acc_ref):
    s = jnp.where(mask, _dot_nt(q4, k), NEG)
    m_old = m_ref[...]
    m_new = jnp.maximum(m_old, jnp.max(s, axis=-1, keepdims=True))
    alpha = jnp.exp(m_old - m_new)
    p = jnp.where(mask, jnp.exp(s - m_new), 0.0)
    l_ref[...] = alpha * l_ref[...] + jnp.sum(p, axis=-1, keepdims=True)
    acc_ref[...] = alpha * acc_ref[...] + _dot(p.astype(BF16), v)
    m_ref[...] = m_new


def _flash_out(l_ref, acc_ref):
    return acc_ref[...] / jnp.maximum(l_ref[...], 1e-30)


def _tile_rows(x, reps):
    return jnp.concatenate([x] * reps, axis=0)


def _attend(q4, keys, vals, masks):
    s = [jnp.where(mk, _dot_nt(q4, k), NEG) for k, mk in zip(keys, masks)]
    m = s[0].max(axis=-1, keepdims=True)
    for si in s[1:]:
        m = jnp.maximum(m, si.max(axis=-1, keepdims=True))
    e = [jnp.where(mk, jnp.exp(si - m), 0.0) for si, mk in zip(s, masks)]
    l = sum(ei.sum(axis=-1, keepdims=True) for ei in e)
    o = sum(_dot(ei.astype(BF16), v) for ei, v in zip(e, vals))
    return o / jnp.maximum(l, 1e-30)


def _compress_kernel(x_ref, w1_ref, w2_ref, o_ref, *, nseg):
    load = lambda j: x_ref[0, pl.ds(j, nseg, stride=CMP_STRIDE), :]
    o_ref[0, 0, 0] = _compress_group(load, w1_ref.at[0], w2_ref[0].astype(BF16), nseg)


def compress(proj3, w1cat, w2s):
    B, L, _ = proj3.shape
    nseg = L // CMP_STRIDE
    G = NSA_KV
    return pl.pallas_call(
        functools.partial(_compress_kernel, nseg=nseg),
        grid=(B, 2, G),
        in_specs=[pl.BlockSpec((1, L, NSA_HD), lambda b, w, g: (b, 0, KC0 // NSA_HD + w * G + g)),
                  pl.BlockSpec((1, CMP_STRIDE, NSA_HD, 2 * NSA_HD), lambda b, w, g: (w, 0, 0, 0)),
                  pl.BlockSpec((1, NSA_HD, NSA_HD), lambda b, w, g: (w, 0, 0))],
        out_specs=pl.BlockSpec((1, 1, 1, nseg, NSA_HD), lambda b, w, g: (w, b, g, 0, 0)),
        out_shape=jax.ShapeDtypeStruct((2, B, G, nseg, NSA_HD), F32),
        compiler_params=_cp(("parallel", "parallel", "parallel")),
        name="compress")(proj3, w1cat, w2s)


def _select_t(pg_t, mm_t, qpos_row, nslc, nsel):
    nsp = -(-nslc // SUBLANE) * SUBLANE
    hi = pg_t.astype(BF16)
    lo = (pg_t - hi.astype(F32)).astype(BF16)
    imp = (_dot(mm_t, hi) + _dot(mm_t, lo))[:nsp]
    blk = lax.broadcasted_iota(jnp.int32, imp.shape, 0)
    cur = qpos_row // SLC_LEN
    forced = jnp.where(blk == 0, 1, 0) + jnp.where(blk == cur, 1, 0) + jnp.where(blk == cur - 1, 1, 0)
    score = jnp.where(blk <= cur, jnp.where(forced > 0, FORCED, imp), -1.0)
    score = jnp.where(blk < nslc, score, -2.0)
    rank = jnp.zeros(imp.shape, F32)
    for j in range(nslc):
        sj = score[j:j + 1, :]
        tie = jnp.where(blk > j, sj, NEG)
        rank = rank + jnp.where(sj > score, 1.0, 0.0) + jnp.where(tie == score, 1.0, 0.0)
    return jnp.where(rank < nsel, jnp.where(blk < nslc, 1.0, 0.0), 0.0)


def _flash_step_t(q_t, k, v_t, bias, m_ref, l_ref, acc_ref):
    s = _dot(k, q_t) + bias
    m_old = m_ref[...]
    m_new = jnp.maximum(m_old, jnp.max(s, axis=0, keepdims=True))
    alpha = jnp.exp(m_old - m_new)
    p = jnp.exp(s - m_new)
    l_ref[...] = alpha * l_ref[...] + jnp.sum(p, axis=0, keepdims=True)
    acc_ref[...] = alpha * acc_ref[...] + _dot(v_t, p.astype(BF16))
    m_ref[...] = m_new


def _nsa_prompt_kernel(q_ref, sm_ref, ck_ref, cv_ref, ks_ref, vs_ref, kw_ref, vw_ref, mm_ref,
                       o_ref, vst_ref, vwt_ref, sel_ref, m_ref, l_ref, acc_ref, mw_ref, lw_ref, accw_ref,
                       *, tq, kstep, nslc, nsel, nb):
    g = pl.program_id(1)
    qb = pl.program_id(2)
    R = NSA_REP
    T = R * tq
    L = ks_ref.shape[1]
    scale = NSA_HD ** -0.5

    @pl.when(qb == 0)
    def _():
        for c in range(L // KCHUNK):
            cs = slice(c * KCHUNK, (c + 1) * KCHUNK)
            vst_ref[:, cs] = jnp.transpose(vs_ref[0, cs, :]).astype(BF16)
            vwt_ref[:, cs] = jnp.transpose(vw_ref[0, cs, :]).astype(BF16)

    q_t = (jnp.concatenate([jnp.transpose(q_ref[0, :, r * NSA_HD:(r + 1) * NSA_HD]) for r in range(R)],
                           axis=1) * scale).astype(BF16)
    qpos_row = qb * tq + lax.broadcasted_iota(jnp.int32, (1, tq), 1)
    qpos4 = jnp.concatenate([qpos_row] * R, axis=1)

    s = _dot(ck_ref[0, 0, 0].astype(BF16), q_t)
    n = lax.broadcasted_iota(jnp.int32, s.shape, 0)
    mask = jnp.where(n * CMP_STRIDE + (CMP_LEN - 1) <= qpos4, n, nb) < nb
    s = jnp.where(mask, s, NEG)
    e = jnp.where(mask, jnp.exp(s - jnp.max(s, axis=0, keepdims=True)), 0.0)
    p = e / jnp.maximum(jnp.sum(e, axis=0, keepdims=True), 1e-30)
    o_c = _dot(jnp.transpose(cv_ref[0, 0, 0]).astype(BF16), p.astype(BF16))
    pg = p[:, 0:tq]
    for r in range(1, R):
        pg = pg + p[:, r * tq:(r + 1) * tq]
    sel_ref[...] = (_select_t(pg, mm_ref[...], qpos_row, nslc, nsel) - 1.0) * (-NEG)

    key = lax.broadcasted_iota(jnp.int32, (kstep, T), 0)
    hi_step = (qb * tq + tq - 1) // kstep + 1
    lo_step = jnp.maximum(qb * tq - (WINDOW - 1), 0) // kstep

    sel_state = (m_ref, l_ref, acc_ref)
    win_state = (mw_ref, lw_ref, accw_ref)
    for m_r, l_r, acc_r in (sel_state, win_state):
        m_r[...] = jnp.full(m_r.shape, M_FLOOR, F32)
        l_r[...] = jnp.zeros(l_r.shape, F32)
        acc_r[...] = jnp.zeros(acc_r.shape, F32)

    def sel_step(c, causal):
        off = pl.multiple_of(c * kstep, kstep)
        k = ks_ref[0, pl.ds(off, kstep), :].astype(BF16)
        v_t = vst_ref[:, pl.ds(off, kstep)]
        blocks = [jnp.broadcast_to(sel_ref[pl.ds((kstep // SLC_LEN) * c + i, 1), :], (SLC_LEN, tq))
                  for i in range(kstep // SLC_LEN)]
        bias = jnp.concatenate(blocks, axis=0)
        bias = jnp.concatenate([bias] * R, axis=1)
        if causal:
            bias = jnp.where(off + key <= qpos4, bias, NEG)
        _flash_step_t(q_t, k, v_t, bias, *sel_state)

    def win_step(c, causal):
        off = pl.multiple_of(c * kstep, kstep)
        k = kw_ref[0, pl.ds(off, kstep), :].astype(BF16)
        v_t = vwt_ref[:, pl.ds(off, kstep)]
        kpos = off + key
        if causal:
            bias = jnp.where(kpos <= qpos4, 0.0, NEG)
        else:
            bias = jnp.where(kpos > qpos4 - WINDOW, 0.0, NEG)
        _flash_step_t(q_t, k, v_t, bias, *win_state)

    def body_sel(c, carry):
        sel_step(c, False)
        return carry

    def body_both(c, carry):
        sel_step(c, False)
        win_step(c, False)
        return carry

    last = hi_step - 1
    lax.fori_loop(0, jnp.minimum(lo_step, last), body_sel, 0)
    lax.fori_loop(lo_step, last, body_both, 0)
    sel_step(last, True)
    win_step(last, True)
    o_s = acc_ref[...] / jnp.maximum(l_ref[...], 1e-30)
    o_w = accw_ref[...] / jnp.maximum(lw_ref[...], 1e-30)

    g_t = jnp.transpose(jax.nn.sigmoid(sm_ref[0]))
    for r in range(R):
        def gate(br, r=r):
            c0 = GATE0 + br * NSA_HEADS + r
            return jnp.where(g == 0, g_t[c0:c0 + 1, :], g_t[c0 + R:c0 + R + 1, :])
        cs = slice(r * tq, (r + 1) * tq)
        o = gate(0) * o_c[:, cs] + gate(1) * o_s[:, cs] + gate(2) * o_w[:, cs]
        o_ref[0, :, r * NSA_HD:(r + 1) * NSA_HD] = jnp.transpose(o).astype(o_ref.dtype)


def nsa_prompt(proj3, small3, cmp, mmat_t):
    B, L, _ = proj3.shape
    tq = _pick_tile(L, (2 * KCHUNK, KCHUNK))
    nseg = L // CMP_STRIDE
    nb = nseg - 1
    nslc = -(-L // SLC_LEN)
    nsel = min(N_SELECT, nslc)
    nsp = -(-nslc // SUBLANE) * SUBLANE
    G, R = NSA_KV, NSA_REP
    kstep = _pick_tile(L, (2 * KCHUNK, KCHUNK))
    assert nseg == LANE and L % KCHUNK == 0 and kstep % SLC_LEN == 0
    kv_spec = lambda c0: pl.BlockSpec((1, L, NSA_HD), lambda b, g, qb: (b, 0, c0 // NSA_HD + g))
    kern = functools.partial(_nsa_prompt_kernel, tq=tq, kstep=kstep, nslc=nslc, nsel=nsel, nb=nb)
    return pl.pallas_call(
        kern,
        grid=(B, G, L // tq),
        in_specs=[pl.BlockSpec((1, tq, R * NSA_HD), lambda b, g, qb: (b, qb, NQ0 // (R * NSA_HD) + g)),
                  pl.BlockSpec((1, tq, LANE), lambda b, g, qb: (b, qb, 0)),
                  pl.BlockSpec((1, 1, 1, nseg, NSA_HD), lambda b, g, qb: (0, b, g, 0, 0)),
                  pl.BlockSpec((1, 1, 1, nseg, NSA_HD), lambda b, g, qb: (1, b, g, 0, 0)),
                  kv_spec(KS0), kv_spec(VS0), kv_spec(KW0), kv_spec(VW0),
                  pl.BlockSpec((LANE, LANE), lambda b, g, qb: (0, 0))],
        out_specs=pl.BlockSpec((1, tq, R * NSA_HD), lambda b, g, qb: (b, qb, g)),
        out_shape=jax.ShapeDtypeStruct((B, L, NSA_HEADS * NSA_HD), BF16),
        scratch_shapes=[pltpu.VMEM((NSA_HD, L), BF16),
                        pltpu.VMEM((NSA_HD, L), BF16),
                        pltpu.VMEM((nsp, tq), F32)]
                       + 2 * [pltpu.VMEM((1, R * tq), F32),
                              pltpu.VMEM((1, R * tq), F32),
                              pltpu.VMEM((NSA_HD, R * tq), F32)],
        compiler_params=_cp(("parallel", "parallel", "arbitrary")),
        name="nsa_prompt")(proj3, small3, cmp, cmp, proj3, proj3, proj3, proj3, mmat_t)


def _nsa_decode_kernel(pt_ref, q_ref, sm_ref, ksn_ref, vsn_ref, kwn_ref, vwn_ref, wk_ref, wv_ref,
                       w1_ref, w2_ref, e_ref, mm_ref, c0_hbm, c1_hbm, c2_hbm, c3_hbm,
                       o_ref, wko_ref, wvo_ref, cbuf_ref, sbuf_ref, sem_ref,
                       *, ns, lp, lreal, plen, rows_pp, n_pages, wlen, nslc, nsel, nb):
    b = pl.program_id(0)
    slot = b % 2
    caches = (c0_hbm, c1_hbm, c2_hbm, c3_hbm)
    G = NSA_KV
    R = NSA_REP
    T = R * lp
    segs_pp = rows_pp // CMP_SEG_ROWS

    def copies(step, sl):
        out = []
        for q in range(ns):
            for p in range(n_pages):
                page = pt_ref[step * ns + q, p]
                for w in range(2):
                    for s in range(segs_pp):
                        out.append(pltpu.make_async_copy(
                            caches[w].at[page, pl.ds(s * CMP_SEG_ROWS, CMP_SEG_ROWS)],
                            cbuf_ref.at[sl, q, w, pl.ds((p * segs_pp + s) * CMP_PITCH, CMP_SEG_ROWS)],
                            sem_ref.at[sl, w]))
                    out.append(pltpu.make_async_copy(
                        caches[2 + w].at[page], sbuf_ref.at[sl, q, w, pl.ds(p * rows_pp, rows_pp)],
                        sem_ref.at[sl, 2 + w]))
        return out

    @pl.when(b == 0)
    def _():
        for cp in copies(0, 0):
            cp.start()

    @pl.when(b + 1 < pl.num_programs(0))
    def _():
        for cp in copies(b + 1, 1 - slot):
            cp.start()

    for w in range(len(caches)):
        for q in range(ns):
            whole = sbuf_ref.at[slot, q, 0]
            pltpu.make_async_copy(whole, whole, sem_ref.at[slot, w]).wait()

    scale = NSA_HD ** -0.5
    nseg = plen // CMP_STRIDE
    tok = lax.broadcasted_iota(jnp.int32, (lp, 1), 0)
    qpos_tok = plen + tok
    qpos4 = _tile_rows(qpos_tok, R)
    lane_n = lax.broadcasted_iota(jnp.int32, (T, KCHUNK), 1)
    new_pos = jnp.where(lane_n < lreal, plen + lane_n, qpos4 + 1)
    past_pos = lax.broadcasted_iota(jnp.int32, (T, plen), 1)
    win_pos = plen - wlen + lax.broadcasted_iota(jnp.int32, (T, wlen), 1)
    win_pos = jnp.where(win_pos > qpos4 - WINDOW, win_pos, qpos4 + 1)
    win_pos = jnp.where(win_pos >= 0, win_pos, qpos4 + 1)
    new_win_pos = jnp.where(new_pos > qpos4 - WINDOW, new_pos, qpos4 + 1)
    zpad_f = jnp.zeros((KCHUNK - lp, NSA_HD), F32)
    pad_new = lambda ref, q, ls: jnp.concatenate([ref[q, :, ls], zpad_f], axis=0).astype(BF16)
    gls = lambda g: slice(g * NSA_HD, (g + 1) * NSA_HD)
    pairs = [(q, g) for q in range(ns) for g in range(G)]
    q4 = [(jnp.concatenate([q_ref[q, :, (g * R + r) * NSA_HD:(g * R + r + 1) * NSA_HD]
                            for r in range(R)], axis=0) * scale).astype(BF16) for q, g in pairs]

    def seg_rows(w, j):
        return jnp.concatenate([cbuf_ref[slot, q, w, pl.ds(G * j + g, nseg, stride=CMP_PITCH), :]
                                for q, g in pairs], axis=0)
    cmp = _compress_groups([functools.partial(seg_rows, w) for w in range(2)],
                           [w1_ref.at[w] for w in range(2)],
                           [w2_ref[w].astype(BF16) for w in range(2)], len(pairs) * nseg)
    prow = lambda i: slice(i * nseg, (i + 1) * nseg)
    branch = [_cmp_branch(q4[i], cmp[0][prow(i)], cmp[1][prow(i)], qpos4, nb) for i in range(len(pairs))]
    pgs = []
    for i in range(len(pairs)):
        p = branch[i][0]
        pg = p[0:lp]
        for r in range(1, R):
            pg = pg + p[r * lp:(r + 1) * lp]
        pgs.append(pg)
    sel = _select(jnp.concatenate(pgs, axis=0), mm_ref[...], _tile_rows(qpos_tok, len(pairs)), nslc, nsel)
    selm_all = _dot(sel.astype(BF16), e_ref[...])

    o_s, o_w = [], []
    for i, (q, g) in enumerate(pairs):
        selm = _tile_rows(selm_all[i * lp:(i + 1) * lp], R)
        k_past = sbuf_ref[slot, q, 0, pl.ds(g, plen, stride=G), :].astype(BF16)
        v_past = sbuf_ref[slot, q, 1, pl.ds(g, plen, stride=G), :].astype(BF16)
        m_past = jnp.where(selm[:, :plen] > 0.5, past_pos, qpos4 + 1) <= qpos4
        m_new = jnp.where(selm[:, plen:] > 0.5, new_pos, qpos4 + 1) <= qpos4
        o_s.append(_attend(q4[i], [k_past, pad_new(ksn_ref, q, gls(g))],
                           [v_past, pad_new(vsn_ref, q, gls(g))], [m_past, m_new]))
    for i, (q, g) in enumerate(pairs):
        k_win = wk_ref[q, pl.ds(g, wlen, stride=G), :].astype(BF16)
        v_win = wv_ref[q, pl.ds(g, wlen, stride=G), :].astype(BF16)
        o_w.append(_attend(q4[i], [k_win, pad_new(kwn_ref, q, gls(g))],
                           [v_win, pad_new(vwn_ref, q, gls(g))], [win_pos <= qpos4, new_win_pos <= qpos4]))

    for i, (q, g) in enumerate(pairs):
        gts = jax.nn.sigmoid(sm_ref[q])
        for r in range(R):
            c0 = GATE0 + g * R + r
            rs = slice(r * lp, (r + 1) * lp)
            o = (gts[:, c0:c0 + 1] * branch[i][1][rs]
                 + gts[:, c0 + NSA_HEADS:c0 + NSA_HEADS + 1] * o_s[i][rs]
                 + gts[:, c0 + 2 * NSA_HEADS:c0 + 2 * NSA_HEADS + 1] * o_w[i][rs])
            hq = (g * R + r) * NSA_HD
            o_ref[q, :, hq:hq + NSA_HD] = o.astype(o_ref.dtype)

    wrows = wlen * G
    shift = lreal * G
    for q in range(ns):
        for src, new, dst in ((wk_ref, kwn_ref, wko_ref), (wv_ref, vwn_ref, wvo_ref)):
            dst[q, 0:wrows - shift, :] = src[q, shift:wrows, :]
            for t in range(lreal):
                for g in range(G):
                    row = wrows - shift + t * G + g
                    dst[q, row:row + 1, :] = new[q, t:t + 1, g * NSA_HD:(g + 1) * NSA_HD]


def nsa_decode(proj3, small3, caches, page_table, win_k, win_v, w1cat, w2s, emat, mmat, lreal, ns):
    B, lp, _ = proj3.shape
    G = NSA_KV
    n_pages = page_table.shape[1]
    rows_pp = caches[0].shape[1]
    plen = n_pages * rows_pp // G
    wlen = win_k.shape[1] // G
    assert plen % CMP_STRIDE == 0 and lreal < CMP_STRIDE and lreal <= lp and B % ns == 0
    nb = (plen + lreal) // CMP_STRIDE - 1
    nslc = -(-(plen + lreal) // SLC_LEN)
    nsel = min(N_SELECT, nslc)
    new_spec = lambda c0: pl.BlockSpec((ns, lp, KVW), lambda b, pt: (b, 0, c0 // KVW))
    const = lambda shape: pl.BlockSpec(shape, lambda b, pt: (0,) * len(shape))
    hbm = pl.BlockSpec(memory_space=pl.ANY)
    kern = functools.partial(_nsa_decode_kernel, ns=ns, lp=lp, lreal=lreal, plen=plen, rows_pp=rows_pp,
                             n_pages=n_pages, wlen=wlen, nslc=nslc, nsel=nsel, nb=nb)
    grid_spec = pltpu.PrefetchScalarGridSpec(
        num_scalar_prefetch=1,
        grid=(B // ns,),
        in_specs=[pl.BlockSpec((ns, lp, NSA_HEADS * NSA_HD), lambda b, pt: (b, 0, NQ0 // (NSA_HEADS * NSA_HD))),
                  pl.BlockSpec((ns, lp, LANE), lambda b, pt: (b, 0, 0)),
                  new_spec(KS0), new_spec(VS0), new_spec(KW0), new_spec(VW0),
                  pl.BlockSpec((ns, wlen * G, NSA_HD), lambda b, pt: (b, 0, 0)),
                  pl.BlockSpec((ns, wlen * G, NSA_HD), lambda b, pt: (b, 0, 0)),
                  const((2, CMP_STRIDE, NSA_HD, 2 * NSA_HD)),
                  const((2, NSA_HD, NSA_HD)),
                  const((LANE, plen + KCHUNK)),
                  const((LANE, LANE)),
                  hbm, hbm, hbm, hbm],
        out_specs=[pl.BlockSpec((ns, lp, NSA_HEADS * NSA_HD), lambda b, pt: (b, 0, 0)),
                   pl.BlockSpec((ns, wlen * G, NSA_HD), lambda b, pt: (b, 0, 0)),
                   pl.BlockSpec((ns, wlen * G, NSA_HD), lambda b, pt: (b, 0, 0))],
        scratch_shapes=[pltpu.VMEM((2, ns, 2, n_pages * rows_pp // CMP_SEG_ROWS * CMP_PITCH, NSA_HD), F32),
                        pltpu.VMEM((2, ns, 2, n_pages * rows_pp, NSA_HD), F32),
                        pltpu.SemaphoreType.DMA((2, len(caches)))])
    return pl.pallas_call(
        kern,
        grid_spec=grid_spec,
        out_shape=[jax.ShapeDtypeStruct((B, lp, NSA_HEADS * NSA_HD), BF16),
                   jax.ShapeDtypeStruct(win_k.shape, F32), jax.ShapeDtypeStruct(win_v.shape, F32)],
        compiler_params=_cp(("arbitrary",)),
        name="nsa_decode")(page_table, proj3, small3, proj3, proj3, proj3, proj3, win_k, win_v,
                           w1cat, w2s, emat, mmat, caches[0], caches[1], caches[2], caches[3])


def _out_proj_kernel(og_ref, on_ref, x_ref, w0_ref, w1_ref, g_ref, h_ref, hn_ref):
    h = x_ref[...] + _dot(og_ref[...], w0_ref[...]) + _dot(on_ref[...], w1_ref[...])
    h_ref[...] = h
    hn_ref[...] = _rms(h, g_ref[...]).astype(BF16)


def out_proj(og, on, x2, w_out, g, tm):
    M, Dm = x2.shape
    Kh = og.shape[1]
    return pl.pallas_call(
        _out_proj_kernel,
        grid=(M // tm,),
        in_specs=[pl.BlockSpec((tm, Kh), lambda i: (i, 0)),
                  pl.BlockSpec((tm, Kh), lambda i: (i, 0)),
                  pl.BlockSpec((tm, Dm), lambda i: (i, 0)),
                  pl.BlockSpec((Kh, Dm), lambda i: (0, 0)),
                  pl.BlockSpec((Kh, Dm), lambda i: (1, 0)),
                  pl.BlockSpec((1, Dm), lambda i: (0, 0))],
        out_specs=[pl.BlockSpec((tm, Dm), lambda i: (i, 0)),
                   pl.BlockSpec((tm, Dm), lambda i: (i, 0))],
        out_shape=[jax.ShapeDtypeStruct((M, Dm), F32), jax.ShapeDtypeStruct((M, Dm), BF16)],
        compiler_params=_cp(("parallel",)),
        name="out_proj")(og, on, x2, w_out, w_out, g)


def _ffn_up_seq_kernel(hn_ref, wa_ref, wu_ref, cw_ref, cb_ref, pre_ref, hm_ref, tail_ref, aext_ref,
                       *, tm, tiles_per_seq):
    m = pl.program_id(1)

    @pl.when(m % tiles_per_seq == 0)
    def _():
        aext_ref[0:SUBLANE, :] = pre_ref[0]

    hn = hn_ref[...]
    a = _dot(hn, wa_ref[...])
    u = _dot(hn, wu_ref[...])
    aext_ref[SUBLANE:SUBLANE + tm, :] = a
    p1 = aext_ref[pl.ds(SUBLANE - 1, tm), :]
    p2 = aext_ref[pl.ds(SUBLANE - 2, tm), :]
    c = p2 * cw_ref[0:1, :] + p1 * cw_ref[1:2, :] + a * cw_ref[2:3, :] + cb_ref[...]
    hm_ref[...] = (jax.nn.gelu(c) * u).astype(hm_ref.dtype)
    tail = a[tm - SUBLANE:tm, :]
    tail_ref[0] = tail
    aext_ref[0:SUBLANE, :] = tail


def ffn_up_seq(hn, w_a, w_u, conv_w, conv_b, prefix, seq_len, tm, tf):
    M, Dm = hn.shape
    Fp = conv_w.shape[1]
    nf = Fp // tf
    B = M // seq_len
    tps = seq_len // tm
    kern = functools.partial(_ffn_up_seq_kernel, tm=tm, tiles_per_seq=tps)
    return pl.pallas_call(
        kern,
        grid=(nf, M // tm),
        in_specs=[pl.BlockSpec((tm, Dm), lambda f, m: (m, 0)),
                  pl.BlockSpec((Dm, tf), lambda f, m: (0, f)),
                  pl.BlockSpec((Dm, tf), lambda f, m: (0, f)),
                  pl.BlockSpec((SUBLANE, tf), lambda f, m: (0, f)),
                  pl.BlockSpec((1, tf), lambda f, m: (0, f)),
                  pl.BlockSpec((1, SUBLANE, tf), lambda f, m: (m // tps, 0, f))],
        out_specs=[pl.BlockSpec((tm, tf), lambda f, m: (m, f)),
                   pl.BlockSpec((1, SUBLANE, tf), lambda f, m: (m // tps, 0, f))],
        out_shape=[jax.ShapeDtypeStruct((M, Fp), BF16), jax.ShapeDtypeStruct((B, SUBLANE, Fp), F32)],
        scratch_shapes=[pltpu.VMEM((SUBLANE + tm, tf), F32)],
        compiler_params=_cp(("parallel", "arbitrary")),
        name="ffn_up_seq")(hn, w_a, w_u, conv_w, conv_b, prefix)


def _ffn_up_tm_kernel(hn_ref, wa_ref, wu_ref, cw_ref, cb_ref, pre_ref, hm_ref, tail_ref, *, steps, nb):
    hn = hn_ref[...]
    a = _dot(hn, wa_ref[...])
    u = _dot(hn, wu_ref[...])
    slabs = [pre_ref[i] for i in range(CONV_W - 1)] + [a[t * nb:(t + 1) * nb] for t in range(steps)]
    for t in range(steps):
        c = (slabs[t] * cw_ref[0:1, :] + slabs[t + 1] * cw_ref[1:2, :] + slabs[t + 2] * cw_ref[2:3, :]
             + cb_ref[...])
        hm_ref[t * nb:(t + 1) * nb, :] = (jax.nn.gelu(c) * u[t * nb:(t + 1) * nb]).astype(hm_ref.dtype)
    for i in range(CONV_W - 1):
        tail_ref[i] = slabs[steps + i]


def ffn_up_tm(hn, w_a, w_u, conv_w, conv_b, prefix, steps, tf):
    M, Dm = hn.shape
    Fp = conv_w.shape[1]
    nf = Fp // tf
    nb = M // steps
    kern = functools.partial(_ffn_up_tm_kernel, steps=steps, nb=nb)
    return pl.pallas_call(
        kern,
        grid=(nf,),
        in_specs=[pl.BlockSpec((M, Dm), lambda f: (0, 0)),
                  pl.BlockSpec((Dm, tf), lambda f: (0, f)),
                  pl.BlockSpec((Dm, tf), lambda f: (0, f)),
                  pl.BlockSpec((SUBLANE, tf), lambda f: (0, f)),
                  pl.BlockSpec((1, tf), lambda f: (0, f)),
                  pl.BlockSpec((CONV_W - 1, nb, tf), lambda f: (0, 0, f))],
        out_specs=[pl.BlockSpec((M, tf), lambda f: (0, f)),
                   pl.BlockSpec((CONV_W - 1, nb, tf), lambda f: (0, 0, f))],
        out_shape=[jax.ShapeDtypeStruct((M, Fp), BF16), jax.ShapeDtypeStruct((CONV_W - 1, nb, Fp), F32)],
        compiler_params=_cp(("parallel",)),
        name="ffn_up_tm")(hn, w_a, w_u, conv_w, conv_b, prefix)


def _ffn_down_kernel(hm_ref, w_ref, h_ref, o_ref):
    o_ref[...] = h_ref[...] + _dot(hm_ref[...], w_ref[...])


def ffn_down(hmid, w_down, h1, tm, tn):
    M = hmid.shape[0]
    F, Dm = w_down.shape
    assert F % LANE == 0 and F <= hmid.shape[1]
    return pl.pallas_call(
        _ffn_down_kernel,
        grid=(M // tm, Dm // tn),
        in_specs=[pl.BlockSpec((tm, F), lambda i, j: (i, 0)),
                  pl.BlockSpec((F, tn), lambda i, j: (0, j)),
                  pl.BlockSpec((tm, tn), lambda i, j: (i, j))],
        out_specs=pl.BlockSpec((tm, tn), lambda i, j: (i, j)),
        out_shape=jax.ShapeDtypeStruct((M, Dm), F32),
        compiler_params=_cp(("parallel", "arbitrary")),
        name="ffn_down")(hmid, w_down, h1)


def _ple_final_kernel(h_ref, p_ref, wg_ref, bg_ref, wp_ref, g_ref, o_ref):
    h = h_ref[...]
    gate = jax.nn.sigmoid(_dot(h.astype(BF16), wg_ref[...]) + bg_ref[...])
    h = h + gate * _dot(p_ref[...].astype(BF16), wp_ref[...])
    o_ref[...] = _rms(h, g_ref[...])


def ple_final(h2, p2, w_gate, b_gate, w_proj, g, tm):
    M, Dm = h2.shape
    Pd = p2.shape[1]
    return pl.pallas_call(
        _ple_final_kernel,
        grid=(M // tm,),
        in_specs=[pl.BlockSpec((tm, Dm), lambda i: (i, 0)),
                  pl.BlockSpec((tm, Pd), lambda i: (i, 0)),
                  pl.BlockSpec((Dm, Dm), lambda i: (0, 0)),
                  pl.BlockSpec((1, Dm), lambda i: (0, 0)),
                  pl.BlockSpec((Pd, Dm), lambda i: (0, 0)),
                  pl.BlockSpec((1, Dm), lambda i: (0, 0))],
        out_specs=pl.BlockSpec((tm, Dm), lambda i: (i, 0)),
        out_shape=jax.ShapeDtypeStruct((M, Dm), F32),
        compiler_params=_cp(("parallel",)),
        name="ple_final")(h2, p2, w_gate, b_gate, w_proj, g)


def _pick_tile(n, prefs):
    for t in prefs:
        if n % t == 0:
            return t
    raise ValueError(f"no tile for {n}")


def _block_matrices(total_keys, nb):
    e = np.zeros((LANE, total_keys), np.float32)
    t = np.arange(total_keys)
    e[t // SLC_LEN, t] = 1.0
    ratio = SLC_LEN // CMP_STRIDE
    m = np.zeros((LANE, LANE), np.float32)
    for n in range(nb):
        for s in range(LANE):
            m[n, s] = float(ratio * s <= n <= ratio * s + ratio - 1) + float(ratio * s - 1 <= n <= ratio * s + ratio - 2)
    return jnp.asarray(e, BF16), jnp.asarray(m, BF16)


def _prep_weights(i, attn_norm_g, w_in, gla_w_a2, gla_b_a, gla_norm_g, cmp_wk1, cmp_wk2, cmp_wv1, cmp_wv2,
                  w_out, ffn_norm_g, ffn_w_up, ffn_conv_w, ffn_conv_b, ffn_w_down, ple_w_proj, ple_w_gate,
                  ple_b_gate):
    Dm = w_in.shape[1]
    F = ffn_w_down.shape[1]
    Fp = -(-F // 512) * 512
    win = w_in[i]
    ga0 = NQ0
    ng0 = ga0 + GLA_RANK + (NMAIN - NQ0)
    ngw = 3 * NSA_HEADS
    before_ga = lax.broadcasted_iota(jnp.int32, (1, NMAIN), 1) < ga0
    w_main = jnp.where(before_ga, win[:, :NMAIN], win[:, GLA_RANK:GLA_RANK + NMAIN]).astype(BF16)
    w_small = jnp.concatenate([win[:, ga0:ga0 + GLA_RANK], win[:, ng0:ng0 + ngw],
                               jnp.zeros((Dm, LANE - GLA_RANK - ngw), F32)], axis=1).astype(BF16)
    wa_pad = jnp.concatenate([gla_w_a2[i], jnp.zeros((LANE - GLA_RANK, GLA_HEADS * GLA_DK), F32)],
                             axis=0).astype(BF16)
    cat = lambda w1: jnp.concatenate([w1[:CMP_STRIDE], w1[CMP_STRIDE:]], axis=-1)
    w1cat = jnp.stack([cat(cmp_wk1[i]), cat(cmp_wv1[i])]).astype(BF16)
    w2s = jnp.stack([cmp_wk2[i], cmp_wv2[i]])
    wup = ffn_w_up[i]
    w_up_a = jnp.pad(wup[:, :F], ((0, 0), (0, Fp - F))).astype(BF16)
    w_up_u = jnp.pad(wup[:, F:], ((0, 0), (0, Fp - F))).astype(BF16)
    conv_w = jnp.pad(ffn_conv_w[i], ((0, SUBLANE - CONV_W), (0, Fp - F)))
    conv_b = jnp.pad(ffn_conv_b[i], (0, Fp - F)).reshape(1, Fp)
    w_down = ffn_w_down[i].astype(BF16)
    return dict(
        g_attn=attn_norm_g[i].reshape(1, Dm), w_main=w_main, w_small=w_small, wa_pad=wa_pad,
        b_a=gla_b_a[i].reshape(1, -1), gnorm=gla_norm_g[i].reshape(1, -1), w1cat=w1cat, w2s=w2s,
        w_out=w_out[i].astype(BF16), g_ffn=ffn_norm_g[i].reshape(1, Dm), w_up_a=w_up_a, w_up_u=w_up_u,
        conv_w=conv_w,
        conv_b=conv_b, w_down=w_down, w_proj=ple_w_proj[i].astype(BF16), w_gate=ple_w_gate[i].astype(BF16),
        b_gate=ple_b_gate[i].reshape(1, Dm), F=F, Fp=Fp)


def _kv_rows(kv, B, rows):
    t = kv.reshape(kv.shape[0], B, -1, NSA_KV, NSA_HD)[:, :, rows]
    return [t[i] for i in range(kv.shape[0])]


def _prompt_layer(h2d, p2d, B, L, W):
    Dm = h2d.shape[1]
    M = B * L
    proj, small, kv = norm_proj(h2d, W["g_attn"], W["w_main"], W["w_small"],
                                _pick_tile(M, (1024, 512, 256, 128)), 512)
    proj3 = proj.reshape(B, L, NMAIN)
    small3 = small.reshape(B, L, LANE)
    cb = _pick_tile(L, (128, 64, 32, 16))
    s0 = jnp.zeros((B, GLA_HEADS, GLA_DK, GLA_DV), F32)
    og, gla_state = gla(proj3, small3, W["wa_pad"], W["b_a"], W["gnorm"], s0, cb, GLA_SUB, cb, 1)
    cmp = compress(proj3, W["w1cat"], W["w2s"])
    _, mmat = _block_matrices(L, L // CMP_STRIDE - 1)
    on = nsa_prompt(proj3, small3, cmp, mmat.T)
    h1, hn = out_proj(og.reshape(M, -1), on.reshape(M, -1), h2d, W["w_out"], W["g_ffn"],
                      _pick_tile(M, (512, 256, 128)))
    prefix = jnp.zeros((B, SUBLANE, W["Fp"]), F32)
    hmid, tail = ffn_up_seq(hn, W["w_up_a"], W["w_up_u"], W["conv_w"], W["conv_b"], prefix, L,
                            _pick_tile(L, (1024, 512, 256, 128)), 512)
    h2 = ffn_down(hmid, W["w_down"], h1, _pick_tile(M, (1024, 512, 256, 128)), 512)
    keep = min(WINDOW, L)
    outs = tuple(_kv_rows(kv[:4], B, slice(0, L)) + _kv_rows(kv[4:], B, slice(L - keep, L))
                 + [gla_state, tail[:, SUBLANE - (CONV_W - 1):, :W["F"]]])
    return h2, outs


def _decode_layer(h3d, p2d_tm, B, L, lp, caches, page_table, win_k, win_v, gla_s, conv_s, W):
    Dm = h3d.shape[2]
    Mp = B * lp
    x2 = h3d.reshape(Mp, Dm)
    proj, small, kv = norm_proj(x2, W["g_attn"], W["w_main"], W["w_small"],
                                _pick_tile(Mp, (1024, 512, 256, 128, 64, 32, 16, 8)), 512)
    proj3 = proj.reshape(B, lp, NMAIN)
    small3 = small.reshape(B, lp, LANE)
    og, gla_state = gla(proj3, small3, W["wa_pad"], W["b_a"], W["gnorm"], gla_s, lp, lp, L,
                        _pick_tile(B, (4, 2, 1)))
    plen = page_table.shape[1] * caches[0].shape[1] // NSA_KV
    emat, mmat = _block_matrices(plen + KCHUNK, (plen + L) // CMP_STRIDE - 1)
    wlen = win_k.shape[1]
    on, wk_new, wv_new = nsa_decode(proj3, small3, caches, page_table,
                                    win_k.reshape(B, wlen * NSA_KV, NSA_HD),
                                    win_v.reshape(B, wlen * NSA_KV, NSA_HD),
                                    W["w1cat"], W["w2s"], emat, mmat, L, _pick_tile(B, (2, 1)))
    h1, hn = out_proj(og.reshape(Mp, -1), on.reshape(Mp, -1), x2, W["w_out"], W["g_ffn"],
                      _pick_tile(Mp, (256, 128, 64, 32, 16, 8)))
    to_tm = lambda t: t.reshape(B, lp, Dm)[:, :L].transpose(1, 0, 2).reshape(L * B, Dm)
    h1_tm, hn_tm = to_tm(h1), to_tm(hn)
    prefix = jnp.pad(conv_s.transpose(1, 0, 2), ((0, 0), (0, 0), (0, W["Fp"] - W["F"])))
    hmid, tail = ffn_up_tm(hn_tm, W["w_up_a"], W["w_up_u"], W["conv_w"], W["conv_b"], prefix, L, 512)
    M = L * B
    h2 = ffn_down(hmid, W["w_down"], h1_tm, _pick_tile(M, (512, 256, 128, 64, 32, 16, 8)), 512)
    outs = tuple(_kv_rows(kv[:4], B, slice(0, L))
                 + [wk_new.reshape(win_k.shape), wv_new.reshape(win_v.shape),
                    gla_state, tail[:, :, :W["F"]].transpose(1, 0, 2)])
    return h2, outs


def kernel(x_prompt, x_sample, p_prompt, p_sample, cache_cmp_k, cache_cmp_v, cache_slc_k, cache_slc_v,
           page_table, state_win_k, state_win_v, state_gla, state_ffn_conv, attn_norm_g, w_in, gla_w_a2,
           gla_b_a, gla_norm_g, cmp_wk1, cmp_wk2, cmp_wv1, cmp_wv2, w_out, ffn_norm_g, ffn_w_up,
           ffn_conv_w, ffn_conv_b, ffn_w_down, ple_w_proj, ple_w_gate, ple_b_gate, final_norm_g):
    depth = w_in.shape[0]
    assert depth == 1, "layers are chained through HBM one at a time; only depth 1 is wired"
    Bp, Lp, Dm = x_prompt.shape
    Bs, Ls, _ = x_sample.shape
    W = _prep_weights(0, attn_norm_g, w_in, gla_w_a2, gla_b_a, gla_norm_g, cmp_wk1, cmp_wk2, cmp_wv1,
                      cmp_wv2, w_out, ffn_norm_g, ffn_w_up, ffn_conv_w, ffn_conv_b, ffn_w_down,
                      ple_w_proj, ple_w_gate, ple_b_gate)
    g_final = final_norm_g.reshape(1, Dm)

    Mp = Bp * Lp
    h2, outs_p = _prompt_layer(x_prompt.reshape(Mp, Dm), None, Bp, Lp, W)
    y_prompt = ple_final(h2, p_prompt[0].reshape(Mp, -1), W["w_gate"], W["b_gate"], W["w_proj"], g_final,
                         _pick_tile(Mp, (512, 256, 128))).reshape(Bp, Lp, Dm)

    lp = -(-Ls // SUBLANE) * SUBLANE
    xs = jnp.pad(x_sample, ((0, 0), (0, lp - Ls), (0, 0)))
    n_pool, psz = cache_cmp_k.shape[1], cache_cmp_k.shape[2]
    caches = [c[0].reshape(n_pool, psz * NSA_KV, NSA_HD)
              for c in (cache_cmp_k, cache_cmp_v, cache_slc_k, cache_slc_v)]
    h2s, outs_s = _decode_layer(xs, None, Bs, Ls, lp, caches, page_table, state_win_k[0], state_win_v[0],
                                state_gla[0], state_ffn_conv[0], W)
    p_tm = p_sample[0].transpose(1, 0, 2).reshape(Ls * Bs, -1)
    Ms = Ls * Bs
    y_tm = ple_final(h2s, p_tm, W["w_gate"], W["b_gate"], W["w_proj"], g_final,
                     _pick_tile(Ms, (256, 128, 64, 32, 16, 8)))
    y_sample = y_tm.reshape(Ls, Bs, Dm).transpose(1, 0, 2)

    lead = lambda t: t[None]
    return (y_prompt, y_sample) + tuple(lead(t) for t in outs_p) + tuple(lead(t) for t in outs_s)
```

```python
import functools
import math

import numpy as np
import jax
import jax.numpy as jnp
from jax import lax
from jax.experimental import pallas as pl
from jax.experimental.pallas import tpu as pltpu

F32 = jnp.float32
BF16 = jnp.bfloat16

GLA_HEADS = 4
GLA_DK = 128
GLA_DV = 256
GLA_RANK = 16
GLA_TAU = 16.0
GLA_SUB = 16
NSA_HEADS = 8
NSA_KV = 2
NSA_REP = NSA_HEADS // NSA_KV
NSA_HD = 128
CMP_LEN = 32
CMP_STRIDE = 16
SLC_LEN = 64
N_SELECT = 16
WINDOW = 512
CONV_W = 3
EPS = 1e-6
NEG = -1e30
M_FLOOR = 0.1 * NEG
FORCED = 1e6

LANE = 128
SUBLANE = 8
KCHUNK = 128
CMP_SEG_ROWS = NSA_KV * CMP_STRIDE
CMP_PITCH = CMP_SEG_ROWS + SUBLANE
VMEM_LIMIT = 56 * 1024 * 1024

GQ0 = 0
GK0 = GQ0 + GLA_HEADS * GLA_DK
GV0 = GK0 + GLA_HEADS * GLA_DK
GR0 = GV0 + GLA_HEADS * GLA_DV
NQ0 = GR0 + GLA_HEADS * GLA_DV
KVW = NSA_KV * NSA_HD
KC0 = NQ0 + NSA_HEADS * NSA_HD
VC0 = KC0 + KVW
KS0 = VC0 + KVW
VS0 = KS0 + KVW
KW0 = VS0 + KVW
VW0 = KW0 + KVW
NMAIN = VW0 + KVW
GATE0 = GLA_RANK


def _cp(sem):
    return pltpu.CompilerParams(dimension_semantics=sem, vmem_limit_bytes=VMEM_LIMIT)


def _dot(a, b):
    return jnp.dot(a, b, preferred_element_type=F32)


def _dot_nt(a, b):
    return lax.dot_general(a, b, (((1,), (1,)), ((), ())), preferred_element_type=F32)


def _rms(x, g):
    return x * lax.rsqrt(jnp.mean(x * x, axis=-1, keepdims=True) + EPS) * g


def _norm_proj_kernel(x_ref, g_ref, wm_ref, ws_ref, om_ref, os_ref, kv_ref, xn_ref, *, j_kv0):
    j = pl.program_id(1)

    @pl.when(j == 0)
    def _():
        xn = _rms(x_ref[...], g_ref[...]).astype(BF16)
        xn_ref[...] = xn
        os_ref[...] = _dot(xn, ws_ref[...])

    tile = _dot(xn_ref[...], wm_ref[...])
    om_ref[...] = tile

    @pl.when(j >= j_kv0)
    def _():
        tm = tile.shape[0]
        for a in range(tile.shape[1] // KVW):
            for g in range(NSA_KV):
                c0 = a * KVW + g * NSA_HD
                kv_ref[a, pl.ds(g, tm, stride=NSA_KV), :] = tile[:, c0:c0 + NSA_HD]


def norm_proj(x2, g, w_main, w_small, tm, tn):
    M, Dm = x2.shape
    N = w_main.shape[1]
    assert KC0 % tn == 0 and tn % KVW == 0
    j_kv0 = KC0 // tn
    per_tile = tn // KVW
    n_kv = (N - KC0) // KVW
    return pl.pallas_call(
        functools.partial(_norm_proj_kernel, j_kv0=j_kv0),
        grid=(M // tm, N // tn),
        in_specs=[pl.BlockSpec((tm, Dm), lambda i, j: (i, 0)),
                  pl.BlockSpec((1, Dm), lambda i, j: (0, 0)),
                  pl.BlockSpec((Dm, tn), lambda i, j: (0, j)),
                  pl.BlockSpec((Dm, LANE), lambda i, j: (0, 0))],
        out_specs=[pl.BlockSpec((tm, tn), lambda i, j: (i, j)),
                   pl.BlockSpec((tm, LANE), lambda i, j: (i, 0)),
                   pl.BlockSpec((per_tile, NSA_KV * tm, NSA_HD),
                                lambda i, j: (jnp.maximum(j - j_kv0, 0), i, 0))],
        out_shape=[jax.ShapeDtypeStruct((M, N), F32), jax.ShapeDtypeStruct((M, LANE), F32),
                   jax.ShapeDtypeStruct((n_kv, NSA_KV * M, NSA_HD), F32)],
        scratch_shapes=[pltpu.VMEM((tm, Dm), BF16)],
        compiler_params=_cp(("parallel", "arbitrary")),
        name="norm_proj")(x2, g, w_main, w_small)


def _gla_kernel(q_ref, k_ref, v_ref, r_ref, sm_ref, wa_ref, ba_ref, gn_ref, s0_ref,
                o_ref, s_ref, st_ref, *, cb, sub, valid, nbb):
    c = pl.program_id(1)

    @pl.when(c == 0)
    def _():
        st_ref[...] = s0_ref[...]

    GH = GLA_HEADS
    H = nbb * GH
    side = lambda ref: ref[0] if nbb == 1 else jnp.concatenate([ref[s] for s in range(nbb)], axis=1)
    hk = lambda h: slice(h * GLA_DK, (h + 1) * GLA_DK)
    hv = lambda h: slice(h * GLA_DV, (h + 1) * GLA_DV)
    ri = lax.broadcasted_iota(jnp.int32, (cb, cb), 0)
    ci = lax.broadcasted_iota(jnp.int32, (cb, cb), 1)
    tri = jnp.where(ci <= ri, 1.0, 0.0)
    rowv = lax.broadcasted_iota(jnp.int32, (cb, 1), 0)
    q = side(q_ref) * (GLA_DK ** -0.5)
    k = side(k_ref)
    v = side(v_ref)
    pre = [_dot(sm_ref[s].astype(BF16), wa_ref[...]) + ba_ref[...] for s in range(nbb)]
    pre = pre[0] if nbb == 1 else jnp.concatenate(pre, axis=1)
    log_a = jax.nn.log_sigmoid(pre) / GLA_TAU
    if valid < cb:
        log_a = jnp.where(rowv < valid, log_a, 0.0)
        k = jnp.where(rowv < valid, k, 0.0)
        v = jnp.where(rowv < valid, v, 0.0)
    hi = log_a.astype(BF16).astype(F32)
    lo = (log_a - hi).astype(BF16).astype(F32)
    b = _dot(tri, hi) + _dot(tri, lo)
    b_last = b[cb - 1:cb, :]
    S = [st_ref[h // GH, h % GH] for h in range(H)]
    vb = v.astype(BF16)

    qe = (q * jnp.exp(b)).astype(BF16)
    o = [_dot(qe[:, hk(h)], S[h].astype(BF16)) for h in range(H)]
    a_rows = [[] for _ in range(H)]
    cc = lax.broadcasted_iota(jnp.int32, (sub, cb), 1)
    for blk in range(cb // sub):
        lo_r, hi_r = blk * sub, (blk + 1) * sub
        r_dec = jnp.zeros((1, H * GLA_DK), F32) if blk == 0 else b[lo_r - 1:lo_r, :]
        q_i = q[lo_r:hi_r] * jnp.exp(b[lo_r:hi_r] - r_dec)
        k_i = k * jnp.exp(jnp.where(rowv < hi_r, r_dec - b, 0.0))
        rr = lax.broadcasted_iota(jnp.int32, (sub, cb), 0) + lo_r
        for h in range(H):
            a = _dot_nt(q_i[:, hk(h)], k_i[:, hk(h)])
            a_rows[h].append(jnp.where(cc <= rr, a, 0.0))
    for h in range(H):
        a = a_rows[h][0] if len(a_rows[h]) == 1 else jnp.concatenate(a_rows[h], axis=0)
        o[h] = o[h] + _dot(a, v[:, hv(h)])

    kh = k * jnp.exp(b_last - b)
    dec = jnp.exp(b_last)
    if cb < LANE:
        kh = jnp.concatenate([kh, jnp.zeros((LANE - cb, H * GLA_DK), F32)], axis=0)
        vpad = jnp.concatenate([vb, jnp.zeros((LANE - cb, H * GLA_DV), BF16)], axis=0)
    else:
        vpad = vb
    for h in range(H):
        dec_col = jnp.transpose(jnp.broadcast_to(dec[:, hk(h)], (GLA_DK, GLA_DK)))[:, 0:1]
        s_new = dec_col * S[h] + _dot(jnp.transpose(kh[:, hk(h)]).astype(BF16), vpad[:, hv(h)])
        st_ref[h // GH, h % GH] = s_new
        s_ref[h // GH, h % GH] = s_new

    r = side(r_ref)
    gate = r * jax.nn.sigmoid(r)
    for h in range(H):
        o_ref[h // GH, :, hv(h % GH)] = (_rms(o[h], gn_ref[...]) * gate[:, hv(h)]).astype(o_ref.dtype)


def gla(proj3, small3, wa_pad, b_a, gnorm, s0, cb, sub, valid, nbb):
    B, L, _ = proj3.shape
    H = GLA_HEADS
    qw, vw = H * GLA_DK, H * GLA_DV
    kern = functools.partial(_gla_kernel, cb=cb, sub=sub, valid=valid, nbb=nbb)
    return pl.pallas_call(
        kern,
        grid=(B // nbb, L // cb),
        in_specs=[pl.BlockSpec((nbb, cb, qw), lambda b, c: (b, c, GQ0 // qw)),
                  pl.BlockSpec((nbb, cb, qw), lambda b, c: (b, c, GK0 // qw)),
                  pl.BlockSpec((nbb, cb, vw), lambda b, c: (b, c, GV0 // vw)),
                  pl.BlockSpec((nbb, cb, vw), lambda b, c: (b, c, GR0 // vw)),
                  pl.BlockSpec((nbb, cb, LANE), lambda b, c: (b, c, 0)),
                  pl.BlockSpec((LANE, qw), lambda b, c: (0, 0)),
                  pl.BlockSpec((1, qw), lambda b, c: (0, 0)),
                  pl.BlockSpec((1, GLA_DV), lambda b, c: (0, 0)),
                  pl.BlockSpec((nbb, H, GLA_DK, GLA_DV), lambda b, c: (b, 0, 0, 0))],
        out_specs=[pl.BlockSpec((nbb, cb, vw), lambda b, c: (b, c, 0)),
                   pl.BlockSpec((nbb, H, GLA_DK, GLA_DV), lambda b, c: (b, 0, 0, 0))],
        out_shape=[jax.ShapeDtypeStruct((B, L, vw), BF16),
                   jax.ShapeDtypeStruct((B, H, GLA_DK, GLA_DV), F32)],
        scratch_shapes=[pltpu.VMEM((nbb, H, GLA_DK, GLA_DV), F32)],
        compiler_params=_cp(("parallel", "arbitrary")),
        name="gla")(proj3, proj3, proj3, proj3, small3, wa_pad, b_a, gnorm, s0)


def _pair_rows(load_rows, j):
    return jnp.concatenate([load_rows(j), load_rows(j + 1)], axis=1).astype(BF16)


def _pair_weights(w1_ref, j):
    w = w1_ref[pl.ds(j, 2)]
    return w.reshape(2 * NSA_HD, 2 * NSA_HD)


def _compress_group(load_rows, w1_ref, w2, nseg):
    acc = jnp.zeros((nseg, 2 * NSA_HD), F32)
    for j in range(0, CMP_STRIDE, 2):
        acc = acc + _dot(_pair_rows(load_rows, j), _pair_weights(w1_ref, j))
    first = acc[:, :NSA_HD]
    second = acc[:, NSA_HD:]
    h = jax.nn.gelu(first + pltpu.roll(second, nseg - 1, axis=0))
    return _dot(h.astype(BF16), w2)


def _compress_groups(loads, w1_refs, w2s, nrow):
    n = len(loads)
    acc = [jnp.zeros((nrow, 2 * NSA_HD), F32) for _ in range(n)]
    for j in range(0, CMP_STRIDE, 2):
        for i in range(n):
            acc[i] = acc[i] + _dot(_pair_rows(loads[i], j), _pair_weights(w1_refs[i], j))
    hs = [jax.nn.gelu(a[:, :NSA_HD] + pltpu.roll(a[:, NSA_HD:], nrow - 1, axis=0)) for a in acc]
    return [_dot(h.astype(BF16), w2) for h, w2 in zip(hs, w2s)]


def _cmp_branch(q4, ck, cv, qpos4, nb):
    s = _dot_nt(q4, ck.astype(BF16))
    n = lax.broadcasted_iota(jnp.int32, s.shape, 1)
    mask = jnp.where(n * CMP_STRIDE + (CMP_LEN - 1) <= qpos4, n, nb) < nb
    s = jnp.where(mask, s, NEG)
    m = jnp.max(s, axis=-1, keepdims=True)
    e = jnp.where(mask, jnp.exp(s - m), 0.0)
    p = e / jnp.maximum(jnp.sum(e, axis=-1, keepdims=True), 1e-30)
    return p, _dot(p.astype(BF16), cv.astype(BF16))


def _select(pg, mmat, qpos_tok, nslc, nsel):
    hi = pg.astype(BF16)
    lo = (pg - hi.astype(F32)).astype(BF16)
    imp = _dot(hi, mmat) + _dot(lo, mmat)
    blk = lax.broadcasted_iota(jnp.int32, pg.shape, 1)
    cur = qpos_tok // SLC_LEN
    forced = jnp.where(blk == 0, 1, 0) + jnp.where(blk == cur, 1, 0) + jnp.where(blk == cur - 1, 1, 0)
    score = jnp.where(blk <= cur, jnp.where(forced > 0, FORCED, imp), -1.0)
    score = jnp.where(blk < nslc, score, -2.0)
    rank = jnp.zeros(pg.shape, F32)
    for j in range(nslc):
        sj = score[:, j:j + 1]
        tie = jnp.where(blk > j, sj, NEG)
        rank = rank + jnp.where(sj > score, 1.0, 0.0) + jnp.where(tie == score, 1.0, 0.0)
    return jnp.where(rank < nsel, jnp.where(blk < nslc, 1.0, 0.0), 0.0)


def _flash_init(m_ref, l_ref, acc_ref):
    m_ref[...] = jnp.full(m_ref.shape, NEG, F32)
    l_ref[...] = jnp.zeros(l_ref.shape, F32)
    acc_ref[...] = jnp.zeros(acc_ref.shape, F32)


def _flash_step(q4, k, v, mask, m_ref, l_ref, acc_ref):
    s = jnp.where(mask, _dot_nt(q4, k), NEG)
    m_old = m_ref[...]
    m_new = jnp.maximum(m_old, jnp.max(s, axis=-1, keepdims=True))
    alpha = jnp.exp(m_old - m_new)
    p = jnp.where(mask, jnp.exp(s - m_new), 0.0)
    l_ref[...] = alpha * l_ref[...] + jnp.sum(p, axis=-1, keepdims=True)
    acc_ref[...] = alpha * acc_ref[...] + _dot(p.astype(BF16), v)
    m_ref[...] = m_new


def _flash_out(l_ref, acc_ref):
    return acc_ref[...] / jnp.maximum(l_ref[...], 1e-30)


def _tile_rows(x, reps):
    return jnp.concatenate([x] * reps, axis=0)


def _attend(q4, keys, vals, masks):
    s = [jnp.where(mk, _dot_nt(q4, k), NEG) for k, mk in zip(keys, masks)]
    m = s[0].max(axis=-1, keepdims=True)
    for si in s[1:]:
        m = jnp.maximum(m, si.max(axis=-1, keepdims=True))
    e = [jnp.where(mk, jnp.exp(si - m), 0.0) for si, mk in zip(s, masks)]
    l = sum(ei.sum(axis=-1, keepdims=True) for ei in e)
    o = sum(_dot(ei.astype(BF16), v) for ei, v in zip(e, vals))
    return o / jnp.maximum(l, 1e-30)


def _compress_kernel(x_ref, w1_ref, w2_ref, o_ref, *, nseg):
    load = lambda j: x_ref[0, pl.ds(j, nseg, stride=CMP_STRIDE), :]
    o_ref[0, 0, 0] = _compress_group(load, w1_ref.at[0], w2_ref[0].astype(BF16), nseg)


def compress(proj3, w1cat, w2s):
    B, L, _ = proj3.shape
    nseg = L // CMP_STRIDE
    G = NSA_KV
    return pl.pallas_call(
        functools.partial(_compress_kernel, nseg=nseg),
        grid=(B, 2, G),
        in_specs=[pl.BlockSpec((1, L, NSA_HD), lambda b, w, g: (b, 0, KC0 // NSA_HD + w * G + g)),
                  pl.BlockSpec((1, CMP_STRIDE, NSA_HD, 2 * NSA_HD), lambda b, w, g: (w, 0, 0, 0)),
                  pl.BlockSpec((1, NSA_HD, NSA_HD), lambda b, w, g: (w, 0, 0))],
        out_specs=pl.BlockSpec((1, 1, 1, nseg, NSA_HD), lambda b, w, g: (w, b, g, 0, 0)),
        out_shape=jax.ShapeDtypeStruct((2, B, G, nseg, NSA_HD), F32),
        compiler_params=_cp(("parallel", "parallel", "parallel")),
        name="compress")(proj3, w1cat, w2s)


def _select_t(pg_t, mm_t, qpos_row, nslc, nsel):
    nsp = -(-nslc // SUBLANE) * SUBLANE
    hi = pg_t.astype(BF16)
    lo = (pg_t - hi.astype(F32)).astype(BF16)
    imp = (_dot(mm_t, hi) + _dot(mm_t, lo))[:nsp]
    blk = lax.broadcasted_iota(jnp.int32, imp.shape, 0)
    cur = qpos_row // SLC_LEN
    forced = jnp.where(blk == 0, 1, 0) + jnp.where(blk == cur, 1, 0) + jnp.where(blk == cur - 1, 1, 0)
    score = jnp.where(blk <= cur, jnp.where(forced > 0, FORCED, imp), -1.0)
    score = jnp.where(blk < nslc, score, -2.0)
    rank = jnp.zeros(imp.shape, F32)
    for j in range(nslc):
        sj = score[j:j + 1, :]
        tie = jnp.where(blk > j, sj, NEG)
        rank = rank + jnp.where(sj > score, 1.0, 0.0) + jnp.where(tie == score, 1.0, 0.0)
    return jnp.where(rank < nsel, jnp.where(blk < nslc, 1.0, 0.0), 0.0)


def _flash_step_t(q_t, k, v_t, bias, m_ref, l_ref, acc_ref):
    s = _dot(k, q_t) + bias
    m_old = m_ref[...]
    m_new = jnp.maximum(m_old, jnp.max(s, axis=0, keepdims=True))
    alpha = jnp.exp(m_old - m_new)
    p = jnp.exp(s - m_new)
    l_ref[...] = alpha * l_ref[...] + jnp.sum(p, axis=0, keepdims=True)
    acc_ref[...] = alpha * acc_ref[...] + _dot(v_t, p.astype(BF16))
    m_ref[...] = m_new


def _nsa_prompt_kernel(q_ref, sm_ref, ck_ref, cv_ref, ks_ref, vs_ref, kw_ref, vw_ref, mm_ref,
                       o_ref, vst_ref, vwt_ref, sel_ref, m_ref, l_ref, acc_ref, mw_ref, lw_ref, accw_ref,
                       *, tq, kstep, nslc, nsel, nb):
    g = pl.program_id(1)
    qb = pl.program_id(2)
    R = NSA_REP
    T = R * tq
    L = ks_ref.shape[1]
    scale = NSA_HD ** -0.5

    @pl.when(qb == 0)
    def _():
        for c in range(L // KCHUNK):
            cs = slice(c * KCHUNK, (c + 1) * KCHUNK)
            vst_ref[:, cs] = jnp.transpose(vs_ref[0, cs, :]).astype(BF16)
            vwt_ref[:, cs] = jnp.transpose(vw_ref[0, cs, :]).astype(BF16)

    q_t = (jnp.concatenate([jnp.transpose(q_ref[0, :, r * NSA_HD:(r + 1) * NSA_HD]) for r in range(R)],
                           axis=1) * scale).astype(BF16)
    qpos_row = qb * tq + lax.broadcasted_iota(jnp.int32, (1, tq), 1)
    qpos4 = jnp.concatenate([qpos_row] * R, axis=1)

    s = _dot(ck_ref[0, 0, 0].astype(BF16), q_t)
    n = lax.broadcasted_iota(jnp.int32, s.shape, 0)
    mask = jnp.where(n * CMP_STRIDE + (CMP_LEN - 1) <= qpos4, n, nb) < nb
    s = jnp.where(mask, s, NEG)
    e = jnp.where(mask, jnp.exp(s - jnp.max(s, axis=0, keepdims=True)), 0.0)
    p = e / jnp.maximum(jnp.sum(e, axis=0, keepdims=True), 1e-30)
    o_c = _dot(jnp.transpose(cv_ref[0, 0, 0]).astype(BF16), p.astype(BF16))
    pg = p[:, 0:tq]
    for r in range(1, R):
        pg = pg + p[:, r * tq:(r + 1) * tq]
    sel_ref[...] = (_select_t(pg, mm_ref[...], qpos_row, nslc, nsel) - 1.0) * (-NEG)

    key = lax.broadcasted_iota(jnp.int32, (kstep, T), 0)
    hi_step = (qb * tq + tq - 1) // kstep + 1
    lo_step = jnp.maximum(qb * tq - (WINDOW - 1), 0) // kstep

    sel_state = (m_ref, l_ref, acc_ref)
    win_state = (mw_ref, lw_ref, accw_ref)
    for m_r, l_r, acc_r in (sel_state, win_state):
        m_r[...] = jnp.full(m_r.shape, M_FLOOR, F32)
        l_r[...] = jnp.zeros(l_r.shape, F32)
        acc_r[...] = jnp.zeros(acc_r.shape, F32)

    def sel_step(c, causal):
        off = pl.multiple_of(c * kstep, kstep)
        k = ks_ref[0, pl.ds(off, kstep), :].astype(BF16)
        v_t = vst_ref[:, pl.ds(off, kstep)]
        blocks = [jnp.broadcast_to(sel_ref[pl.ds((kstep // SLC_LEN) * c + i, 1), :], (SLC_LEN, tq))
                  for i in range(kstep // SLC_LEN)]
        bias = jnp.concatenate(blocks, axis=0)
        bias = jnp.concatenate([bias] * R, axis=1)
        if causal:
            bias = jnp.where(off + key <= qpos4, bias, NEG)
        _flash_step_t(q_t, k, v_t, bias, *sel_state)

    def win_step(c, causal):
        off = pl.multiple_of(c * kstep, kstep)
        k = kw_ref[0, pl.ds(off, kstep), :].astype(BF16)
        v_t = vwt_ref[:, pl.ds(off, kstep)]
        kpos = off + key
        if causal:
            bias = jnp.where(kpos <= qpos4, 0.0, NEG)
        else:
            bias = jnp.where(kpos > qpos4 - WINDOW, 0.0, NEG)
        _flash_step_t(q_t, k, v_t, bias, *win_state)

    def body_sel(c, carry):
        sel_step(c, False)
        return carry

    def body_both(c, carry):
        sel_step(c, False)
        win_step(c, False)
        return carry

    last = hi_step - 1
    lax.fori_loop(0, jnp.minimum(lo_step, last), body_sel, 0)
    lax.fori_loop(lo_step, last, body_both, 0)
    sel_step(last, True)
    win_step(last, True)
    o_s = acc_ref[...] / jnp.maximum(l_ref[...], 1e-30)
    o_w = accw_ref[...] / jnp.maximum(lw_ref[...], 1e-30)

    g_t = jnp.transpose(jax.nn.sigmoid(sm_ref[0]))
    for r in range(R):
        def gate(br, r=r):
            c0 = GATE0 + br * NSA_HEADS + r
            return jnp.where(g == 0, g_t[c0:c0 + 1, :], g_t[c0 + R:c0 + R + 1, :])
        cs = slice(r * tq, (r + 1) * tq)
        o = gate(0) * o_c[:, cs] + gate(1) * o_s[:, cs] + gate(2) * o_w[:, cs]
        o_ref[0, :, r * NSA_HD:(r + 1) * NSA_HD] = jnp.transpose(o).astype(o_ref.dtype)


def nsa_prompt(proj3, small3, cmp, mmat_t):
    B, L, _ = proj3.shape
    tq = _pick_tile(L, (2 * KCHUNK, KCHUNK))
    nseg = L // CMP_STRIDE
    nb = nseg - 1
    nslc = -(-L // SLC_LEN)
    nsel = min(N_SELECT, nslc)
    nsp = -(-nslc // SUBLANE) * SUBLANE
    G, R = NSA_KV, NSA_REP
    kstep = _pick_tile(L, (2 * KCHUNK, KCHUNK))
    assert nseg == LANE and L % KCHUNK == 0 and kstep % SLC_LEN == 0
    kv_spec = lambda c0: pl.BlockSpec((1, L, NSA_HD), lambda b, g, qb: (b, 0, c0 // NSA_HD + g))
    kern = functools.partial(_nsa_prompt_kernel, tq=tq, kstep=kstep, nslc=nslc, nsel=nsel, nb=nb)
    return pl.pallas_call(
        kern,
        grid=(B, G, L // tq),
        in_specs=[pl.BlockSpec((1, tq, R * NSA_HD), lambda b, g, qb: (b, qb, NQ0 // (R * NSA_HD) + g)),
                  pl.BlockSpec((1, tq, LANE), lambda b, g, qb: (b, qb, 0)),
                  pl.BlockSpec((1, 1, 1, nseg, NSA_HD), lambda b, g, qb: (0, b, g, 0, 0)),
                  pl.BlockSpec((1, 1, 1, nseg, NSA_HD), lambda b, g, qb: (1, b, g, 0, 0)),
                  kv_spec(KS0), kv_spec(VS0), kv_spec(KW0), kv_spec(VW0),
                  pl.BlockSpec((LANE, LANE), lambda b, g, qb: (0, 0))],
        out_specs=pl.BlockSpec((1, tq, R * NSA_HD), lambda b, g, qb: (b, qb, g)),
        out_shape=jax.ShapeDtypeStruct((B, L, NSA_HEADS * NSA_HD), BF16),
        scratch_shapes=[pltpu.VMEM((NSA_HD, L), BF16),
                        pltpu.VMEM((NSA_HD, L), BF16),
                        pltpu.VMEM((nsp, tq), F32)]
                       + 2 * [pltpu.VMEM((1, R * tq), F32),
                              pltpu.VMEM((1, R * tq), F32),
                              pltpu.VMEM((NSA_HD, R * tq), F32)],
        compiler_params=_cp(("parallel", "parallel", "arbitrary")),
        name="nsa_prompt")(proj3, small3, cmp, cmp, proj3, proj3, proj3, proj3, mmat_t)


def _nsa_decode_kernel(pt_ref, q_ref, sm_ref, ksn_ref, vsn_ref, kwn_ref, vwn_ref, wk_ref, wv_ref,
                       w1_ref, w2_ref, e_ref, mm_ref, c0_hbm, c1_hbm, c2_hbm, c3_hbm,
                       o_ref, wko_ref, wvo_ref, cbuf_ref, sbuf_ref, sem_ref,
                       *, ns, lp, lreal, plen, rows_pp, n_pages, wlen, nslc, nsel, nb):
    b = pl.program_id(0)
    slot = b % 2
    caches = (c0_hbm, c1_hbm, c2_hbm, c3_hbm)
    G = NSA_KV
    R = NSA_REP
    T = R * lp
    segs_pp = rows_pp // CMP_SEG_ROWS

    def copies(step, sl):
        out = []
        for q in range(ns):
            for p in range(n_pages):
                page = pt_ref[step * ns + q, p]
                for w in range(2):
                    for s in range(segs_pp):
                        out.append(pltpu.make_async_copy(
                            caches[w].at[page, pl.ds(s * CMP_SEG_ROWS, CMP_SEG_ROWS)],
                            cbuf_ref.at[sl, q, w, pl.ds((p * segs_pp + s) * CMP_PITCH, CMP_SEG_ROWS)],
                            sem_ref.at[sl, w]))
                    out.append(pltpu.make_async_copy(
                        caches[2 + w].at[page], sbuf_ref.at[sl, q, w, pl.ds(p * rows_pp, rows_pp)],
                        sem_ref.at[sl, 2 + w]))
        return out

    @pl.when(b == 0)
    def _():
        for cp in copies(0, 0):
            cp.start()

    @pl.when(b + 1 < pl.num_programs(0))
    def _():
        for cp in copies(b + 1, 1 - slot):
            cp.start()

    for w in range(len(caches)):
        for q in range(ns):
            whole = sbuf_ref.at[slot, q, 0]
            pltpu.make_async_copy(whole, whole, sem_ref.at[slot, w]).wait()

    scale = NSA_HD ** -0.5
    nseg = plen // CMP_STRIDE
    tok = lax.broadcasted_iota(jnp.int32, (lp, 1), 0)
    qpos_tok = plen + tok
    qpos4 = _tile_rows(qpos_tok, R)
    lane_n = lax.broadcasted_iota(jnp.int32, (T, KCHUNK), 1)
    new_pos = jnp.where(lane_n < lreal, plen + lane_n, qpos4 + 1)
    past_pos = lax.broadcasted_iota(jnp.int32, (T, plen), 1)
    win_pos = plen - wlen + lax.broadcasted_iota(jnp.int32, (T, wlen), 1)
    win_pos = jnp.where(win_pos > qpos4 - WINDOW, win_pos, qpos4 + 1)
    win_pos = jnp.where(win_pos >= 0, win_pos, qpos4 + 1)
    new_win_pos = jnp.where(new_pos > qpos4 - WINDOW, new_pos, qpos4 + 1)
    zpad_f = jnp.zeros((KCHUNK - lp, NSA_HD), F32)
    pad_new = lambda ref, q, ls: jnp.concatenate([ref[q, :, ls], zpad_f], axis=0).astype(BF16)
    gls = lambda g: slice(g * NSA_HD, (g + 1) * NSA_HD)
    pairs = [(q, g) for q in range(ns) for g in range(G)]
    q4 = [(jnp.concatenate([q_ref[q, :, (g * R + r) * NSA_HD:(g * R + r + 1) * NSA_HD]
                            for r in range(R)], axis=0) * scale).astype(BF16) for q, g in pairs]

    def seg_rows(w, j):
        return jnp.concatenate([cbuf_ref[slot, q, w, pl.ds(G * j + g, nseg, stride=CMP_PITCH), :]
                                for q, g in pairs], axis=0)
    cmp = _compress_groups([functools.partial(seg_rows, w) for w in range(2)],
                           [w1_ref.at[w] for w in range(2)],
                           [w2_ref[w].astype(BF16) for w in range(2)], len(pairs) * nseg)
    prow = lambda i: slice(i * nseg, (i + 1) * nseg)
    branch = [_cmp_branch(q4[i], cmp[0][prow(i)], cmp[1][prow(i)], qpos4, nb) for i in range(len(pairs))]
    pgs = []
    for i in range(len(pairs)):
        p = branch[i][0]
        pg = p[0:lp]
        for r in range(1, R):
            pg = pg + p[r * lp:(r + 1) * lp]
        pgs.append(pg)
    sel = _select(jnp.concatenate(pgs, axis=0), mm_ref[...], _tile_rows(qpos_tok, len(pairs)), nslc, nsel)
    selm_all = _dot(sel.astype(BF16), e_ref[...])

    o_s, o_w = [], []
    for i, (q, g) in enumerate(pairs):
        selm = _tile_rows(selm_all[i * lp:(i + 1) * lp], R)
        k_past = sbuf_ref[slot, q, 0, pl.ds(g, plen, stride=G), :].astype(BF16)
        v_past = sbuf_ref[slot, q, 1, pl.ds(g, plen, stride=G), :].astype(BF16)
        m_past = jnp.where(selm[:, :plen] > 0.5, past_pos, qpos4 + 1) <= qpos4
        m_new = jnp.where(selm[:, plen:] > 0.5, new_pos, qpos4 + 1) <= qpos4
        o_s.append(_attend(q4[i], [k_past, pad_new(ksn_ref, q, gls(g))],
                           [v_past, pad_new(vsn_ref, q, gls(g))], [m_past, m_new]))
    for i, (q, g) in enumerate(pairs):
        k_win = wk_ref[q, pl.ds(g, wlen, stride=G), :].astype(BF16)
        v_win = wv_ref[q, pl.ds(g, wlen, stride=G), :].astype(BF16)
        o_w.append(_attend(q4[i], [k_win, pad_new(kwn_ref, q, gls(g))],
                           [v_win, pad_new(vwn_ref, q, gls(g))], [win_pos <= qpos4, new_win_pos <= qpos4]))

    for i, (q, g) in enumerate(pairs):
        gts = jax.nn.sigmoid(sm_ref[q])
        for r in range(R):
            c0 = GATE0 + g * R + r
            rs = slice(r * lp, (r + 1) * lp)
            o = (gts[:, c0:c0 + 1] * branch[i][1][rs]
                 + gts[:, c0 + NSA_HEADS:c0 + NSA_HEADS + 1] * o_s[i][rs]
                 + gts[:, c0 + 2 * NSA_HEADS:c0 + 2 * NSA_HEADS + 1] * o_w[i][rs])
            hq = (g * R + r) * NSA_HD
            o_ref[q, :, hq:hq + NSA_HD] = o.astype(o_ref.dtype)

    wrows = wlen * G
    shift = lreal * G
    for q in range(ns):
        for src, new, dst in ((wk_ref, kwn_ref, wko_ref), (wv_ref, vwn_ref, wvo_ref)):
            dst[q, 0:wrows - shift, :] = src[q, shift:wrows, :]
            for t in range(lreal):
                for g in range(G):
                    row = wrows - shift + t * G + g
                    dst[q, row:row + 1, :] = new[q, t:t + 1, g * NSA_HD:(g + 1) * NSA_HD]


def nsa_decode(proj3, small3, caches, page_table, win_k, win_v, w1cat, w2s, emat, mmat, lreal, ns):
    B, lp, _ = proj3.shape
    G = NSA_KV
    n_pages = page_table.shape[1]
    rows_pp = caches[0].shape[1]
    plen = n_pages * rows_pp // G
    wlen = win_k.shape[1] // G
    assert plen % CMP_STRIDE == 0 and lreal < CMP_STRIDE and lreal <= lp and B % ns == 0
    nb = (plen + lreal) // CMP_STRIDE - 1
    nslc = -(-(plen + lreal) // SLC_LEN)
    nsel = min(N_SELECT, nslc)
    new_spec = lambda c0: pl.BlockSpec((ns, lp, KVW), lambda b, pt: (b, 0, c0 // KVW))
    const = lambda shape: pl.BlockSpec(shape, lambda b, pt: (0,) * len(shape))
    hbm = pl.BlockSpec(memory_space=pl.ANY)
    kern = functools.partial(_nsa_decode_kernel, ns=ns, lp=lp, lreal=lreal, plen=plen, rows_pp=rows_pp,
                             n_pages=n_pages, wlen=wlen, nslc=nslc, nsel=nsel, nb=nb)
    grid_spec = pltpu.PrefetchScalarGridSpec(
        num_scalar_prefetch=1,
        grid=(B // ns,),
        in_specs=[pl.BlockSpec((ns, lp, NSA_HEADS * NSA_HD), lambda b, pt: (b, 0, NQ0 // (NSA_HEADS * NSA_HD))),
                  pl.BlockSpec((ns, lp, LANE), lambda b, pt: (b, 0, 0)),
                  new_spec(KS0), new_spec(VS0), new_spec(KW0), new_spec(VW0),
                  pl.BlockSpec((ns, wlen * G, NSA_HD), lambda b, pt: (b, 0, 0)),
                  pl.BlockSpec((ns, wlen * G, NSA_HD), lambda b, pt: (b, 0, 0)),
                  const((2, CMP_STRIDE, NSA_HD, 2 * NSA_HD)),
                  const((2, NSA_HD, NSA_HD)),
                  const((LANE, plen + KCHUNK)),
                  const((LANE, LANE)),
                  hbm, hbm, hbm, hbm],
        out_specs=[pl.BlockSpec((ns, lp, NSA_HEADS * NSA_HD), lambda b, pt: (b, 0, 0)),
                   pl.BlockSpec((ns, wlen * G, NSA_HD), lambda b, pt: (b, 0, 0)),
                   pl.BlockSpec((ns, wlen * G, NSA_HD), lambda b, pt: (b, 0, 0))],
        scratch_shapes=[pltpu.VMEM((2, ns, 2, n_pages * rows_pp // CMP_SEG_ROWS * CMP_PITCH, NSA_HD), F32),
                        pltpu.VMEM((2, ns, 2, n_pages * rows_pp, NSA_HD), F32),
                        pltpu.SemaphoreType.DMA((2, len(caches)))])
    return pl.pallas_call(
        kern,
        grid_spec=grid_spec,
        out_shape=[jax.ShapeDtypeStruct((B, lp, NSA_HEADS * NSA_HD), BF16),
                   jax.ShapeDtypeStruct(win_k.shape, F32), jax.ShapeDtypeStruct(win_v.shape, F32)],
        compiler_params=_cp(("arbitrary",)),
        name="nsa_decode")(page_table, proj3, small3, proj3, proj3, proj3, proj3, win_k, win_v,
                           w1cat, w2s, emat, mmat, caches[0], caches[1], caches[2], caches[3])


def _out_proj_kernel(og_ref, on_ref, x_ref, w0_ref, w1_ref, g_ref, h_ref, hn_ref):
    h = x_ref[...] + _dot(og_ref[...], w0_ref[...]) + _dot(on_ref[...], w1_ref[...])
    h_ref[...] = h
    hn_ref[...] = _rms(h, g_ref[...]).astype(BF16)


def out_proj(og, on, x2, w_out, g, tm):
    M, Dm = x2.shape
    Kh = og.shape[1]
    return pl.pallas_call(
        _out_proj_kernel,
        grid=(M // tm,),
        in_specs=[pl.BlockSpec((tm, Kh), lambda i: (i, 0)),
                  pl.BlockSpec((tm, Kh), lambda i: (i, 0)),
                  pl.BlockSpec((tm, Dm), lambda i: (i, 0)),
                  pl.BlockSpec((Kh, Dm), lambda i: (0, 0)),
                  pl.BlockSpec((Kh, Dm), lambda i: (1, 0)),
                  pl.BlockSpec((1, Dm), lambda i: (0, 0))],
        out_specs=[pl.BlockSpec((tm, Dm), lambda i: (i, 0)),
                   pl.BlockSpec((tm, Dm), lambda i: (i, 0))],
        out_shape=[jax.ShapeDtypeStruct((M, Dm), F32), jax.ShapeDtypeStruct((M, Dm), BF16)],
        compiler_params=_cp(("parallel",)),
        name="out_proj")(og, on, x2, w_out, w_out, g)


def _ffn_up_seq_kernel(hn_ref, wa_ref, wu_ref, cw_ref, cb_ref, pre_ref, hm_ref, tail_ref, aext_ref,
                       *, tm, tiles_per_seq):
    m = pl.program_id(1)

    @pl.when(m % tiles_per_seq == 0)
    def _():
        aext_ref[0:SUBLANE, :] = pre_ref[0]

    hn = hn_ref[...]
    a = _dot(hn, wa_ref[...])
    u = _dot(hn, wu_ref[...])
    aext_ref[SUBLANE:SUBLANE + tm, :] = a
    p1 = aext_ref[pl.ds(SUBLANE - 1, tm), :]
    p2 = aext_ref[pl.ds(SUBLANE - 2, tm), :]
    c = p2 * cw_ref[0:1, :] + p1 * cw_ref[1:2, :] + a * cw_ref[2:3, :] + cb_ref[...]
    hm_ref[...] = (jax.nn.gelu(c) * u).astype(hm_ref.dtype)
    tail = a[tm - SUBLANE:tm, :]
    tail_ref[0] = tail
    aext_ref[0:SUBLANE, :] = tail


def ffn_up_seq(hn, w_a, w_u, conv_w, conv_b, prefix, seq_len, tm, tf):
    M, Dm = hn.shape
    Fp = conv_w.shape[1]
    nf = Fp // tf
    B = M // seq_len
    tps = seq_len // tm
    kern = functools.partial(_ffn_up_seq_kernel, tm=tm, tiles_per_seq=tps)
    return pl.pallas_call(
        kern,
        grid=(nf, M // tm),
        in_specs=[pl.BlockSpec((tm, Dm), lambda f, m: (m, 0)),
                  pl.BlockSpec((Dm, tf), lambda f, m: (0, f)),
                  pl.BlockSpec((Dm, tf), lambda f, m: (0, f)),
                  pl.BlockSpec((SUBLANE, tf), lambda f, m: (0, f)),
                  pl.BlockSpec((1, tf), lambda f, m: (0, f)),
                  pl.BlockSpec((1, SUBLANE, tf), lambda f, m: (m // tps, 0, f))],
        out_specs=[pl.BlockSpec((tm, tf), lambda f, m: (m, f)),
                   pl.BlockSpec((1, SUBLANE, tf), lambda f, m: (m // tps, 0, f))],
        out_shape=[jax.ShapeDtypeStruct((M, Fp), BF16), jax.ShapeDtypeStruct((B, SUBLANE, Fp), F32)],
        scratch_shapes=[pltpu.VMEM((SUBLANE + tm, tf), F32)],
        compiler_params=_cp(("parallel", "arbitrary")),
        name="ffn_up_seq")(hn, w_a, w_u, conv_w, conv_b, prefix)


def _ffn_up_tm_kernel(hn_ref, wa_ref, wu_ref, cw_ref, cb_ref, pre_ref, hm_ref, tail_ref, *, steps, nb):
    hn = hn_ref[...]
    a = _dot(hn, wa_ref[...])
    u = _dot(hn, wu_ref[...])
    slabs = [pre_ref[i] for i in range(CONV_W - 1)] + [a[t * nb:(t + 1) * nb] for t in range(steps)]
    for t in range(steps):
        c = (slabs[t] * cw_ref[0:1, :] + slabs[t + 1] * cw_ref[1:2, :] + slabs[t + 2] * cw_ref[2:3, :]
             + cb_ref[...])
        hm_ref[t * nb:(t + 1) * nb, :] = (jax.nn.gelu(c) * u[t * nb:(t + 1) * nb]).astype(hm_ref.dtype)
    for i in range(CONV_W - 1):
        tail_ref[i] = slabs[steps + i]


def ffn_up_tm(hn, w_a, w_u, conv_w, conv_b, prefix, steps, tf):
    M, Dm = hn.shape
    Fp = conv_w.shape[1]
    nf = Fp // tf
    nb = M // steps
    kern = functools.partial(_ffn_up_tm_kernel, steps=steps, nb=nb)
    return pl.pallas_call(
        kern,
        grid=(nf,),
        in_specs=[pl.BlockSpec((M, Dm), lambda f: (0, 0)),
                  pl.BlockSpec((Dm, tf), lambda f: (0, f)),
                  pl.BlockSpec((Dm, tf), lambda f: (0, f)),
                  pl.BlockSpec((SUBLANE, tf), lambda f: (0, f)),
                  pl.BlockSpec((1, tf), lambda f: (0, f)),
                  pl.BlockSpec((CONV_W - 1, nb, tf), lambda f: (0, 0, f))],
        out_specs=[pl.BlockSpec((M, tf), lambda f: (0, f)),
                   pl.BlockSpec((CONV_W - 1, nb, tf), lambda f: (0, 0, f))],
        out_shape=[jax.ShapeDtypeStruct((M, Fp), BF16), jax.ShapeDtypeStruct((CONV_W - 1, nb, Fp), F32)],
        compiler_params=_cp(("parallel",)),
        name="ffn_up_tm")(hn, w_a, w_u, conv_w, conv_b, prefix)


def _ffn_down_kernel(hm_ref, w_ref, h_ref, o_ref):
    o_ref[...] = h_ref[...] + _dot(hm_ref[...], w_ref[...])


def ffn_down(hmid, w_down, h1, tm, tn):
    M = hmid.shape[0]
    F, Dm = w_down.shape
    assert F % LANE == 0 and F <= hmid.shape[1]
    return pl.pallas_call(
        _ffn_down_kernel,
        grid=(M // tm, Dm // tn),
        in_specs=[pl.BlockSpec((tm, F), lambda i, j: (i, 0)),
                  pl.BlockSpec((F, tn), lambda i, j: (0, j)),
                  pl.BlockSpec((tm, tn), lambda i, j: (i, j))],
        out_specs=pl.BlockSpec((tm, tn), lambda i, j: (i, j)),
        out_shape=jax.ShapeDtypeStruct((M, Dm), F32),
        compiler_params=_cp(("parallel", "arbitrary")),
        name="ffn_down")(hmid, w_down, h1)


def _ple_final_kernel(h_ref, p_ref, wg_ref, bg_ref, wp_ref, g_ref, o_ref):
    h = h_ref[...]
    gate = jax.nn.sigmoid(_dot(h.astype(BF16), wg_ref[...]) + bg_ref[...])
    h = h + gate * _dot(p_ref[...].astype(BF16), wp_ref[...])
    o_ref[...] = _rms(h, g_ref[...])


def ple_final(h2, p2, w_gate, b_gate, w_proj, g, tm):
    M, Dm = h2.shape
    Pd = p2.shape[1]
    return pl.pallas_call(
        _ple_final_kernel,
        grid=(M // tm,),
        in_specs=[pl.BlockSpec((tm, Dm), lambda i: (i, 0)),
                  pl.BlockSpec((tm, Pd), lambda i: (i, 0)),
                  pl.BlockSpec((Dm, Dm), lambda i: (0, 0)),
                  pl.BlockSpec((1, Dm), lambda i: (0, 0)),
                  pl.BlockSpec((Pd, Dm), lambda i: (0, 0)),
                  pl.BlockSpec((1, Dm), lambda i: (0, 0))],
        out_specs=pl.BlockSpec((tm, Dm), lambda i: (i, 0)),
        out_shape=jax.ShapeDtypeStruct((M, Dm), F32),
        compiler_params=_cp(("parallel",)),
        name="ple_final")(h2, p2, w_gate, b_gate, w_proj, g)


def _pick_tile(n, prefs):
    for t in prefs:
        if n % t == 0:
            return t
    raise ValueError(f"no tile for {n}")


def _block_matrices(total_keys, nb):
    e = np.zeros((LANE, total_keys), np.float32)
    t = np.arange(total_keys)
    e[t // SLC_LEN, t] = 1.0
    ratio = SLC_LEN // CMP_STRIDE
    m = np.zeros((LANE, LANE), np.float32)
    for n in range(nb):
        for s in range(LANE):
            m[n, s] = float(ratio * s <= n <= ratio * s + ratio - 1) + float(ratio * s - 1 <= n <= ratio * s + ratio - 2)
    return jnp.asarray(e, BF16), jnp.asarray(m, BF16)


def _prep_weights(i, attn_norm_g, w_in, gla_w_a2, gla_b_a, gla_norm_g, cmp_wk1, cmp_wk2, cmp_wv1, cmp_wv2,
                  w_out, ffn_norm_g, ffn_w_up, ffn_conv_w, ffn_conv_b, ffn_w_down, ple_w_proj, ple_w_gate,
                  ple_b_gate):
    Dm = w_in.shape[1]
    F = ffn_w_down.shape[1]
    Fp = -(-F // 512) * 512
    win = w_in[i]
    ga0 = NQ0
    ng0 = ga0 + GLA_RANK + (NMAIN - NQ0)
    ngw = 3 * NSA_HEADS
    before_ga = lax.broadcasted_iota(jnp.int32, (1, NMAIN), 1) < ga0
    w_main = jnp.where(before_ga, win[:, :NMAIN], win[:, GLA_RANK:GLA_RANK + NMAIN]).astype(BF16)
    w_small = jnp.concatenate([win[:, ga0:ga0 + GLA_RANK], win[:, ng0:ng0 + ngw],
                               jnp.zeros((Dm, LANE - GLA_RANK - ngw), F32)], axis=1).astype(BF16)
    wa_pad = jnp.concatenate([gla_w_a2[i], jnp.zeros((LANE - GLA_RANK, GLA_HEADS * GLA_DK), F32)],
                             axis=0).astype(BF16)
    cat = lambda w1: jnp.concatenate([w1[:CMP_STRIDE], w1[CMP_STRIDE:]], axis=-1)
    w1cat = jnp.stack([cat(cmp_wk1[i]), cat(cmp_wv1[i])]).astype(BF16)
    w2s = jnp.stack([cmp_wk2[i], cmp_wv2[i]])
    wup = ffn_w_up[i]
    w_up_a = jnp.pad(wup[:, :F], ((0, 0), (0, Fp - F))).astype(BF16)
    w_up_u = jnp.pad(wup[:, F:], ((0, 0), (0, Fp - F))).astype(BF16)
    conv_w = jnp.pad(ffn_conv_w[i], ((0, SUBLANE - CONV_W), (0, Fp - F)))
    conv_b = jnp.pad(ffn_conv_b[i], (0, Fp - F)).reshape(1, Fp)
    w_down = ffn_w_down[i].astype(BF16)
    return dict(
        g_attn=attn_norm_g[i].reshape(1, Dm), w_main=w_main, w_small=w_small, wa_pad=wa_pad,
        b_a=gla_b_a[i].reshape(1, -1), gnorm=gla_norm_g[i].reshape(1, -1), w1cat=w1cat, w2s=w2s,
        w_out=w_out[i].astype(BF16), g_ffn=ffn_norm_g[i].reshape(1, Dm), w_up_a=w_up_a, w_up_u=w_up_u,
        conv_w=conv_w,
        conv_b=conv_b, w_down=w_down, w_proj=ple_w_proj[i].astype(BF16), w_gate=ple_w_gate[i].astype(BF16),
        b_gate=ple_b_gate[i].reshape(1, Dm), F=F, Fp=Fp)


def _kv_rows(kv, B, rows):
    t = kv.reshape(kv.shape[0], B, -1, NSA_KV, NSA_HD)[:, :, rows]
    return [t[i] for i in range(kv.shape[0])]


def _prompt_layer(h2d, p2d, B, L, W):
    Dm = h2d.shape[1]
    M = B * L
    proj, small, kv = norm_proj(h2d, W["g_attn"], W["w_main"], W["w_small"],
                                _pick_tile(M, (1024, 512, 256, 128)), 512)
    proj3 = proj.reshape(B, L, NMAIN)
    small3 = small.reshape(B, L, LANE)
    cb = _pick_tile(L, (128, 64, 32, 16))
    s0 = jnp.zeros((B, GLA_HEADS, GLA_DK, GLA_DV), F32)
    og, gla_state = gla(proj3, small3, W["wa_pad"], W["b_a"], W["gnorm"], s0, cb, GLA_SUB, cb,
                        _pick_tile(B, (4, 2, 1)))
    cmp = compress(proj3, W["w1cat"], W["w2s"])
    _, mmat = _block_matrices(L, L // CMP_STRIDE - 1)
    on = nsa_prompt(proj3, small3, cmp, mmat.T)
    h1, hn = out_proj(og.reshape(M, -1), on.reshape(M, -1), h2d, W["w_out"], W["g_ffn"],
                      _pick_tile(M, (512, 256, 128)))
    prefix = jnp.zeros((B, SUBLANE, W["Fp"]), F32)
    hmid, tail = ffn_up_seq(hn, W["w_up_a"], W["w_up_u"], W["conv_w"], W["conv_b"], prefix, L,
                            _pick_tile(L, (1024, 512, 256, 128)), 512)
    h2 = ffn_down(hmid, W["w_down"], h1, _pick_tile(M, (1024, 512, 256, 128)), 512)
    keep = min(WINDOW, L)
    outs = tuple(_kv_rows(kv[:4], B, slice(0, L)) + _kv_rows(kv[4:], B, slice(L - keep, L))
                 + [gla_state, tail[:, SUBLANE - (CONV_W - 1):, :W["F"]]])
    return h2, outs


def _decode_layer(h3d, p2d_tm, B, L, lp, caches, page_table, win_k, win_v, gla_s, conv_s, W):
    Dm = h3d.shape[2]
    Mp = B * lp
    x2 = h3d.reshape(Mp, Dm)
    proj, small, kv = norm_proj(x2, W["g_attn"], W["w_main"], W["w_small"],
                                _pick_tile(Mp, (1024, 512, 256, 128, 64, 32, 16, 8)), 512)
    proj3 = proj.reshape(B, lp, NMAIN)
    small3 = small.reshape(B, lp, LANE)
    og, gla_state = gla(proj3, small3, W["wa_pad"], W["b_a"], W["gnorm"], gla_s, lp, lp, L,
                        _pick_tile(B, (4, 2, 1)))
    plen = page_table.shape[1] * caches[0].shape[1] // NSA_KV
    emat, mmat = _block_matrices(plen + KCHUNK, (plen + L) // CMP_STRIDE - 1)
    wlen = win_k.shape[1]
    on, wk_new, wv_new = nsa_decode(proj3, small3, caches, page_table,
                                    win_k.reshape(B, wlen * NSA_KV, NSA_HD),
                                    win_v.reshape(B, wlen * NSA_KV, NSA_HD),
                                    W["w1cat"], W["w2s"], emat, mmat, L, _pick_tile(B, (2, 1)))
    h1, hn = out_proj(og.reshape(Mp, -1), on.reshape(Mp, -1), x2, W["w_out"], W["g_ffn"],
                      _pick_tile(Mp, (256, 128, 64, 32, 16, 8)))
    to_tm = lambda t: t.reshape(B, lp, Dm)[:, :L].transpose(1, 0, 2).reshape(L * B, Dm)
    h1_tm, hn_tm = to_tm(h1), to_tm(hn)
    prefix = jnp.pad(conv_s.transpose(1, 0, 2), ((0, 0), (0, 0), (0, W["Fp"] - W["F"])))
    hmid, tail = ffn_up_tm(hn_tm, W["w_up_a"], W["w_up_u"], W["conv_w"], W["conv_b"], prefix, L, 512)
    M = L * B
    h2 = ffn_down(hmid, W["w_down"], h1_tm, _pick_tile(M, (512, 256, 128, 64, 32, 16, 8)), 512)
    outs = tuple(_kv_rows(kv[:4], B, slice(0, L))
                 + [wk_new.reshape(win_k.shape), wv_new.reshape(win_v.shape),
                    gla_state, tail[:, :, :W["F"]].transpose(1, 0, 2)])
    return h2, outs


def kernel(x_prompt, x_sample, p_prompt, p_sample, cache_cmp_k, cache_cmp_v, cache_slc_k, cache_slc_v,
           page_table, state_win_k, state_win_v, state_gla, state_ffn_conv, attn_norm_g, w_in, gla_w_a2,
           gla_b_a, gla_norm_g, cmp_wk1, cmp_wk2, cmp_wv1, cmp_wv2, w_out, ffn_norm_g, ffn_w_up,
           ffn_conv_w, ffn_conv_b, ffn_w_down, ple_w_proj, ple_w_gate, ple_b_gate, final_norm_g):
    depth = w_in.shape[0]
    assert depth == 1, "layers are chained through HBM one at a time; only depth 1 is wired"
    Bp, Lp, Dm = x_prompt.shape
    Bs, Ls, _ = x_sample.shape
    W = _prep_weights(0, attn_norm_g, w_in, gla_w_a2, gla_b_a, gla_norm_g, cmp_wk1, cmp_wk2, cmp_wv1,
                      cmp_wv2, w_out, ffn_norm_g, ffn_w_up, ffn_conv_w, ffn_conv_b, ffn_w_down,
                      ple_w_proj, ple_w_gate, ple_b_gate)
    g_final = final_norm_g.reshape(1, Dm)

    Mp = Bp * Lp
    h2, outs_p = _prompt_layer(x_prompt.reshape(Mp, Dm), None, Bp, Lp, W)
    y_prompt = ple_final(h2, p_prompt[0].reshape(Mp, -1), W["w_gate"], W["b_gate"], W["w_proj"], g_final,
                         _pick_tile(Mp, (512, 256, 128))).reshape(Bp, Lp, Dm)

    lp = -(-Ls // SUBLANE) * SUBLANE
    xs = jnp.pad(x_sample, ((0, 0), (0, lp - Ls), (0, 0)))
    n_pool, psz = cache_cmp_k.shape[1], cache_cmp_k.shape[2]
    caches = [c[0].reshape(n_pool, psz * NSA_KV, NSA_HD)
              for c in (cache_cmp_k, cache_cmp_v, cache_slc_k, cache_slc_v)]
    h2s, outs_s = _decode_layer(xs, None, Bs, Ls, lp, caches, page_table, state_win_k[0], state_win_v[0],
                                state_gla[0], state_ffn_conv[0], W)
    p_tm = p_sample[0].transpose(1, 0, 2).reshape(Ls * Bs, -1)
    Ms = Ls * Bs
    y_tm = ple_final(h2s, p_tm, W["w_gate"], W["b_gate"], W["w_proj"], g_final,
                     _pick_tile(Ms, (256, 128, 64, 32, 16, 8)))
    y_sample = y_tm.reshape(Ls, Bs, Dm).transpose(1, 0, 2)

    lead = lambda t: t[None]
    return (y_prompt, y_sample) + tuple(lead(t) for t in outs_p) + tuple(lead(t) for t in outs_s)
```

```python
import functools
import math

import numpy as np
import jax
import jax.numpy as jnp
from jax import lax
from jax.experimental import pallas as pl
from jax.experimental.pallas import tpu as pltpu

F32 = jnp.float32
BF16 = jnp.bfloat16

GLA_HEADS = 4
GLA_DK = 128
GLA_DV = 256
GLA_RANK = 16
GLA_TAU = 16.0
GLA_SUB = 16
NSA_HEADS = 8
NSA_KV = 2
NSA_REP = NSA_HEADS // NSA_KV
NSA_HD = 128
CMP_LEN = 32
CMP_STRIDE = 16
SLC_LEN = 64
N_SELECT = 16
WINDOW = 512
CONV_W = 3
EPS = 1e-6
NEG = -1e30
M_FLOOR = 0.1 * NEG
FORCED = 1e6

LANE = 128
SUBLANE = 8
KCHUNK = 128
CMP_SEG_ROWS = NSA_KV * CMP_STRIDE
CMP_PITCH = CMP_SEG_ROWS + SUBLANE
VMEM_LIMIT = 56 * 1024 * 1024

GQ0 = 0
GK0 = GQ0 + GLA_HEADS * GLA_DK
GV0 = GK0 + GLA_HEADS * GLA_DK
GR0 = GV0 + GLA_HEADS * GLA_DV
NQ0 = GR0 + GLA_HEADS * GLA_DV
KVW = NSA_KV * NSA_HD
KC0 = NQ0 + NSA_HEADS * NSA_HD
VC0 = KC0 + KVW
KS0 = VC0 + KVW
VS0 = KS0 + KVW
KW0 = VS0 + KVW
VW0 = KW0 + KVW
NMAIN = VW0 + KVW
GATE0 = GLA_RANK


def _cp(sem):
    return pltpu.CompilerParams(dimension_semantics=sem, vmem_limit_bytes=VMEM_LIMIT)


def _dot(a, b):
    return jnp.dot(a, b, preferred_element_type=F32)


def _dot_nt(a, b):
    return lax.dot_general(a, b, (((1,), (1,)), ((), ())), preferred_element_type=F32)


def _rms(x, g):
    return x * lax.rsqrt(jnp.mean(x * x, axis=-1, keepdims=True) + EPS) * g


def _norm_proj_kernel(x_ref, g_ref, wm_ref, ws_ref, om_ref, os_ref, kv_ref, xn_ref, *, j_kv0):
    j = pl.program_id(1)

    @pl.when(j == 0)
    def _():
        xn = _rms(x_ref[...], g_ref[...]).astype(BF16)
        xn_ref[...] = xn
        os_ref[...] = _dot(xn, ws_ref[...])

    tile = _dot(xn_ref[...], wm_ref[...])
    om_ref[...] = tile

    @pl.when(j >= j_kv0)
    def _():
        tm = tile.shape[0]
        for a in range(tile.shape[1] // KVW):
            for g in range(NSA_KV):
                c0 = a * KVW + g * NSA_HD
                kv_ref[a, pl.ds(g, tm, stride=NSA_KV), :] = tile[:, c0:c0 + NSA_HD]


def norm_proj(x2, g, w_main, w_small, tm, tn):
    M, Dm = x2.shape
    N = w_main.shape[1]
    assert KC0 % tn == 0 and tn % KVW == 0
    j_kv0 = KC0 // tn
    per_tile = tn // KVW
    n_kv = (N - KC0) // KVW
    return pl.pallas_call(
        functools.partial(_norm_proj_kernel, j_kv0=j_kv0),
        grid=(M // tm, N // tn),
        in_specs=[pl.BlockSpec((tm, Dm), lambda i, j: (i, 0)),
                  pl.BlockSpec((1, Dm), lambda i, j: (0, 0)),
                  pl.BlockSpec((Dm, tn), lambda i, j: (0, j)),
                  pl.BlockSpec((Dm, LANE), lambda i, j: (0, 0))],
        out_specs=[pl.BlockSpec((tm, tn), lambda i, j: (i, j)),
                   pl.BlockSpec((tm, LANE), lambda i, j: (i, 0)),
                   pl.BlockSpec((per_tile, NSA_KV * tm, NSA_HD),
                                lambda i, j: (jnp.maximum(j - j_kv0, 0), i, 0))],
        out_shape=[jax.ShapeDtypeStruct((M, N), F32), jax.ShapeDtypeStruct((M, LANE), F32),
                   jax.ShapeDtypeStruct((n_kv, NSA_KV * M, NSA_HD), F32)],
        scratch_shapes=[pltpu.VMEM((tm, Dm), BF16)],
        compiler_params=_cp(("parallel", "arbitrary")),
        name="norm_proj")(x2, g, w_main, w_small)


def _gla_kernel(q_ref, k_ref, v_ref, r_ref, sm_ref, wa_ref, ba_ref, gn_ref, s0_ref,
                o_ref, s_ref, st_ref, *, cb, sub, valid, nbb):
    c = pl.program_id(1)

    @pl.when(c == 0)
    def _():
        st_ref[...] = s0_ref[...]

    GH = GLA_HEADS
    H = nbb * GH
    rows_in = q_ref.shape[1]

    def take(ref, s):
        x = ref[s]
        if rows_in < cb:
            x = jnp.concatenate([x, jnp.zeros((cb - rows_in, x.shape[1]), x.dtype)], axis=0)
        return x

    side = lambda ref: take(ref, 0) if nbb == 1 else jnp.concatenate(
        [take(ref, s) for s in range(nbb)], axis=1)
    hk = lambda h: slice(h * GLA_DK, (h + 1) * GLA_DK)
    hv = lambda h: slice(h * GLA_DV, (h + 1) * GLA_DV)
    ri = lax.broadcasted_iota(jnp.int32, (cb, cb), 0)
    ci = lax.broadcasted_iota(jnp.int32, (cb, cb), 1)
    tri = jnp.where(ci <= ri, 1.0, 0.0)
    rowv = lax.broadcasted_iota(jnp.int32, (cb, 1), 0)
    q = side(q_ref) * (GLA_DK ** -0.5)
    k = side(k_ref)
    v = side(v_ref)
    pre = [_dot(take(sm_ref, s).astype(BF16), wa_ref[...]) + ba_ref[...] for s in range(nbb)]
    pre = pre[0] if nbb == 1 else jnp.concatenate(pre, axis=1)
    log_a = jax.nn.log_sigmoid(pre) / GLA_TAU
    if valid < cb:
        log_a = jnp.where(rowv < valid, log_a, 0.0)
        k = jnp.where(rowv < valid, k, 0.0)
        v = jnp.where(rowv < valid, v, 0.0)
    hi = log_a.astype(BF16).astype(F32)
    lo = (log_a - hi).astype(BF16).astype(F32)
    b = _dot(tri, hi) + _dot(tri, lo)
    b_last = b[cb - 1:cb, :]
    S = [st_ref[h // GH, h % GH] for h in range(H)]
    vb = v.astype(BF16)

    qe = (q * jnp.exp(b)).astype(BF16)
    o = [_dot(qe[:, hk(h)], S[h].astype(BF16)) for h in range(H)]
    a_rows = [[] for _ in range(H)]
    cc = lax.broadcasted_iota(jnp.int32, (sub, cb), 1)
    for blk in range(cb // sub):
        lo_r, hi_r = blk * sub, (blk + 1) * sub
        r_dec = jnp.zeros((1, H * GLA_DK), F32) if blk == 0 else b[lo_r - 1:lo_r, :]
        q_i = q[lo_r:hi_r] * jnp.exp(b[lo_r:hi_r] - r_dec)
        k_i = k * jnp.exp(jnp.where(rowv < hi_r, r_dec - b, 0.0))
        rr = lax.broadcasted_iota(jnp.int32, (sub, cb), 0) + lo_r
        for h in range(H):
            a = _dot_nt(q_i[:, hk(h)], k_i[:, hk(h)])
            a_rows[h].append(jnp.where(cc <= rr, a, 0.0))
    for h in range(H):
        a = a_rows[h][0] if len(a_rows[h]) == 1 else jnp.concatenate(a_rows[h], axis=0)
        o[h] = o[h] + _dot(a, v[:, hv(h)])

    kh = k * jnp.exp(b_last - b)
    dec = jnp.exp(b_last)
    if cb < LANE:
        kh = jnp.concatenate([kh, jnp.zeros((LANE - cb, H * GLA_DK), F32)], axis=0)
        vpad = jnp.concatenate([vb, jnp.zeros((LANE - cb, H * GLA_DV), BF16)], axis=0)
    else:
        vpad = vb
    for h in range(H):
        dec_col = jnp.transpose(jnp.broadcast_to(dec[:, hk(h)], (GLA_DK, GLA_DK)))[:, 0:1]
        s_new = dec_col * S[h] + _dot(jnp.transpose(kh[:, hk(h)]).astype(BF16), vpad[:, hv(h)])
        st_ref[h // GH, h % GH] = s_new
        s_ref[h // GH, h % GH] = s_new

    r = side(r_ref)
    gate = r * jax.nn.sigmoid(r)
    for h in range(H):
        out = (_rms(o[h], gn_ref[...]) * gate[:, hv(h)]).astype(o_ref.dtype)
        o_ref[h // GH, :, hv(h % GH)] = out[:rows_in]


def gla(proj3, small3, wa_pad, b_a, gnorm, s0, cb, sub, valid, nbb, out_dtype):
    B, L, _ = proj3.shape
    rb = min(cb, L)
    assert L % rb == 0 and valid <= rb
    H = GLA_HEADS
    qw, vw = H * GLA_DK, H * GLA_DV
    kern = functools.partial(_gla_kernel, cb=cb, sub=sub, valid=valid, nbb=nbb)
    return pl.pallas_call(
        kern,
        grid=(B // nbb, L // rb),
        in_specs=[pl.BlockSpec((nbb, rb, qw), lambda b, c: (b, c, GQ0 // qw)),
                  pl.BlockSpec((nbb, rb, qw), lambda b, c: (b, c, GK0 // qw)),
                  pl.BlockSpec((nbb, rb, vw), lambda b, c: (b, c, GV0 // vw)),
                  pl.BlockSpec((nbb, rb, vw), lambda b, c: (b, c, GR0 // vw)),
                  pl.BlockSpec((nbb, rb, LANE), lambda b, c: (b, c, 0)),
                  pl.BlockSpec((LANE, qw), lambda b, c: (0, 0)),
                  pl.BlockSpec((1, qw), lambda b, c: (0, 0)),
                  pl.BlockSpec((1, GLA_DV), lambda b, c: (0, 0)),
                  pl.BlockSpec((nbb, H, GLA_DK, GLA_DV), lambda b, c: (b, 0, 0, 0))],
        out_specs=[pl.BlockSpec((nbb, rb, vw), lambda b, c: (b, c, 0)),
                   pl.BlockSpec((nbb, H, GLA_DK, GLA_DV), lambda b, c: (b, 0, 0, 0))],
        out_shape=[jax.ShapeDtypeStruct((B, L, vw), out_dtype),
                   jax.ShapeDtypeStruct((B, H, GLA_DK, GLA_DV), F32)],
        scratch_shapes=[pltpu.VMEM((nbb, H, GLA_DK, GLA_DV), F32)],
        compiler_params=_cp(("parallel", "arbitrary")),
        name="gla")(proj3, proj3, proj3, proj3, small3, wa_pad, b_a, gnorm, s0)


def _pair_rows(load_rows, j):
    return jnp.concatenate([load_rows(j), load_rows(j + 1)], axis=1).astype(BF16)


def _pair_weights(w1_ref, j):
    w = w1_ref[pl.ds(j, 2)]
    return w.reshape(2 * NSA_HD, 2 * NSA_HD)


def _compress_group(load_rows, w1_ref, w2, nseg):
    acc = jnp.zeros((nseg, 2 * NSA_HD), F32)
    for j in range(0, CMP_STRIDE, 2):
        acc = acc + _dot(_pair_rows(load_rows, j), _pair_weights(w1_ref, j))
    first = acc[:, :NSA_HD]
    second = acc[:, NSA_HD:]
    h = jax.nn.gelu(first + pltpu.roll(second, nseg - 1, axis=0))
    return _dot(h.astype(BF16), w2)


def _compress_groups(loads, w1_refs, w2s, nrow):
    n = len(loads)
    acc = [jnp.zeros((nrow, 2 * NSA_HD), F32) for _ in range(n)]
    for j in range(0, CMP_STRIDE, 2):
        for i in range(n):
            acc[i] = acc[i] + _dot(_pair_rows(loads[i], j), _pair_weights(w1_refs[i], j))
    hs = [jax.nn.gelu(a[:, :NSA_HD] + pltpu.roll(a[:, NSA_HD:], nrow - 1, axis=0)) for a in acc]
    return [_dot(h.astype(BF16), w2) for h, w2 in zip(hs, w2s)]


def _cmp_branch(q4, ck, cv, qpos4, nb):
    s = _dot_nt(q4, ck.astype(BF16))
    n = lax.broadcasted_iota(jnp.int32, s.shape, 1)
    mask = jnp.where(n * CMP_STRIDE + (CMP_LEN - 1) <= qpos4, n, nb) < nb
    s = jnp.where(mask, s, NEG)
    m = jnp.max(s, axis=-1, keepdims=True)
    e = jnp.where(mask, jnp.exp(s - m), 0.0)
    p = e / jnp.maximum(jnp.sum(e, axis=-1, keepdims=True), 1e-30)
    return p, _dot(p.astype(BF16), cv.astype(BF16))


def _select(pg, mmat, qpos_tok, nslc, nsel):
    hi = pg.astype(BF16)
    lo = (pg - hi.astype(F32)).astype(BF16)
    imp = _dot(hi, mmat) + _dot(lo, mmat)
    blk = lax.broadcasted_iota(jnp.int32, pg.shape, 1)
    cur = qpos_tok // SLC_LEN
    forced = jnp.where(blk == 0, 1, 0) + jnp.where(blk == cur, 1, 0) + jnp.where(blk == cur - 1, 1, 0)
    score = jnp.where(blk <= cur, jnp.where(forced > 0, FORCED, imp), -1.0)
    score = jnp.where(blk < nslc, score, -2.0)
    rank = jnp.zeros(pg.shape, F32)
    for j in range(nslc):
        sj = score[:, j:j + 1]
        tie = jnp.where(blk > j, sj, NEG)
        rank = rank + jnp.where(sj > score, 1.0, 0.0) + jnp.where(tie == score, 1.0, 0.0)
    return jnp.where(rank < nsel, jnp.where(blk < nslc, 1.0, 0.0), 0.0)


def _flash_init(m_ref, l_ref, acc_ref):
    m_ref[...] = jnp.full(m_ref.shape, NEG, F32)
    l_ref[...] = jnp.zeros(l_ref.shape, F32)
    acc_ref[...] = jnp.zeros(acc_ref.shape, F32)


def _flash_step(q4, k, v, mask, m_ref, l_ref, acc_ref):
    s = jnp.where(mask, _dot_nt(q4, k), NEG)
    m_old = m_ref[...]
    m_new = jnp.maximum(m_old, jnp.max(s, axis=-1, keepdims=True))
    alpha = jnp.exp(m_old - m_new)
    p = jnp.where(mask, jnp.exp(s - m_new), 0.0)
    l_ref[...] = alpha * l_ref[...] + jnp.sum(p, axis=-1, keepdims=True)
    acc_ref[...] = alpha * acc_ref[...] + _dot(p.astype(BF16), v)
    m_ref[...] = m_new


def _flash_out(l_ref, acc_ref):
    return acc_ref[...] / jnp.maximum(l_ref[...], 1e-30)


def _tile_rows(x, reps):
    return jnp.concatenate([x] * reps, axis=0)


def _attend(q4, keys, vals, masks):
    s = [jnp.where(mk, _dot_nt(q4, k), NEG) for k, mk in zip(keys, masks)]
    m = s[0].max(axis=-1, keepdims=True)
    for si in s[1:]:
        m = jnp.maximum(m, si.max(axis=-1, keepdims=True))
    e = [jnp.where(mk, jnp.exp(si - m), 0.0) for si, mk in zip(s, masks)]
    l = sum(ei.sum(axis=-1, keepdims=True) for ei in e)
    o = sum(_dot(ei.astype(BF16), v) for ei, v in zip(e, vals))
    return o / jnp.maximum(l, 1e-30)


def _compress_kernel(x_ref, w1_ref, w2_ref, o_ref, *, nseg):
    load = lambda j: x_ref[0, pl.ds(j, nseg, stride=CMP_STRIDE), :]
    o_ref[0, 0, 0] = _compress_group(load, w1_ref.at[0], w2_ref[0].astype(BF16), nseg)


def compress(proj3, w1cat, w2s):
    B, L, _ = proj3.shape
    nseg = L // CMP_STRIDE
    G = NSA_KV
    return pl.pallas_call(
        functools.partial(_compress_kernel, nseg=nseg),
        grid=(B, 2, G),
        in_specs=[pl.BlockSpec((1, L, NSA_HD), lambda b, w, g: (b, 0, KC0 // NSA_HD + w * G + g)),
                  pl.BlockSpec((1, CMP_STRIDE, NSA_HD, 2 * NSA_HD), lambda b, w, g: (w, 0, 0, 0)),
                  pl.BlockSpec((1, NSA_HD, NSA_HD), lambda b, w, g: (w, 0, 0))],
        out_specs=pl.BlockSpec((1, 1, 1, nseg, NSA_HD), lambda b, w, g: (w, b, g, 0, 0)),
        out_shape=jax.ShapeDtypeStruct((2, B, G, nseg, NSA_HD), F32),
        compiler_params=_cp(("parallel", "parallel", "parallel")),
        name="compress")(proj3, w1cat, w2s)


def _select_t(pg_t, mm_t, qpos_row, nslc, nsel):
    nsp = -(-nslc // SUBLANE) * SUBLANE
    hi = pg_t.astype(BF16)
    lo = (pg_t - hi.astype(F32)).astype(BF16)
    imp = (_dot(mm_t, hi) + _dot(mm_t, lo))[:nsp]
    blk = lax.broadcasted_iota(jnp.int32, imp.shape, 0)
    cur = qpos_row // SLC_LEN
    forced = jnp.where(blk == 0, 1, 0) + jnp.where(blk == cur, 1, 0) + jnp.where(blk == cur - 1, 1, 0)
    score = jnp.where(blk <= cur, jnp.where(forced > 0, FORCED, imp), -1.0)
    score = jnp.where(blk < nslc, score, -2.0)
    rank = jnp.zeros(imp.shape, F32)
    for j in range(nslc):
        sj = score[j:j + 1, :]
        tie = jnp.where(blk > j, sj, NEG)
        rank = rank + jnp.where(sj > score, 1.0, 0.0) + jnp.where(tie == score, 1.0, 0.0)
    return jnp.where(rank < nsel, jnp.where(blk < nslc, 1.0, 0.0), 0.0)


def _flash_step_t(q_t, k, v_t, bias, m_ref, l_ref, acc_ref):
    s = _dot(k, q_t) + bias
    m_old = m_ref[...]
    m_new = jnp.maximum(m_old, jnp.max(s, axis=0, keepdims=True))
    alpha = jnp.exp(m_old - m_new)
    p = jnp.exp(s - m_new)
    l_ref[...] = alpha * l_ref[...] + jnp.sum(p, axis=0, keepdims=True)
    acc_ref[...] = alpha * acc_ref[...] + _dot(v_t, p.astype(BF16))
    m_ref[...] = m_new


def _nsa_prompt_kernel(q_ref, sm_ref, ck_ref, cv_ref, ks_ref, vs_ref, kw_ref, vw_ref, mm_ref,
                       o_ref, vst_ref, vwt_ref, sel_ref, m_ref, l_ref, acc_ref, mw_ref, lw_ref, accw_ref,
                       *, tq, kstep, nslc, nsel, nb):
    g = pl.program_id(1)
    qb = pl.program_id(2)
    R = NSA_REP
    T = R * tq
    L = ks_ref.shape[1]
    scale = NSA_HD ** -0.5

    @pl.when(qb == 0)
    def _():
        for c in range(L // KCHUNK):
            cs = slice(c * KCHUNK, (c + 1) * KCHUNK)
            vst_ref[:, cs] = jnp.transpose(vs_ref[0, cs, :]).astype(BF16)
            vwt_ref[:, cs] = jnp.transpose(vw_ref[0, cs, :]).astype(BF16)

    q_t = (jnp.concatenate([jnp.transpose(q_ref[0, :, r * NSA_HD:(r + 1) * NSA_HD]) for r in range(R)],
                           axis=1) * scale).astype(BF16)
    qpos_row = qb * tq + lax.broadcasted_iota(jnp.int32, (1, tq), 1)
    qpos4 = jnp.concatenate([qpos_row] * R, axis=1)

    s = _dot(ck_ref[0, 0, 0].astype(BF16), q_t)
    n = lax.broadcasted_iota(jnp.int32, s.shape, 0)
    mask = jnp.where(n * CMP_STRIDE + (CMP_LEN - 1) <= qpos4, n, nb) < nb
    s = jnp.where(mask, s, NEG)
    e = jnp.where(mask, jnp.exp(s - jnp.max(s, axis=0, keepdims=True)), 0.0)
    p = e / jnp.maximum(jnp.sum(e, axis=0, keepdims=True), 1e-30)
    o_c = _dot(jnp.transpose(cv_ref[0, 0, 0]).astype(BF16), p.astype(BF16))
    pg = p[:, 0:tq]
    for r in range(1, R):
        pg = pg + p[:, r * tq:(r + 1) * tq]
    sel_ref[...] = (_select_t(pg, mm_ref[...], qpos_row, nslc, nsel) - 1.0) * (-NEG)

    key = lax.broadcasted_iota(jnp.int32, (kstep, T), 0)
    hi_step = (qb * tq + tq - 1) // kstep + 1
    lo_step = jnp.maximum(qb * tq - (WINDOW - 1), 0) // kstep

    sel_state = (m_ref, l_ref, acc_ref)
    win_state = (mw_ref, lw_ref, accw_ref)
    for m_r, l_r, acc_r in (sel_state, win_state):
        m_r[...] = jnp.full(m_r.shape, M_FLOOR, F32)
        l_r[...] = jnp.zeros(l_r.shape, F32)
        acc_r[...] = jnp.zeros(acc_r.shape, F32)

    def sel_step(c, causal):
        off = pl.multiple_of(c * kstep, kstep)
        k = ks_ref[0, pl.ds(off, kstep), :].astype(BF16)
        v_t = vst_ref[:, pl.ds(off, kstep)]
        blocks = [jnp.broadcast_to(sel_ref[pl.ds((kstep // SLC_LEN) * c + i, 1), :], (SLC_LEN, tq))
                  for i in range(kstep // SLC_LEN)]
        bias = jnp.concatenate(blocks, axis=0)
        bias = jnp.concatenate([bias] * R, axis=1)
        if causal:
            bias = jnp.where(off + key <= qpos4, bias, NEG)
        _flash_step_t(q_t, k, v_t, bias, *sel_state)

    def win_step(c, causal):
        off = pl.multiple_of(c * kstep, kstep)
        k = kw_ref[0, pl.ds(off, kstep), :].astype(BF16)
        v_t = vwt_ref[:, pl.ds(off, kstep)]
        kpos = off + key
        if causal:
            bias = jnp.where(kpos <= qpos4, 0.0, NEG)
        else:
            bias = jnp.where(kpos > qpos4 - WINDOW, 0.0, NEG)
        _flash_step_t(q_t, k, v_t, bias, *win_state)

    def body_sel(c, carry):
        sel_step(c, False)
        return carry

    def body_both(c, carry):
        sel_step(c, False)
        win_step(c, False)
        return carry

    last = hi_step - 1
    lax.fori_loop(0, jnp.minimum(lo_step, last), body_sel, 0)
    lax.fori_loop(lo_step, last, body_both, 0)
    sel_step(last, True)
    win_step(last, True)
    o_s = acc_ref[...] / jnp.maximum(l_ref[...], 1e-30)
    o_w = accw_ref[...] / jnp.maximum(lw_ref[...], 1e-30)

    g_t = jnp.transpose(jax.nn.sigmoid(sm_ref[0]))
    for r in range(R):
        def gate(br, r=r):
            c0 = GATE0 + br * NSA_HEADS + r
            return jnp.where(g == 0, g_t[c0:c0 + 1, :], g_t[c0 + R:c0 + R + 1, :])
        cs = slice(r * tq, (r + 1) * tq)
        o = gate(0) * o_c[:, cs] + gate(1) * o_s[:, cs] + gate(2) * o_w[:, cs]
        o_ref[0, :, r * NSA_HD:(r + 1) * NSA_HD] = jnp.transpose(o).astype(o_ref.dtype)


def nsa_prompt(proj3, small3, cmp, mmat_t):
    B, L, _ = proj3.shape
    tq = _pick_tile(L, (2 * KCHUNK, KCHUNK))
    nseg = L // CMP_STRIDE
    nb = nseg - 1
    nslc = -(-L // SLC_LEN)
    nsel = min(N_SELECT, nslc)
    nsp = -(-nslc // SUBLANE) * SUBLANE
    G, R = NSA_KV, NSA_REP
    kstep = _pick_tile(L, (2 * KCHUNK, KCHUNK))
    assert nseg == LANE and L % KCHUNK == 0 and kstep % SLC_LEN == 0
    kv_spec = lambda c0: pl.BlockSpec((1, L, NSA_HD), lambda b, g, qb: (b, 0, c0 // NSA_HD + g))
    kern = functools.partial(_nsa_prompt_kernel, tq=tq, kstep=kstep, nslc=nslc, nsel=nsel, nb=nb)
    return pl.pallas_call(
        kern,
        grid=(B, G, L // tq),
        in_specs=[pl.BlockSpec((1, tq, R * NSA_HD), lambda b, g, qb: (b, qb, NQ0 // (R * NSA_HD) + g)),
                  pl.BlockSpec((1, tq, LANE), lambda b, g, qb: (b, qb, 0)),
                  pl.BlockSpec((1, 1, 1, nseg, NSA_HD), lambda b, g, qb: (0, b, g, 0, 0)),
                  pl.BlockSpec((1, 1, 1, nseg, NSA_HD), lambda b, g, qb: (1, b, g, 0, 0)),
                  kv_spec(KS0), kv_spec(VS0), kv_spec(KW0), kv_spec(VW0),
                  pl.BlockSpec((LANE, LANE), lambda b, g, qb: (0, 0))],
        out_specs=pl.BlockSpec((1, tq, R * NSA_HD), lambda b, g, qb: (b, qb, g)),
        out_shape=jax.ShapeDtypeStruct((B, L, NSA_HEADS * NSA_HD), BF16),
        scratch_shapes=[pltpu.VMEM((NSA_HD, L), BF16),
                        pltpu.VMEM((NSA_HD, L), BF16),
                        pltpu.VMEM((nsp, tq), F32)]
                       + 2 * [pltpu.VMEM((1, R * tq), F32),
                              pltpu.VMEM((1, R * tq), F32),
                              pltpu.VMEM((NSA_HD, R * tq), F32)],
        compiler_params=_cp(("parallel", "parallel", "arbitrary")),
        name="nsa_prompt")(proj3, small3, cmp, cmp, proj3, proj3, proj3, proj3, mmat_t)


def _nsa_decode_kernel(pt_ref, q_ref, sm_ref, ksn_ref, vsn_ref, kwn_ref, vwn_ref, wk_ref, wv_ref,
                       w1_ref, w2_ref, e_ref, mm_ref, c0_hbm, c1_hbm, c2_hbm, c3_hbm,
                       o_ref, wko_ref, wvo_ref, cbuf_ref, sbuf_ref, sem_ref,
                       *, ns, lp, lreal, plen, rows_pp, n_pages, wlen, nslc, nsel, nb):
    b = pl.program_id(0)
    slot = b % 2
    caches = (c0_hbm, c1_hbm, c2_hbm, c3_hbm)
    G = NSA_KV
    R = NSA_REP
    T = R * lp
    segs_pp = rows_pp // CMP_SEG_ROWS

    def copies(step, sl):
        out = []
        for q in range(ns):
            for p in range(n_pages):
                page = pt_ref[step * ns + q, p]
                for w in range(2):
                    for s in range(segs_pp):
                        out.append(pltpu.make_async_copy(
                            caches[w].at[page, pl.ds(s * CMP_SEG_ROWS, CMP_SEG_ROWS)],
                            cbuf_ref.at[sl, q, w, pl.ds((p * segs_pp + s) * CMP_PITCH, CMP_SEG_ROWS)],
                            sem_ref.at[sl, w]))
                    out.append(pltpu.make_async_copy(
                        caches[2 + w].at[page], sbuf_ref.at[sl, q, w, pl.ds(p * rows_pp, rows_pp)],
                        sem_ref.at[sl, 2 + w]))
        return out

    @pl.when(b == 0)
    def _():
        for cp in copies(0, 0):
            cp.start()

    @pl.when(b + 1 < pl.num_programs(0))
    def _():
        for cp in copies(b + 1, 1 - slot):
            cp.start()

    for w in range(len(caches)):
        for q in range(ns):
            whole = sbuf_ref.at[slot, q, 0]
            pltpu.make_async_copy(whole, whole, sem_ref.at[slot, w]).wait()

    scale = NSA_HD ** -0.5
    nseg = plen // CMP_STRIDE
    tok = lax.broadcasted_iota(jnp.int32, (lp, 1), 0)
    qpos_tok = plen + tok
    qpos4 = _tile_rows(qpos_tok, R)
    lane_n = lax.broadcasted_iota(jnp.int32, (T, KCHUNK), 1)
    new_pos = jnp.where(lane_n < lreal, plen + lane_n, qpos4 + 1)
    past_pos = lax.broadcasted_iota(jnp.int32, (T, plen), 1)
    win_pos = plen - wlen + lax.broadcasted_iota(jnp.int32, (T, wlen), 1)
    win_pos = jnp.where(win_pos > qpos4 - WINDOW, win_pos, qpos4 + 1)
    win_pos = jnp.where(win_pos >= 0, win_pos, qpos4 + 1)
    new_win_pos = jnp.where(new_pos > qpos4 - WINDOW, new_pos, qpos4 + 1)
    lin = q_ref.shape[1]

    def rows(x):
        if lin < lp:
            x = jnp.concatenate([x, jnp.zeros((lp - lin, x.shape[1]), x.dtype)], axis=0)
        return x

    zpad_f = jnp.zeros((KCHUNK - lp, NSA_HD), F32)
    pad_new = lambda ref, q, ls: jnp.concatenate([rows(ref[q, :, ls]), zpad_f], axis=0).astype(BF16)
    gls = lambda g: slice(g * NSA_HD, (g + 1) * NSA_HD)
    pairs = [(q, g) for q in range(ns) for g in range(G)]
    q4 = [(jnp.concatenate([rows(q_ref[q, :, (g * R + r) * NSA_HD:(g * R + r + 1) * NSA_HD])
                            for r in range(R)], axis=0) * scale).astype(BF16) for q, g in pairs]

    def seg_rows(w, j):
        return jnp.concatenate([cbuf_ref[slot, q, w, pl.ds(G * j + g, nseg, stride=CMP_PITCH), :]
                                for q, g in pairs], axis=0)
    cmp = _compress_groups([functools.partial(seg_rows, w) for w in range(2)],
                           [w1_ref.at[w] for w in range(2)],
                           [w2_ref[w].astype(BF16) for w in range(2)], len(pairs) * nseg)
    prow = lambda i: slice(i * nseg, (i + 1) * nseg)
    branch = [_cmp_branch(q4[i], cmp[0][prow(i)], cmp[1][prow(i)], qpos4, nb) for i in range(len(pairs))]
    pgs = []
    for i in range(len(pairs)):
        p = branch[i][0]
        pg = p[0:lp]
        for r in range(1, R):
            pg = pg + p[r * lp:(r + 1) * lp]
        pgs.append(pg)
    sel = _select(jnp.concatenate(pgs, axis=0), mm_ref[...], _tile_rows(qpos_tok, len(pairs)), nslc, nsel)
    selm_all = _dot(sel.astype(BF16), e_ref[...])

    o_s, o_w = [], []
    for i, (q, g) in enumerate(pairs):
        selm = _tile_rows(selm_all[i * lp:(i + 1) * lp], R)
        k_past = sbuf_ref[slot, q, 0, pl.ds(g, plen, stride=G), :].astype(BF16)
        v_past = sbuf_ref[slot, q, 1, pl.ds(g, plen, stride=G), :].astype(BF16)
        m_past = jnp.where(selm[:, :plen] > 0.5, past_pos, qpos4 + 1) <= qpos4
        m_new = jnp.where(selm[:, plen:] > 0.5, new_pos, qpos4 + 1) <= qpos4
        o_s.append(_attend(q4[i], [k_past, pad_new(ksn_ref, q, gls(g))],
                           [v_past, pad_new(vsn_ref, q, gls(g))], [m_past, m_new]))
    for i, (q, g) in enumerate(pairs):
        k_win = wk_ref[q, pl.ds(g, wlen, stride=G), :].astype(BF16)
        v_win = wv_ref[q, pl.ds(g, wlen, stride=G), :].astype(BF16)
        o_w.append(_attend(q4[i], [k_win, pad_new(kwn_ref, q, gls(g))],
                           [v_win, pad_new(vwn_ref, q, gls(g))], [win_pos <= qpos4, new_win_pos <= qpos4]))

    for i, (q, g) in enumerate(pairs):
        gts = jax.nn.sigmoid(rows(sm_ref[q]))
        for r in range(R):
            c0 = GATE0 + g * R + r
            rs = slice(r * lp, (r + 1) * lp)
            o = (gts[:, c0:c0 + 1] * branch[i][1][rs]
                 + gts[:, c0 + NSA_HEADS:c0 + NSA_HEADS + 1] * o_s[i][rs]
                 + gts[:, c0 + 2 * NSA_HEADS:c0 + 2 * NSA_HEADS + 1] * o_w[i][rs])
            hq = (g * R + r) * NSA_HD
            o_ref[q, :, hq:hq + NSA_HD] = o[:lin].astype(o_ref.dtype)

    wrows = wlen * G
    shift = lreal * G
    for q in range(ns):
        for src, new, dst in ((wk_ref, kwn_ref, wko_ref), (wv_ref, vwn_ref, wvo_ref)):
            dst[q, 0:wrows - shift, :] = src[q, shift:wrows, :]
            for t in range(lreal):
                for g in range(G):
                    row = wrows - shift + t * G + g
                    dst[q, row:row + 1, :] = new[q, t:t + 1, g * NSA_HD:(g + 1) * NSA_HD]


def nsa_decode(proj3, small3, caches, page_table, win_k, win_v, w1cat, w2s, emat, mmat, lreal, ns):
    B, lin, _ = proj3.shape
    lp = -(-lin // SUBLANE) * SUBLANE
    G = NSA_KV
    n_pages = page_table.shape[1]
    rows_pp = caches[0].shape[1]
    plen = n_pages * rows_pp // G
    wlen = win_k.shape[1] // G
    assert plen % CMP_STRIDE == 0 and lreal < CMP_STRIDE and lreal <= lin and B % ns == 0
    nb = (plen + lreal) // CMP_STRIDE - 1
    nslc = -(-(plen + lreal) // SLC_LEN)
    nsel = min(N_SELECT, nslc)
    new_spec = lambda c0: pl.BlockSpec((ns, lin, KVW), lambda b, pt: (b, 0, c0 // KVW))
    const = lambda shape: pl.BlockSpec(shape, lambda b, pt: (0,) * len(shape))
    hbm = pl.BlockSpec(memory_space=pl.ANY)
    kern = functools.partial(_nsa_decode_kernel, ns=ns, lp=lp, lreal=lreal, plen=plen, rows_pp=rows_pp,
                             n_pages=n_pages, wlen=wlen, nslc=nslc, nsel=nsel, nb=nb)
    grid_spec = pltpu.PrefetchScalarGridSpec(
        num_scalar_prefetch=1,
        grid=(B // ns,),
        in_specs=[pl.BlockSpec((ns, lin, NSA_HEADS * NSA_HD), lambda b, pt: (b, 0, NQ0 // (NSA_HEADS * NSA_HD))),
                  pl.BlockSpec((ns, lin, LANE), lambda b, pt: (b, 0, 0)),
                  new_spec(KS0), new_spec(VS0), new_spec(KW0), new_spec(VW0),
                  pl.BlockSpec((ns, wlen * G, NSA_HD), lambda b, pt: (b, 0, 0)),
                  pl.BlockSpec((ns, wlen * G, NSA_HD), lambda b, pt: (b, 0, 0)),
                  const((2, CMP_STRIDE, NSA_HD, 2 * NSA_HD)),
                  const((2, NSA_HD, NSA_HD)),
                  const((LANE, plen + KCHUNK)),
                  const((LANE, LANE)),
                  hbm, hbm, hbm, hbm],
        out_specs=[pl.BlockSpec((ns, lin, NSA_HEADS * NSA_HD), lambda b, pt: (b, 0, 0)),
                   pl.BlockSpec((ns, wlen * G, NSA_HD), lambda b, pt: (b, 0, 0)),
                   pl.BlockSpec((ns, wlen * G, NSA_HD), lambda b, pt: (b, 0, 0))],
        scratch_shapes=[pltpu.VMEM((2, ns, 2, n_pages * rows_pp // CMP_SEG_ROWS * CMP_PITCH, NSA_HD), F32),
                        pltpu.VMEM((2, ns, 2, n_pages * rows_pp, NSA_HD), F32),
                        pltpu.SemaphoreType.DMA((2, len(caches)))])
    return pl.pallas_call(
        kern,
        grid_spec=grid_spec,
        out_shape=[jax.ShapeDtypeStruct((B, lin, NSA_HEADS * NSA_HD), F32 if lin < lp else BF16),
                   jax.ShapeDtypeStruct(win_k.shape, F32), jax.ShapeDtypeStruct(win_v.shape, F32)],
        compiler_params=_cp(("arbitrary",)),
        name="nsa_decode")(page_table, proj3, small3, proj3, proj3, proj3, proj3, win_k, win_v,
                           w1cat, w2s, emat, mmat, caches[0], caches[1], caches[2], caches[3])


def _out_proj_kernel(og_ref, on_ref, x_ref, w0_ref, w1_ref, g_ref, h_ref, hn_ref):
    h = (x_ref[...] + _dot(og_ref[...].astype(BF16), w0_ref[...])
         + _dot(on_ref[...].astype(BF16), w1_ref[...]))
    h_ref[...] = h
    hn_ref[...] = _rms(h, g_ref[...]).astype(BF16)


def out_proj(og, on, x2, w_out, g, tm):
    M, Dm = x2.shape
    Kh = og.shape[1]
    return pl.pallas_call(
        _out_proj_kernel,
        grid=(M // tm,),
        in_specs=[pl.BlockSpec((tm, Kh), lambda i: (i, 0)),
                  pl.BlockSpec((tm, Kh), lambda i: (i, 0)),
                  pl.BlockSpec((tm, Dm), lambda i: (i, 0)),
                  pl.BlockSpec((Kh, Dm), lambda i: (0, 0)),
                  pl.BlockSpec((Kh, Dm), lambda i: (1, 0)),
                  pl.BlockSpec((1, Dm), lambda i: (0, 0))],
        out_specs=[pl.BlockSpec((tm, Dm), lambda i: (i, 0)),
                   pl.BlockSpec((tm, Dm), lambda i: (i, 0))],
        out_shape=[jax.ShapeDtypeStruct((M, Dm), F32), jax.ShapeDtypeStruct((M, Dm), BF16)],
        compiler_params=_cp(("parallel",)),
        name="out_proj")(og, on, x2, w_out, w_out, g)


def _ffn_up_seq_kernel(hn_ref, wa_ref, wu_ref, cw_ref, cb_ref, pre_ref, hm_ref, tail_ref, aext_ref,
                       *, tm, tiles_per_seq):
    m = pl.program_id(1)

    @pl.when(m % tiles_per_seq == 0)
    def _():
        aext_ref[0:SUBLANE, :] = pre_ref[0]

    hn = hn_ref[...]
    a = _dot(hn, wa_ref[...])
    u = _dot(hn, wu_ref[...])
    aext_ref[SUBLANE:SUBLANE + tm, :] = a
    p1 = aext_ref[pl.ds(SUBLANE - 1, tm), :]
    p2 = aext_ref[pl.ds(SUBLANE - 2, tm), :]
    c = p2 * cw_ref[0:1, :] + p1 * cw_ref[1:2, :] + a * cw_ref[2:3, :] + cb_ref[...]
    hm_ref[...] = (jax.nn.gelu(c) * u).astype(hm_ref.dtype)
    tail = a[tm - SUBLANE:tm, :]
    tail_ref[0] = tail
    aext_ref[0:SUBLANE, :] = tail


def ffn_up_seq(hn, w_a, w_u, conv_w, conv_b, prefix, seq_len, tm, tf):
    M, Dm = hn.shape
    Fp = conv_w.shape[1]
    nf = Fp // tf
    B = M // seq_len
    tps = seq_len // tm
    kern = functools.partial(_ffn_up_seq_kernel, tm=tm, tiles_per_seq=tps)
    return pl.pallas_call(
        kern,
        grid=(nf, M // tm),
        in_specs=[pl.BlockSpec((tm, Dm), lambda f, m: (m, 0)),
                  pl.BlockSpec((Dm, tf), lambda f, m: (0, f)),
                  pl.BlockSpec((Dm, tf), lambda f, m: (0, f)),
                  pl.BlockSpec((SUBLANE, tf), lambda f, m: (0, f)),
                  pl.BlockSpec((1, tf), lambda f, m: (0, f)),
                  pl.BlockSpec((1, SUBLANE, tf), lambda f, m: (m // tps, 0, f))],
        out_specs=[pl.BlockSpec((tm, tf), lambda f, m: (m, f)),
                   pl.BlockSpec((1, SUBLANE, tf), lambda f, m: (m // tps, 0, f))],
        out_shape=[jax.ShapeDtypeStruct((M, Fp), BF16), jax.ShapeDtypeStruct((B, SUBLANE, Fp), F32)],
        scratch_shapes=[pltpu.VMEM((SUBLANE + tm, tf), F32)],
        compiler_params=_cp(("parallel", "arbitrary")),
        name="ffn_up_seq")(hn, w_a, w_u, conv_w, conv_b, prefix)


def _ffn_up_tm_kernel(hn_ref, wa_ref, wu_ref, cw_ref, cb_ref, pre_ref, hm_ref, tail_ref, *, steps, nb):
    hn = hn_ref[...]
    a = _dot(hn, wa_ref[...])
    u = _dot(hn, wu_ref[...])
    slabs = [pre_ref[i] for i in range(CONV_W - 1)] + [a[t * nb:(t + 1) * nb] for t in range(steps)]
    for t in range(steps):
        c = (slabs[t] * cw_ref[0:1, :] + slabs[t + 1] * cw_ref[1:2, :] + slabs[t + 2] * cw_ref[2:3, :]
             + cb_ref[...])
        hm_ref[t * nb:(t + 1) * nb, :] = (jax.nn.gelu(c) * u[t * nb:(t + 1) * nb]).astype(hm_ref.dtype)
    for i in range(CONV_W - 1):
        tail_ref[i] = slabs[steps + i]


def ffn_up_tm(hn, w_a, w_u, conv_w, conv_b, prefix, steps, tf):
    M, Dm = hn.shape
    Fp = conv_w.shape[1]
    nf = Fp // tf
    nb = M // steps
    kern = functools.partial(_ffn_up_tm_kernel, steps=steps, nb=nb)
    return pl.pallas_call(
        kern,
        grid=(nf,),
        in_specs=[pl.BlockSpec((M, Dm), lambda f: (0, 0)),
                  pl.BlockSpec((Dm, tf), lambda f: (0, f)),
                  pl.BlockSpec((Dm, tf), lambda f: (0, f)),
                  pl.BlockSpec((SUBLANE, tf), lambda f: (0, f)),
                  pl.BlockSpec((1, tf), lambda f: (0, f)),
                  pl.BlockSpec((CONV_W - 1, nb, tf), lambda f: (0, 0, f))],
        out_specs=[pl.BlockSpec((M, tf), lambda f: (0, f)),
                   pl.BlockSpec((CONV_W - 1, nb, tf), lambda f: (0, 0, f))],
        out_shape=[jax.ShapeDtypeStruct((M, Fp), BF16), jax.ShapeDtypeStruct((CONV_W - 1, nb, Fp), F32)],
        compiler_params=_cp(("parallel",)),
        name="ffn_up_tm")(hn, w_a, w_u, conv_w, conv_b, prefix)


def _ffn_down_kernel(hm_ref, w_ref, h_ref, o_ref):
    o_ref[...] = h_ref[...] + _dot(hm_ref[...], w_ref[...])


def ffn_down(hmid, w_down, h1, tm, tn):
    M = hmid.shape[0]
    F, Dm = w_down.shape
    assert F % LANE == 0 and F <= hmid.shape[1]
    return pl.pallas_call(
        _ffn_down_kernel,
        grid=(M // tm, Dm // tn),
        in_specs=[pl.BlockSpec((tm, F), lambda i, j: (i, 0)),
                  pl.BlockSpec((F, tn), lambda i, j: (0, j)),
                  pl.BlockSpec((tm, tn), lambda i, j: (i, j))],
        out_specs=pl.BlockSpec((tm, tn), lambda i, j: (i, j)),
        out_shape=jax.ShapeDtypeStruct((M, Dm), F32),
        compiler_params=_cp(("parallel", "arbitrary")),
        name="ffn_down")(hmid, w_down, h1)


def _ple_final_kernel(h_ref, p_ref, wg_ref, bg_ref, wp_ref, g_ref, o_ref):
    h = h_ref[...]
    gate = jax.nn.sigmoid(_dot(h.astype(BF16), wg_ref[...]) + bg_ref[...])
    h = h + gate * _dot(p_ref[...].astype(BF16), wp_ref[...])
    o_ref[...] = _rms(h, g_ref[...])


def ple_final(h2, p2, w_gate, b_gate, w_proj, g, tm):
    M, Dm = h2.shape
    Pd = p2.shape[1]
    return pl.pallas_call(
        _ple_final_kernel,
        grid=(M // tm,),
        in_specs=[pl.BlockSpec((tm, Dm), lambda i: (i, 0)),
                  pl.BlockSpec((tm, Pd), lambda i: (i, 0)),
                  pl.BlockSpec((Dm, Dm), lambda i: (0, 0)),
                  pl.BlockSpec((1, Dm), lambda i: (0, 0)),
                  pl.BlockSpec((Pd, Dm), lambda i: (0, 0)),
                  pl.BlockSpec((1, Dm), lambda i: (0, 0))],
        out_specs=pl.BlockSpec((tm, Dm), lambda i: (i, 0)),
        out_shape=jax.ShapeDtypeStruct((M, Dm), F32),
        compiler_params=_cp(("parallel",)),
        name="ple_final")(h2, p2, w_gate, b_gate, w_proj, g)


def _pick_tile(n, prefs):
    for t in prefs:
        if n % t == 0:
            return t
    raise ValueError(f"no tile for {n}")


def _block_matrices(total_keys, nb):
    e = np.zeros((LANE, total_keys), np.float32)
    t = np.arange(total_keys)
    e[t // SLC_LEN, t] = 1.0
    ratio = SLC_LEN // CMP_STRIDE
    m = np.zeros((LANE, LANE), np.float32)
    for n in range(nb):
        for s in range(LANE):
            m[n, s] = float(ratio * s <= n <= ratio * s + ratio - 1) + float(ratio * s - 1 <= n <= ratio * s + ratio - 2)
    return jnp.asarray(e, BF16), jnp.asarray(m, BF16)


def _prep_weights(i, attn_norm_g, w_in, gla_w_a2, gla_b_a, gla_norm_g, cmp_wk1, cmp_wk2, cmp_wv1, cmp_wv2,
                  w_out, ffn_norm_g, ffn_w_up, ffn_conv_w, ffn_conv_b, ffn_w_down, ple_w_proj, ple_w_gate,
                  ple_b_gate):
    Dm = w_in.shape[1]
    F = ffn_w_down.shape[1]
    Fp = -(-F // 512) * 512
    win = w_in[i]
    ga0 = NQ0
    ng0 = ga0 + GLA_RANK + (NMAIN - NQ0)
    ngw = 3 * NSA_HEADS
    before_ga = lax.broadcasted_iota(jnp.int32, (1, NMAIN), 1) < ga0
    w_main = jnp.where(before_ga, win[:, :NMAIN], win[:, GLA_RANK:GLA_RANK + NMAIN]).astype(BF16)
    w_small = jnp.concatenate([win[:, ga0:ga0 + GLA_RANK], win[:, ng0:ng0 + ngw],
                               jnp.zeros((Dm, LANE - GLA_RANK - ngw), F32)], axis=1).astype(BF16)
    wa_pad = jnp.concatenate([gla_w_a2[i], jnp.zeros((LANE - GLA_RANK, GLA_HEADS * GLA_DK), F32)],
                             axis=0).astype(BF16)
    cat = lambda w1: jnp.concatenate([w1[:CMP_STRIDE], w1[CMP_STRIDE:]], axis=-1)
    w1cat = jnp.stack([cat(cmp_wk1[i]), cat(cmp_wv1[i])]).astype(BF16)
    w2s = jnp.stack([cmp_wk2[i], cmp_wv2[i]])
    wup = ffn_w_up[i]
    w_up_a = jnp.pad(wup[:, :F], ((0, 0), (0, Fp - F))).astype(BF16)
    w_up_u = jnp.pad(wup[:, F:], ((0, 0), (0, Fp - F))).astype(BF16)
    conv_w = jnp.pad(ffn_conv_w[i], ((0, SUBLANE - CONV_W), (0, Fp - F)))
    conv_b = jnp.pad(ffn_conv_b[i], (0, Fp - F)).reshape(1, Fp)
    w_down = ffn_w_down[i].astype(BF16)
    return dict(
        g_attn=attn_norm_g[i].reshape(1, Dm), w_main=w_main, w_small=w_small, wa_pad=wa_pad,
        b_a=gla_b_a[i].reshape(1, -1), gnorm=gla_norm_g[i].reshape(1, -1), w1cat=w1cat, w2s=w2s,
        w_out=w_out[i].astype(BF16), g_ffn=ffn_norm_g[i].reshape(1, Dm), w_up_a=w_up_a, w_up_u=w_up_u,
        conv_w=conv_w,
        conv_b=conv_b, w_down=w_down, w_proj=ple_w_proj[i].astype(BF16), w_gate=ple_w_gate[i].astype(BF16),
        b_gate=ple_b_gate[i].reshape(1, Dm), F=F, Fp=Fp)


def _kv_rows(kv, B, rows):
    t = kv.reshape(kv.shape[0], B, -1, NSA_KV, NSA_HD)[:, :, rows]
    return [t[i] for i in range(kv.shape[0])]


def _prompt_layer(h2d, p2d, B, L, W):
    Dm = h2d.shape[1]
    M = B * L
    proj, small, kv = norm_proj(h2d, W["g_attn"], W["w_main"], W["w_small"],
                                _pick_tile(M, (1024, 512, 256, 128)), 512)
    proj3 = proj.reshape(B, L, NMAIN)
    small3 = small.reshape(B, L, LANE)
    cb = _pick_tile(L, (128, 64, 32, 16))
    s0 = jnp.zeros((B, GLA_HEADS, GLA_DK, GLA_DV), F32)
    og, gla_state = gla(proj3, small3, W["wa_pad"], W["b_a"], W["gnorm"], s0, cb, GLA_SUB, cb,
                        _pick_tile(B, (4, 2, 1)), BF16)
    cmp = compress(proj3, W["w1cat"], W["w2s"])
    _, mmat = _block_matrices(L, L // CMP_STRIDE - 1)
    on = nsa_prompt(proj3, small3, cmp, mmat.T)
    h1, hn = out_proj(og.reshape(M, -1), on.reshape(M, -1), h2d, W["w_out"], W["g_ffn"],
                      _pick_tile(M, (512, 256, 128)))
    prefix = jnp.zeros((B, SUBLANE, W["Fp"]), F32)
    hmid, tail = ffn_up_seq(hn, W["w_up_a"], W["w_up_u"], W["conv_w"], W["conv_b"], prefix, L,
                            _pick_tile(L, (1024, 512, 256, 128)), 512)
    h2 = ffn_down(hmid, W["w_down"], h1, _pick_tile(M, (1024, 512, 256, 128)), 512)
    keep = min(WINDOW, L)
    outs = tuple(_kv_rows(kv[:4], B, slice(0, L)) + _kv_rows(kv[4:], B, slice(L - keep, L))
                 + [gla_state, tail[:, SUBLANE - (CONV_W - 1):, :W["F"]]])
    return h2, outs


def _decode_layer(h3d, p2d_tm, B, L, caches, page_table, win_k, win_v, gla_s, conv_s, W):
    Dm = h3d.shape[2]
    Mp = B * L
    lp = -(-L // SUBLANE) * SUBLANE
    x2 = h3d.reshape(Mp, Dm)
    proj, small, kv = norm_proj(x2, W["g_attn"], W["w_main"], W["w_small"],
                                _pick_tile(Mp, (1024, 512, 256, 128, 64, 32, 16, 8)), 512)
    proj3 = proj.reshape(B, L, NMAIN)
    small3 = small.reshape(B, L, LANE)
    og, gla_state = gla(proj3, small3, W["wa_pad"], W["b_a"], W["gnorm"], gla_s, lp, lp, L,
                        _pick_tile(B, (4, 2, 1)), F32 if L < lp else BF16)
    plen = page_table.shape[1] * caches[0].shape[1] // NSA_KV
    emat, mmat = _block_matrices(plen + KCHUNK, (plen + L) // CMP_STRIDE - 1)
    wlen = win_k.shape[1]
    on, wk_new, wv_new = nsa_decode(proj3, small3, caches, page_table,
                                    win_k.reshape(B, wlen * NSA_KV, NSA_HD),
                                    win_v.reshape(B, wlen * NSA_KV, NSA_HD),
                                    W["w1cat"], W["w2s"], emat, mmat, L, _pick_tile(B, (2, 1)))
    h1, hn = out_proj(og.reshape(Mp, -1), on.reshape(Mp, -1), x2, W["w_out"], W["g_ffn"],
                      _pick_tile(Mp, (256, 128, 64, 32, 16, 8)))
    to_tm = lambda t: t.reshape(B, L, Dm).transpose(1, 0, 2).reshape(L * B, Dm)
    h1_tm, hn_tm = to_tm(h1), to_tm(hn)
    prefix = jnp.pad(conv_s.transpose(1, 0, 2), ((0, 0), (0, 0), (0, W["Fp"] - W["F"])))
    hmid, tail = ffn_up_tm(hn_tm, W["w_up_a"], W["w_up_u"], W["conv_w"], W["conv_b"], prefix, L, 512)
    M = L * B
    h2 = ffn_down(hmid, W["w_down"], h1_tm, _pick_tile(M, (512, 256, 128, 64, 32, 16, 8)), 512)
    outs = tuple(_kv_rows(kv[:4], B, slice(0, L))
                 + [wk_new.reshape(win_k.shape), wv_new.reshape(win_v.shape),
                    gla_state, tail[:, :, :W["F"]].transpose(1, 0, 2)])
    return h2, outs


def kernel(x_prompt, x_sample, p_prompt, p_sample, cache_cmp_k, cache_cmp_v, cache_slc_k, cache_slc_v,
           page_table, state_win_k, state_win_v, state_gla, state_ffn_conv, attn_norm_g, w_in, gla_w_a2,
           gla_b_a, gla_norm_g, cmp_wk1, cmp_wk2, cmp_wv1, cmp_wv2, w_out, ffn_norm_g, ffn_w_up,
           ffn_conv_w, ffn_conv_b, ffn_w_down, ple_w_proj, ple_w_gate, ple_b_gate, final_norm_g):
    depth = w_in.shape[0]
    assert depth == 1, "layers are chained through HBM one at a time; only depth 1 is wired"
    Bp, Lp, Dm = x_prompt.shape
    Bs, Ls, _ = x_sample.shape
    W = _prep_weights(0, attn_norm_g, w_in, gla_w_a2, gla_b_a, gla_norm_g, cmp_wk1, cmp_wk2, cmp_wv1,
                      cmp_wv2, w_out, ffn_norm_g, ffn_w_up, ffn_conv_w, ffn_conv_b, ffn_w_down,
                      ple_w_proj, ple_w_gate, ple_b_gate)
    g_final = final_norm_g.reshape(1, Dm)

    Mp = Bp * Lp
    h2, outs_p = _prompt_layer(x_prompt.reshape(Mp, Dm), None, Bp, Lp, W)
    y_prompt = ple_final(h2, p_prompt[0].reshape(Mp, -1), W["w_gate"], W["b_gate"], W["w_proj"], g_final,
                         _pick_tile(Mp, (512, 256, 128))).reshape(Bp, Lp, Dm)

    n_pool, psz = cache_cmp_k.shape[1], cache_cmp_k.shape[2]
    caches = [c[0].reshape(n_pool, psz * NSA_KV, NSA_HD)
              for c in (cache_cmp_k, cache_cmp_v, cache_slc_k, cache_slc_v)]
    h2s, outs_s = _decode_layer(x_sample, None, Bs, Ls, caches, page_table, state_win_k[0],
                                state_win_v[0], state_gla[0], state_ffn_conv[0], W)
    p_tm = p_sample[0].transpose(1, 0, 2).reshape(Ls * Bs, -1)
    Ms = Ls * Bs
    y_tm = ple_final(h2s, p_tm, W["w_gate"], W["b_gate"], W["w_proj"], g_final,
                     _pick_tile(Ms, (256, 128, 64, 32, 16, 8)))
    y_sample = y_tm.reshape(Ls, Bs, Dm).transpose(1, 0, 2)

    lead = lambda t: t[None]
    return (y_prompt, y_sample) + tuple(lead(t) for t in outs_p) + tuple(lead(t) for t in outs_s)
```

```python
import functools
import math

import numpy as np
import jax
import jax.numpy as jnp
from jax import lax
from jax.experimental import pallas as pl
from jax.experimental.pallas import tpu as pltpu

F32 = jnp.float32
BF16 = jnp.bfloat16

GLA_HEADS = 4
GLA_DK = 128
GLA_DV = 256
GLA_RANK = 16
GLA_TAU = 16.0
GLA_SUB = 16
NSA_HEADS = 8
NSA_KV = 2
NSA_REP = NSA_HEADS // NSA_KV
NSA_HD = 128
CMP_LEN = 32
CMP_STRIDE = 16
SLC_LEN = 64
N_SELECT = 16
WINDOW = 512
CONV_W = 3
EPS = 1e-6
NEG = -1e30
M_FLOOR = 0.1 * NEG
FORCED = 1e6

LANE = 128
SUBLANE = 8
KCHUNK = 128
CMP_SEG_ROWS = NSA_KV * CMP_STRIDE
CMP_PITCH = CMP_SEG_ROWS + SUBLANE
VMEM_LIMIT = 56 * 1024 * 1024

GQ0 = 0
GK0 = GQ0 + GLA_HEADS * GLA_DK
GV0 = GK0 + GLA_HEADS * GLA_DK
GR0 = GV0 + GLA_HEADS * GLA_DV
NQ0 = GR0 + GLA_HEADS * GLA_DV
KVW = NSA_KV * NSA_HD
KC0 = NQ0 + NSA_HEADS * NSA_HD
VC0 = KC0 + KVW
KS0 = VC0 + KVW
VS0 = KS0 + KVW
KW0 = VS0 + KVW
VW0 = KW0 + KVW
NMAIN = VW0 + KVW
GATE0 = GLA_RANK


def _cp(sem):
    return pltpu.CompilerParams(dimension_semantics=sem, vmem_limit_bytes=VMEM_LIMIT)


def _dot(a, b):
    return jnp.dot(a, b, preferred_element_type=F32)


def _dot_nt(a, b):
    return lax.dot_general(a, b, (((1,), (1,)), ((), ())), preferred_element_type=F32)


def _rms(x, g):
    return x * lax.rsqrt(jnp.mean(x * x, axis=-1, keepdims=True) + EPS) * g


def _norm_proj_kernel(x_ref, g_ref, wm_ref, ws_ref, om_ref, os_ref, kv_ref, xn_ref, *, j_kv0):
    j = pl.program_id(1)

    @pl.when(j == 0)
    def _():
        xn = _rms(x_ref[...], g_ref[...]).astype(BF16)
        xn_ref[...] = xn
        os_ref[...] = _dot(xn, ws_ref[...])

    tile = _dot(xn_ref[...], wm_ref[...])
    om_ref[...] = tile

    @pl.when(j >= j_kv0)
    def _():
        tm = tile.shape[0]
        for a in range(tile.shape[1] // KVW):
            for g in range(NSA_KV):
                c0 = a * KVW + g * NSA_HD
                kv_ref[a, pl.ds(g, tm, stride=NSA_KV), :] = tile[:, c0:c0 + NSA_HD]


def norm_proj(x2, g, w_main, w_small, tm, tn):
    M, Dm = x2.shape
    N = w_main.shape[1]
    assert KC0 % tn == 0 and tn % KVW == 0
    j_kv0 = KC0 // tn
    per_tile = tn // KVW
    n_kv = (N - KC0) // KVW
    return pl.pallas_call(
        functools.partial(_norm_proj_kernel, j_kv0=j_kv0),
        grid=(M // tm, N // tn),
        in_specs=[pl.BlockSpec((tm, Dm), lambda i, j: (i, 0)),
                  pl.BlockSpec((1, Dm), lambda i, j: (0, 0)),
                  pl.BlockSpec((Dm, tn), lambda i, j: (0, j)),
                  pl.BlockSpec((Dm, LANE), lambda i, j: (0, 0))],
        out_specs=[pl.BlockSpec((tm, tn), lambda i, j: (i, j)),
                   pl.BlockSpec((tm, LANE), lambda i, j: (i, 0)),
                   pl.BlockSpec((per_tile, NSA_KV * tm, NSA_HD),
                                lambda i, j: (jnp.maximum(j - j_kv0, 0), i, 0))],
        out_shape=[jax.ShapeDtypeStruct((M, N), F32), jax.ShapeDtypeStruct((M, LANE), F32),
                   jax.ShapeDtypeStruct((n_kv, NSA_KV * M, NSA_HD), F32)],
        scratch_shapes=[pltpu.VMEM((tm, Dm), BF16)],
        compiler_params=_cp(("parallel", "arbitrary")),
        name="norm_proj")(x2, g, w_main, w_small)


def _gla_kernel(q_ref, k_ref, v_ref, r_ref, sm_ref, wa_ref, ba_ref, gn_ref, s0_ref,
                o_ref, s_ref, st_ref, *, cb, sub, valid, nbb):
    c = pl.program_id(1)

    @pl.when(c == 0)
    def _():
        st_ref[...] = s0_ref[...]

    GH = GLA_HEADS
    H = nbb * GH
    side = lambda ref: ref[0] if nbb == 1 else jnp.concatenate([ref[s] for s in range(nbb)], axis=1)
    hk = lambda h: slice(h * GLA_DK, (h + 1) * GLA_DK)
    hv = lambda h: slice(h * GLA_DV, (h + 1) * GLA_DV)
    ri = lax.broadcasted_iota(jnp.int32, (cb, cb), 0)
    ci = lax.broadcasted_iota(jnp.int32, (cb, cb), 1)
    tri = jnp.where(ci <= ri, 1.0, 0.0)
    rowv = lax.broadcasted_iota(jnp.int32, (cb, 1), 0)
    q = side(q_ref) * (GLA_DK ** -0.5)
    k = side(k_ref)
    v = side(v_ref)
    pre = [_dot(sm_ref[s].astype(BF16), wa_ref[...]) + ba_ref[...] for s in range(nbb)]
    pre = pre[0] if nbb == 1 else jnp.concatenate(pre, axis=1)
    log_a = jax.nn.log_sigmoid(pre) / GLA_TAU
    if valid < cb:
        log_a = jnp.where(rowv < valid, log_a, 0.0)
        k = jnp.where(rowv < valid, k, 0.0)
        v = jnp.where(rowv < valid, v, 0.0)
    hi = log_a.astype(BF16).astype(F32)
    lo = (log_a - hi).astype(BF16).astype(F32)
    b = _dot(tri, hi) + _dot(tri, lo)
    b_last = b[cb - 1:cb, :]
    S = [st_ref[h // GH, h % GH] for h in range(H)]
    vb = v.astype(BF16)

    qe = (q * jnp.exp(b)).astype(BF16)
    o = [_dot(qe[:, hk(h)], S[h].astype(BF16)) for h in range(H)]
    a_rows = [[] for _ in range(H)]
    cc = lax.broadcasted_iota(jnp.int32, (sub, cb), 1)
    for blk in range(cb // sub):
        lo_r, hi_r = blk * sub, (blk + 1) * sub
        r_dec = jnp.zeros((1, H * GLA_DK), F32) if blk == 0 else b[lo_r - 1:lo_r, :]
        q_i = q[lo_r:hi_r] * jnp.exp(b[lo_r:hi_r] - r_dec)
        k_i = k * jnp.exp(jnp.where(rowv < hi_r, r_dec - b, 0.0))
        rr = lax.broadcasted_iota(jnp.int32, (sub, cb), 0) + lo_r
        for h in range(H):
            a = _dot_nt(q_i[:, hk(h)], k_i[:, hk(h)])
            a_rows[h].append(jnp.where(cc <= rr, a, 0.0))
    for h in range(H):
        a = a_rows[h][0] if len(a_rows[h]) == 1 else jnp.concatenate(a_rows[h], axis=0)
        o[h] = o[h] + _dot(a, v[:, hv(h)])

    kh = k * jnp.exp(b_last - b)
    dec = jnp.exp(b_last)
    if cb < LANE:
        kh = jnp.concatenate([kh, jnp.zeros((LANE - cb, H * GLA_DK), F32)], axis=0)
        vpad = jnp.concatenate([vb, jnp.zeros((LANE - cb, H * GLA_DV), BF16)], axis=0)
    else:
        vpad = vb
    for h in range(H):
        dec_col = jnp.transpose(jnp.broadcast_to(dec[:, hk(h)], (GLA_DK, GLA_DK)))[:, 0:1]
        s_new = dec_col * S[h] + _dot(jnp.transpose(kh[:, hk(h)]).astype(BF16), vpad[:, hv(h)])
        st_ref[h // GH, h % GH] = s_new
        s_ref[h // GH, h % GH] = s_new

    r = side(r_ref)
    gate = r * jax.nn.sigmoid(r)
    for h in range(H):
        o_ref[h // GH, :, hv(h % GH)] = (_rms(o[h], gn_ref[...]) * gate[:, hv(h)]).astype(o_ref.dtype)


def gla(proj3, small3, wa_pad, b_a, gnorm, s0, cb, sub, valid, nbb):
    B, L, _ = proj3.shape
    H = GLA_HEADS
    qw, vw = H * GLA_DK, H * GLA_DV
    kern = functools.partial(_gla_kernel, cb=cb, sub=sub, valid=valid, nbb=nbb)
    return pl.pallas_call(
        kern,
        grid=(B // nbb, L // cb),
        in_specs=[pl.BlockSpec((nbb, cb, qw), lambda b, c: (b, c, GQ0 // qw)),
                  pl.BlockSpec((nbb, cb, qw), lambda b, c: (b, c, GK0 // qw)),
                  pl.BlockSpec((nbb, cb, vw), lambda b, c: (b, c, GV0 // vw)),
                  pl.BlockSpec((nbb, cb, vw), lambda b, c: (b, c, GR0 // vw)),
                  pl.BlockSpec((nbb, cb, LANE), lambda b, c: (b, c, 0)),
                  pl.BlockSpec((LANE, qw), lambda b, c: (0, 0)),
                  pl.BlockSpec((1, qw), lambda b, c: (0, 0)),
                  pl.BlockSpec((1, GLA_DV), lambda b, c: (0, 0)),
                  pl.BlockSpec((nbb, H, GLA_DK, GLA_DV), lambda b, c: (b, 0, 0, 0))],
        out_specs=[pl.BlockSpec((nbb, cb, vw), lambda b, c: (b, c, 0)),
                   pl.BlockSpec((nbb, H, GLA_DK, GLA_DV), lambda b, c: (b, 0, 0, 0))],
        out_shape=[jax.ShapeDtypeStruct((B, L, vw), BF16),
                   jax.ShapeDtypeStruct((B, H, GLA_DK, GLA_DV), F32)],
        scratch_shapes=[pltpu.VMEM((nbb, H, GLA_DK, GLA_DV), F32)],
        compiler_params=_cp(("parallel", "arbitrary")),
        name="gla")(proj3, proj3, proj3, proj3, small3, wa_pad, b_a, gnorm, s0)


def _pair_rows(load_rows, j):
    return jnp.concatenate([load_rows(j), load_rows(j + 1)], axis=1).astype(BF16)


def _pair_weights(w1_ref, j):
    w = w1_ref[pl.ds(j, 2)]
    return w.reshape(2 * NSA_HD, 2 * NSA_HD)


def _compress_group(load_rows, w1_ref, w2, nseg):
    acc = jnp.zeros((nseg, 2 * NSA_HD), F32)
    for j in range(0, CMP_STRIDE, 2):
        acc = acc + _dot(_pair_rows(load_rows, j), _pair_weights(w1_ref, j))
    first = acc[:, :NSA_HD]
    second = acc[:, NSA_HD:]
    h = jax.nn.gelu(first + pltpu.roll(second, nseg - 1, axis=0))
    return _dot(h.astype(BF16), w2)


def _compress_groups(loads, w1_refs, w2s, nrow):
    n = len(loads)
    acc = [jnp.zeros((nrow, 2 * NSA_HD), F32) for _ in range(n)]
    for j in range(0, CMP_STRIDE, 2):
        for i in range(n):
            acc[i] = acc[i] + _dot(_pair_rows(loads[i], j), _pair_weights(w1_refs[i], j))
    hs = [jax.nn.gelu(a[:, :NSA_HD] + pltpu.roll(a[:, NSA_HD:], nrow - 1, axis=0)) for a in acc]
    return [_dot(h.astype(BF16), w2) for h, w2 in zip(hs, w2s)]


def _cmp_branch_many(q4s, cks, cvs, qpos4, nb):
    s = [_dot_nt(q4, ck.astype(BF16)) for q4, ck in zip(q4s, cks)]
    n = lax.broadcasted_iota(jnp.int32, s[0].shape, 1)
    mask = jnp.where(n * CMP_STRIDE + (CMP_LEN - 1) <= qpos4, n, nb) < nb
    s = [jnp.where(mask, si, NEG) for si in s]
    m = [jnp.max(si, axis=-1, keepdims=True) for si in s]
    e = [jnp.where(mask, jnp.exp(si - mi), 0.0) for si, mi in zip(s, m)]
    p = [ei / jnp.maximum(jnp.sum(ei, axis=-1, keepdims=True), 1e-30) for ei in e]
    o = [_dot(pi.astype(BF16), cv.astype(BF16)) for pi, cv in zip(p, cvs)]
    return list(zip(p, o))


def _select(pg, mmat, qpos_tok, nslc, nsel):
    hi = pg.astype(BF16)
    lo = (pg - hi.astype(F32)).astype(BF16)
    imp = _dot(hi, mmat) + _dot(lo, mmat)
    blk = lax.broadcasted_iota(jnp.int32, pg.shape, 1)
    cur = qpos_tok // SLC_LEN
    forced = jnp.where(blk == 0, 1, 0) + jnp.where(blk == cur, 1, 0) + jnp.where(blk == cur - 1, 1, 0)
    score = jnp.where(blk <= cur, jnp.where(forced > 0, FORCED, imp), -1.0)
    score = jnp.where(blk < nslc, score, -2.0)
    rank = jnp.zeros(pg.shape, F32)
    for j in range(nslc):
        sj = score[:, j:j + 1]
        tie = jnp.where(blk > j, sj, NEG)
        rank = rank + jnp.where(sj > score, 1.0, 0.0) + jnp.where(tie == score, 1.0, 0.0)
    return jnp.where(rank < nsel, jnp.where(blk < nslc, 1.0, 0.0), 0.0)


def _flash_init(m_ref, l_ref, acc_ref):
    m_ref[...] = jnp.full(m_ref.shape, NEG, F32)
    l_ref[...] = jnp.zeros(l_ref.shape, F32)
    acc_ref[...] = jnp.zeros(acc_ref.shape, F32)


def _flash_step(q4, k, v, mask, m_ref, l_ref, acc_ref):
    s = jnp.where(mask, _dot_nt(q4, k), NEG)
    m_old = m_ref[...]
    m_new = jnp.maximum(m_old, jnp.max(s, axis=-1, keepdims=True))
    alpha = jnp.exp(m_old - m_new)
    p = jnp.where(mask, jnp.exp(s - m_new), 0.0)
    l_ref[...] = alpha * l_ref[...] + jnp.sum(p, axis=-1, keepdims=True)
    acc_ref[...] = alpha * acc_ref[...] + _dot(p.astype(BF16), v)
    m_ref[...] = m_new


def _flash_out(l_ref, acc_ref):
    return acc_ref[...] / jnp.maximum(l_ref[...], 1e-30)


def _tile_rows(x, reps):
    return jnp.concatenate([x] * reps, axis=0)


def _attend_many(jobs):
    s = [[jnp.where(mk, _dot_nt(q4, k), NEG) for k, mk in zip(keys, masks)] for q4, keys, _, masks in jobs]
    m = []
    for sj in s:
        mj = sj[0].max(axis=-1, keepdims=True)
        for si in sj[1:]:
            mj = jnp.maximum(mj, si.max(axis=-1, keepdims=True))
        m.append(mj)
    e = [[jnp.where(mk, jnp.exp(si - mj), 0.0) for si, mk in zip(sj, job[3])]
         for sj, mj, job in zip(s, m, jobs)]
    l = [sum(ei.sum(axis=-1, keepdims=True) for ei in ej) for ej in e]
    o = [sum(_dot(ei.astype(BF16), v) for ei, v in zip(ej, job[2])) for ej, job in zip(e, jobs)]
    return [oj / jnp.maximum(lj, 1e-30) for oj, lj in zip(o, l)]


def _compress_kernel(x_ref, w1_ref, w2_ref, o_ref, *, nseg):
    load = lambda j: x_ref[0, pl.ds(j, nseg, stride=CMP_STRIDE), :]
    o_ref[0, 0, 0] = _compress_group(load, w1_ref.at[0], w2_ref[0].astype(BF16), nseg)


def compress(proj3, w1cat, w2s):
    B, L, _ = proj3.shape
    nseg = L // CMP_STRIDE
    G = NSA_KV
    return pl.pallas_call(
        functools.partial(_compress_kernel, nseg=nseg),
        grid=(B, 2, G),
        in_specs=[pl.BlockSpec((1, L, NSA_HD), lambda b, w, g: (b, 0, KC0 // NSA_HD + w * G + g)),
                  pl.BlockSpec((1, CMP_STRIDE, NSA_HD, 2 * NSA_HD), lambda b, w, g: (w, 0, 0, 0)),
                  pl.BlockSpec((1, NSA_HD, NSA_HD), lambda b, w, g: (w, 0, 0))],
        out_specs=pl.BlockSpec((1, 1, 1, nseg, NSA_HD), lambda b, w, g: (w, b, g, 0, 0)),
        out_shape=jax.ShapeDtypeStruct((2, B, G, nseg, NSA_HD), F32),
        compiler_params=_cp(("parallel", "parallel", "parallel")),
        name="compress")(proj3, w1cat, w2s)


def _select_t(pg_t, mm_t, qpos_row, nslc, nsel):
    nsp = -(-nslc // SUBLANE) * SUBLANE
    hi = pg_t.astype(BF16)
    lo = (pg_t - hi.astype(F32)).astype(BF16)
    imp = (_dot(mm_t, hi) + _dot(mm_t, lo))[:nsp]
    blk = lax.broadcasted_iota(jnp.int32, imp.shape, 0)
    cur = qpos_row // SLC_LEN
    forced = jnp.where(blk == 0, 1, 0) + jnp.where(blk == cur, 1, 0) + jnp.where(blk == cur - 1, 1, 0)
    score = jnp.where(blk <= cur, jnp.where(forced > 0, FORCED, imp), -1.0)
    score = jnp.where(blk < nslc, score, -2.0)
    rank = jnp.zeros(imp.shape, F32)
    for j in range(nslc):
        sj = score[j:j + 1, :]
        tie = jnp.where(blk > j, sj, NEG)
        rank = rank + jnp.where(sj > score, 1.0, 0.0) + jnp.where(tie == score, 1.0, 0.0)
    return jnp.where(rank < nsel, jnp.where(blk < nslc, 1.0, 0.0), 0.0)


def _flash_steps_t(q_t, jobs):
    s = [_dot(k, q_t) + bias for k, _, bias, _ in jobs]
    m_old = [st[0][...] for _, _, _, st in jobs]
    m_new = [jnp.maximum(mo, jnp.max(si, axis=0, keepdims=True)) for mo, si in zip(m_old, s)]
    alpha = [jnp.exp(mo - mn) for mo, mn in zip(m_old, m_new)]
    p = [jnp.exp(si - mn) for si, mn in zip(s, m_new)]
    pv = [_dot(v_t, pi.astype(BF16)) for (_, v_t, _, _), pi in zip(jobs, p)]
    for (_, _, _, (m_ref, l_ref, acc_ref)), mn, al, pi, pvi in zip(jobs, m_new, alpha, p, pv):
        l_ref[...] = al * l_ref[...] + jnp.sum(pi, axis=0, keepdims=True)
        acc_ref[...] = al * acc_ref[...] + pvi
        m_ref[...] = mn


def _nsa_prompt_kernel(q_ref, sm_ref, ck_ref, cv_ref, ks_ref, vs_ref, kw_ref, vw_ref, mm_ref,
                       o_ref, vst_ref, vwt_ref, sel_ref, m_ref, l_ref, acc_ref, mw_ref, lw_ref, accw_ref,
                       *, tq, kstep, nslc, nsel, nb):
    g = pl.program_id(1)
    qb = pl.program_id(2)
    R = NSA_REP
    T = R * tq
    L = ks_ref.shape[1]
    scale = NSA_HD ** -0.5

    @pl.when(qb == 0)
    def _():
        for c in range(L // KCHUNK):
            cs = slice(c * KCHUNK, (c + 1) * KCHUNK)
            vst_ref[:, cs] = jnp.transpose(vs_ref[0, cs, :]).astype(BF16)
            vwt_ref[:, cs] = jnp.transpose(vw_ref[0, cs, :]).astype(BF16)

    q_t = (jnp.concatenate([jnp.transpose(q_ref[0, :, r * NSA_HD:(r + 1) * NSA_HD]) for r in range(R)],
                           axis=1) * scale).astype(BF16)
    qpos_row = qb * tq + lax.broadcasted_iota(jnp.int32, (1, tq), 1)
    qpos4 = jnp.concatenate([qpos_row] * R, axis=1)

    s = _dot(ck_ref[0, 0, 0].astype(BF16), q_t)
    n = lax.broadcasted_iota(jnp.int32, s.shape, 0)
    mask = jnp.where(n * CMP_STRIDE + (CMP_LEN - 1) <= qpos4, n, nb) < nb
    s = jnp.where(mask, s, NEG)
    e = jnp.where(mask, jnp.exp(s - jnp.max(s, axis=0, keepdims=True)), 0.0)
    p = e / jnp.maximum(jnp.sum(e, axis=0, keepdims=True), 1e-30)
    o_c = _dot(jnp.transpose(cv_ref[0, 0, 0]).astype(BF16), p.astype(BF16))
    pg = p[:, 0:tq]
    for r in range(1, R):
        pg = pg + p[:, r * tq:(r + 1) * tq]
    sel_ref[...] = (_select_t(pg, mm_ref[...], qpos_row, nslc, nsel) - 1.0) * (-NEG)

    key = lax.broadcasted_iota(jnp.int32, (kstep, T), 0)
    hi_step = (qb * tq + tq - 1) // kstep + 1
    lo_step = jnp.maximum(qb * tq - (WINDOW - 1), 0) // kstep

    sel_state = (m_ref, l_ref, acc_ref)
    win_state = (mw_ref, lw_ref, accw_ref)
    for m_r, l_r, acc_r in (sel_state, win_state):
        m_r[...] = jnp.full(m_r.shape, M_FLOOR, F32)
        l_r[...] = jnp.zeros(l_r.shape, F32)
        acc_r[...] = jnp.zeros(acc_r.shape, F32)

    def sel_step(c, causal):
        off = pl.multiple_of(c * kstep, kstep)
        k = ks_ref[0, pl.ds(off, kstep), :].astype(BF16)
        v_t = vst_ref[:, pl.ds(off, kstep)]
        blocks = [jnp.broadcast_to(sel_ref[pl.ds((kstep // SLC_LEN) * c + i, 1), :], (SLC_LEN, tq))
                  for i in range(kstep // SLC_LEN)]
        bias = jnp.concatenate(blocks, axis=0)
        bias = jnp.concatenate([bias] * R, axis=1)
        if causal:
            bias = jnp.where(off + key <= qpos4, bias, NEG)
        return k, v_t, bias, sel_state

    def win_step(c, causal):
        off = pl.multiple_of(c * kstep, kstep)
        k = kw_ref[0, pl.ds(off, kstep), :].astype(BF16)
        v_t = vwt_ref[:, pl.ds(off, kstep)]
        kpos = off + key
        if causal:
            bias = jnp.where(kpos <= qpos4, 0.0, NEG)
        else:
            bias = jnp.where(kpos > qpos4 - WINDOW, 0.0, NEG)
        return k, v_t, bias, win_state

    def body_sel(c, carry):
        _flash_steps_t(q_t, [sel_step(c, False)])
        return carry

    def body_both(c, carry):
        _flash_steps_t(q_t, [sel_step(c, False), win_step(c, False)])
        return carry

    last = hi_step - 1
    lax.fori_loop(0, jnp.minimum(lo_step, last), body_sel, 0)
    lax.fori_loop(lo_step, last, body_both, 0)
    _flash_steps_t(q_t, [sel_step(last, True), win_step(last, True)])
    o_s = acc_ref[...] / jnp.maximum(l_ref[...], 1e-30)
    o_w = accw_ref[...] / jnp.maximum(lw_ref[...], 1e-30)

    g_t = jnp.transpose(jax.nn.sigmoid(sm_ref[0]))
    for r in range(R):
        def gate(br, r=r):
            c0 = GATE0 + br * NSA_HEADS + r
            return jnp.where(g == 0, g_t[c0:c0 + 1, :], g_t[c0 + R:c0 + R + 1, :])
        cs = slice(r * tq, (r + 1) * tq)
        o = gate(0) * o_c[:, cs] + gate(1) * o_s[:, cs] + gate(2) * o_w[:, cs]
        o_ref[0, :, r * NSA_HD:(r + 1) * NSA_HD] = jnp.transpose(o).astype(o_ref.dtype)


def nsa_prompt(proj3, small3, cmp, mmat_t):
    B, L, _ = proj3.shape
    tq = _pick_tile(L, (2 * KCHUNK, KCHUNK))
    nseg = L // CMP_STRIDE
    nb = nseg - 1
    nslc = -(-L // SLC_LEN)
    nsel = min(N_SELECT, nslc)
    nsp = -(-nslc // SUBLANE) * SUBLANE
    G, R = NSA_KV, NSA_REP
    kstep = _pick_tile(L, (2 * KCHUNK, KCHUNK))
    assert nseg == LANE and L % KCHUNK == 0 and kstep % SLC_LEN == 0
    kv_spec = lambda c0: pl.BlockSpec((1, L, NSA_HD), lambda b, g, qb: (b, 0, c0 // NSA_HD + g))
    kern = functools.partial(_nsa_prompt_kernel, tq=tq, kstep=kstep, nslc=nslc, nsel=nsel, nb=nb)
    return pl.pallas_call(
        kern,
        grid=(B, G, L // tq),
        in_specs=[pl.BlockSpec((1, tq, R * NSA_HD), lambda b, g, qb: (b, qb, NQ0 // (R * NSA_HD) + g)),
                  pl.BlockSpec((1, tq, LANE), lambda b, g, qb: (b, qb, 0)),
                  pl.BlockSpec((1, 1, 1, nseg, NSA_HD), lambda b, g, qb: (0, b, g, 0, 0)),
                  pl.BlockSpec((1, 1, 1, nseg, NSA_HD), lambda b, g, qb: (1, b, g, 0, 0)),
                  kv_spec(KS0), kv_spec(VS0), kv_spec(KW0), kv_spec(VW0),
                  pl.BlockSpec((LANE, LANE), lambda b, g, qb: (0, 0))],
        out_specs=pl.BlockSpec((1, tq, R * NSA_HD), lambda b, g, qb: (b, qb, g)),
        out_shape=jax.ShapeDtypeStruct((B, L, NSA_HEADS * NSA_HD), BF16),
        scratch_shapes=[pltpu.VMEM((NSA_HD, L), BF16),
                        pltpu.VMEM((NSA_HD, L), BF16),
                        pltpu.VMEM((nsp, tq), F32)]
                       + 2 * [pltpu.VMEM((1, R * tq), F32),
                              pltpu.VMEM((1, R * tq), F32),
                              pltpu.VMEM((NSA_HD, R * tq), F32)],
        compiler_params=_cp(("parallel", "parallel", "arbitrary")),
        name="nsa_prompt")(proj3, small3, cmp, cmp, proj3, proj3, proj3, proj3, mmat_t)


def _nsa_decode_kernel(pt_ref, q_ref, sm_ref, ksn_ref, vsn_ref, kwn_ref, vwn_ref, wk_ref, wv_ref,
                       w1_ref, w2_ref, e_ref, mm_ref, c0_hbm, c1_hbm, c2_hbm, c3_hbm,
                       o_ref, wko_ref, wvo_ref, cbuf_ref, sbuf_ref, sem_ref,
                       *, ns, lp, lreal, plen, rows_pp, n_pages, wlen, nslc, nsel, nb):
    b = pl.program_id(0)
    slot = b % 2
    caches = (c0_hbm, c1_hbm, c2_hbm, c3_hbm)
    G = NSA_KV
    R = NSA_REP
    T = R * lp
    segs_pp = rows_pp // CMP_SEG_ROWS

    def copies(step, sl):
        out = []
        for q in range(ns):
            for p in range(n_pages):
                page = pt_ref[step * ns + q, p]
                for w in range(2):
                    for s in range(segs_pp):
                        out.append(pltpu.make_async_copy(
                            caches[w].at[page, pl.ds(s * CMP_SEG_ROWS, CMP_SEG_ROWS)],
                            cbuf_ref.at[sl, q, w, pl.ds((p * segs_pp + s) * CMP_PITCH, CMP_SEG_ROWS)],
                            sem_ref.at[sl, w]))
                    out.append(pltpu.make_async_copy(
                        caches[2 + w].at[page], sbuf_ref.at[sl, q, w, pl.ds(p * rows_pp, rows_pp)],
                        sem_ref.at[sl, 2 + w]))
        return out

    @pl.when(b == 0)
    def _():
        for cp in copies(0, 0):
            cp.start()

    @pl.when(b + 1 < pl.num_programs(0))
    def _():
        for cp in copies(b + 1, 1 - slot):
            cp.start()

    for w in range(len(caches)):
        for q in range(ns):
            whole = sbuf_ref.at[slot, q, 0]
            pltpu.make_async_copy(whole, whole, sem_ref.at[slot, w]).wait()

    scale = NSA_HD ** -0.5
    nseg = plen // CMP_STRIDE
    tok = lax.broadcasted_iota(jnp.int32, (lp, 1), 0)
    qpos_tok = plen + tok
    qpos4 = _tile_rows(qpos_tok, R)
    lane_n = lax.broadcasted_iota(jnp.int32, (T, KCHUNK), 1)
    new_pos = jnp.where(lane_n < lreal, plen + lane_n, qpos4 + 1)
    past_pos = lax.broadcasted_iota(jnp.int32, (T, plen), 1)
    win_pos = plen - wlen + lax.broadcasted_iota(jnp.int32, (T, wlen), 1)
    win_pos = jnp.where(win_pos > qpos4 - WINDOW, win_pos, qpos4 + 1)
    win_pos = jnp.where(win_pos >= 0, win_pos, qpos4 + 1)
    new_win_pos = jnp.where(new_pos > qpos4 - WINDOW, new_pos, qpos4 + 1)
    zpad_f = jnp.zeros((KCHUNK - lp, NSA_HD), F32)
    pad_new = lambda ref, q, ls: jnp.concatenate([ref[q, :, ls], zpad_f], axis=0).astype(BF16)
    gls = lambda g: slice(g * NSA_HD, (g + 1) * NSA_HD)
    pairs = [(q, g) for q in range(ns) for g in range(G)]
    q4 = [(jnp.concatenate([q_ref[q, :, (g * R + r) * NSA_HD:(g * R + r + 1) * NSA_HD]
                            for r in range(R)], axis=0) * scale).astype(BF16) for q, g in pairs]

    def seg_rows(w, j):
        return jnp.concatenate([cbuf_ref[slot, q, w, pl.ds(G * j + g, nseg, stride=CMP_PITCH), :]
                                for q, g in pairs], axis=0)
    cmp = _compress_groups([functools.partial(seg_rows, w) for w in range(2)],
                           [w1_ref.at[w] for w in range(2)],
                           [w2_ref[w].astype(BF16) for w in range(2)], len(pairs) * nseg)
    prow = lambda i: slice(i * nseg, (i + 1) * nseg)
    branch = _cmp_branch_many(q4, [cmp[0][prow(i)] for i in range(len(pairs))],
                              [cmp[1][prow(i)] for i in range(len(pairs))], qpos4, nb)
    pgs = []
    for i in range(len(pairs)):
        p = branch[i][0]
        pg = p[0:lp]
        for r in range(1, R):
            pg = pg + p[r * lp:(r + 1) * lp]
        pgs.append(pg)
    sel = _select(jnp.concatenate(pgs, axis=0), mm_ref[...], _tile_rows(qpos_tok, len(pairs)), nslc, nsel)
    selm_all = _dot(sel.astype(BF16), e_ref[...])

    sel_jobs, win_jobs = [], []
    for i, (q, g) in enumerate(pairs):
        selm = _tile_rows(selm_all[i * lp:(i + 1) * lp], R)
        k_past = sbuf_ref[slot, q, 0, pl.ds(g, plen, stride=G), :].astype(BF16)
        v_past = sbuf_ref[slot, q, 1, pl.ds(g, plen, stride=G), :].astype(BF16)
        m_past = jnp.where(selm[:, :plen] > 0.5, past_pos, qpos4 + 1) <= qpos4
        m_new = jnp.where(selm[:, plen:] > 0.5, new_pos, qpos4 + 1) <= qpos4
        sel_jobs.append((q4[i], [k_past, pad_new(ksn_ref, q, gls(g))],
                         [v_past, pad_new(vsn_ref, q, gls(g))], [m_past, m_new]))
        k_win = wk_ref[q, pl.ds(g, wlen, stride=G), :].astype(BF16)
        v_win = wv_ref[q, pl.ds(g, wlen, stride=G), :].astype(BF16)
        win_jobs.append((q4[i], [k_win, pad_new(kwn_ref, q, gls(g))],
                         [v_win, pad_new(vwn_ref, q, gls(g))], [win_pos <= qpos4, new_win_pos <= qpos4]))
    outs = _attend_many(sel_jobs + win_jobs)
    o_s, o_w = outs[:len(pairs)], outs[len(pairs):]

    for i, (q, g) in enumerate(pairs):
        gts = jax.nn.sigmoid(sm_ref[q])
        for r in range(R):
            c0 = GATE0 + g * R + r
            rs = slice(r * lp, (r + 1) * lp)
            o = (gts[:, c0:c0 + 1] * branch[i][1][rs]
                 + gts[:, c0 + NSA_HEADS:c0 + NSA_HEADS + 1] * o_s[i][rs]
                 + gts[:, c0 + 2 * NSA_HEADS:c0 + 2 * NSA_HEADS + 1] * o_w[i][rs])
            hq = (g * R + r) * NSA_HD
            o_ref[q, :, hq:hq + NSA_HD] = o.astype(o_ref.dtype)

    wrows = wlen * G
    shift = lreal * G
    for q in range(ns):
        for src, new, dst in ((wk_ref, kwn_ref, wko_ref), (wv_ref, vwn_ref, wvo_ref)):
            dst[q, 0:wrows - shift, :] = src[q, shift:wrows, :]
            for t in range(lreal):
                for g in range(G):
                    row = wrows - shift + t * G + g
                    dst[q, row:row + 1, :] = new[q, t:t + 1, g * NSA_HD:(g + 1) * NSA_HD]


def nsa_decode(proj3, small3, caches, page_table, win_k, win_v, w1cat, w2s, emat, mmat, lreal, ns):
    B, lp, _ = proj3.shape
    G = NSA_KV
    n_pages = page_table.shape[1]
    rows_pp = caches[0].shape[1]
    plen = n_pages * rows_pp // G
    wlen = win_k.shape[1] // G
    assert plen % CMP_STRIDE == 0 and lreal < CMP_STRIDE and lreal <= lp and B % ns == 0
    nb = (plen + lreal) // CMP_STRIDE - 1
    nslc = -(-(plen + lreal) // SLC_LEN)
    nsel = min(N_SELECT, nslc)
    new_spec = lambda c0: pl.BlockSpec((ns, lp, KVW), lambda b, pt: (b, 0, c0 // KVW))
    const = lambda shape: pl.BlockSpec(shape, lambda b, pt: (0,) * len(shape))
    hbm = pl.BlockSpec(memory_space=pl.ANY)
    kern = functools.partial(_nsa_decode_kernel, ns=ns, lp=lp, lreal=lreal, plen=plen, rows_pp=rows_pp,
                             n_pages=n_pages, wlen=wlen, nslc=nslc, nsel=nsel, nb=nb)
    grid_spec = pltpu.PrefetchScalarGridSpec(
        num_scalar_prefetch=1,
        grid=(B // ns,),
        in_specs=[pl.BlockSpec((ns, lp, NSA_HEADS * NSA_HD), lambda b, pt: (b, 0, NQ0 // (NSA_HEADS * NSA_HD))),
                  pl.BlockSpec((ns, lp, LANE), lambda b, pt: (b, 0, 0)),
                  new_spec(KS0), new_spec(VS0), new_spec(KW0), new_spec(VW0),
                  pl.BlockSpec((ns, wlen * G, NSA_HD), lambda b, pt: (b, 0, 0)),
                  pl.BlockSpec((ns, wlen * G, NSA_HD), lambda b, pt: (b, 0, 0)),
                  const((2, CMP_STRIDE, NSA_HD, 2 * NSA_HD)),
                  const((2, NSA_HD, NSA_HD)),
                  const((LANE, plen + KCHUNK)),
                  const((LANE, LANE)),
                  hbm, hbm, hbm, hbm],
        out_specs=[pl.BlockSpec((ns, lp, NSA_HEADS * NSA_HD), lambda b, pt: (b, 0, 0)),
                   pl.BlockSpec((ns, wlen * G, NSA_HD), lambda b, pt: (b, 0, 0)),
                   pl.BlockSpec((ns, wlen * G, NSA_HD), lambda b, pt: (b, 0, 0))],
        scratch_shapes=[pltpu.VMEM((2, ns, 2, n_pages * rows_pp // CMP_SEG_ROWS * CMP_PITCH, NSA_HD), F32),
                        pltpu.VMEM((2, ns, 2, n_pages * rows_pp, NSA_HD), F32),
                        pltpu.SemaphoreType.DMA((2, len(caches)))])
    return pl.pallas_call(
        kern,
        grid_spec=grid_spec,
        out_shape=[jax.ShapeDtypeStruct((B, lp, NSA_HEADS * NSA_HD), BF16),
                   jax.ShapeDtypeStruct(win_k.shape, F32), jax.ShapeDtypeStruct(win_v.shape, F32)],
        compiler_params=_cp(("arbitrary",)),
        name="nsa_decode")(page_table, proj3, small3, proj3, proj3, proj3, proj3, win_k, win_v,
                           w1cat, w2s, emat, mmat, caches[0], caches[1], caches[2], caches[3])


def _out_proj_kernel(og_ref, on_ref, x_ref, w0_ref, w1_ref, g_ref, h_ref, hn_ref):
    h = x_ref[...] + _dot(og_ref[...], w0_ref[...]) + _dot(on_ref[...], w1_ref[...])
    h_ref[...] = h
    hn_ref[...] = _rms(h, g_ref[...]).astype(BF16)


def out_proj(og, on, x2, w_out, g, tm):
    M, Dm = x2.shape
    Kh = og.shape[1]
    return pl.pallas_call(
        _out_proj_kernel,
        grid=(M // tm,),
        in_specs=[pl.BlockSpec((tm, Kh), lambda i: (i, 0)),
                  pl.BlockSpec((tm, Kh), lambda i: (i, 0)),
                  pl.BlockSpec((tm, Dm), lambda i: (i, 0)),
                  pl.BlockSpec((Kh, Dm), lambda i: (0, 0)),
                  pl.BlockSpec((Kh, Dm), lambda i: (1, 0)),
                  pl.BlockSpec((1, Dm), lambda i: (0, 0))],
        out_specs=[pl.BlockSpec((tm, Dm), lambda i: (i, 0)),
                   pl.BlockSpec((tm, Dm), lambda i: (i, 0))],
        out_shape=[jax.ShapeDtypeStruct((M, Dm), F32), jax.ShapeDtypeStruct((M, Dm), BF16)],
        compiler_params=_cp(("parallel",)),
        name="out_proj")(og, on, x2, w_out, w_out, g)


def _ffn_up_seq_kernel(hn_ref, wa_ref, wu_ref, cw_ref, cb_ref, pre_ref, hm_ref, tail_ref, aext_ref,
                       *, tm, tiles_per_seq):
    m = pl.program_id(1)

    @pl.when(m % tiles_per_seq == 0)
    def _():
        aext_ref[0:SUBLANE, :] = pre_ref[0]

    hn = hn_ref[...]
    a = _dot(hn, wa_ref[...])
    u = _dot(hn, wu_ref[...])
    aext_ref[SUBLANE:SUBLANE + tm, :] = a
    p1 = aext_ref[pl.ds(SUBLANE - 1, tm), :]
    p2 = aext_ref[pl.ds(SUBLANE - 2, tm), :]
    c = p2 * cw_ref[0:1, :] + p1 * cw_ref[1:2, :] + a * cw_ref[2:3, :] + cb_ref[...]
    hm_ref[...] = (jax.nn.gelu(c) * u).astype(hm_ref.dtype)
    tail = a[tm - SUBLANE:tm, :]
    tail_ref[0] = tail
    aext_ref[0:SUBLANE, :] = tail


def ffn_up_seq(hn, w_a, w_u, conv_w, conv_b, prefix, seq_len, tm, tf):
    M, Dm = hn.shape
    Fp = conv_w.shape[1]
    nf = Fp // tf
    B = M // seq_len
    tps = seq_len // tm
    kern = functools.partial(_ffn_up_seq_kernel, tm=tm, tiles_per_seq=tps)
    return pl.pallas_call(
        kern,
        grid=(nf, M // tm),
        in_specs=[pl.BlockSpec((tm, Dm), lambda f, m: (m, 0)),
                  pl.BlockSpec((Dm, tf), lambda f, m: (0, f)),
                  pl.BlockSpec((Dm, tf), lambda f, m: (0, f)),
                  pl.BlockSpec((SUBLANE, tf), lambda f, m: (0, f)),
                  pl.BlockSpec((1, tf), lambda f, m: (0, f)),
                  pl.BlockSpec((1, SUBLANE, tf), lambda f, m: (m // tps, 0, f))],
        out_specs=[pl.BlockSpec((tm, tf), lambda f, m: (m, f)),
                   pl.BlockSpec((1, SUBLANE, tf), lambda f, m: (m // tps, 0, f))],
        out_shape=[jax.ShapeDtypeStruct((M, Fp), BF16), jax.ShapeDtypeStruct((B, SUBLANE, Fp), F32)],
        scratch_shapes=[pltpu.VMEM((SUBLANE + tm, tf), F32)],
        compiler_params=_cp(("parallel", "arbitrary")),
        name="ffn_up_seq")(hn, w_a, w_u, conv_w, conv_b, prefix)


def _ffn_up_tm_kernel(hn_ref, wa_ref, wu_ref, cw_ref, cb_ref, pre_ref, hm_ref, tail_ref, *, steps, nb):
    hn = hn_ref[...]
    a = _dot(hn, wa_ref[...])
    u = _dot(hn, wu_ref[...])
    slabs = [pre_ref[i] for i in range(CONV_W - 1)] + [a[t * nb:(t + 1) * nb] for t in range(steps)]
    for t in range(steps):
        c = (slabs[t] * cw_ref[0:1, :] + slabs[t + 1] * cw_ref[1:2, :] + slabs[t + 2] * cw_ref[2:3, :]
             + cb_ref[...])
        hm_ref[t * nb:(t + 1) * nb, :] = (jax.nn.gelu(c) * u[t * nb:(t + 1) * nb]).astype(hm_ref.dtype)
    for i in range(CONV_W - 1):
        tail_ref[i] = slabs[steps + i]


def ffn_up_tm(hn, w_a, w_u, conv_w, conv_b, prefix, steps, tf):
    M, Dm = hn.shape
    Fp = conv_w.shape[1]
    nf = Fp // tf
    nb = M // steps
    kern = functools.partial(_ffn_up_tm_kernel, steps=steps, nb=nb)
    return pl.pallas_call(
        kern,
        grid=(nf,),
        in_specs=[pl.BlockSpec((M, Dm), lambda f: (0, 0)),
                  pl.BlockSpec((Dm, tf), lambda f: (0, f)),
                  pl.BlockSpec((Dm, tf), lambda f: (0, f)),
                  pl.BlockSpec((SUBLANE, tf), lambda f: (0, f)),
                  pl.BlockSpec((1, tf), lambda f: (0, f)),
                  pl.BlockSpec((CONV_W - 1, nb, tf), lambda f: (0, 0, f))],
        out_specs=[pl.BlockSpec((M, tf), lambda f: (0, f)),
                   pl.BlockSpec((CONV_W - 1, nb, tf), lambda f: (0, 0, f))],
        out_shape=[jax.ShapeDtypeStruct((M, Fp), BF16), jax.ShapeDtypeStruct((CONV_W - 1, nb, Fp), F32)],
        compiler_params=_cp(("parallel",)),
        name="ffn_up_tm")(hn, w_a, w_u, conv_w, conv_b, prefix)


def _ffn_down_kernel(hm_ref, w_ref, h_ref, o_ref):
    o_ref[...] = h_ref[...] + _dot(hm_ref[...], w_ref[...])


def ffn_down(hmid, w_down, h1, tm, tn):
    M = hmid.shape[0]
    F, Dm = w_down.shape
    assert F % LANE == 0 and F <= hmid.shape[1]
    return pl.pallas_call(
        _ffn_down_kernel,
        grid=(M // tm, Dm // tn),
        in_specs=[pl.BlockSpec((tm, F), lambda i, j: (i, 0)),
                  pl.BlockSpec((F, tn), lambda i, j: (0, j)),
                  pl.BlockSpec((tm, tn), lambda i, j: (i, j))],
        out_specs=pl.BlockSpec((tm, tn), lambda i, j: (i, j)),
        out_shape=jax.ShapeDtypeStruct((M, Dm), F32),
        compiler_params=_cp(("parallel", "arbitrary")),
        name="ffn_down")(hmid, w_down, h1)


def _ple_final_kernel(h_ref, p_ref, wg_ref, bg_ref, wp_ref, g_ref, o_ref):
    h = h_ref[...]
    gate = jax.nn.sigmoid(_dot(h.astype(BF16), wg_ref[...]) + bg_ref[...])
    h = h + gate * _dot(p_ref[...].astype(BF16), wp_ref[...])
    o_ref[...] = _rms(h, g_ref[...])


def ple_final(h2, p2, w_gate, b_gate, w_proj, g, tm):
    M, Dm = h2.shape
    Pd = p2.shape[1]
    return pl.pallas_call(
        _ple_final_kernel,
        grid=(M // tm,),
        in_specs=[pl.BlockSpec((tm, Dm), lambda i: (i, 0)),
                  pl.BlockSpec((tm, Pd), lambda i: (i, 0)),
                  pl.BlockSpec((Dm, Dm), lambda i: (0, 0)),
                  pl.BlockSpec((1, Dm), lambda i: (0, 0)),
                  pl.BlockSpec((Pd, Dm), lambda i: (0, 0)),
                  pl.BlockSpec((1, Dm), lambda i: (0, 0))],
        out_specs=pl.BlockSpec((tm, Dm), lambda i: (i, 0)),
        out_shape=jax.ShapeDtypeStruct((M, Dm), F32),
        compiler_params=_cp(("parallel",)),
        name="ple_final")(h2, p2, w_gate, b_gate, w_proj, g)


def _pick_tile(n, prefs):
    for t in prefs:
        if n % t == 0:
            return t
    raise ValueError(f"no tile for {n}")


def _block_matrices(total_keys, nb):
    e = np.zeros((LANE, total_keys), np.float32)
    t = np.arange(total_keys)
    e[t // SLC_LEN, t] = 1.0
    ratio = SLC_LEN // CMP_STRIDE
    m = np.zeros((LANE, LANE), np.float32)
    for n in range(nb):
        for s in range(LANE):
            m[n, s] = float(ratio * s <= n <= ratio * s + ratio - 1) + float(ratio * s - 1 <= n <= ratio * s + ratio - 2)
    return jnp.asarray(e, BF16), jnp.asarray(m, BF16)


def _prep_weights(i, attn_norm_g, w_in, gla_w_a2, gla_b_a, gla_norm_g, cmp_wk1, cmp_wk2, cmp_wv1, cmp_wv2,
                  w_out, ffn_norm_g, ffn_w_up, ffn_conv_w, ffn_conv_b, ffn_w_down, ple_w_proj, ple_w_gate,
                  ple_b_gate):
    Dm = w_in.shape[1]
    F = ffn_w_down.shape[1]
    Fp = -(-F // 512) * 512
    win = w_in[i]
    ga0 = NQ0
    ng0 = ga0 + GLA_RANK + (NMAIN - NQ0)
    ngw = 3 * NSA_HEADS
    before_ga = lax.broadcasted_iota(jnp.int32, (1, NMAIN), 1) < ga0
    w_main = jnp.where(before_ga, win[:, :NMAIN], win[:, GLA_RANK:GLA_RANK + NMAIN]).astype(BF16)
    w_small = jnp.concatenate([win[:, ga0:ga0 + GLA_RANK], win[:, ng0:ng0 + ngw],
                               jnp.zeros((Dm, LANE - GLA_RANK - ngw), F32)], axis=1).astype(BF16)
    wa_pad = jnp.concatenate([gla_w_a2[i], jnp.zeros((LANE - GLA_RANK, GLA_HEADS * GLA_DK), F32)],
                             axis=0).astype(BF16)
    cat = lambda w1: jnp.concatenate([w1[:CMP_STRIDE], w1[CMP_STRIDE:]], axis=-1)
    w1cat = jnp.stack([cat(cmp_wk1[i]), cat(cmp_wv1[i])]).astype(BF16)
    w2s = jnp.stack([cmp_wk2[i], cmp_wv2[i]])
    wup = ffn_w_up[i]
    w_up_a = jnp.pad(wup[:, :F], ((0, 0), (0, Fp - F))).astype(BF16)
    w_up_u = jnp.pad(wup[:, F:], ((0, 0), (0, Fp - F))).astype(BF16)
    conv_w = jnp.pad(ffn_conv_w[i], ((0, SUBLANE - CONV_W), (0, Fp - F)))
    conv_b = jnp.pad(ffn_conv_b[i], (0, Fp - F)).reshape(1, Fp)
    w_down = ffn_w_down[i].astype(BF16)
    return dict(
        g_attn=attn_norm_g[i].reshape(1, Dm), w_main=w_main, w_small=w_small, wa_pad=wa_pad,
        b_a=gla_b_a[i].reshape(1, -1), gnorm=gla_norm_g[i].reshape(1, -1), w1cat=w1cat, w2s=w2s,
        w_out=w_out[i].astype(BF16), g_ffn=ffn_norm_g[i].reshape(1, Dm), w_up_a=w_up_a, w_up_u=w_up_u,
        conv_w=conv_w,
        conv_b=conv_b, w_down=w_down, w_proj=ple_w_proj[i].astype(BF16), w_gate=ple_w_gate[i].astype(BF16),
        b_gate=ple_b_gate[i].reshape(1, Dm), F=F, Fp=Fp)


def _kv_rows(kv, B, rows):
    t = kv.reshape(kv.shape[0], B, -1, NSA_KV, NSA_HD)[:, :, rows]
    return [t[i] for i in range(kv.shape[0])]


def _prompt_layer(h2d, p2d, B, L, W):
    Dm = h2d.shape[1]
    M = B * L
    proj, small, kv = norm_proj(h2d, W["g_attn"], W["w_main"], W["w_small"],
                                _pick_tile(M, (1024, 512, 256, 128)), 512)
    proj3 = proj.reshape(B, L, NMAIN)
    small3 = small.reshape(B, L, LANE)
    cb = _pick_tile(L, (128, 64, 32, 16))
    s0 = jnp.zeros((B, GLA_HEADS, GLA_DK, GLA_DV), F32)
    og, gla_state = gla(proj3, small3, W["wa_pad"], W["b_a"], W["gnorm"], s0, cb, GLA_SUB, cb,
                        _pick_tile(B, (4, 2, 1)))
    cmp = compress(proj3, W["w1cat"], W["w2s"])
    _, mmat = _block_matrices(L, L // CMP_STRIDE - 1)
    on = nsa_prompt(proj3, small3, cmp, mmat.T)
    h1, hn = out_proj(og.reshape(M, -1), on.reshape(M, -1), h2d, W["w_out"], W["g_ffn"],
                      _pick_tile(M, (512, 256, 128)))
    prefix = jnp.zeros((B, SUBLANE, W["Fp"]), F32)
    hmid, tail = ffn_up_seq(hn, W["w_up_a"], W["w_up_u"], W["conv_w"], W["conv_b"], prefix, L,
                            _pick_tile(L, (1024, 512, 256, 128)), 512)
    h2 = ffn_down(hmid, W["w_down"], h1, _pick_tile(M, (1024, 512, 256, 128)), 512)
    keep = min(WINDOW, L)
    outs = tuple(_kv_rows(kv[:4], B, slice(0, L)) + _kv_rows(kv[4:], B, slice(L - keep, L))
                 + [gla_state, tail[:, SUBLANE - (CONV_W - 1):, :W["F"]]])
    return h2, outs


def _decode_layer(h3d, p2d_tm, B, L, lp, caches, page_table, win_k, win_v, gla_s, conv_s, W):
    Dm = h3d.shape[2]
    Mp = B * lp
    x2 = h3d.reshape(Mp, Dm)
    proj, small, kv = norm_proj(x2, W["g_attn"], W["w_main"], W["w_small"],
                                _pick_tile(Mp, (1024, 512, 256, 128, 64, 32, 16, 8)), 512)
    proj3 = proj.reshape(B, lp, NMAIN)
    small3 = small.reshape(B, lp, LANE)
    og, gla_state = gla(proj3, small3, W["wa_pad"], W["b_a"], W["gnorm"], gla_s, lp, lp, L,
                        _pick_tile(B, (4, 2, 1)))
    plen = page_table.shape[1] * caches[0].shape[1] // NSA_KV
    emat, mmat = _block_matrices(plen + KCHUNK, (plen + L) // CMP_STRIDE - 1)
    wlen = win_k.shape[1]
    on, wk_new, wv_new = nsa_decode(proj3, small3, caches, page_table,
                                    win_k.reshape(B, wlen * NSA_KV, NSA_HD),
                                    win_v.reshape(B, wlen * NSA_KV, NSA_HD),
                                    W["w1cat"], W["w2s"], emat, mmat, L, _pick_tile(B, (2, 1)))
    h1, hn = out_proj(og.reshape(Mp, -1), on.reshape(Mp, -1), x2, W["w_out"], W["g_ffn"],
                      _pick_tile(Mp, (256, 128, 64, 32, 16, 8)))
    to_tm = lambda t: t.reshape(B, lp, Dm)[:, :L].transpose(1, 0, 2).reshape(L * B, Dm)
    h1_tm, hn_tm = to_tm(h1), to_tm(hn)
    prefix = jnp.pad(conv_s.transpose(1, 0, 2), ((0, 0), (0, 0), (0, W["Fp"] - W["F"])))
    hmid, tail = ffn_up_tm(hn_tm, W["w_up_a"], W["w_up_u"], W["conv_w"], W["conv_b"], prefix, L, 512)
    M = L * B
    h2 = ffn_down(hmid, W["w_down"], h1_tm, _pick_tile(M, (512, 256, 128, 64, 32, 16, 8)), 512)
    outs = tuple(_kv_rows(kv[:4], B, slice(0, L))
                 + [wk_new.reshape(win_k.shape), wv_new.reshape(win_v.shape),
                    gla_state, tail[:, :, :W["F"]].transpose(1, 0, 2)])
    return h2, outs


def kernel(x_prompt, x_sample, p_prompt, p_sample, cache_cmp_k, cache_cmp_v, cache_slc_k, cache_slc_v,
           page_table, state_win_k, state_win_v, state_gla, state_ffn_conv, attn_norm_g, w_in, gla_w_a2,
           gla_b_a, gla_norm_g, cmp_wk1, cmp_wk2, cmp_wv1, cmp_wv2, w_out, ffn_norm_g, ffn_w_up,
           ffn_conv_w, ffn_conv_b, ffn_w_down, ple_w_proj, ple_w_gate, ple_b_gate, final_norm_g):
    depth = w_in.shape[0]
    assert depth == 1, "layers are chained through HBM one at a time; only depth 1 is wired"
    Bp, Lp, Dm = x_prompt.shape
    Bs, Ls, _ = x_sample.shape
    W = _prep_weights(0, attn_norm_g, w_in, gla_w_a2, gla_b_a, gla_norm_g, cmp_wk1, cmp_wk2, cmp_wv1,
                      cmp_wv2, w_out, ffn_norm_g, ffn_w_up, ffn_conv_w, ffn_conv_b, ffn_w_down,
                      ple_w_proj, ple_w_gate, ple_b_gate)
    g_final = final_norm_g.reshape(1, Dm)

    Mp = Bp * Lp
    h2, outs_p = _prompt_layer(x_prompt.reshape(Mp, Dm), None, Bp, Lp, W)
    y_prompt = ple_final(h2, p_prompt[0].reshape(Mp, -1), W["w_gate"], W["b_gate"], W["w_proj"], g_final,
                         _pick_tile(Mp, (512, 256, 128))).reshape(Bp, Lp, Dm)

    lp = -(-Ls // SUBLANE) * SUBLANE
    xs = jnp.pad(x_sample, ((0, 0), (0, lp - Ls), (0, 0)))
    n_pool, psz = cache_cmp_k.shape[1], cache_cmp_k.shape[2]
    caches = [c[0].reshape(n_pool, psz * NSA_KV, NSA_HD)
              for c in (cache_cmp_k, cache_cmp_v, cache_slc_k, cache_slc_v)]
    h2s, outs_s = _decode_layer(xs, None, Bs, Ls, lp, caches, page_table, state_win_k[0], state_win_v[0],
                                state_gla[0], state_ffn_conv[0], W)
    p_tm = p_sample[0].transpose(1, 0, 2).reshape(Ls * Bs, -1)
    Ms = Ls * Bs
    y_tm = ple_final(h2s, p_tm, W["w_gate"], W["b_gate"], W["w_proj"], g_final,
                     _pick_tile(Ms, (256, 128, 64, 32, 16, 8)))
    y_sample = y_tm.reshape(Ls, Bs, Dm).transpose(1, 0, 2)

    lead = lambda t: t[None]
    return (y_prompt, y_sample) + tuple(lead(t) for t in outs_p) + tuple(lead(t) for t in outs_s)
```

```python
import functools
import math

import numpy as np
import jax
import jax.numpy as jnp
from jax import lax
from jax.experimental import pallas as pl
from jax.experimental.pallas import tpu as pltpu

F32 = jnp.float32
BF16 = jnp.bfloat16

GLA_HEADS = 4
GLA_DK = 128
GLA_DV = 256
GLA_RANK = 16
GLA_TAU = 16.0
GLA_SUB = 16
NSA_HEADS = 8
NSA_KV = 2
NSA_REP = NSA_HEADS // NSA_KV
NSA_HD = 128
CMP_LEN = 32
CMP_STRIDE = 16
SLC_LEN = 64
N_SELECT = 16
WINDOW = 512
CONV_W = 3
EPS = 1e-6
NEG = -1e30
M_FLOOR = 0.1 * NEG
FORCED = 1e6

LANE = 128
SUBLANE = 8
KCHUNK = 128
CMP_SEG_ROWS = NSA_KV * CMP_STRIDE
CMP_PITCH = CMP_SEG_ROWS + SUBLANE
VMEM_LIMIT = 56 * 1024 * 1024

GQ0 = 0
GK0 = GQ0 + GLA_HEADS * GLA_DK
GV0 = GK0 + GLA_HEADS * GLA_DK
GR0 = GV0 + GLA_HEADS * GLA_DV
NQ0 = GR0 + GLA_HEADS * GLA_DV
KVW = NSA_KV * NSA_HD
KC0 = NQ0 + NSA_HEADS * NSA_HD
VC0 = KC0 + KVW
KS0 = VC0 + KVW
VS0 = KS0 + KVW
KW0 = VS0 + KVW
VW0 = KW0 + KVW
NMAIN = VW0 + KVW
GATE0 = GLA_RANK


def _cp(sem):
    return pltpu.CompilerParams(dimension_semantics=sem, vmem_limit_bytes=VMEM_LIMIT)


def _dot(a, b):
    return jnp.dot(a, b, preferred_element_type=F32)


def _dot_nt(a, b):
    return lax.dot_general(a, b, (((1,), (1,)), ((), ())), preferred_element_type=F32)


def _rms(x, g):
    return x * lax.rsqrt(jnp.mean(x * x, axis=-1, keepdims=True) + EPS) * g


def _norm_proj_kernel(x_ref, g_ref, wm_ref, ws_ref, om_ref, os_ref, kv_ref, xn_ref, *, j_kv0):
    j = pl.program_id(1)

    @pl.when(j == 0)
    def _():
        xn = _rms(x_ref[...], g_ref[...]).astype(BF16)
        xn_ref[...] = xn
        os_ref[...] = _dot(xn, ws_ref[...])

    tile = _dot(xn_ref[...], wm_ref[...])
    om_ref[...] = tile

    @pl.when(j >= j_kv0)
    def _():
        tm = tile.shape[0]
        for a in range(tile.shape[1] // KVW):
            for g in range(NSA_KV):
                c0 = a * KVW + g * NSA_HD
                kv_ref[a, pl.ds(g, tm, stride=NSA_KV), :] = tile[:, c0:c0 + NSA_HD]


def norm_proj(x2, g, w_main, w_small, tm, tn):
    M, Dm = x2.shape
    N = w_main.shape[1]
    assert KC0 % tn == 0 and tn % KVW == 0
    j_kv0 = KC0 // tn
    per_tile = tn // KVW
    n_kv = (N - KC0) // KVW
    return pl.pallas_call(
        functools.partial(_norm_proj_kernel, j_kv0=j_kv0),
        grid=(M // tm, N // tn),
        in_specs=[pl.BlockSpec((tm, Dm), lambda i, j: (i, 0)),
                  pl.BlockSpec((1, Dm), lambda i, j: (0, 0)),
                  pl.BlockSpec((Dm, tn), lambda i, j: (0, j)),
                  pl.BlockSpec((Dm, LANE), lambda i, j: (0, 0))],
        out_specs=[pl.BlockSpec((tm, tn), lambda i, j: (i, j)),
                   pl.BlockSpec((tm, LANE), lambda i, j: (i, 0)),
                   pl.BlockSpec((per_tile, NSA_KV * tm, NSA_HD),
                                lambda i, j: (jnp.maximum(j - j_kv0, 0), i, 0))],
        out_shape=[jax.ShapeDtypeStruct((M, N), F32), jax.ShapeDtypeStruct((M, LANE), F32),
                   jax.ShapeDtypeStruct((n_kv, NSA_KV * M, NSA_HD), F32)],
        scratch_shapes=[pltpu.VMEM((tm, Dm), BF16)],
        compiler_params=_cp(("parallel", "arbitrary")),
        name="norm_proj")(x2, g, w_main, w_small)


def _gla_kernel(q_ref, k_ref, v_ref, r_ref, sm_ref, wa_ref, ba_ref, gn_ref, s0_ref,
                o_ref, s_ref, st_ref, *, cb, sub, valid, nbb):
    c = pl.program_id(1)

    @pl.when(c == 0)
    def _():
        st_ref[...] = s0_ref[...]

    GH = GLA_HEADS
    H = nbb * GH
    side = lambda ref: ref[0] if nbb == 1 else jnp.concatenate([ref[s] for s in range(nbb)], axis=1)
    hk = lambda h: slice(h * GLA_DK, (h + 1) * GLA_DK)
    hv = lambda h: slice(h * GLA_DV, (h + 1) * GLA_DV)
    ri = lax.broadcasted_iota(jnp.int32, (cb, cb), 0)
    ci = lax.broadcasted_iota(jnp.int32, (cb, cb), 1)
    tri = jnp.where(ci <= ri, 1.0, 0.0)
    rowv = lax.broadcasted_iota(jnp.int32, (cb, 1), 0)
    q = side(q_ref) * (GLA_DK ** -0.5)
    k = side(k_ref)
    v = side(v_ref)
    pre = [_dot(sm_ref[s].astype(BF16), wa_ref[...]) + ba_ref[...] for s in range(nbb)]
    pre = pre[0] if nbb == 1 else jnp.concatenate(pre, axis=1)
    log_a = jax.nn.log_sigmoid(pre) / GLA_TAU
    if valid < cb:
        log_a = jnp.where(rowv < valid, log_a, 0.0)
        k = jnp.where(rowv < valid, k, 0.0)
        v = jnp.where(rowv < valid, v, 0.0)
    hi = log_a.astype(BF16).astype(F32)
    lo = (log_a - hi).astype(BF16).astype(F32)
    b = _dot(tri, hi) + _dot(tri, lo)
    b_last = b[cb - 1:cb, :]
    S = [st_ref[h // GH, h % GH] for h in range(H)]
    vb = v.astype(BF16)

    qe = (q * jnp.exp(b)).astype(BF16)
    o = [_dot(qe[:, hk(h)], S[h].astype(BF16)) for h in range(H)]
    a_rows = [[] for _ in range(H)]
    cc = lax.broadcasted_iota(jnp.int32, (sub, cb), 1)
    for blk in range(cb // sub):
        lo_r, hi_r = blk * sub, (blk + 1) * sub
        r_dec = jnp.zeros((1, H * GLA_DK), F32) if blk == 0 else b[lo_r - 1:lo_r, :]
        q_i = q[lo_r:hi_r] * jnp.exp(b[lo_r:hi_r] - r_dec)
        k_i = k * jnp.exp(jnp.where(rowv < hi_r, r_dec - b, 0.0))
        rr = lax.broadcasted_iota(jnp.int32, (sub, cb), 0) + lo_r
        for h in range(H):
            a = _dot_nt(q_i[:, hk(h)], k_i[:, hk(h)])
            a_rows[h].append(jnp.where(cc <= rr, a, 0.0))
    for h in range(H):
        a = a_rows[h][0] if len(a_rows[h]) == 1 else jnp.concatenate(a_rows[h], axis=0)
        o[h] = o[h] + _dot(a, v[:, hv(h)])

    kh = k * jnp.exp(b_last - b)
    dec = jnp.exp(b_last)
    if cb < LANE:
        kh = jnp.concatenate([kh, jnp.zeros((LANE - cb, H * GLA_DK), F32)], axis=0)
        vpad = jnp.concatenate([vb, jnp.zeros((LANE - cb, H * GLA_DV), BF16)], axis=0)
    else:
        vpad = vb
    for h in range(H):
        dec_col = jnp.transpose(jnp.broadcast_to(dec[:, hk(h)], (GLA_DK, GLA_DK)))[:, 0:1]
        s_new = dec_col * S[h] + _dot(jnp.transpose(kh[:, hk(h)]).astype(BF16), vpad[:, hv(h)])
        st_ref[h // GH, h % GH] = s_new
        s_ref[h // GH, h % GH] = s_new

    r = side(r_ref)
    gate = r * jax.nn.sigmoid(r)
    for h in range(H):
        o_ref[h // GH, :, hv(h % GH)] = (_rms(o[h], gn_ref[...]) * gate[:, hv(h)]).astype(o_ref.dtype)


def gla(proj3, small3, wa_pad, b_a, gnorm, s0, cb, sub, valid, nbb):
    B, L, _ = proj3.shape
    H = GLA_HEADS
    qw, vw = H * GLA_DK, H * GLA_DV
    kern = functools.partial(_gla_kernel, cb=cb, sub=sub, valid=valid, nbb=nbb)
    return pl.pallas_call(
        kern,
        grid=(B // nbb, L // cb),
        in_specs=[pl.BlockSpec((nbb, cb, qw), lambda b, c: (b, c, GQ0 // qw)),
                  pl.BlockSpec((nbb, cb, qw), lambda b, c: (b, c, GK0 // qw)),
                  pl.BlockSpec((nbb, cb, vw), lambda b, c: (b, c, GV0 // vw)),
                  pl.BlockSpec((nbb, cb, vw), lambda b, c: (b, c, GR0 // vw)),
                  pl.BlockSpec((nbb, cb, LANE), lambda b, c: (b, c, 0)),
                  pl.BlockSpec((LANE, qw), lambda b, c: (0, 0)),
                  pl.BlockSpec((1, qw), lambda b, c: (0, 0)),
                  pl.BlockSpec((1, GLA_DV), lambda b, c: (0, 0)),
                  pl.BlockSpec((nbb, H, GLA_DK, GLA_DV), lambda b, c: (b, 0, 0, 0))],
        out_specs=[pl.BlockSpec((nbb, cb, vw), lambda b, c: (b, c, 0)),
                   pl.BlockSpec((nbb, H, GLA_DK, GLA_DV), lambda b, c: (b, 0, 0, 0))],
        out_shape=[jax.ShapeDtypeStruct((B, L, vw), BF16),
                   jax.ShapeDtypeStruct((B, H, GLA_DK, GLA_DV), F32)],
        scratch_shapes=[pltpu.VMEM((nbb, H, GLA_DK, GLA_DV), F32)],
        compiler_params=_cp(("parallel", "arbitrary")),
        name="gla")(proj3, proj3, proj3, proj3, small3, wa_pad, b_a, gnorm, s0)


def _pair_rows(load_rows, j):
    return jnp.concatenate([load_rows(j), load_rows(j + 1)], axis=1).astype(BF16)


def _pair_weights(w1_ref, j):
    w = w1_ref[pl.ds(j, 2)]
    return w.reshape(2 * NSA_HD, 2 * NSA_HD)


def _compress_group(load_rows, w1_ref, w2, nseg):
    acc = jnp.zeros((nseg, 2 * NSA_HD), F32)
    for j in range(0, CMP_STRIDE, 2):
        acc = acc + _dot(_pair_rows(load_rows, j), _pair_weights(w1_ref, j))
    first = acc[:, :NSA_HD]
    second = acc[:, NSA_HD:]
    h = jax.nn.gelu(first + pltpu.roll(second, nseg - 1, axis=0))
    return _dot(h.astype(BF16), w2)


def _compress_groups(loads, w1_refs, w2s, nrow):
    n = len(loads)
    acc = [jnp.zeros((nrow, 2 * NSA_HD), F32) for _ in range(n)]
    for j in range(0, CMP_STRIDE, 2):
        for i in range(n):
            acc[i] = acc[i] + _dot(_pair_rows(loads[i], j), _pair_weights(w1_refs[i], j))
    hs = [jax.nn.gelu(a[:, :NSA_HD] + pltpu.roll(a[:, NSA_HD:], nrow - 1, axis=0)) for a in acc]
    return [_dot(h.astype(BF16), w2) for h, w2 in zip(hs, w2s)]


def _cmp_branch_many(q4s, cks, cvs, qpos4, nb):
    s = [_dot_nt(q4, ck.astype(BF16)) for q4, ck in zip(q4s, cks)]
    n = lax.broadcasted_iota(jnp.int32, s[0].shape, 1)
    mask = jnp.where(n * CMP_STRIDE + (CMP_LEN - 1) <= qpos4, n, nb) < nb
    s = [jnp.where(mask, si, NEG) for si in s]
    m = [jnp.max(si, axis=-1, keepdims=True) for si in s]
    e = [jnp.where(mask, jnp.exp(si - mi), 0.0) for si, mi in zip(s, m)]
    p = [ei / jnp.maximum(jnp.sum(ei, axis=-1, keepdims=True), 1e-30) for ei in e]
    o = [_dot(pi.astype(BF16), cv.astype(BF16)) for pi, cv in zip(p, cvs)]
    return list(zip(p, o))


def _select(pg, mmat, qpos_tok, nslc, nsel):
    hi = pg.astype(BF16)
    lo = (pg - hi.astype(F32)).astype(BF16)
    imp = _dot(hi, mmat) + _dot(lo, mmat)
    blk = lax.broadcasted_iota(jnp.int32, pg.shape, 1)
    cur = qpos_tok // SLC_LEN
    forced = jnp.where(blk == 0, 1, 0) + jnp.where(blk == cur, 1, 0) + jnp.where(blk == cur - 1, 1, 0)
    score = jnp.where(blk <= cur, jnp.where(forced > 0, FORCED, imp), -1.0)
    score = jnp.where(blk < nslc, score, -2.0)
    rank = jnp.zeros(pg.shape, F32)
    for j in range(nslc):
        sj = score[:, j:j + 1]
        tie = jnp.where(blk > j, sj, NEG)
        rank = rank + jnp.where(sj > score, 1.0, 0.0) + jnp.where(tie == score, 1.0, 0.0)
    return jnp.where(rank < nsel, jnp.where(blk < nslc, 1.0, 0.0), 0.0)


def _flash_init(m_ref, l_ref, acc_ref):
    m_ref[...] = jnp.full(m_ref.shape, NEG, F32)
    l_ref[...] = jnp.zeros(l_ref.shape, F32)
    acc_ref[...] = jnp.zeros(acc_ref.shape, F32)


def _flash_step(q4, k, v, mask, m_ref, l_ref, acc_ref):
    s = jnp.where(mask, _dot_nt(q4, k), NEG)
    m_old = m_ref[...]
    m_new = jnp.maximum(m_old, jnp.max(s, axis=-1, keepdims=True))
    alpha = jnp.exp(m_old - m_new)
    p = jnp.where(mask, jnp.exp(s - m_new), 0.0)
    l_ref[...] = alpha * l_ref[...] + jnp.sum(p, axis=-1, keepdims=True)
    acc_ref[...] = alpha * acc_ref[...] + _dot(p.astype(BF16), v)
    m_ref[...] = m_new


def _flash_out(l_ref, acc_ref):
    return acc_ref[...] / jnp.maximum(l_ref[...], 1e-30)


def _tile_rows(x, reps):
    return jnp.concatenate([x] * reps, axis=0)


def _attend_many(jobs):
    s = [[jnp.where(mk, _dot_nt(q4, k), NEG) for k, mk in zip(keys, masks)] for q4, keys, _, masks in jobs]
    m = []
    for sj in s:
        mj = sj[0].max(axis=-1, keepdims=True)
        for si in sj[1:]:
            mj = jnp.maximum(mj, si.max(axis=-1, keepdims=True))
        m.append(mj)
    e = [[jnp.where(mk, jnp.exp(si - mj), 0.0) for si, mk in zip(sj, job[3])]
         for sj, mj, job in zip(s, m, jobs)]
    l = [sum(ei.sum(axis=-1, keepdims=True) for ei in ej) for ej in e]
    o = [sum(_dot(ei.astype(BF16), v) for ei, v in zip(ej, job[2])) for ej, job in zip(e, jobs)]
    return [oj / jnp.maximum(lj, 1e-30) for oj, lj in zip(o, l)]


def _compress_kernel(x_ref, w1_ref, w2_ref, o_ref, *, nseg):
    load = lambda j: x_ref[0, pl.ds(j, nseg, stride=CMP_STRIDE), :]
    o_ref[0, 0, 0] = _compress_group(load, w1_ref.at[0], w2_ref[0].astype(BF16), nseg)


def compress(proj3, w1cat, w2s):
    B, L, _ = proj3.shape
    nseg = L // CMP_STRIDE
    G = NSA_KV
    return pl.pallas_call(
        functools.partial(_compress_kernel, nseg=nseg),
        grid=(B, 2, G),
        in_specs=[pl.BlockSpec((1, L, NSA_HD), lambda b, w, g: (b, 0, KC0 // NSA_HD + w * G + g)),
                  pl.BlockSpec((1, CMP_STRIDE, NSA_HD, 2 * NSA_HD), lambda b, w, g: (w, 0, 0, 0)),
                  pl.BlockSpec((1, NSA_HD, NSA_HD), lambda b, w, g: (w, 0, 0))],
        out_specs=pl.BlockSpec((1, 1, 1, nseg, NSA_HD), lambda b, w, g: (w, b, g, 0, 0)),
        out_shape=jax.ShapeDtypeStruct((2, B, G, nseg, NSA_HD), F32),
        compiler_params=_cp(("parallel", "parallel", "parallel")),
        name="compress")(proj3, w1cat, w2s)


def _select_t(pg_t, mm_t, qpos_row, nslc, nsel):
    nsp = -(-nslc // SUBLANE) * SUBLANE
    hi = pg_t.astype(BF16)
    lo = (pg_t - hi.astype(F32)).astype(BF16)
    imp = (_dot(mm_t, hi) + _dot(mm_t, lo))[:nsp]
    blk = lax.broadcasted_iota(jnp.int32, imp.shape, 0)
    cur = qpos_row // SLC_LEN
    forced = jnp.where(blk == 0, 1, 0) + jnp.where(blk == cur, 1, 0) + jnp.where(blk == cur - 1, 1, 0)
    score = jnp.where(blk <= cur, jnp.where(forced > 0, FORCED, imp), -1.0)
    score = jnp.where(blk < nslc, score, -2.0)
    rank = jnp.zeros(imp.shape, F32)
    for j in range(nslc):
        sj = score[j:j + 1, :]
        tie = jnp.where(blk > j, sj, NEG)
        rank = rank + jnp.where(sj > score, 1.0, 0.0) + jnp.where(tie == score, 1.0, 0.0)
    return jnp.where(rank < nsel, jnp.where(blk < nslc, 1.0, 0.0), 0.0)


def _flash_steps_t(q_t, jobs):
    s = [_dot(k, q_t) + bias for k, _, bias, _ in jobs]
    m_old = [st[0][...] for _, _, _, st in jobs]
    m_new = [jnp.maximum(mo, jnp.max(si, axis=0, keepdims=True)) for mo, si in zip(m_old, s)]
    alpha = [jnp.exp(mo - mn) for mo, mn in zip(m_old, m_new)]
    p = [jnp.exp(si - mn) for si, mn in zip(s, m_new)]
    pv = [_dot(v_t, pi.astype(BF16)) for (_, v_t, _, _), pi in zip(jobs, p)]
    for (_, _, _, (m_ref, l_ref, acc_ref)), mn, al, pi, pvi in zip(jobs, m_new, alpha, p, pv):
        l_ref[...] = al * l_ref[...] + jnp.sum(pi, axis=0, keepdims=True)
        acc_ref[...] = al * acc_ref[...] + pvi
        m_ref[...] = mn


def _nsa_prompt_kernel(q_ref, sm_ref, ck_ref, cv_ref, ks_ref, vs_ref, kw_ref, vw_ref, mm_ref,
                       o_ref, vst_ref, vwt_ref, sel_ref, m_ref, l_ref, acc_ref, mw_ref, lw_ref, accw_ref,
                       *, tq, kstep, nslc, nsel, nb):
    g = pl.program_id(1)
    qb = pl.program_id(2)
    R = NSA_REP
    T = R * tq
    L = ks_ref.shape[1]
    scale = NSA_HD ** -0.5

    @pl.when(qb == 0)
    def _():
        for c in range(L // KCHUNK):
            cs = slice(c * KCHUNK, (c + 1) * KCHUNK)
            vst_ref[:, cs] = jnp.transpose(vs_ref[0, cs, :]).astype(BF16)
            vwt_ref[:, cs] = jnp.transpose(vw_ref[0, cs, :]).astype(BF16)

    q_t = (jnp.concatenate([jnp.transpose(q_ref[0, :, r * NSA_HD:(r + 1) * NSA_HD]) for r in range(R)],
                           axis=1) * scale).astype(BF16)
    qpos_row = qb * tq + lax.broadcasted_iota(jnp.int32, (1, tq), 1)
    qpos4 = jnp.concatenate([qpos_row] * R, axis=1)

    key = lax.broadcasted_iota(jnp.int32, (kstep, T), 0)
    hi_step = (qb * tq + tq - 1) // kstep + 1
    lo_step = jnp.maximum(qb * tq - (WINDOW - 1), 0) // kstep
    last = hi_step - 1
    sel_state = (m_ref, l_ref, acc_ref)
    win_state = (mw_ref, lw_ref, accw_ref)

    def sel_step(c, causal):
        off = pl.multiple_of(c * kstep, kstep)
        k = ks_ref[0, pl.ds(off, kstep), :].astype(BF16)
        v_t = vst_ref[:, pl.ds(off, kstep)]
        blocks = [jnp.broadcast_to(sel_ref[pl.ds((kstep // SLC_LEN) * c + i, 1), :], (SLC_LEN, tq))
                  for i in range(kstep // SLC_LEN)]
        bias = jnp.concatenate(blocks, axis=0)
        bias = jnp.concatenate([bias] * R, axis=1)
        if causal:
            bias = jnp.where(off + key <= qpos4, bias, NEG)
        return k, v_t, bias, sel_state

    def win_step(c, causal):
        off = pl.multiple_of(c * kstep, kstep)
        k = kw_ref[0, pl.ds(off, kstep), :].astype(BF16)
        v_t = vwt_ref[:, pl.ds(off, kstep)]
        kpos = off + key
        if causal:
            bias = jnp.where(kpos <= qpos4, 0.0, NEG)
        else:
            bias = jnp.where(kpos > qpos4 - WINDOW, 0.0, NEG)
        return k, v_t, bias, win_state

    k_d, vt_d, bias_d, _ = win_step(last, True)
    s = _dot(ck_ref[0, 0, 0].astype(BF16), q_t)
    s_d = _dot(k_d, q_t) + bias_d
    n = lax.broadcasted_iota(jnp.int32, s.shape, 0)
    mask = jnp.where(n * CMP_STRIDE + (CMP_LEN - 1) <= qpos4, n, nb) < nb
    s = jnp.where(mask, s, NEG)
    m_c = jnp.max(s, axis=0, keepdims=True)
    m_d = jnp.maximum(jnp.max(s_d, axis=0, keepdims=True), M_FLOOR)
    e = jnp.where(mask, jnp.exp(s - m_c), 0.0)
    p_d = jnp.exp(s_d - m_d)
    p = e / jnp.maximum(jnp.sum(e, axis=0, keepdims=True), 1e-30)
    lw_ref[...] = jnp.sum(p_d, axis=0, keepdims=True)
    o_c = _dot(jnp.transpose(cv_ref[0, 0, 0]).astype(BF16), p.astype(BF16))
    accw_ref[...] = _dot(vt_d, p_d.astype(BF16))
    mw_ref[...] = m_d
    pg = p[:, 0:tq]
    for r in range(1, R):
        pg = pg + p[:, r * tq:(r + 1) * tq]
    sel_ref[...] = (_select_t(pg, mm_ref[...], qpos_row, nslc, nsel) - 1.0) * (-NEG)

    m_ref[...] = jnp.full(m_ref.shape, M_FLOOR, F32)
    l_ref[...] = jnp.zeros(l_ref.shape, F32)
    acc_ref[...] = jnp.zeros(acc_ref.shape, F32)

    def body_sel(c, carry):
        _flash_steps_t(q_t, [sel_step(c, False)])
        return carry

    def body_both(c, carry):
        _flash_steps_t(q_t, [sel_step(c, False), win_step(c, False)])
        return carry

    lax.fori_loop(0, jnp.minimum(lo_step, last), body_sel, 0)
    lax.fori_loop(lo_step, last, body_both, 0)
    _flash_steps_t(q_t, [sel_step(last, True)])
    o_s = acc_ref[...] / jnp.maximum(l_ref[...], 1e-30)
    o_w = accw_ref[...] / jnp.maximum(lw_ref[...], 1e-30)

    g_t = jnp.transpose(jax.nn.sigmoid(sm_ref[0]))
    for r in range(R):
        def gate(br, r=r):
            c0 = GATE0 + br * NSA_HEADS + r
            return jnp.where(g == 0, g_t[c0:c0 + 1, :], g_t[c0 + R:c0 + R + 1, :])
        cs = slice(r * tq, (r + 1) * tq)
        o = gate(0) * o_c[:, cs] + gate(1) * o_s[:, cs] + gate(2) * o_w[:, cs]
        o_ref[0, :, r * NSA_HD:(r + 1) * NSA_HD] = jnp.transpose(o).astype(o_ref.dtype)


def nsa_prompt(proj3, small3, cmp, mmat_t):
    B, L, _ = proj3.shape
    tq = _pick_tile(L, (2 * KCHUNK, KCHUNK))
    nseg = L // CMP_STRIDE
    nb = nseg - 1
    nslc = -(-L // SLC_LEN)
    nsel = min(N_SELECT, nslc)
    nsp = -(-nslc // SUBLANE) * SUBLANE
    G, R = NSA_KV, NSA_REP
    kstep = _pick_tile(L, (2 * KCHUNK, KCHUNK))
    assert nseg == LANE and L % KCHUNK == 0 and kstep % SLC_LEN == 0
    kv_spec = lambda c0: pl.BlockSpec((1, L, NSA_HD), lambda b, g, qb: (b, 0, c0 // NSA_HD + g))
    kern = functools.partial(_nsa_prompt_kernel, tq=tq, kstep=kstep, nslc=nslc, nsel=nsel, nb=nb)
    return pl.pallas_call(
        kern,
        grid=(B, G, L // tq),
        in_specs=[pl.BlockSpec((1, tq, R * NSA_HD), lambda b, g, qb: (b, qb, NQ0 // (R * NSA_HD) + g)),
                  pl.BlockSpec((1, tq, LANE), lambda b, g, qb: (b, qb, 0)),
                  pl.BlockSpec((1, 1, 1, nseg, NSA_HD), lambda b, g, qb: (0, b, g, 0, 0)),
                  pl.BlockSpec((1, 1, 1, nseg, NSA_HD), lambda b, g, qb: (1, b, g, 0, 0)),
                  kv_spec(KS0), kv_spec(VS0), kv_spec(KW0), kv_spec(VW0),
                  pl.BlockSpec((LANE, LANE), lambda b, g, qb: (0, 0))],
        out_specs=pl.BlockSpec((1, tq, R * NSA_HD), lambda b, g, qb: (b, qb, g)),
        out_shape=jax.ShapeDtypeStruct((B, L, NSA_HEADS * NSA_HD), BF16),
        scratch_shapes=[pltpu.VMEM((NSA_HD, L), BF16),
                        pltpu.VMEM((NSA_HD, L), BF16),
                        pltpu.VMEM((nsp, tq), F32)]
                       + 2 * [pltpu.VMEM((1, R * tq), F32),
                              pltpu.VMEM((1, R * tq), F32),
                              pltpu.VMEM((NSA_HD, R * tq), F32)],
        compiler_params=_cp(("parallel", "parallel", "arbitrary")),
        name="nsa_prompt")(proj3, small3, cmp, cmp, proj3, proj3, proj3, proj3, mmat_t)


def _nsa_decode_kernel(pt_ref, q_ref, sm_ref, ksn_ref, vsn_ref, kwn_ref, vwn_ref, wk_ref, wv_ref,
                       w1_ref, w2_ref, e_ref, mm_ref, c0_hbm, c1_hbm, c2_hbm, c3_hbm,
                       o_ref, wko_ref, wvo_ref, cbuf_ref, sbuf_ref, sem_ref,
                       *, ns, lp, lreal, plen, rows_pp, n_pages, wlen, nslc, nsel, nb):
    b = pl.program_id(0)
    slot = b % 2
    caches = (c0_hbm, c1_hbm, c2_hbm, c3_hbm)
    G = NSA_KV
    R = NSA_REP
    T = R * lp
    segs_pp = rows_pp // CMP_SEG_ROWS

    def copies(step, sl):
        out = []
        for q in range(ns):
            for p in range(n_pages):
                page = pt_ref[step * ns + q, p]
                for w in range(2):
                    for s in range(segs_pp):
                        out.append(pltpu.make_async_copy(
                            caches[w].at[page, pl.ds(s * CMP_SEG_ROWS, CMP_SEG_ROWS)],
                            cbuf_ref.at[sl, q, w, pl.ds((p * segs_pp + s) * CMP_PITCH, CMP_SEG_ROWS)],
                            sem_ref.at[sl, w]))
                    out.append(pltpu.make_async_copy(
                        caches[2 + w].at[page], sbuf_ref.at[sl, q, w, pl.ds(p * rows_pp, rows_pp)],
                        sem_ref.at[sl, 2 + w]))
        return out

    @pl.when(b == 0)
    def _():
        for cp in copies(0, 0):
            cp.start()

    @pl.when(b + 1 < pl.num_programs(0))
    def _():
        for cp in copies(b + 1, 1 - slot):
            cp.start()

    for w in range(len(caches)):
        for q in range(ns):
            whole = sbuf_ref.at[slot, q, 0]
            pltpu.make_async_copy(whole, whole, sem_ref.at[slot, w]).wait()

    scale = NSA_HD ** -0.5
    nseg = plen // CMP_STRIDE
    tok = lax.broadcasted_iota(jnp.int32, (lp, 1), 0)
    qpos_tok = plen + tok
    qpos4 = _tile_rows(qpos_tok, R)
    lane_n = lax.broadcasted_iota(jnp.int32, (T, KCHUNK), 1)
    new_pos = jnp.where(lane_n < lreal, plen + lane_n, qpos4 + 1)
    past_pos = lax.broadcasted_iota(jnp.int32, (T, plen), 1)
    win_pos = plen - wlen + lax.broadcasted_iota(jnp.int32, (T, wlen), 1)
    win_pos = jnp.where(win_pos > qpos4 - WINDOW, win_pos, qpos4 + 1)
    win_pos = jnp.where(win_pos >= 0, win_pos, qpos4 + 1)
    new_win_pos = jnp.where(new_pos > qpos4 - WINDOW, new_pos, qpos4 + 1)
    zpad_f = jnp.zeros((KCHUNK - lp, NSA_HD), F32)
    pad_new = lambda ref, q, ls: jnp.concatenate([ref[q, :, ls], zpad_f], axis=0).astype(BF16)
    gls = lambda g: slice(g * NSA_HD, (g + 1) * NSA_HD)
    pairs = [(q, g) for q in range(ns) for g in range(G)]
    q4 = [(jnp.concatenate([q_ref[q, :, (g * R + r) * NSA_HD:(g * R + r + 1) * NSA_HD]
                            for r in range(R)], axis=0) * scale).astype(BF16) for q, g in pairs]

    def seg_rows(w, j):
        return jnp.concatenate([cbuf_ref[slot, q, w, pl.ds(G * j + g, nseg, stride=CMP_PITCH), :]
                                for q, g in pairs], axis=0)
    cmp = _compress_groups([functools.partial(seg_rows, w) for w in range(2)],
                           [w1_ref.at[w] for w in range(2)],
                           [w2_ref[w].astype(BF16) for w in range(2)], len(pairs) * nseg)
    prow = lambda i: slice(i * nseg, (i + 1) * nseg)
    branch = _cmp_branch_many(q4, [cmp[0][prow(i)] for i in range(len(pairs))],
                              [cmp[1][prow(i)] for i in range(len(pairs))], qpos4, nb)
    pgs = []
    for i in range(len(pairs)):
        p = branch[i][0]
        pg = p[0:lp]
        for r in range(1, R):
            pg = pg + p[r * lp:(r + 1) * lp]
        pgs.append(pg)
    sel = _select(jnp.concatenate(pgs, axis=0), mm_ref[...], _tile_rows(qpos_tok, len(pairs)), nslc, nsel)
    selm_all = _dot(sel.astype(BF16), e_ref[...])

    sel_jobs, win_jobs = [], []
    for i, (q, g) in enumerate(pairs):
        selm = _tile_rows(selm_all[i * lp:(i + 1) * lp], R)
        k_past = sbuf_ref[slot, q, 0, pl.ds(g, plen, stride=G), :].astype(BF16)
        v_past = sbuf_ref[slot, q, 1, pl.ds(g, plen, stride=G), :].astype(BF16)
        m_past = jnp.where(selm[:, :plen] > 0.5, past_pos, qpos4 + 1) <= qpos4
        m_new = jnp.where(selm[:, plen:] > 0.5, new_pos, qpos4 + 1) <= qpos4
        sel_jobs.append((q4[i], [k_past, pad_new(ksn_ref, q, gls(g))],
                         [v_past, pad_new(vsn_ref, q, gls(g))], [m_past, m_new]))
        k_win = wk_ref[q, pl.ds(g, wlen, stride=G), :].astype(BF16)
        v_win = wv_ref[q, pl.ds(g, wlen, stride=G), :].astype(BF16)
        win_jobs.append((q4[i], [k_win, pad_new(kwn_ref, q, gls(g))],
                         [v_win, pad_new(vwn_ref, q, gls(g))], [win_pos <= qpos4, new_win_pos <= qpos4]))
    outs = _attend_many(sel_jobs + win_jobs)
    o_s, o_w = outs[:len(pairs)], outs[len(pairs):]

    for i, (q, g) in enumerate(pairs):
        gts = jax.nn.sigmoid(sm_ref[q])
        for r in range(R):
            c0 = GATE0 + g * R + r
            rs = slice(r * lp, (r + 1) * lp)
            o = (gts[:, c0:c0 + 1] * branch[i][1][rs]
                 + gts[:, c0 + NSA_HEADS:c0 + NSA_HEADS + 1] * o_s[i][rs]
                 + gts[:, c0 + 2 * NSA_HEADS:c0 + 2 * NSA_HEADS + 1] * o_w[i][rs])
            hq = (g * R + r) * NSA_HD
            o_ref[q, :, hq:hq + NSA_HD] = o.astype(o_ref.dtype)

    wrows = wlen * G
    shift = lreal * G
    for q in range(ns):
        for src, new, dst in ((wk_ref, kwn_ref, wko_ref), (wv_ref, vwn_ref, wvo_ref)):
            dst[q, 0:wrows - shift, :] = src[q, shift:wrows, :]
            for t in range(lreal):
                for g in range(G):
                    row = wrows - shift + t * G + g
                    dst[q, row:row + 1, :] = new[q, t:t + 1, g * NSA_HD:(g + 1) * NSA_HD]


def nsa_decode(proj3, small3, caches, page_table, win_k, win_v, w1cat, w2s, emat, mmat, lreal, ns):
    B, lp, _ = proj3.shape
    G = NSA_KV
    n_pages = page_table.shape[1]
    rows_pp = caches[0].shape[1]
    plen = n_pages * rows_pp // G
    wlen = win_k.shape[1] // G
    assert plen % CMP_STRIDE == 0 and lreal < CMP_STRIDE and lreal <= lp and B % ns == 0
    nb = (plen + lreal) // CMP_STRIDE - 1
    nslc = -(-(plen + lreal) // SLC_LEN)
    nsel = min(N_SELECT, nslc)
    new_spec = lambda c0: pl.BlockSpec((ns, lp, KVW), lambda b, pt: (b, 0, c0 // KVW))
    const = lambda shape: pl.BlockSpec(shape, lambda b, pt: (0,) * len(shape))
    hbm = pl.BlockSpec(memory_space=pl.ANY)
    kern = functools.partial(_nsa_decode_kernel, ns=ns, lp=lp, lreal=lreal, plen=plen, rows_pp=rows_pp,
                             n_pages=n_pages, wlen=wlen, nslc=nslc, nsel=nsel, nb=nb)
    grid_spec = pltpu.PrefetchScalarGridSpec(
        num_scalar_prefetch=1,
        grid=(B // ns,),
        in_specs=[pl.BlockSpec((ns, lp, NSA_HEADS * NSA_HD), lambda b, pt: (b, 0, NQ0 // (NSA_HEADS * NSA_HD))),
                  pl.BlockSpec((ns, lp, LANE), lambda b, pt: (b, 0, 0)),
                  new_spec(KS0), new_spec(VS0), new_spec(KW0), new_spec(VW0),
                  pl.BlockSpec((ns, wlen * G, NSA_HD), lambda b, pt: (b, 0, 0)),
                  pl.BlockSpec((ns, wlen * G, NSA_HD), lambda b, pt: (b, 0, 0)),
                  const((2, CMP_STRIDE, NSA_HD, 2 * NSA_HD)),
                  const((2, NSA_HD, NSA_HD)),
                  const((LANE, plen + KCHUNK)),
                  const((LANE, LANE)),
                  hbm, hbm, hbm, hbm],
        out_specs=[pl.BlockSpec((ns, lp, NSA_HEADS * NSA_HD), lambda b, pt: (b, 0, 0)),
                   pl.BlockSpec((ns, wlen * G, NSA_HD), lambda b, pt: (b, 0, 0)),
                   pl.BlockSpec((ns, wlen * G, NSA_HD), lambda b, pt: (b, 0, 0))],
        scratch_shapes=[pltpu.VMEM((2, ns, 2, n_pages * rows_pp // CMP_SEG_ROWS * CMP_PITCH, NSA_HD), F32),
                        pltpu.VMEM((2, ns, 2, n_pages * rows_pp, NSA_HD), F32),
                        pltpu.SemaphoreType.DMA((2, len(caches)))])
    return pl.pallas_call(
        kern,
        grid_spec=grid_spec,
        out_shape=[jax.ShapeDtypeStruct((B, lp, NSA_HEADS * NSA_HD), BF16),
                   jax.ShapeDtypeStruct(win_k.shape, F32), jax.ShapeDtypeStruct(win_v.shape, F32)],
        compiler_params=_cp(("arbitrary",)),
        name="nsa_decode")(page_table, proj3, small3, proj3, proj3, proj3, proj3, win_k, win_v,
                           w1cat, w2s, emat, mmat, caches[0], caches[1], caches[2], caches[3])


def _out_proj_kernel(og_ref, on_ref, x_ref, w0_ref, w1_ref, g_ref, h_ref, hn_ref):
    h = x_ref[...] + _dot(og_ref[...], w0_ref[...]) + _dot(on_ref[...], w1_ref[...])
    h_ref[...] = h
    hn_ref[...] = _rms(h, g_ref[...]).astype(BF16)


def out_proj(og, on, x2, w_out, g, tm):
    M, Dm = x2.shape
    Kh = og.shape[1]
    return pl.pallas_call(
        _out_proj_kernel,
        grid=(M // tm,),
        in_specs=[pl.BlockSpec((tm, Kh), lambda i: (i, 0)),
                  pl.BlockSpec((tm, Kh), lambda i: (i, 0)),
                  pl.BlockSpec((tm, Dm), lambda i: (i, 0)),
                  pl.BlockSpec((Kh, Dm), lambda i: (0, 0)),
                  pl.BlockSpec((Kh, Dm), lambda i: (1, 0)),
                  pl.BlockSpec((1, Dm), lambda i: (0, 0))],
        out_specs=[pl.BlockSpec((tm, Dm), lambda i: (i, 0)),
                   pl.BlockSpec((tm, Dm), lambda i: (i, 0))],
        out_shape=[jax.ShapeDtypeStruct((M, Dm), F32), jax.ShapeDtypeStruct((M, Dm), BF16)],
        compiler_params=_cp(("parallel",)),
        name="out_proj")(og, on, x2, w_out, w_out, g)


def _ffn_up_seq_kernel(hn_ref, wa_ref, wu_ref, cw_ref, cb_ref, pre_ref, hm_ref, tail_ref, aext_ref,
                       *, tm, tiles_per_seq):
    m = pl.program_id(1)

    @pl.when(m % tiles_per_seq == 0)
    def _():
        aext_ref[0:SUBLANE, :] = pre_ref[0]

    hn = hn_ref[...]
    a = _dot(hn, wa_ref[...])
    u = _dot(hn, wu_ref[...])
    aext_ref[SUBLANE:SUBLANE + tm, :] = a
    p1 = aext_ref[pl.ds(SUBLANE - 1, tm), :]
    p2 = aext_ref[pl.ds(SUBLANE - 2, tm), :]
    c = p2 * cw_ref[0:1, :] + p1 * cw_ref[1:2, :] + a * cw_ref[2:3, :] + cb_ref[...]
    hm_ref[...] = (jax.nn.gelu(c) * u).astype(hm_ref.dtype)
    tail = a[tm - SUBLANE:tm, :]
    tail_ref[0] = tail
    aext_ref[0:SUBLANE, :] = tail


def ffn_up_seq(hn, w_a, w_u, conv_w, conv_b, prefix, seq_len, tm, tf):
    M, Dm = hn.shape
    Fp = conv_w.shape[1]
    nf = Fp // tf
    B = M // seq_len
    tps = seq_len // tm
    kern = functools.partial(_ffn_up_seq_kernel, tm=tm, tiles_per_seq=tps)
    return pl.pallas_call(
        kern,
        grid=(nf, M // tm),
        in_specs=[pl.BlockSpec((tm, Dm), lambda f, m: (m, 0)),
                  pl.BlockSpec((Dm, tf), lambda f, m: (0, f)),
                  pl.BlockSpec((Dm, tf), lambda f, m: (0, f)),
                  pl.BlockSpec((SUBLANE, tf), lambda f, m: (0, f)),
                  pl.BlockSpec((1, tf), lambda f, m: (0, f)),
                  pl.BlockSpec((1, SUBLANE, tf), lambda f, m: (m // tps, 0, f))],
        out_specs=[pl.BlockSpec((tm, tf), lambda f, m: (m, f)),
                   pl.BlockSpec((1, SUBLANE, tf), lambda f, m: (m // tps, 0, f))],
        out_shape=[jax.ShapeDtypeStruct((M, Fp), BF16), jax.ShapeDtypeStruct((B, SUBLANE, Fp), F32)],
        scratch_shapes=[pltpu.VMEM((SUBLANE + tm, tf), F32)],
        compiler_params=_cp(("parallel", "arbitrary")),
        name="ffn_up_seq")(hn, w_a, w_u, conv_w, conv_b, prefix)


def _ffn_up_tm_kernel(hn_ref, wa_ref, wu_ref, cw_ref, cb_ref, pre_ref, hm_ref, tail_ref, *, steps, nb):
    hn = hn_ref[...]
    a = _dot(hn, wa_ref[...])
    u = _dot(hn, wu_ref[...])
    slabs = [pre_ref[i] for i in range(CONV_W - 1)] + [a[t * nb:(t + 1) * nb] for t in range(steps)]
    for t in range(steps):
        c = (slabs[t] * cw_ref[0:1, :] + slabs[t + 1] * cw_ref[1:2, :] + slabs[t + 2] * cw_ref[2:3, :]
             + cb_ref[...])
        hm_ref[t * nb:(t + 1) * nb, :] = (jax.nn.gelu(c) * u[t * nb:(t + 1) * nb]).astype(hm_ref.dtype)
    for i in range(CONV_W - 1):
        tail_ref[i] = slabs[steps + i]


def ffn_up_tm(hn, w_a, w_u, conv_w, conv_b, prefix, steps, tf):
    M, Dm = hn.shape
    Fp = conv_w.shape[1]
    nf = Fp // tf
    nb = M // steps
    kern = functools.partial(_ffn_up_tm_kernel, steps=steps, nb=nb)
    return pl.pallas_call(
        kern,
        grid=(nf,),
        in_specs=[pl.BlockSpec((M, Dm), lambda f: (0, 0)),
                  pl.BlockSpec((Dm, tf), lambda f: (0, f)),
                  pl.BlockSpec((Dm, tf), lambda f: (0, f)),
                  pl.BlockSpec((SUBLANE, tf), lambda f: (0, f)),
                  pl.BlockSpec((1, tf), lambda f: (0, f)),
                  pl.BlockSpec((CONV_W - 1, nb, tf), lambda f: (0, 0, f))],
        out_specs=[pl.BlockSpec((M, tf), lambda f: (0, f)),
                   pl.BlockSpec((CONV_W - 1, nb, tf), lambda f: (0, 0, f))],
        out_shape=[jax.ShapeDtypeStruct((M, Fp), BF16), jax.ShapeDtypeStruct((CONV_W - 1, nb, Fp), F32)],
        compiler_params=_cp(("parallel",)),
        name="ffn_up_tm")(hn, w_a, w_u, conv_w, conv_b, prefix)


def _ffn_down_kernel(hm_ref, w_ref, h_ref, o_ref):
    o_ref[...] = h_ref[...] + _dot(hm_ref[...], w_ref[...])


def ffn_down(hmid, w_down, h1, tm, tn):
    M = hmid.shape[0]
    F, Dm = w_down.shape
    assert F % LANE == 0 and F <= hmid.shape[1]
    return pl.pallas_call(
        _ffn_down_kernel,
        grid=(M // tm, Dm // tn),
        in_specs=[pl.BlockSpec((tm, F), lambda i, j: (i, 0)),
                  pl.BlockSpec((F, tn), lambda i, j: (0, j)),
                  pl.BlockSpec((tm, tn), lambda i, j: (i, j))],
        out_specs=pl.BlockSpec((tm, tn), lambda i, j: (i, j)),
        out_shape=jax.ShapeDtypeStruct((M, Dm), F32),
        compiler_params=_cp(("parallel", "arbitrary")),
        name="ffn_down")(hmid, w_down, h1)


def _ple_final_kernel(h_ref, p_ref, wg_ref, bg_ref, wp_ref, g_ref, o_ref):
    h = h_ref[...]
    gate = jax.nn.sigmoid(_dot(h.astype(BF16), wg_ref[...]) + bg_ref[...])
    h = h + gate * _dot(p_ref[...].astype(BF16), wp_ref[...])
    o_ref[...] = _rms(h, g_ref[...])


def ple_final(h2, p2, w_gate, b_gate, w_proj, g, tm):
    M, Dm = h2.shape
    Pd = p2.shape[1]
    return pl.pallas_call(
        _ple_final_kernel,
        grid=(M // tm,),
        in_specs=[pl.BlockSpec((tm, Dm), lambda i: (i, 0)),
                  pl.BlockSpec((tm, Pd), lambda i: (i, 0)),
                  pl.BlockSpec((Dm, Dm), lambda i: (0, 0)),
                  pl.BlockSpec((1, Dm), lambda i: (0, 0)),
                  pl.BlockSpec((Pd, Dm), lambda i: (0, 0)),
                  pl.BlockSpec((1, Dm), lambda i: (0, 0))],
        out_specs=pl.BlockSpec((tm, Dm), lambda i: (i, 0)),
        out_shape=jax.ShapeDtypeStruct((M, Dm), F32),
        compiler_params=_cp(("parallel",)),
        name="ple_final")(h2, p2, w_gate, b_gate, w_proj, g)


def _pick_tile(n, prefs):
    for t in prefs:
        if n % t == 0:
            return t
    raise ValueError(f"no tile for {n}")


def _block_matrices(total_keys, nb):
    e = np.zeros((LANE, total_keys), np.float32)
    t = np.arange(total_keys)
    e[t // SLC_LEN, t] = 1.0
    ratio = SLC_LEN // CMP_STRIDE
    m = np.zeros((LANE, LANE), np.float32)
    for n in range(nb):
        for s in range(LANE):
            m[n, s] = float(ratio * s <= n <= ratio * s + ratio - 1) + float(ratio * s - 1 <= n <= ratio * s + ratio - 2)
    return jnp.asarray(e, BF16), jnp.asarray(m, BF16)


def _prep_weights(i, attn_norm_g, w_in, gla_w_a2, gla_b_a, gla_norm_g, cmp_wk1, cmp_wk2, cmp_wv1, cmp_wv2,
                  w_out, ffn_norm_g, ffn_w_up, ffn_conv_w, ffn_conv_b, ffn_w_down, ple_w_proj, ple_w_gate,
                  ple_b_gate):
    Dm = w_in.shape[1]
    F = ffn_w_down.shape[1]
    Fp = -(-F // 512) * 512
    win = w_in[i]
    ga0 = NQ0
    ng0 = ga0 + GLA_RANK + (NMAIN - NQ0)
    ngw = 3 * NSA_HEADS
    before_ga = lax.broadcasted_iota(jnp.int32, (1, NMAIN), 1) < ga0
    w_main = jnp.where(before_ga, win[:, :NMAIN], win[:, GLA_RANK:GLA_RANK + NMAIN]).astype(BF16)
    w_small = jnp.concatenate([win[:, ga0:ga0 + GLA_RANK], win[:, ng0:ng0 + ngw],
                               jnp.zeros((Dm, LANE - GLA_RANK - ngw), F32)], axis=1).astype(BF16)
    wa_pad = jnp.concatenate([gla_w_a2[i], jnp.zeros((LANE - GLA_RANK, GLA_HEADS * GLA_DK), F32)],
                             axis=0).astype(BF16)
    cat = lambda w1: jnp.concatenate([w1[:CMP_STRIDE], w1[CMP_STRIDE:]], axis=-1)
    w1cat = jnp.stack([cat(cmp_wk1[i]), cat(cmp_wv1[i])]).astype(BF16)
    w2s = jnp.stack([cmp_wk2[i], cmp_wv2[i]])
    wup = ffn_w_up[i]
    w_up_a = jnp.pad(wup[:, :F], ((0, 0), (0, Fp - F))).astype(BF16)
    w_up_u = jnp.pad(wup[:, F:], ((0, 0), (0, Fp - F))).astype(BF16)
    conv_w = jnp.pad(ffn_conv_w[i], ((0, SUBLANE - CONV_W), (0, Fp - F)))
    conv_b = jnp.pad(ffn_conv_b[i], (0, Fp - F)).reshape(1, Fp)
    w_down = ffn_w_down[i].astype(BF16)
    return dict(
        g_attn=attn_norm_g[i].reshape(1, Dm), w_main=w_main, w_small=w_small, wa_pad=wa_pad,
        b_a=gla_b_a[i].reshape(1, -1), gnorm=gla_norm_g[i].reshape(1, -1), w1cat=w1cat, w2s=w2s,
        w_out=w_out[i].astype(BF16), g_ffn=ffn_norm_g[i].reshape(1, Dm), w_up_a=w_up_a, w_up_u=w_up_u,
        conv_w=conv_w,
        conv_b=conv_b, w_down=w_down, w_proj=ple_w_proj[i].astype(BF16), w_gate=ple_w_gate[i].astype(BF16),
        b_gate=ple_b_gate[i].reshape(1, Dm), F=F, Fp=Fp)


def _kv_rows(kv, B, rows):
    t = kv.reshape(kv.shape[0], B, -1, NSA_KV, NSA_HD)[:, :, rows]
    return [t[i] for i in range(kv.shape[0])]


def _prompt_layer(h2d, p2d, B, L, W):
    Dm = h2d.shape[1]
    M = B * L
    proj, small, kv = norm_proj(h2d, W["g_attn"], W["w_main"], W["w_small"],
                                _pick_tile(M, (1024, 512, 256, 128)), 512)
    proj3 = proj.reshape(B, L, NMAIN)
    small3 = small.reshape(B, L, LANE)
    cb = _pick_tile(L, (128, 64, 32, 16))
    s0 = jnp.zeros((B, GLA_HEADS, GLA_DK, GLA_DV), F32)
    og, gla_state = gla(proj3, small3, W["wa_pad"], W["b_a"], W["gnorm"], s0, cb, GLA_SUB, cb,
                        _pick_tile(B, (4, 2, 1)))
    cmp = compress(proj3, W["w1cat"], W["w2s"])
    _, mmat = _block_matrices(L, L // CMP_STRIDE - 1)
    on = nsa_prompt(proj3, small3, cmp, mmat.T)
    h1, hn = out_proj(og.reshape(M, -1), on.reshape(M, -1), h2d, W["w_out"], W["g_ffn"],
                      _pick_tile(M, (512, 256, 128)))
    prefix = jnp.zeros((B, SUBLANE, W["Fp"]), F32)
    hmid, tail = ffn_up_seq(hn, W["w_up_a"], W["w_up_u"], W["conv_w"], W["conv_b"], prefix, L,
                            _pick_tile(L, (1024, 512, 256, 128)), 512)
    h2 = ffn_down(hmid, W["w_down"], h1, _pick_tile(M, (1024, 512, 256, 128)), 512)
    keep = min(WINDOW, L)
    outs = tuple(_kv_rows(kv[:4], B, slice(0, L)) + _kv_rows(kv[4:], B, slice(L - keep, L))
                 + [gla_state, tail[:, SUBLANE - (CONV_W - 1):, :W["F"]]])
    return h2, outs


def _decode_layer(h3d, p2d_tm, B, L, lp, caches, page_table, win_k, win_v, gla_s, conv_s, W):
    Dm = h3d.shape[2]
    Mp = B * lp
    x2 = h3d.reshape(Mp, Dm)
    proj, small, kv = norm_proj(x2, W["g_attn"], W["w_main"], W["w_small"],
                                _pick_tile(Mp, (1024, 512, 256, 128, 64, 32, 16, 8)), 512)
    proj3 = proj.reshape(B, lp, NMAIN)
    small3 = small.reshape(B, lp, LANE)
    og, gla_state = gla(proj3, small3, W["wa_pad"], W["b_a"], W["gnorm"], gla_s, lp, lp, L,
                        _pick_tile(B, (4, 2, 1)))
    plen = page_table.shape[1] * caches[0].shape[1] // NSA_KV
    emat, mmat = _block_matrices(plen + KCHUNK, (plen + L) // CMP_STRIDE - 1)
    wlen = win_k.shape[1]
    on, wk_new, wv_new = nsa_decode(proj3, small3, caches, page_table,
                                    win_k.reshape(B, wlen * NSA_KV, NSA_HD),
                                    win_v.reshape(B, wlen * NSA_KV, NSA_HD),
                                    W["w1cat"], W["w2s"], emat, mmat, L, _pick_tile(B, (2, 1)))
    h1, hn = out_proj(og.reshape(Mp, -1), on.reshape(Mp, -1), x2, W["w_out"], W["g_ffn"],
                      _pick_tile(Mp, (256, 128, 64, 32, 16, 8)))
    to_tm = lambda t: t.reshape(B, lp, Dm)[:, :L].transpose(1, 0, 2).reshape(L * B, Dm)
    h1_tm, hn_tm = to_tm(h1), to_tm(hn)
    prefix = jnp.pad(conv_s.transpose(1, 0, 2), ((0, 0), (0, 0), (0, W["Fp"] - W["F"])))
    hmid, tail = ffn_up_tm(hn_tm, W["w_up_a"], W["w_up_u"], W["conv_w"], W["conv_b"], prefix, L, 512)
    M = L * B
    h2 = ffn_down(hmid, W["w_down"], h1_tm, _pick_tile(M, (512, 256, 128, 64, 32, 16, 8)), 512)
    outs = tuple(_kv_rows(kv[:4], B, slice(0, L))
                 + [wk_new.reshape(win_k.shape), wv_new.reshape(win_v.shape),
                    gla_state, tail[:, :, :W["F"]].transpose(1, 0, 2)])
    return h2, outs


def kernel(x_prompt, x_sample, p_prompt, p_sample, cache_cmp_k, cache_cmp_v, cache_slc_k, cache_slc_v,
           page_table, state_win_k, state_win_v, state_gla, state_ffn_conv, attn_norm_g, w_in, gla_w_a2,
           gla_b_a, gla_norm_g, cmp_wk1, cmp_wk2, cmp_wv1, cmp_wv2, w_out, ffn_norm_g, ffn_w_up,
           ffn_conv_w, ffn_conv_b, ffn_w_down, ple_w_proj, ple_w_gate, ple_b_gate, final_norm_g):
    depth = w_in.shape[0]
    assert depth == 1, "layers are chained through HBM one at a time; only depth 1 is wired"
    Bp, Lp, Dm = x_prompt.shape
    Bs, Ls, _ = x_sample.shape
    W = _prep_weights(0, attn_norm_g, w_in, gla_w_a2, gla_b_a, gla_norm_g, cmp_wk1, cmp_wk2, cmp_wv1,
                      cmp_wv2, w_out, ffn_norm_g, ffn_w_up, ffn_conv_w, ffn_conv_b, ffn_w_down,
                      ple_w_proj, ple_w_gate, ple_b_gate)
    g_final = final_norm_g.reshape(1, Dm)

    Mp = Bp * Lp
    h2, outs_p = _prompt_layer(x_prompt.reshape(Mp, Dm), None, Bp, Lp, W)
    y_prompt = ple_final(h2, p_prompt[0].reshape(Mp, -1), W["w_gate"], W["b_gate"], W["w_proj"], g_final,
                         _pick_tile(Mp, (512, 256, 128))).reshape(Bp, Lp, Dm)

    lp = -(-Ls // SUBLANE) * SUBLANE
    xs = jnp.pad(x_sample, ((0, 0), (0, lp - Ls), (0, 0)))
    n_pool, psz = cache_cmp_k.shape[1], cache_cmp_k.shape[2]
    caches = [c[0].reshape(n_pool, psz * NSA_KV, NSA_HD)
              for c in (cache_cmp_k, cache_cmp_v, cache_slc_k, cache_slc_v)]
    h2s, outs_s = _decode_layer(xs, None, Bs, Ls, lp, caches, page_table, state_win_k[0], state_win_v[0],
                                state_gla[0], state_ffn_conv[0], W)
    p_tm = p_sample[0].transpose(1, 0, 2).reshape(Ls * Bs, -1)
    Ms = Ls * Bs
    y_tm = ple_final(h2s, p_tm, W["w_gate"], W["b_gate"], W["w_proj"], g_final,
                     _pick_tile(Ms, (256, 128, 64, 32, 16, 8)))
    y_sample = y_tm.reshape(Ls, Bs, Dm).transpose(1, 0, 2)

    lead = lambda t: t[None]
    return (y_prompt, y_sample) + tuple(lead(t) for t in outs_p) + tuple(lead(t) for t in outs_s)
```

```python
import functools

import numpy as np
import jax
import jax.numpy as jnp
from jax import lax
from jax.experimental import pallas as pl
from jax.experimental.pallas import tpu as pltpu

F32 = jnp.float32
BF16 = jnp.bfloat16

GLA_HEADS = 4
GLA_DK = 128
GLA_DV = 256
GLA_RANK = 16
GLA_TAU = 16.0
GLA_SUB = 16
NSA_HEADS = 8
NSA_KV = 2
NSA_REP = NSA_HEADS // NSA_KV
NSA_HD = 128
CMP_LEN = 32
CMP_STRIDE = 16
SLC_LEN = 64
N_SELECT = 16
WINDOW = 512
CONV_W = 3
EPS = 1e-6
NEG = -1e30
M_FLOOR = 0.1 * NEG
FORCED = 1e6

LANE = 128
SUBLANE = 8
KCHUNK = 128
CMP_SEG_ROWS = NSA_KV * CMP_STRIDE
CMP_PITCH = CMP_SEG_ROWS + SUBLANE
VMEM_LIMIT = 56 * 1024 * 1024

GQ0 = 0
GK0 = GQ0 + GLA_HEADS * GLA_DK
GV0 = GK0 + GLA_HEADS * GLA_DK
GR0 = GV0 + GLA_HEADS * GLA_DV
NQ0 = GR0 + GLA_HEADS * GLA_DV
KVW = NSA_KV * NSA_HD
KC0 = NQ0 + NSA_HEADS * NSA_HD
VC0 = KC0 + KVW
KS0 = VC0 + KVW
VS0 = KS0 + KVW
KW0 = VS0 + KVW
VW0 = KW0 + KVW
NMAIN = VW0 + KVW
GATE0 = GLA_RANK


def _cp(sem):
    return pltpu.CompilerParams(dimension_semantics=sem, vmem_limit_bytes=VMEM_LIMIT)


def _dot(a, b):
    return jnp.dot(a, b, preferred_element_type=F32)


def _dot_nt(a, b):
    return lax.dot_general(a, b, (((1,), (1,)), ((), ())), preferred_element_type=F32)


def _rms(x, g):
    return x * lax.rsqrt(jnp.mean(x * x, axis=-1, keepdims=True) + EPS) * g


def _norm_proj_kernel(x_ref, g_ref, wm_ref, ws_ref, om_ref, os_ref, kv_ref, xn_ref, *, j_kv0):
    j = pl.program_id(1)

    @pl.when(j == 0)
    def _():
        xn = _rms(x_ref[...], g_ref[...]).astype(BF16)
        xn_ref[...] = xn
        os_ref[...] = _dot(xn, ws_ref[...])

    tile = _dot(xn_ref[...], wm_ref[...])
    om_ref[...] = tile

    @pl.when(j >= j_kv0)
    def _():
        tm = tile.shape[0]
        for a in range(tile.shape[1] // KVW):
            for g in range(NSA_KV):
                c0 = a * KVW + g * NSA_HD
                kv_ref[a, pl.ds(g, tm, stride=NSA_KV), :] = tile[:, c0:c0 + NSA_HD]


def norm_proj(x2, g, w_main, w_small, tm, tn):
    M, Dm = x2.shape
    N = w_main.shape[1]
    assert KC0 % tn == 0 and tn % KVW == 0
    j_kv0 = KC0 // tn
    per_tile = tn // KVW
    n_kv = (N - KC0) // KVW
    return pl.pallas_call(
        functools.partial(_norm_proj_kernel, j_kv0=j_kv0),
        grid=(M // tm, N // tn),
        in_specs=[pl.BlockSpec((tm, Dm), lambda i, j: (i, 0)),
                  pl.BlockSpec((1, Dm), lambda i, j: (0, 0)),
                  pl.BlockSpec((Dm, tn), lambda i, j: (0, j)),
                  pl.BlockSpec((Dm, LANE), lambda i, j: (0, 0))],
        out_specs=[pl.BlockSpec((tm, tn), lambda i, j: (i, j)),
                   pl.BlockSpec((tm, LANE), lambda i, j: (i, 0)),
                   pl.BlockSpec((per_tile, NSA_KV * tm, NSA_HD),
                                lambda i, j: (jnp.maximum(j - j_kv0, 0), i, 0))],
        out_shape=[jax.ShapeDtypeStruct((M, N), F32), jax.ShapeDtypeStruct((M, LANE), F32),
                   jax.ShapeDtypeStruct((n_kv, NSA_KV * M, NSA_HD), F32)],
        scratch_shapes=[pltpu.VMEM((tm, Dm), BF16)],
        compiler_params=_cp(("parallel", "arbitrary")),
        name="norm_proj")(x2, g, w_main, w_small)


def _gla_kernel(q_ref, k_ref, v_ref, r_ref, sm_ref, wa_ref, ba_ref, gn_ref, s0_ref,
                o_ref, s_ref, st_ref, *, cb, sub, valid, nbb):
    c = pl.program_id(1)

    @pl.when(c == 0)
    def _():
        st_ref[...] = s0_ref[...]

    GH = GLA_HEADS
    H = nbb * GH
    side = lambda ref: ref[0] if nbb == 1 else jnp.concatenate([ref[s] for s in range(nbb)], axis=1)
    hk = lambda h: slice(h * GLA_DK, (h + 1) * GLA_DK)
    hv = lambda h: slice(h * GLA_DV, (h + 1) * GLA_DV)
    ri = lax.broadcasted_iota(jnp.int32, (cb, cb), 0)
    ci = lax.broadcasted_iota(jnp.int32, (cb, cb), 1)
    tri = jnp.where(ci <= ri, 1.0, 0.0)
    rowv = lax.broadcasted_iota(jnp.int32, (cb, 1), 0)
    q = side(q_ref) * (GLA_DK ** -0.5)
    k = side(k_ref)
    v = side(v_ref)
    pre = [_dot(sm_ref[s].astype(BF16), wa_ref[...]) + ba_ref[...] for s in range(nbb)]
    pre = pre[0] if nbb == 1 else jnp.concatenate(pre, axis=1)
    log_a = jax.nn.log_sigmoid(pre) / GLA_TAU
    if valid < cb:
        log_a = jnp.where(rowv < valid, log_a, 0.0)
        k = jnp.where(rowv < valid, k, 0.0)
        v = jnp.where(rowv < valid, v, 0.0)
    hi = log_a.astype(BF16).astype(F32)
    lo = (log_a - hi).astype(BF16).astype(F32)
    b = _dot(tri, hi) + _dot(tri, lo)
    b_last = b[cb - 1:cb, :]
    S = [st_ref[h // GH, h % GH] for h in range(H)]
    vb = v.astype(BF16)

    qe = (q * jnp.exp(b)).astype(BF16)
    o = [_dot(qe[:, hk(h)], S[h].astype(BF16)) for h in range(H)]
    a_rows = [[] for _ in range(H)]
    cc = lax.broadcasted_iota(jnp.int32, (sub, cb), 1)
    for blk in range(cb // sub):
        lo_r, hi_r = blk * sub, (blk + 1) * sub
        r_dec = jnp.zeros((1, H * GLA_DK), F32) if blk == 0 else b[lo_r - 1:lo_r, :]
        q_i = q[lo_r:hi_r] * jnp.exp(b[lo_r:hi_r] - r_dec)
        k_i = k * jnp.exp(jnp.where(rowv < hi_r, r_dec - b, 0.0))
        rr = lax.broadcasted_iota(jnp.int32, (sub, cb), 0) + lo_r
        for h in range(H):
            a = _dot_nt(q_i[:, hk(h)], k_i[:, hk(h)])
            a_rows[h].append(jnp.where(cc <= rr, a, 0.0))
    for h in range(H):
        a = a_rows[h][0] if len(a_rows[h]) == 1 else jnp.concatenate(a_rows[h], axis=0)
        o[h] = o[h] + _dot(a, v[:, hv(h)])

    kh = k * jnp.exp(b_last - b)
    dec = jnp.exp(b_last)
    if cb < LANE:
        kh = jnp.concatenate([kh, jnp.zeros((LANE - cb, H * GLA_DK), F32)], axis=0)
        vpad = jnp.concatenate([vb, jnp.zeros((LANE - cb, H * GLA_DV), BF16)], axis=0)
    else:
        vpad = vb
    for h in range(H):
        dec_col = jnp.transpose(jnp.broadcast_to(dec[:, hk(h)], (GLA_DK, GLA_DK)))[:, 0:1]
        s_new = dec_col * S[h] + _dot(jnp.transpose(kh[:, hk(h)]).astype(BF16), vpad[:, hv(h)])
        st_ref[h // GH, h % GH] = s_new
        s_ref[h // GH, h % GH] = s_new

    r = side(r_ref)
    gate = r * jax.nn.sigmoid(r)
    for h in range(H):
        o_ref[h // GH, :, hv(h % GH)] = (_rms(o[h], gn_ref[...]) * gate[:, hv(h)]).astype(o_ref.dtype)


def gla(proj3, small3, wa_pad, b_a, gnorm, s0, cb, sub, valid, nbb):
    B, L, _ = proj3.shape
    H = GLA_HEADS
    qw, vw = H * GLA_DK, H * GLA_DV
    kern = functools.partial(_gla_kernel, cb=cb, sub=sub, valid=valid, nbb=nbb)
    return pl.pallas_call(
        kern,
        grid=(B // nbb, L // cb),
        in_specs=[pl.BlockSpec((nbb, cb, qw), lambda b, c: (b, c, GQ0 // qw)),
                  pl.BlockSpec((nbb, cb, qw), lambda b, c: (b, c, GK0 // qw)),
                  pl.BlockSpec((nbb, cb, vw), lambda b, c: (b, c, GV0 // vw)),
                  pl.BlockSpec((nbb, cb, vw), lambda b, c: (b, c, GR0 // vw)),
                  pl.BlockSpec((nbb, cb, LANE), lambda b, c: (b, c, 0)),
                  pl.BlockSpec((LANE, qw), lambda b, c: (0, 0)),
                  pl.BlockSpec((1, qw), lambda b, c: (0, 0)),
                  pl.BlockSpec((1, GLA_DV), lambda b, c: (0, 0)),
                  pl.BlockSpec((nbb, H, GLA_DK, GLA_DV), lambda b, c: (b, 0, 0, 0))],
        out_specs=[pl.BlockSpec((nbb, cb, vw), lambda b, c: (b, c, 0)),
                   pl.BlockSpec((nbb, H, GLA_DK, GLA_DV), lambda b, c: (b, 0, 0, 0))],
        out_shape=[jax.ShapeDtypeStruct((B, L, vw), BF16),
                   jax.ShapeDtypeStruct((B, H, GLA_DK, GLA_DV), F32)],
        scratch_shapes=[pltpu.VMEM((nbb, H, GLA_DK, GLA_DV), F32)],
        compiler_params=_cp(("parallel", "arbitrary")),
        name="gla")(proj3, proj3, proj3, proj3, small3, wa_pad, b_a, gnorm, s0)


def _pair_rows(load_rows, j):
    return jnp.concatenate([load_rows(j), load_rows(j + 1)], axis=1).astype(BF16)


def _pair_weights(w1_ref, j):
    w = w1_ref[pl.ds(j, 2)]
    return w.reshape(2 * NSA_HD, 2 * NSA_HD)


def _compress_group(load_rows, w1_ref, w2, nseg):
    acc = jnp.zeros((nseg, 2 * NSA_HD), F32)
    for j in range(0, CMP_STRIDE, 2):
        acc = acc + _dot(_pair_rows(load_rows, j), _pair_weights(w1_ref, j))
    first = acc[:, :NSA_HD]
    second = acc[:, NSA_HD:]
    h = jax.nn.gelu(first + pltpu.roll(second, nseg - 1, axis=0))
    return _dot(h.astype(BF16), w2)


def _compress_groups(loads, w1_refs, w2s, nrow):
    n = len(loads)
    acc = [jnp.zeros((nrow, 2 * NSA_HD), F32) for _ in range(n)]
    for j in range(0, CMP_STRIDE, 2):
        for i in range(n):
            acc[i] = acc[i] + _dot(_pair_rows(loads[i], j), _pair_weights(w1_refs[i], j))
    hs = [jax.nn.gelu(a[:, :NSA_HD] + pltpu.roll(a[:, NSA_HD:], nrow - 1, axis=0)) for a in acc]
    return [_dot(h.astype(BF16), w2) for h, w2 in zip(hs, w2s)]


def _cmp_branch_many(q4s, cks, cvs, qpos4, nb):
    s = [_dot_nt(q4, ck.astype(BF16)) for q4, ck in zip(q4s, cks)]
    n = lax.broadcasted_iota(jnp.int32, s[0].shape, 1)
    mask = jnp.where(n * CMP_STRIDE + (CMP_LEN - 1) <= qpos4, n, nb) < nb
    s = [jnp.where(mask, si, NEG) for si in s]
    m = [jnp.max(si, axis=-1, keepdims=True) for si in s]
    e = [jnp.where(mask, jnp.exp(si - mi), 0.0) for si, mi in zip(s, m)]
    p = [ei / jnp.maximum(jnp.sum(ei, axis=-1, keepdims=True), 1e-30) for ei in e]
    o = [_dot(pi.astype(BF16), cv.astype(BF16)) for pi, cv in zip(p, cvs)]
    return list(zip(p, o))


def _select(pg, mmat, qpos_tok, nslc, nsel):
    hi = pg.astype(BF16)
    lo = (pg - hi.astype(F32)).astype(BF16)
    imp = _dot(hi, mmat) + _dot(lo, mmat)
    blk = lax.broadcasted_iota(jnp.int32, pg.shape, 1)
    cur = qpos_tok // SLC_LEN
    forced = jnp.where(blk == 0, 1, 0) + jnp.where(blk == cur, 1, 0) + jnp.where(blk == cur - 1, 1, 0)
    score = jnp.where(blk <= cur, jnp.where(forced > 0, FORCED, imp), -1.0)
    score = jnp.where(blk < nslc, score, -2.0)
    rank = jnp.zeros(pg.shape, F32)
    for j in range(nslc):
        sj = score[:, j:j + 1]
        tie = jnp.where(blk > j, sj, NEG)
        rank = rank + jnp.where(sj > score, 1.0, 0.0) + jnp.where(tie == score, 1.0, 0.0)
    return jnp.where(rank < nsel, jnp.where(blk < nslc, 1.0, 0.0), 0.0)


def _tile_rows(x, reps):
    return jnp.concatenate([x] * reps, axis=0)


def _attend_many(jobs):
    s = [[jnp.where(mk, _dot_nt(q4, k), NEG) for k, mk in zip(keys, masks)] for q4, keys, _, masks in jobs]
    m = []
    for sj in s:
        mj = sj[0].max(axis=-1, keepdims=True)
        for si in sj[1:]:
            mj = jnp.maximum(mj, si.max(axis=-1, keepdims=True))
        m.append(mj)
    e = [[jnp.where(mk, jnp.exp(si - mj), 0.0) for si, mk in zip(sj, job[3])]
         for sj, mj, job in zip(s, m, jobs)]
    l = [sum(ei.sum(axis=-1, keepdims=True) for ei in ej) for ej in e]
    o = [sum(_dot(ei.astype(BF16), v) for ei, v in zip(ej, job[2])) for ej, job in zip(e, jobs)]
    return [oj / jnp.maximum(lj, 1e-30) for oj, lj in zip(o, l)]


def _compress_kernel(x_ref, w1_ref, w2_ref, o_ref, *, nseg):
    load = lambda j: x_ref[0, pl.ds(j, nseg, stride=CMP_STRIDE), :]
    o_ref[0, 0, 0] = _compress_group(load, w1_ref.at[0], w2_ref[0].astype(BF16), nseg)


def compress(proj3, w1cat, w2s):
    B, L, _ = proj3.shape
    nseg = L // CMP_STRIDE
    G = NSA_KV
    return pl.pallas_call(
        functools.partial(_compress_kernel, nseg=nseg),
        grid=(B, 2, G),
        in_specs=[pl.BlockSpec((1, L, NSA_HD), lambda b, w, g: (b, 0, KC0 // NSA_HD + w * G + g)),
                  pl.BlockSpec((1, CMP_STRIDE, NSA_HD, 2 * NSA_HD), lambda b, w, g: (w, 0, 0, 0)),
                  pl.BlockSpec((1, NSA_HD, NSA_HD), lambda b, w, g: (w, 0, 0))],
        out_specs=pl.BlockSpec((1, 1, 1, nseg, NSA_HD), lambda b, w, g: (w, b, g, 0, 0)),
        out_shape=jax.ShapeDtypeStruct((2, B, G, nseg, NSA_HD), F32),
        compiler_params=_cp(("parallel", "parallel", "parallel")),
        name="compress")(proj3, w1cat, w2s)


def _select_t(pg_t, mm_t, qpos_row, nslc, nsel):
    nsp = -(-nslc // SUBLANE) * SUBLANE
    hi = pg_t.astype(BF16)
    lo = (pg_t - hi.astype(F32)).astype(BF16)
    imp = (_dot(mm_t, hi) + _dot(mm_t, lo))[:nsp]
    blk = lax.broadcasted_iota(jnp.int32, imp.shape, 0)
    cur = qpos_row // SLC_LEN
    forced = jnp.where(blk == 0, 1, 0) + jnp.where(blk == cur, 1, 0) + jnp.where(blk == cur - 1, 1, 0)
    score = jnp.where(blk <= cur, jnp.where(forced > 0, FORCED, imp), -1.0)
    score = jnp.where(blk < nslc, score, -2.0)
    rank = jnp.zeros(imp.shape, F32)
    for j in range(nslc):
        sj = score[j:j + 1, :]
        tie = jnp.where(blk > j, sj, NEG)
        rank = rank + jnp.where(sj > score, 1.0, 0.0) + jnp.where(tie == score, 1.0, 0.0)
    return jnp.where(rank < nsel, jnp.where(blk < nslc, 1.0, 0.0), 0.0)


def _flash_steps_t(q_t, jobs):
    s = [_dot(k, q_t) + bias for k, _, bias, _ in jobs]
    m_old = [st[0][...] for _, _, _, st in jobs]
    m_new = [jnp.maximum(mo, jnp.max(si, axis=0, keepdims=True)) for mo, si in zip(m_old, s)]
    alpha = [jnp.exp(mo - mn) for mo, mn in zip(m_old, m_new)]
    p = [jnp.exp(si - mn) for si, mn in zip(s, m_new)]
    pv = [_dot(v_t, pi.astype(BF16)) for (_, v_t, _, _), pi in zip(jobs, p)]
    for (_, _, _, (m_ref, l_ref, acc_ref)), mn, al, pi, pvi in zip(jobs, m_new, alpha, p, pv):
        l_ref[...] = al * l_ref[...] + jnp.sum(pi, axis=0, keepdims=True)
        acc_ref[...] = al * acc_ref[...] + pvi
        m_ref[...] = mn


def _nsa_prompt_kernel(q_ref, sm_ref, ck_ref, cv_ref, ks_ref, vs_ref, kw_ref, vw_ref, mm_ref,
                       o_ref, vst_ref, vwt_ref, sel_ref, m_ref, l_ref, acc_ref, mw_ref, lw_ref, accw_ref,
                       *, tq, kstep, nslc, nsel, nb):
    g = pl.program_id(1)
    qb = pl.program_id(2)
    R = NSA_REP
    T = R * tq
    L = ks_ref.shape[1]
    scale = NSA_HD ** -0.5

    @pl.when(qb == 0)
    def _():
        for c in range(L // KCHUNK):
            cs = slice(c * KCHUNK, (c + 1) * KCHUNK)
            vst_ref[:, cs] = jnp.transpose(vs_ref[0, cs, :]).astype(BF16)
            vwt_ref[:, cs] = jnp.transpose(vw_ref[0, cs, :]).astype(BF16)

    q_t = (jnp.concatenate([jnp.transpose(q_ref[0, :, r * NSA_HD:(r + 1) * NSA_HD]) for r in range(R)],
                           axis=1) * scale).astype(BF16)
    qpos_row = qb * tq + lax.broadcasted_iota(jnp.int32, (1, tq), 1)
    qpos4 = jnp.concatenate([qpos_row] * R, axis=1)

    key = lax.broadcasted_iota(jnp.int32, (kstep, T), 0)
    hi_step = (qb * tq + tq - 1) // kstep + 1
    lo_step = jnp.maximum(qb * tq - (WINDOW - 1), 0) // kstep
    last = hi_step - 1
    sel_state = (m_ref, l_ref, acc_ref)
    win_state = (mw_ref, lw_ref, accw_ref)

    def sel_step(c, causal):
        off = pl.multiple_of(c * kstep, kstep)
        k = ks_ref[0, pl.ds(off, kstep), :].astype(BF16)
        v_t = vst_ref[:, pl.ds(off, kstep)]
        blocks = [jnp.broadcast_to(sel_ref[pl.ds((kstep // SLC_LEN) * c + i, 1), :], (SLC_LEN, tq))
                  for i in range(kstep // SLC_LEN)]
        bias = jnp.concatenate(blocks, axis=0)
        bias = jnp.concatenate([bias] * R, axis=1)
        if causal:
            bias = jnp.where(off + key <= qpos4, bias, NEG)
        return k, v_t, bias, sel_state

    def win_step(c, causal):
        off = pl.multiple_of(c * kstep, kstep)
        k = kw_ref[0, pl.ds(off, kstep), :].astype(BF16)
        v_t = vwt_ref[:, pl.ds(off, kstep)]
        kpos = off + key
        if causal:
            bias = jnp.where(kpos <= qpos4, 0.0, NEG)
        else:
            bias = jnp.where(kpos > qpos4 - WINDOW, 0.0, NEG)
        return k, v_t, bias, win_state

    k_d, vt_d, bias_d, _ = win_step(last, True)
    s = _dot(ck_ref[0, 0, 0].astype(BF16), q_t)
    s_d = _dot(k_d, q_t) + bias_d
    n = lax.broadcasted_iota(jnp.int32, s.shape, 0)
    mask = jnp.where(n * CMP_STRIDE + (CMP_LEN - 1) <= qpos4, n, nb) < nb
    s = jnp.where(mask, s, NEG)
    m_c = jnp.max(s, axis=0, keepdims=True)
    m_d = jnp.maximum(jnp.max(s_d, axis=0, keepdims=True), M_FLOOR)
    e = jnp.where(mask, jnp.exp(s - m_c), 0.0)
    p_d = jnp.exp(s_d - m_d)
    p = e / jnp.maximum(jnp.sum(e, axis=0, keepdims=True), 1e-30)
    lw_ref[...] = jnp.sum(p_d, axis=0, keepdims=True)
    o_c = _dot(jnp.transpose(cv_ref[0, 0, 0]).astype(BF16), p.astype(BF16))
    accw_ref[...] = _dot(vt_d, p_d.astype(BF16))
    mw_ref[...] = m_d
    pg = p[:, 0:tq]
    for r in range(1, R):
        pg = pg + p[:, r * tq:(r + 1) * tq]
    sel_ref[...] = (_select_t(pg, mm_ref[...], qpos_row, nslc, nsel) - 1.0) * (-NEG)

    m_ref[...] = jnp.full(m_ref.shape, M_FLOOR, F32)
    l_ref[...] = jnp.zeros(l_ref.shape, F32)
    acc_ref[...] = jnp.zeros(acc_ref.shape, F32)

    def body_sel(c, carry):
        _flash_steps_t(q_t, [sel_step(c, False)])
        return carry

    def body_both(c, carry):
        _flash_steps_t(q_t, [sel_step(c, False), win_step(c, False)])
        return carry

    lax.fori_loop(0, jnp.minimum(lo_step, last), body_sel, 0)
    lax.fori_loop(lo_step, last, body_both, 0)
    _flash_steps_t(q_t, [sel_step(last, True)])
    o_s = acc_ref[...] / jnp.maximum(l_ref[...], 1e-30)
    o_w = accw_ref[...] / jnp.maximum(lw_ref[...], 1e-30)

    g_t = jnp.transpose(jax.nn.sigmoid(sm_ref[0]))
    for r in range(R):
        def gate(br, r=r):
            c0 = GATE0 + br * NSA_HEADS + r
            return jnp.where(g == 0, g_t[c0:c0 + 1, :], g_t[c0 + R:c0 + R + 1, :])
        cs = slice(r * tq, (r + 1) * tq)
        o = gate(0) * o_c[:, cs] + gate(1) * o_s[:, cs] + gate(2) * o_w[:, cs]
        o_ref[0, :, r * NSA_HD:(r + 1) * NSA_HD] = jnp.transpose(o).astype(o_ref.dtype)


def nsa_prompt(proj3, small3, cmp, mmat_t):
    B, L, _ = proj3.shape
    tq = _pick_tile(L, (2 * KCHUNK, KCHUNK))
    nseg = L // CMP_STRIDE
    nb = nseg - 1
    nslc = -(-L // SLC_LEN)
    nsel = min(N_SELECT, nslc)
    nsp = -(-nslc // SUBLANE) * SUBLANE
    G, R = NSA_KV, NSA_REP
    kstep = _pick_tile(L, (2 * KCHUNK, KCHUNK))
    assert nseg == LANE and L % KCHUNK == 0 and kstep % SLC_LEN == 0
    kv_spec = lambda c0: pl.BlockSpec((1, L, NSA_HD), lambda b, g, qb: (b, 0, c0 // NSA_HD + g))
    kern = functools.partial(_nsa_prompt_kernel, tq=tq, kstep=kstep, nslc=nslc, nsel=nsel, nb=nb)
    return pl.pallas_call(
        kern,
        grid=(B, G, L // tq),
        in_specs=[pl.BlockSpec((1, tq, R * NSA_HD), lambda b, g, qb: (b, qb, NQ0 // (R * NSA_HD) + g)),
                  pl.BlockSpec((1, tq, LANE), lambda b, g, qb: (b, qb, 0)),
                  pl.BlockSpec((1, 1, 1, nseg, NSA_HD), lambda b, g, qb: (0, b, g, 0, 0)),
                  pl.BlockSpec((1, 1, 1, nseg, NSA_HD), lambda b, g, qb: (1, b, g, 0, 0)),
                  kv_spec(KS0), kv_spec(VS0), kv_spec(KW0), kv_spec(VW0),
                  pl.BlockSpec((LANE, LANE), lambda b, g, qb: (0, 0))],
        out_specs=pl.BlockSpec((1, tq, R * NSA_HD), lambda b, g, qb: (b, qb, g)),
        out_shape=jax.ShapeDtypeStruct((B, L, NSA_HEADS * NSA_HD), BF16),
        scratch_shapes=[pltpu.VMEM((NSA_HD, L), BF16),
                        pltpu.VMEM((NSA_HD, L), BF16),
                        pltpu.VMEM((nsp, tq), F32)]
                       + 2 * [pltpu.VMEM((1, R * tq), F32),
                              pltpu.VMEM((1, R * tq), F32),
                              pltpu.VMEM((NSA_HD, R * tq), F32)],
        compiler_params=_cp(("parallel", "parallel", "arbitrary")),
        name="nsa_prompt")(proj3, small3, cmp, cmp, proj3, proj3, proj3, proj3, mmat_t)


def _nsa_decode_kernel(pt_ref, q_ref, sm_ref, ksn_ref, vsn_ref, kwn_ref, vwn_ref, wk_ref, wv_ref,
                       w1_ref, w2_ref, e_ref, mm_ref, c0_hbm, c1_hbm, c2_hbm, c3_hbm,
                       o_ref, wko_ref, wvo_ref, cbuf_ref, sbuf_ref, sem_ref,
                       *, ns, lp, lreal, plen, rows_pp, n_pages, wlen, nslc, nsel, nb):
    b = pl.program_id(0)
    slot = b % 2
    caches = (c0_hbm, c1_hbm, c2_hbm, c3_hbm)
    G = NSA_KV
    R = NSA_REP
    T = R * lp
    segs_pp = rows_pp // CMP_SEG_ROWS

    def copies(step, sl):
        out = []
        for q in range(ns):
            for p in range(n_pages):
                page = pt_ref[step * ns + q, p]
                for w in range(2):
                    for s in range(segs_pp):
                        out.append(pltpu.make_async_copy(
                            caches[w].at[page, pl.ds(s * CMP_SEG_ROWS, CMP_SEG_ROWS)],
                            cbuf_ref.at[sl, q, w, pl.ds((p * segs_pp + s) * CMP_PITCH, CMP_SEG_ROWS)],
                            sem_ref.at[sl, w]))
                    out.append(pltpu.make_async_copy(
                        caches[2 + w].at[page], sbuf_ref.at[sl, q, w, pl.ds(p * rows_pp, rows_pp)],
                        sem_ref.at[sl, 2 + w]))
        return out

    @pl.when(b == 0)
    def _():
        for cp in copies(0, 0):
            cp.start()

    @pl.when(b + 1 < pl.num_programs(0))
    def _():
        for cp in copies(b + 1, 1 - slot):
            cp.start()

    for w in range(len(caches)):
        for q in range(ns):
            whole = sbuf_ref.at[slot, q, 0]
            pltpu.make_async_copy(whole, whole, sem_ref.at[slot, w]).wait()

    scale = NSA_HD ** -0.5
    nseg = plen // CMP_STRIDE
    tok = lax.broadcasted_iota(jnp.int32, (lp, 1), 0)
    qpos_tok = plen + tok
    qpos4 = _tile_rows(qpos_tok, R)
    lane_n = lax.broadcasted_iota(jnp.int32, (T, KCHUNK), 1)
    new_pos = jnp.where(lane_n < lreal, plen + lane_n, qpos4 + 1)
    past_pos = lax.broadcasted_iota(jnp.int32, (T, plen), 1)
    win_pos = plen - wlen + lax.broadcasted_iota(jnp.int32, (T, wlen), 1)
    win_pos = jnp.where(win_pos > qpos4 - WINDOW, win_pos, qpos4 + 1)
    win_pos = jnp.where(win_pos >= 0, win_pos, qpos4 + 1)
    new_win_pos = jnp.where(new_pos > qpos4 - WINDOW, new_pos, qpos4 + 1)
    zpad_f = jnp.zeros((KCHUNK - lp, NSA_HD), F32)
    pad_new = lambda ref, q, ls: jnp.concatenate([ref[q, :, ls], zpad_f], axis=0).astype(BF16)
    gls = lambda g: slice(g * NSA_HD, (g + 1) * NSA_HD)
    pairs = [(q, g) for q in range(ns) for g in range(G)]
    q4 = [(jnp.concatenate([q_ref[q, :, (g * R + r) * NSA_HD:(g * R + r + 1) * NSA_HD]
                            for r in range(R)], axis=0) * scale).astype(BF16) for q, g in pairs]

    def seg_rows(w, j):
        return jnp.concatenate([cbuf_ref[slot, q, w, pl.ds(G * j + g, nseg, stride=CMP_PITCH), :]
                                for q, g in pairs], axis=0)
    cmp = _compress_groups([functools.partial(seg_rows, w) for w in range(2)],
                           [w1_ref.at[w] for w in range(2)],
                           [w2_ref[w].astype(BF16) for w in range(2)], len(pairs) * nseg)
    prow = lambda i: slice(i * nseg, (i + 1) * nseg)
    branch = _cmp_branch_many(q4, [cmp[0][prow(i)] for i in range(len(pairs))],
                              [cmp[1][prow(i)] for i in range(len(pairs))], qpos4, nb)
    pgs = []
    for i in range(len(pairs)):
        p = branch[i][0]
        pg = p[0:lp]
        for r in range(1, R):
            pg = pg + p[r * lp:(r + 1) * lp]
        pgs.append(pg)
    sel = _select(jnp.concatenate(pgs, axis=0), mm_ref[...], _tile_rows(qpos_tok, len(pairs)), nslc, nsel)
    selm_all = _dot(sel.astype(BF16), e_ref[...])

    sel_jobs, win_jobs = [], []
    for i, (q, g) in enumerate(pairs):
        selm = _tile_rows(selm_all[i * lp:(i + 1) * lp], R)
        k_past = sbuf_ref[slot, q, 0, pl.ds(g, plen, stride=G), :].astype(BF16)
        v_past = sbuf_ref[slot, q, 1, pl.ds(g, plen, stride=G), :].astype(BF16)
        m_past = jnp.where(selm[:, :plen] > 0.5, past_pos, qpos4 + 1) <= qpos4
        m_new = jnp.where(selm[:, plen:] > 0.5, new_pos, qpos4 + 1) <= qpos4
        sel_jobs.append((q4[i], [k_past, pad_new(ksn_ref, q, gls(g))],
                         [v_past, pad_new(vsn_ref, q, gls(g))], [m_past, m_new]))
        k_win = wk_ref[q, pl.ds(g, wlen, stride=G), :].astype(BF16)
        v_win = wv_ref[q, pl.ds(g, wlen, stride=G), :].astype(BF16)
        win_jobs.append((q4[i], [k_win, pad_new(kwn_ref, q, gls(g))],
                         [v_win, pad_new(vwn_ref, q, gls(g))], [win_pos <= qpos4, new_win_pos <= qpos4]))
    outs = _attend_many(sel_jobs + win_jobs)
    o_s, o_w = outs[:len(pairs)], outs[len(pairs):]

    for i, (q, g) in enumerate(pairs):
        gts = jax.nn.sigmoid(sm_ref[q])
        for r in range(R):
            c0 = GATE0 + g * R + r
            rs = slice(r * lp, (r + 1) * lp)
            o = (gts[:, c0:c0 + 1] * branch[i][1][rs]
                 + gts[:, c0 + NSA_HEADS:c0 + NSA_HEADS + 1] * o_s[i][rs]
                 + gts[:, c0 + 2 * NSA_HEADS:c0 + 2 * NSA_HEADS + 1] * o_w[i][rs])
            hq = (g * R + r) * NSA_HD
            o_ref[q, :, hq:hq + NSA_HD] = o.astype(o_ref.dtype)

    wrows = wlen * G
    shift = lreal * G
    for q in range(ns):
        for src, new, dst in ((wk_ref, kwn_ref, wko_ref), (wv_ref, vwn_ref, wvo_ref)):
            dst[q, 0:wrows - shift, :] = src[q, shift:wrows, :]
            for t in range(lreal):
                for g in range(G):
                    row = wrows - shift + t * G + g
                    dst[q, row:row + 1, :] = new[q, t:t + 1, g * NSA_HD:(g + 1) * NSA_HD]


def nsa_decode(proj3, small3, caches, page_table, win_k, win_v, w1cat, w2s, emat, mmat, lreal, ns):
    B, lp, _ = proj3.shape
    G = NSA_KV
    n_pages = page_table.shape[1]
    rows_pp = caches[0].shape[1]
    plen = n_pages * rows_pp // G
    wlen = win_k.shape[1] // G
    assert plen % CMP_STRIDE == 0 and lreal < CMP_STRIDE and lreal <= lp and B % ns == 0
    nb = (plen + lreal) // CMP_STRIDE - 1
    nslc = -(-(plen + lreal) // SLC_LEN)
    nsel = min(N_SELECT, nslc)
    new_spec = lambda c0: pl.BlockSpec((ns, lp, KVW), lambda b, pt: (b, 0, c0 // KVW))
    const = lambda shape: pl.BlockSpec(shape, lambda b, pt: (0,) * len(shape))
    hbm = pl.BlockSpec(memory_space=pl.ANY)
    kern = functools.partial(_nsa_decode_kernel, ns=ns, lp=lp, lreal=lreal, plen=plen, rows_pp=rows_pp,
                             n_pages=n_pages, wlen=wlen, nslc=nslc, nsel=nsel, nb=nb)
    grid_spec = pltpu.PrefetchScalarGridSpec(
        num_scalar_prefetch=1,
        grid=(B // ns,),
        in_specs=[pl.BlockSpec((ns, lp, NSA_HEADS * NSA_HD), lambda b, pt: (b, 0, NQ0 // (NSA_HEADS * NSA_HD))),
                  pl.BlockSpec((ns, lp, LANE), lambda b, pt: (b, 0, 0)),
                  new_spec(KS0), new_spec(VS0), new_spec(KW0), new_spec(VW0),
                  pl.BlockSpec((ns, wlen * G, NSA_HD), lambda b, pt: (b, 0, 0)),
                  pl.BlockSpec((ns, wlen * G, NSA_HD), lambda b, pt: (b, 0, 0)),
                  const((2, CMP_STRIDE, NSA_HD, 2 * NSA_HD)),
                  const((2, NSA_HD, NSA_HD)),
                  const((LANE, plen + KCHUNK)),
                  const((LANE, LANE)),
                  hbm, hbm, hbm, hbm],
        out_specs=[pl.BlockSpec((ns, lp, NSA_HEADS * NSA_HD), lambda b, pt: (b, 0, 0)),
                   pl.BlockSpec((ns, wlen * G, NSA_HD), lambda b, pt: (b, 0, 0)),
                   pl.BlockSpec((ns, wlen * G, NSA_HD), lambda b, pt: (b, 0, 0))],
        scratch_shapes=[pltpu.VMEM((2, ns, 2, n_pages * rows_pp // CMP_SEG_ROWS * CMP_PITCH, NSA_HD), F32),
                        pltpu.VMEM((2, ns, 2, n_pages * rows_pp, NSA_HD), F32),
                        pltpu.SemaphoreType.DMA((2, len(caches)))])
    return pl.pallas_call(
        kern,
        grid_spec=grid_spec,
        out_shape=[jax.ShapeDtypeStruct((B, lp, NSA_HEADS * NSA_HD), BF16),
                   jax.ShapeDtypeStruct(win_k.shape, F32), jax.ShapeDtypeStruct(win_v.shape, F32)],
        compiler_params=_cp(("arbitrary",)),
        name="nsa_decode")(page_table, proj3, small3, proj3, proj3, proj3, proj3, win_k, win_v,
                           w1cat, w2s, emat, mmat, caches[0], caches[1], caches[2], caches[3])


def _out_proj_kernel(og_ref, on_ref, x_ref, w0_ref, w1_ref, g_ref, h_ref, hn_ref):
    h = x_ref[...] + _dot(og_ref[...], w0_ref[...]) + _dot(on_ref[...], w1_ref[...])
    h_ref[...] = h
    hn_ref[...] = _rms(h, g_ref[...]).astype(BF16)


def out_proj(og, on, x2, w_out, g, tm):
    M, Dm = x2.shape
    Kh = og.shape[1]
    return pl.pallas_call(
        _out_proj_kernel,
        grid=(M // tm,),
        in_specs=[pl.BlockSpec((tm, Kh), lambda i: (i, 0)),
                  pl.BlockSpec((tm, Kh), lambda i: (i, 0)),
                  pl.BlockSpec((tm, Dm), lambda i: (i, 0)),
                  pl.BlockSpec((Kh, Dm), lambda i: (0, 0)),
                  pl.BlockSpec((Kh, Dm), lambda i: (1, 0)),
                  pl.BlockSpec((1, Dm), lambda i: (0, 0))],
        out_specs=[pl.BlockSpec((tm, Dm), lambda i: (i, 0)),
                   pl.BlockSpec((tm, Dm), lambda i: (i, 0))],
        out_shape=[jax.ShapeDtypeStruct((M, Dm), F32), jax.ShapeDtypeStruct((M, Dm), BF16)],
        compiler_params=_cp(("parallel",)),
        name="out_proj")(og, on, x2, w_out, w_out, g)


def _ffn_up_seq_kernel(hn_ref, wa_ref, wu_ref, cw_ref, cb_ref, pre_ref, hm_ref, tail_ref, aext_ref,
                       *, tm, tiles_per_seq):
    m = pl.program_id(1)

    @pl.when(m % tiles_per_seq == 0)
    def _():
        aext_ref[0:SUBLANE, :] = pre_ref[0]

    hn = hn_ref[...]
    a = _dot(hn, wa_ref[...])
    u = _dot(hn, wu_ref[...])
    aext_ref[SUBLANE:SUBLANE + tm, :] = a
    p1 = aext_ref[pl.ds(SUBLANE - 1, tm), :]
    p2 = aext_ref[pl.ds(SUBLANE - 2, tm), :]
    c = p2 * cw_ref[0:1, :] + p1 * cw_ref[1:2, :] + a * cw_ref[2:3, :] + cb_ref[...]
    hm_ref[...] = (jax.nn.gelu(c) * u).astype(hm_ref.dtype)
    tail = a[tm - SUBLANE:tm, :]
    tail_ref[0] = tail
    aext_ref[0:SUBLANE, :] = tail


def ffn_up_seq(hn, w_a, w_u, conv_w, conv_b, prefix, seq_len, tm, tf):
    M, Dm = hn.shape
    Fp = conv_w.shape[1]
    nf = Fp // tf
    B = M // seq_len
    tps = seq_len // tm
    kern = functools.partial(_ffn_up_seq_kernel, tm=tm, tiles_per_seq=tps)
    return pl.pallas_call(
        kern,
        grid=(nf, M // tm),
        in_specs=[pl.BlockSpec((tm, Dm), lambda f, m: (m, 0)),
                  pl.BlockSpec((Dm, tf), lambda f, m: (0, f)),
                  pl.BlockSpec((Dm, tf), lambda f, m: (0, f)),
                  pl.BlockSpec((SUBLANE, tf), lambda f, m: (0, f)),
                  pl.BlockSpec((1, tf), lambda f, m: (0, f)),
                  pl.BlockSpec((1, SUBLANE, tf), lambda f, m: (m // tps, 0, f))],
        out_specs=[pl.BlockSpec((tm, tf), lambda f, m: (m, f)),
                   pl.BlockSpec((1, SUBLANE, tf), lambda f, m: (m // tps, 0, f))],
        out_shape=[jax.ShapeDtypeStruct((M, Fp), BF16), jax.ShapeDtypeStruct((B, SUBLANE, Fp), F32)],
        scratch_shapes=[pltpu.VMEM((SUBLANE + tm, tf), F32)],
        compiler_params=_cp(("parallel", "arbitrary")),
        name="ffn_up_seq")(hn, w_a, w_u, conv_w, conv_b, prefix)


def _ffn_up_tm_kernel(hn_ref, wa_ref, wu_ref, cw_ref, cb_ref, pre_ref, hm_ref, tail_ref, *, steps, nb):
    hn = hn_ref[...]
    a = _dot(hn, wa_ref[...])
    u = _dot(hn, wu_ref[...])
    slabs = [pre_ref[i] for i in range(CONV_W - 1)] + [a[t * nb:(t + 1) * nb] for t in range(steps)]
    for t in range(steps):
        c = (slabs[t] * cw_ref[0:1, :] + slabs[t + 1] * cw_ref[1:2, :] + slabs[t + 2] * cw_ref[2:3, :]
             + cb_ref[...])
        hm_ref[t * nb:(t + 1) * nb, :] = (jax.nn.gelu(c) * u[t * nb:(t + 1) * nb]).astype(hm_ref.dtype)
    for i in range(CONV_W - 1):
        tail_ref[i] = slabs[steps + i]


def ffn_up_tm(hn, w_a, w_u, conv_w, conv_b, prefix, steps, tf):
    M, Dm = hn.shape
    Fp = conv_w.shape[1]
    nf = Fp // tf
    nb = M // steps
    kern = functools.partial(_ffn_up_tm_kernel, steps=steps, nb=nb)
    return pl.pallas_call(
        kern,
        grid=(nf,),
        in_specs=[pl.BlockSpec((M, Dm), lambda f: (0, 0)),
                  pl.BlockSpec((Dm, tf), lambda f: (0, f)),
                  pl.BlockSpec((Dm, tf), lambda f: (0, f)),
                  pl.BlockSpec((SUBLANE, tf), lambda f: (0, f)),
                  pl.BlockSpec((1, tf), lambda f: (0, f)),
                  pl.BlockSpec((CONV_W - 1, nb, tf), lambda f: (0, 0, f))],
        out_specs=[pl.BlockSpec((M, tf), lambda f: (0, f)),
                   pl.BlockSpec((CONV_W - 1, nb, tf), lambda f: (0, 0, f))],
        out_shape=[jax.ShapeDtypeStruct((M, Fp), BF16), jax.ShapeDtypeStruct((CONV_W - 1, nb, Fp), F32)],
        compiler_params=_cp(("parallel",)),
        name="ffn_up_tm")(hn, w_a, w_u, conv_w, conv_b, prefix)


def _ffn_down_kernel(hm_ref, w_ref, h_ref, o_ref):
    o_ref[...] = h_ref[...] + _dot(hm_ref[...], w_ref[...])


def ffn_down(hmid, w_down, h1, tm, tn):
    M = hmid.shape[0]
    F, Dm = w_down.shape
    assert F % LANE == 0 and F <= hmid.shape[1]
    return pl.pallas_call(
        _ffn_down_kernel,
        grid=(M // tm, Dm // tn),
        in_specs=[pl.BlockSpec((tm, F), lambda i, j: (i, 0)),
                  pl.BlockSpec((F, tn), lambda i, j: (0, j)),
                  pl.BlockSpec((tm, tn), lambda i, j: (i, j))],
        out_specs=pl.BlockSpec((tm, tn), lambda i, j: (i, j)),
        out_shape=jax.ShapeDtypeStruct((M, Dm), F32),
        compiler_params=_cp(("parallel", "arbitrary")),
        name="ffn_down")(hmid, w_down, h1)


def _ple_final_kernel(h_ref, p_ref, wg_ref, bg_ref, wp_ref, g_ref, o_ref):
    h = h_ref[...]
    gate = jax.nn.sigmoid(_dot(h.astype(BF16), wg_ref[...]) + bg_ref[...])
    h = h + gate * _dot(p_ref[...].astype(BF16), wp_ref[...])
    o_ref[...] = _rms(h, g_ref[...])


def ple_final(h2, p2, w_gate, b_gate, w_proj, g, tm):
    M, Dm = h2.shape
    Pd = p2.shape[1]
    return pl.pallas_call(
        _ple_final_kernel,
        grid=(M // tm,),
        in_specs=[pl.BlockSpec((tm, Dm), lambda i: (i, 0)),
                  pl.BlockSpec((tm, Pd), lambda i: (i, 0)),
                  pl.BlockSpec((Dm, Dm), lambda i: (0, 0)),
                  pl.BlockSpec((1, Dm), lambda i: (0, 0)),
                  pl.BlockSpec((Pd, Dm), lambda i: (0, 0)),
                  pl.BlockSpec((1, Dm), lambda i: (0, 0))],
        out_specs=pl.BlockSpec((tm, Dm), lambda i: (i, 0)),
        out_shape=jax.ShapeDtypeStruct((M, Dm), F32),
        compiler_params=_cp(("parallel",)),
        name="ple_final")(h2, p2, w_gate, b_gate, w_proj, g)


def _pick_tile(n, prefs):
    for t in prefs:
        if n % t == 0:
            return t
    raise ValueError(f"no tile for {n}")


def _block_matrices(total_keys, nb):
    e = np.zeros((LANE, total_keys), np.float32)
    t = np.arange(total_keys)
    e[t // SLC_LEN, t] = 1.0
    ratio = SLC_LEN // CMP_STRIDE
    m = np.zeros((LANE, LANE), np.float32)
    for n in range(nb):
        for s in range(LANE):
            m[n, s] = float(ratio * s <= n <= ratio * s + ratio - 1) + float(ratio * s - 1 <= n <= ratio * s + ratio - 2)
    return jnp.asarray(e, BF16), jnp.asarray(m, BF16)


def _prep_weights(i, attn_norm_g, w_in, gla_w_a2, gla_b_a, gla_norm_g, cmp_wk1, cmp_wk2, cmp_wv1, cmp_wv2,
                  w_out, ffn_norm_g, ffn_w_up, ffn_conv_w, ffn_conv_b, ffn_w_down, ple_w_proj, ple_w_gate,
                  ple_b_gate):
    Dm = w_in.shape[1]
    F = ffn_w_down.shape[1]
    Fp = -(-F // 512) * 512
    win = w_in[i]
    ga0 = NQ0
    ng0 = ga0 + GLA_RANK + (NMAIN - NQ0)
    ngw = 3 * NSA_HEADS
    before_ga = lax.broadcasted_iota(jnp.int32, (1, NMAIN), 1) < ga0
    w_main = jnp.where(before_ga, win[:, :NMAIN], win[:, GLA_RANK:GLA_RANK + NMAIN]).astype(BF16)
    w_small = jnp.concatenate([win[:, ga0:ga0 + GLA_RANK], win[:, ng0:ng0 + ngw],
                               jnp.zeros((Dm, LANE - GLA_RANK - ngw), F32)], axis=1).astype(BF16)
    wa_pad = jnp.concatenate([gla_w_a2[i], jnp.zeros((LANE - GLA_RANK, GLA_HEADS * GLA_DK), F32)],
                             axis=0).astype(BF16)
    cat = lambda w1: jnp.concatenate([w1[:CMP_STRIDE], w1[CMP_STRIDE:]], axis=-1)
    w1cat = jnp.stack([cat(cmp_wk1[i]), cat(cmp_wv1[i])]).astype(BF16)
    w2s = jnp.stack([cmp_wk2[i], cmp_wv2[i]])
    wup = ffn_w_up[i]
    w_up_a = jnp.pad(wup[:, :F], ((0, 0), (0, Fp - F))).astype(BF16)
    w_up_u = jnp.pad(wup[:, F:], ((0, 0), (0, Fp - F))).astype(BF16)
    conv_w = jnp.pad(ffn_conv_w[i], ((0, SUBLANE - CONV_W), (0, Fp - F)))
    conv_b = jnp.pad(ffn_conv_b[i], (0, Fp - F)).reshape(1, Fp)
    w_down = ffn_w_down[i].astype(BF16)
    return dict(
        g_attn=attn_norm_g[i].reshape(1, Dm), w_main=w_main, w_small=w_small, wa_pad=wa_pad,
        b_a=gla_b_a[i].reshape(1, -1), gnorm=gla_norm_g[i].reshape(1, -1), w1cat=w1cat, w2s=w2s,
        w_out=w_out[i].astype(BF16), g_ffn=ffn_norm_g[i].reshape(1, Dm), w_up_a=w_up_a, w_up_u=w_up_u,
        conv_w=conv_w,
        conv_b=conv_b, w_down=w_down, w_proj=ple_w_proj[i].astype(BF16), w_gate=ple_w_gate[i].astype(BF16),
        b_gate=ple_b_gate[i].reshape(1, Dm), F=F, Fp=Fp)


def _kv_rows(kv, B, rows):
    t = kv.reshape(kv.shape[0], B, -1, NSA_KV, NSA_HD)[:, :, rows]
    return [t[i] for i in range(kv.shape[0])]


def _prompt_layer(h2d, B, L, W):
    Dm = h2d.shape[1]
    M = B * L
    proj, small, kv = norm_proj(h2d, W["g_attn"], W["w_main"], W["w_small"],
                                _pick_tile(M, (1024, 512, 256, 128)), 512)
    proj3 = proj.reshape(B, L, NMAIN)
    small3 = small.reshape(B, L, LANE)
    cb = _pick_tile(L, (128, 64, 32, 16))
    s0 = jnp.zeros((B, GLA_HEADS, GLA_DK, GLA_DV), F32)
    og, gla_state = gla(proj3, small3, W["wa_pad"], W["b_a"], W["gnorm"], s0, cb, GLA_SUB, cb,
                        _pick_tile(B, (4, 2, 1)))
    cmp = compress(proj3, W["w1cat"], W["w2s"])
    _, mmat = _block_matrices(L, L // CMP_STRIDE - 1)
    on = nsa_prompt(proj3, small3, cmp, mmat.T)
    h1, hn = out_proj(og.reshape(M, -1), on.reshape(M, -1), h2d, W["w_out"], W["g_ffn"],
                      _pick_tile(M, (512, 256, 128)))
    prefix = jnp.zeros((B, SUBLANE, W["Fp"]), F32)
    hmid, tail = ffn_up_seq(hn, W["w_up_a"], W["w_up_u"], W["conv_w"], W["conv_b"], prefix, L,
                            _pick_tile(L, (1024, 512, 256, 128)), 512)
    h2 = ffn_down(hmid, W["w_down"], h1, _pick_tile(M, (1024, 512, 256, 128)), 512)
    keep = min(WINDOW, L)
    outs = tuple(_kv_rows(kv[:4], B, slice(0, L)) + _kv_rows(kv[4:], B, slice(L - keep, L))
                 + [gla_state, tail[:, SUBLANE - (CONV_W - 1):, :W["F"]]])
    return h2, outs


def _decode_layer(h3d, B, L, lp, caches, page_table, win_k, win_v, gla_s, conv_s, W):
    Dm = h3d.shape[2]
    Mp = B * lp
    x2 = h3d.reshape(Mp, Dm)
    proj, small, kv = norm_proj(x2, W["g_attn"], W["w_main"], W["w_small"],
                                _pick_tile(Mp, (1024, 512, 256, 128, 64, 32, 16, 8)), 512)
    proj3 = proj.reshape(B, lp, NMAIN)
    small3 = small.reshape(B, lp, LANE)
    og, gla_state = gla(proj3, small3, W["wa_pad"], W["b_a"], W["gnorm"], gla_s, lp, lp, L,
                        _pick_tile(B, (4, 2, 1)))
    plen = page_table.shape[1] * caches[0].shape[1] // NSA_KV
    emat, mmat = _block_matrices(plen + KCHUNK, (plen + L) // CMP_STRIDE - 1)
    wlen = win_k.shape[1]
    on, wk_new, wv_new = nsa_decode(proj3, small3, caches, page_table,
                                    win_k.reshape(B, wlen * NSA_KV, NSA_HD),
                                    win_v.reshape(B, wlen * NSA_KV, NSA_HD),
                                    W["w1cat"], W["w2s"], emat, mmat, L, _pick_tile(B, (2, 1)))
    h1, hn = out_proj(og.reshape(Mp, -1), on.reshape(Mp, -1), x2, W["w_out"], W["g_ffn"],
                      _pick_tile(Mp, (256, 128, 64, 32, 16, 8)))
    to_tm = lambda t: t.reshape(B, lp, Dm)[:, :L].transpose(1, 0, 2).reshape(L * B, Dm)
    h1_tm, hn_tm = to_tm(h1), to_tm(hn)
    prefix = jnp.pad(conv_s.transpose(1, 0, 2), ((0, 0), (0, 0), (0, W["Fp"] - W["F"])))
    hmid, tail = ffn_up_tm(hn_tm, W["w_up_a"], W["w_up_u"], W["conv_w"], W["conv_b"], prefix, L, 512)
    M = L * B
    h2 = ffn_down(hmid, W["w_down"], h1_tm, _pick_tile(M, (512, 256, 128, 64, 32, 16, 8)), 512)
    outs = tuple(_kv_rows(kv[:4], B, slice(0, L))
                 + [wk_new.reshape(win_k.shape), wv_new.reshape(win_v.shape),
                    gla_state, tail[:, :, :W["F"]].transpose(1, 0, 2)])
    return h2, outs


def kernel(x_prompt, x_sample, p_prompt, p_sample, cache_cmp_k, cache_cmp_v, cache_slc_k, cache_slc_v,
           page_table, state_win_k, state_win_v, state_gla, state_ffn_conv, attn_norm_g, w_in, gla_w_a2,
           gla_b_a, gla_norm_g, cmp_wk1, cmp_wk2, cmp_wv1, cmp_wv2, w_out, ffn_norm_g, ffn_w_up,
           ffn_conv_w, ffn_conv_b, ffn_w_down, ple_w_proj, ple_w_gate, ple_b_gate, final_norm_g):
    depth = w_in.shape[0]
    assert depth == 1, "layers are chained through HBM one at a time; only depth 1 is wired"
    Bp, Lp, Dm = x_prompt.shape
    Bs, Ls, _ = x_sample.shape
    W = _prep_weights(0, attn_norm_g, w_in, gla_w_a2, gla_b_a, gla_norm_g, cmp_wk1, cmp_wk2, cmp_wv1,
                      cmp_wv2, w_out, ffn_norm_g, ffn_w_up, ffn_conv_w, ffn_conv_b, ffn_w_down,
                      ple_w_proj, ple_w_gate, ple_b_gate)
    g_final = final_norm_g.reshape(1, Dm)

    Mp = Bp * Lp
    h2, outs_p = _prompt_layer(x_prompt.reshape(Mp, Dm), Bp, Lp, W)
    y_prompt = ple_final(h2, p_prompt[0].reshape(Mp, -1), W["w_gate"], W["b_gate"], W["w_proj"], g_final,
                         _pick_tile(Mp, (512, 256, 128))).reshape(Bp, Lp, Dm)

    lp = -(-Ls // SUBLANE) * SUBLANE
    xs = jnp.pad(x_sample, ((0, 0), (0, lp - Ls), (0, 0)))
    n_pool, psz = cache_cmp_k.shape[1], cache_cmp_k.shape[2]
    caches = [c[0].reshape(n_pool, psz * NSA_KV, NSA_HD)
              for c in (cache_cmp_k, cache_cmp_v, cache_slc_k, cache_slc_v)]
    h2s, outs_s = _decode_layer(xs, Bs, Ls, lp, caches, page_table, state_win_k[0], state_win_v[0],
                                state_gla[0], state_ffn_conv[0], W)
    p_tm = p_sample[0].transpose(1, 0, 2).reshape(Ls * Bs, -1)
    Ms = Ls * Bs
    y_tm = ple_final(h2s, p_tm, W["w_gate"], W["b_gate"], W["w_proj"], g_final,
                     _pick_tile(Ms, (256, 128, 64, 32, 16, 8)))
    y_sample = y_tm.reshape(Ls, Bs, Dm).transpose(1, 0, 2)

    lead = lambda t: t[None]
    return (y_prompt, y_sample) + tuple(lead(t) for t in outs_p) + tuple(lead(t) for t in outs_s)
```

```python
import functools

import numpy as np
import jax
import jax.numpy as jnp
from jax import lax
from jax.experimental import pallas as pl
from jax.experimental.pallas import tpu as pltpu

F32 = jnp.float32
BF16 = jnp.bfloat16

GLA_HEADS = 4
GLA_DK = 128
GLA_DV = 256
GLA_RANK = 16
GLA_TAU = 16.0
GLA_SUB = 16
NSA_HEADS = 8
NSA_KV = 2
NSA_REP = NSA_HEADS // NSA_KV
NSA_HD = 128
CMP_LEN = 32
CMP_STRIDE = 16
SLC_LEN = 64
N_SELECT = 16
WINDOW = 512
CONV_W = 3
EPS = 1e-6
NEG = -1e30
M_FLOOR = 0.1 * NEG
FORCED = 1e6

LANE = 128
SUBLANE = 8
KCHUNK = 128
CMP_SEG_ROWS = NSA_KV * CMP_STRIDE
VMEM_LIMIT = 56 * 1024 * 1024

GQ0 = 0
GK0 = GQ0 + GLA_HEADS * GLA_DK
GV0 = GK0 + GLA_HEADS * GLA_DK
GR0 = GV0 + GLA_HEADS * GLA_DV
NQ0 = GR0 + GLA_HEADS * GLA_DV
KVW = NSA_KV * NSA_HD
KC0 = NQ0 + NSA_HEADS * NSA_HD
VC0 = KC0 + KVW
KS0 = VC0 + KVW
VS0 = KS0 + KVW
KW0 = VS0 + KVW
VW0 = KW0 + KVW
NMAIN = VW0 + KVW
GATE0 = GLA_RANK


def _cp(sem):
    return pltpu.CompilerParams(dimension_semantics=sem, vmem_limit_bytes=VMEM_LIMIT)


def _dot(a, b):
    return jnp.dot(a, b, preferred_element_type=F32)


def _dot_nt(a, b):
    return lax.dot_general(a, b, (((1,), (1,)), ((), ())), preferred_element_type=F32)


def _rms(x, g):
    return x * lax.rsqrt(jnp.mean(x * x, axis=-1, keepdims=True) + EPS) * g


def _norm_proj_kernel(x_ref, g_ref, wm_ref, ws_ref, om_ref, os_ref, kv_ref, xn_ref, *, j_kv0):
    j = pl.program_id(1)

    @pl.when(j == 0)
    def _():
        xn = _rms(x_ref[...], g_ref[...]).astype(BF16)
        xn_ref[...] = xn
        os_ref[...] = _dot(xn, ws_ref[...])

    tile = _dot(xn_ref[...], wm_ref[...])
    om_ref[...] = tile

    @pl.when(j >= j_kv0)
    def _():
        tm = tile.shape[0]
        for a in range(tile.shape[1] // KVW):
            for g in range(NSA_KV):
                c0 = a * KVW + g * NSA_HD
                kv_ref[a, pl.ds(g, tm, stride=NSA_KV), :] = tile[:, c0:c0 + NSA_HD]


def norm_proj(x2, g, w_main, w_small, tm, tn):
    M, Dm = x2.shape
    N = w_main.shape[1]
    assert KC0 % tn == 0 and tn % KVW == 0
    j_kv0 = KC0 // tn
    per_tile = tn // KVW
    n_kv = (N - KC0) // KVW
    return pl.pallas_call(
        functools.partial(_norm_proj_kernel, j_kv0=j_kv0),
        grid=(M // tm, N // tn),
        in_specs=[pl.BlockSpec((tm, Dm), lambda i, j: (i, 0)),
                  pl.BlockSpec((1, Dm), lambda i, j: (0, 0)),
                  pl.BlockSpec((Dm, tn), lambda i, j: (0, j)),
                  pl.BlockSpec((Dm, LANE), lambda i, j: (0, 0))],
        out_specs=[pl.BlockSpec((tm, tn), lambda i, j: (i, j)),
                   pl.BlockSpec((tm, LANE), lambda i, j: (i, 0)),
                   pl.BlockSpec((per_tile, NSA_KV * tm, NSA_HD),
                                lambda i, j: (jnp.maximum(j - j_kv0, 0), i, 0))],
        out_shape=[jax.ShapeDtypeStruct((M, N), F32), jax.ShapeDtypeStruct((M, LANE), F32),
                   jax.ShapeDtypeStruct((n_kv, NSA_KV * M, NSA_HD), F32)],
        scratch_shapes=[pltpu.VMEM((tm, Dm), BF16)],
        compiler_params=_cp(("parallel", "arbitrary")),
        name="norm_proj")(x2, g, w_main, w_small)


def _gla_kernel(q_ref, k_ref, v_ref, r_ref, sm_ref, wa_ref, ba_ref, gn_ref, s0_ref,
                o_ref, s_ref, st_ref, *, cb, sub, valid, nbb):
    c = pl.program_id(1)

    @pl.when(c == 0)
    def _():
        st_ref[...] = s0_ref[...]

    GH = GLA_HEADS
    H = nbb * GH
    side = lambda ref: ref[0] if nbb == 1 else jnp.concatenate([ref[s] for s in range(nbb)], axis=1)
    hk = lambda h: slice(h * GLA_DK, (h + 1) * GLA_DK)
    hv = lambda h: slice(h * GLA_DV, (h + 1) * GLA_DV)
    ri = lax.broadcasted_iota(jnp.int32, (cb, cb), 0)
    ci = lax.broadcasted_iota(jnp.int32, (cb, cb), 1)
    tri = jnp.where(ci <= ri, 1.0, 0.0)
    rowv = lax.broadcasted_iota(jnp.int32, (cb, 1), 0)
    q = side(q_ref) * (GLA_DK ** -0.5)
    k = side(k_ref)
    v = side(v_ref)
    pre = [_dot(sm_ref[s].astype(BF16), wa_ref[...]) + ba_ref[...] for s in range(nbb)]
    pre = pre[0] if nbb == 1 else jnp.concatenate(pre, axis=1)
    log_a = jax.nn.log_sigmoid(pre) / GLA_TAU
    if valid < cb:
        log_a = jnp.where(rowv < valid, log_a, 0.0)
        k = jnp.where(rowv < valid, k, 0.0)
        v = jnp.where(rowv < valid, v, 0.0)
    hi = log_a.astype(BF16).astype(F32)
    lo = (log_a - hi).astype(BF16).astype(F32)
    b = _dot(tri, hi) + _dot(tri, lo)
    b_last = b[cb - 1:cb, :]
    S = [st_ref[h // GH, h % GH] for h in range(H)]
    vb = v.astype(BF16)

    qe = (q * jnp.exp(b)).astype(BF16)
    o = [_dot(qe[:, hk(h)], S[h].astype(BF16)) for h in range(H)]
    a_rows = [[] for _ in range(H)]
    cc = lax.broadcasted_iota(jnp.int32, (sub, cb), 1)
    for blk in range(cb // sub):
        lo_r, hi_r = blk * sub, (blk + 1) * sub
        r_dec = jnp.zeros((1, H * GLA_DK), F32) if blk == 0 else b[lo_r - 1:lo_r, :]
        q_i = q[lo_r:hi_r] * jnp.exp(b[lo_r:hi_r] - r_dec)
        k_i = k * jnp.exp(jnp.where(rowv < hi_r, r_dec - b, 0.0))
        rr = lax.broadcasted_iota(jnp.int32, (sub, cb), 0) + lo_r
        for h in range(H):
            a = _dot_nt(q_i[:, hk(h)], k_i[:, hk(h)])
            a_rows[h].append(jnp.where(cc <= rr, a, 0.0))
    for h in range(H):
        a = a_rows[h][0] if len(a_rows[h]) == 1 else jnp.concatenate(a_rows[h], axis=0)
        o[h] = o[h] + _dot(a, v[:, hv(h)])

    kh = k * jnp.exp(b_last - b)
    dec = jnp.exp(b_last)
    if cb < LANE:
        kh = jnp.concatenate([kh, jnp.zeros((LANE - cb, H * GLA_DK), F32)], axis=0)
        vpad = jnp.concatenate([vb, jnp.zeros((LANE - cb, H * GLA_DV), BF16)], axis=0)
    else:
        vpad = vb
    for h in range(H):
        dec_col = jnp.transpose(jnp.broadcast_to(dec[:, hk(h)], (GLA_DK, GLA_DK)))[:, 0:1]
        s_new = dec_col * S[h] + _dot(jnp.transpose(kh[:, hk(h)]).astype(BF16), vpad[:, hv(h)])
        st_ref[h // GH, h % GH] = s_new
        s_ref[h // GH, h % GH] = s_new

    r = side(r_ref)
    gate = r * jax.nn.sigmoid(r)
    for h in range(H):
        o_ref[h // GH, :, hv(h % GH)] = (_rms(o[h], gn_ref[...]) * gate[:, hv(h)]).astype(o_ref.dtype)


def gla(proj3, small3, wa_pad, b_a, gnorm, s0, cb, sub, valid, nbb):
    B, L, _ = proj3.shape
    H = GLA_HEADS
    qw, vw = H * GLA_DK, H * GLA_DV
    kern = functools.partial(_gla_kernel, cb=cb, sub=sub, valid=valid, nbb=nbb)
    return pl.pallas_call(
        kern,
        grid=(B // nbb, L // cb),
        in_specs=[pl.BlockSpec((nbb, cb, qw), lambda b, c: (b, c, GQ0 // qw)),
                  pl.BlockSpec((nbb, cb, qw), lambda b, c: (b, c, GK0 // qw)),
                  pl.BlockSpec((nbb, cb, vw), lambda b, c: (b, c, GV0 // vw)),
                  pl.BlockSpec((nbb, cb, vw), lambda b, c: (b, c, GR0 // vw)),
                  pl.BlockSpec((nbb, cb, LANE), lambda b, c: (b, c, 0)),
                  pl.BlockSpec((LANE, qw), lambda b, c: (0, 0)),
                  pl.BlockSpec((1, qw), lambda b, c: (0, 0)),
                  pl.BlockSpec((1, GLA_DV), lambda b, c: (0, 0)),
                  pl.BlockSpec((nbb, H, GLA_DK, GLA_DV), lambda b, c: (b, 0, 0, 0))],
        out_specs=[pl.BlockSpec((nbb, cb, vw), lambda b, c: (b, c, 0)),
                   pl.BlockSpec((nbb, H, GLA_DK, GLA_DV), lambda b, c: (b, 0, 0, 0))],
        out_shape=[jax.ShapeDtypeStruct((B, L, vw), BF16),
                   jax.ShapeDtypeStruct((B, H, GLA_DK, GLA_DV), F32)],
        scratch_shapes=[pltpu.VMEM((nbb, H, GLA_DK, GLA_DV), F32)],
        compiler_params=_cp(("parallel", "arbitrary")),
        name="gla")(proj3, proj3, proj3, proj3, small3, wa_pad, b_a, gnorm, s0)


def _pair_rows(load_rows, j):
    return jnp.concatenate([load_rows(j), load_rows(j + 1)], axis=1).astype(BF16)


def _pair_weights(w1_ref, j):
    w = w1_ref[pl.ds(j, 2)]
    return w.reshape(2 * NSA_HD, 2 * NSA_HD)


def _compress_group(load_rows, w1_ref, w2, nseg):
    acc = jnp.zeros((nseg, 2 * NSA_HD), F32)
    for j in range(0, CMP_STRIDE, 2):
        acc = acc + _dot(_pair_rows(load_rows, j), _pair_weights(w1_ref, j))
    first = acc[:, :NSA_HD]
    second = acc[:, NSA_HD:]
    h = jax.nn.gelu(first + pltpu.roll(second, nseg - 1, axis=0))
    return _dot(h.astype(BF16), w2)


def _compress_groups(loads, w1_refs, w2s, nrow, halves=None):
    n = len(loads)
    acc = [jnp.zeros((nrow, 2 * NSA_HD), F32) for _ in range(n)]
    for j in range(0, CMP_STRIDE, 2):
        for i in range(n):
            acc[i] = acc[i] + _dot(_pair_rows(loads[i], j), _pair_weights(w1_refs[i], j))
    if halves is None:
        halves = lambda i, a: (a[:, :NSA_HD], a[:, NSA_HD:])
    fs = [halves(i, a) for i, a in enumerate(acc)]
    hs = [jax.nn.gelu(first + pltpu.roll(second, nrow - 1, axis=0)) for first, second in fs]
    return [_dot(h.astype(BF16), w2) for h, w2 in zip(hs, w2s)]


def _cmp_branch_many(q4s, cks, cvs, qpos4, nb):
    s = [_dot_nt(q4, ck.astype(BF16)) for q4, ck in zip(q4s, cks)]
    n = lax.broadcasted_iota(jnp.int32, s[0].shape, 1)
    mask = jnp.where(n * CMP_STRIDE + (CMP_LEN - 1) <= qpos4, n, nb) < nb
    s = [jnp.where(mask, si, NEG) for si in s]
    m = [jnp.max(si, axis=-1, keepdims=True) for si in s]
    e = [jnp.where(mask, jnp.exp(si - mi), 0.0) for si, mi in zip(s, m)]
    p = [ei / jnp.maximum(jnp.sum(ei, axis=-1, keepdims=True), 1e-30) for ei in e]
    o = [_dot(pi.astype(BF16), cv.astype(BF16)) for pi, cv in zip(p, cvs)]
    return list(zip(p, o))


def _select(pg, mmat, qpos_tok, nslc, nsel):
    hi = pg.astype(BF16)
    lo = (pg - hi.astype(F32)).astype(BF16)
    imp = _dot(hi, mmat) + _dot(lo, mmat)
    blk = lax.broadcasted_iota(jnp.int32, pg.shape, 1)
    cur = qpos_tok // SLC_LEN
    forced = jnp.where(blk == 0, 1, 0) + jnp.where(blk == cur, 1, 0) + jnp.where(blk == cur - 1, 1, 0)
    score = jnp.where(blk <= cur, jnp.where(forced > 0, FORCED, imp), -1.0)
    score = jnp.where(blk < nslc, score, -2.0)
    rank = jnp.zeros(pg.shape, F32)
    for j in range(nslc):
        sj = score[:, j:j + 1]
        tie = jnp.where(blk > j, sj, NEG)
        rank = rank + jnp.where(sj > score, 1.0, 0.0) + jnp.where(tie == score, 1.0, 0.0)
    return jnp.where(rank < nsel, jnp.where(blk < nslc, 1.0, 0.0), 0.0)


def _tile_rows(x, reps):
    return jnp.concatenate([x] * reps, axis=0)


def _attend_many(jobs):
    s = [[jnp.where(mk, _dot_nt(q4, k), NEG) for k, mk in zip(keys, masks)] for q4, keys, _, masks in jobs]
    m = []
    for sj in s:
        mj = sj[0].max(axis=-1, keepdims=True)
        for si in sj[1:]:
            mj = jnp.maximum(mj, si.max(axis=-1, keepdims=True))
        m.append(mj)
    e = [[jnp.where(mk, jnp.exp(si - mj), 0.0) for si, mk in zip(sj, job[3])]
         for sj, mj, job in zip(s, m, jobs)]
    l = [sum(ei.sum(axis=-1, keepdims=True) for ei in ej) for ej in e]
    o = [sum(_dot(ei.astype(BF16), v) for ei, v in zip(ej, job[2])) for ej, job in zip(e, jobs)]
    return [oj / jnp.maximum(lj, 1e-30) for oj, lj in zip(o, l)]


def _compress_kernel(x_ref, w1_ref, w2_ref, o_ref, *, nseg):
    load = lambda j: x_ref[0, pl.ds(j, nseg, stride=CMP_STRIDE), :]
    o_ref[0, 0, 0] = _compress_group(load, w1_ref.at[0], w2_ref[0].astype(BF16), nseg)


def compress(proj3, w1cat, w2s):
    B, L, _ = proj3.shape
    nseg = L // CMP_STRIDE
    G = NSA_KV
    return pl.pallas_call(
        functools.partial(_compress_kernel, nseg=nseg),
        grid=(B, 2, G),
        in_specs=[pl.BlockSpec((1, L, NSA_HD), lambda b, w, g: (b, 0, KC0 // NSA_HD + w * G + g)),
                  pl.BlockSpec((1, CMP_STRIDE, NSA_HD, 2 * NSA_HD), lambda b, w, g: (w, 0, 0, 0)),
                  pl.BlockSpec((1, NSA_HD, NSA_HD), lambda b, w, g: (w, 0, 0))],
        out_specs=pl.BlockSpec((1, 1, 1, nseg, NSA_HD), lambda b, w, g: (w, b, g, 0, 0)),
        out_shape=jax.ShapeDtypeStruct((2, B, G, nseg, NSA_HD), F32),
        compiler_params=_cp(("parallel", "parallel", "parallel")),
        name="compress")(proj3, w1cat, w2s)


def _select_t(pg_t, mm_t, qpos_row, nslc, nsel):
    nsp = -(-nslc // SUBLANE) * SUBLANE
    hi = pg_t.astype(BF16)
    lo = (pg_t - hi.astype(F32)).astype(BF16)
    imp = (_dot(mm_t, hi) + _dot(mm_t, lo))[:nsp]
    blk = lax.broadcasted_iota(jnp.int32, imp.shape, 0)
    cur = qpos_row // SLC_LEN
    forced = jnp.where(blk == 0, 1, 0) + jnp.where(blk == cur, 1, 0) + jnp.where(blk == cur - 1, 1, 0)
    score = jnp.where(blk <= cur, jnp.where(forced > 0, FORCED, imp), -1.0)
    score = jnp.where(blk < nslc, score, -2.0)
    rank = jnp.zeros(imp.shape, F32)
    for j in range(nslc):
        sj = score[j:j + 1, :]
        tie = jnp.where(blk > j, sj, NEG)
        rank = rank + jnp.where(sj > score, 1.0, 0.0) + jnp.where(tie == score, 1.0, 0.0)
    return jnp.where(rank < nsel, jnp.where(blk < nslc, 1.0, 0.0), 0.0)


def _flash_steps_t(q_t, jobs):
    s = [_dot(k, q_t) + bias for k, _, bias, _ in jobs]
    m_old = [st[0][...] for _, _, _, st in jobs]
    m_new = [jnp.maximum(mo, jnp.max(si, axis=0, keepdims=True)) for mo, si in zip(m_old, s)]
    alpha = [jnp.exp(mo - mn) for mo, mn in zip(m_old, m_new)]
    p = [jnp.exp(si - mn) for si, mn in zip(s, m_new)]
    pv = [_dot(v_t, pi.astype(BF16)) for (_, v_t, _, _), pi in zip(jobs, p)]
    for (_, _, _, (m_ref, l_ref, acc_ref)), mn, al, pi, pvi in zip(jobs, m_new, alpha, p, pv):
        l_ref[...] = al * l_ref[...] + jnp.sum(pi, axis=0, keepdims=True)
        acc_ref[...] = al * acc_ref[...] + pvi
        m_ref[...] = mn


def _nsa_prompt_kernel(q_ref, sm_ref, ck_ref, cv_ref, ks_ref, vs_ref, kw_ref, vw_ref, mm_ref,
                       o_ref, vst_ref, vwt_ref, sel_ref, m_ref, l_ref, acc_ref, mw_ref, lw_ref, accw_ref,
                       *, tq, kstep, nslc, nsel, nb):
    g = pl.program_id(1)
    qb = pl.program_id(2)
    R = NSA_REP
    T = R * tq
    L = ks_ref.shape[1]
    scale = NSA_HD ** -0.5

    @pl.when(qb == 0)
    def _():
        for c in range(L // KCHUNK):
            cs = slice(c * KCHUNK, (c + 1) * KCHUNK)
            vst_ref[:, cs] = jnp.transpose(vs_ref[0, cs, :]).astype(BF16)
            vwt_ref[:, cs] = jnp.transpose(vw_ref[0, cs, :]).astype(BF16)

    q_t = (jnp.concatenate([jnp.transpose(q_ref[0, :, r * NSA_HD:(r + 1) * NSA_HD]) for r in range(R)],
                           axis=1) * scale).astype(BF16)
    qpos_row = qb * tq + lax.broadcasted_iota(jnp.int32, (1, tq), 1)
    qpos4 = jnp.concatenate([qpos_row] * R, axis=1)

    key = lax.broadcasted_iota(jnp.int32, (kstep, T), 0)
    hi_step = (qb * tq + tq - 1) // kstep + 1
    lo_step = jnp.maximum(qb * tq - (WINDOW - 1), 0) // kstep
    last = hi_step - 1
    sel_state = (m_ref, l_ref, acc_ref)
    win_state = (mw_ref, lw_ref, accw_ref)

    def sel_step(c, causal):
        off = pl.multiple_of(c * kstep, kstep)
        k = ks_ref[0, pl.ds(off, kstep), :].astype(BF16)
        v_t = vst_ref[:, pl.ds(off, kstep)]
        blocks = [jnp.broadcast_to(sel_ref[pl.ds((kstep // SLC_LEN) * c + i, 1), :], (SLC_LEN, tq))
                  for i in range(kstep // SLC_LEN)]
        bias = jnp.concatenate(blocks, axis=0)
        bias = jnp.concatenate([bias] * R, axis=1)
        if causal:
            bias = jnp.where(off + key <= qpos4, bias, NEG)
        return k, v_t, bias, sel_state

    def win_step(c, causal):
        off = pl.multiple_of(c * kstep, kstep)
        k = kw_ref[0, pl.ds(off, kstep), :].astype(BF16)
        v_t = vwt_ref[:, pl.ds(off, kstep)]
        kpos = off + key
        if causal:
            bias = jnp.where(kpos <= qpos4, 0.0, NEG)
        else:
            bias = jnp.where(kpos > qpos4 - WINDOW, 0.0, NEG)
        return k, v_t, bias, win_state

    k_d, vt_d, bias_d, _ = win_step(last, True)
    s = _dot(ck_ref[0, 0, 0].astype(BF16), q_t)
    s_d = _dot(k_d, q_t) + bias_d
    n = lax.broadcasted_iota(jnp.int32, s.shape, 0)
    mask = jnp.where(n * CMP_STRIDE + (CMP_LEN - 1) <= qpos4, n, nb) < nb
    s = jnp.where(mask, s, NEG)
    m_c = jnp.max(s, axis=0, keepdims=True)
    m_d = jnp.maximum(jnp.max(s_d, axis=0, keepdims=True), M_FLOOR)
    e = jnp.where(mask, jnp.exp(s - m_c), 0.0)
    p_d = jnp.exp(s_d - m_d)
    p = e / jnp.maximum(jnp.sum(e, axis=0, keepdims=True), 1e-30)
    lw_ref[...] = jnp.sum(p_d, axis=0, keepdims=True)
    o_c = _dot(jnp.transpose(cv_ref[0, 0, 0]).astype(BF16), p.astype(BF16))
    accw_ref[...] = _dot(vt_d, p_d.astype(BF16))
    mw_ref[...] = m_d
    pg = p[:, 0:tq]
    for r in range(1, R):
        pg = pg + p[:, r * tq:(r + 1) * tq]
    sel_ref[...] = (_select_t(pg, mm_ref[...], qpos_row, nslc, nsel) - 1.0) * (-NEG)

    m_ref[...] = jnp.full(m_ref.shape, M_FLOOR, F32)
    l_ref[...] = jnp.zeros(l_ref.shape, F32)
    acc_ref[...] = jnp.zeros(acc_ref.shape, F32)

    def body_sel(c, carry):
        _flash_steps_t(q_t, [sel_step(c, False)])
        return carry

    def body_both(c, carry):
        _flash_steps_t(q_t, [sel_step(c, False), win_step(c, False)])
        return carry

    lax.fori_loop(0, jnp.minimum(lo_step, last), body_sel, 0)
    lax.fori_loop(lo_step, last, body_both, 0)
    _flash_steps_t(q_t, [sel_step(last, True)])
    o_s = acc_ref[...] / jnp.maximum(l_ref[...], 1e-30)
    o_w = accw_ref[...] / jnp.maximum(lw_ref[...], 1e-30)

    g_t = jnp.transpose(jax.nn.sigmoid(sm_ref[0]))
    for r in range(R):
        def gate(br, r=r):
            c0 = GATE0 + br * NSA_HEADS + r
            return jnp.where(g == 0, g_t[c0:c0 + 1, :], g_t[c0 + R:c0 + R + 1, :])
        cs = slice(r * tq, (r + 1) * tq)
        o = gate(0) * o_c[:, cs] + gate(1) * o_s[:, cs] + gate(2) * o_w[:, cs]
        o_ref[0, :, r * NSA_HD:(r + 1) * NSA_HD] = jnp.transpose(o).astype(o_ref.dtype)


def nsa_prompt(proj3, small3, cmp, mmat_t):
    B, L, _ = proj3.shape
    tq = _pick_tile(L, (2 * KCHUNK, KCHUNK))
    nseg = L // CMP_STRIDE
    nb = nseg - 1
    nslc = -(-L // SLC_LEN)
    nsel = min(N_SELECT, nslc)
    nsp = -(-nslc // SUBLANE) * SUBLANE
    G, R = NSA_KV, NSA_REP
    kstep = _pick_tile(L, (2 * KCHUNK, KCHUNK))
    assert nseg == LANE and L % KCHUNK == 0 and kstep % SLC_LEN == 0
    kv_spec = lambda c0: pl.BlockSpec((1, L, NSA_HD), lambda b, g, qb: (b, 0, c0 // NSA_HD + g))
    kern = functools.partial(_nsa_prompt_kernel, tq=tq, kstep=kstep, nslc=nslc, nsel=nsel, nb=nb)
    return pl.pallas_call(
        kern,
        grid=(B, G, L // tq),
        in_specs=[pl.BlockSpec((1, tq, R * NSA_HD), lambda b, g, qb: (b, qb, NQ0 // (R * NSA_HD) + g)),
                  pl.BlockSpec((1, tq, LANE), lambda b, g, qb: (b, qb, 0)),
                  pl.BlockSpec((1, 1, 1, nseg, NSA_HD), lambda b, g, qb: (0, b, g, 0, 0)),
                  pl.BlockSpec((1, 1, 1, nseg, NSA_HD), lambda b, g, qb: (1, b, g, 0, 0)),
                  kv_spec(KS0), kv_spec(VS0), kv_spec(KW0), kv_spec(VW0),
                  pl.BlockSpec((LANE, LANE), lambda b, g, qb: (0, 0))],
        out_specs=pl.BlockSpec((1, tq, R * NSA_HD), lambda b, g, qb: (b, qb, g)),
        out_shape=jax.ShapeDtypeStruct((B, L, NSA_HEADS * NSA_HD), BF16),
        scratch_shapes=[pltpu.VMEM((NSA_HD, L), BF16),
                        pltpu.VMEM((NSA_HD, L), BF16),
                        pltpu.VMEM((nsp, tq), F32)]
                       + 2 * [pltpu.VMEM((1, R * tq), F32),
                              pltpu.VMEM((1, R * tq), F32),
                              pltpu.VMEM((NSA_HD, R * tq), F32)],
        compiler_params=_cp(("parallel", "parallel", "arbitrary")),
        name="nsa_prompt")(proj3, small3, cmp, cmp, proj3, proj3, proj3, proj3, mmat_t)


def _nsa_decode_kernel(pt_ref, q_ref, sm_ref, ksn_ref, vsn_ref, kwn_ref, vwn_ref, wk_ref, wv_ref,
                       w1_ref, w2_ref, e_ref, mm_ref, c0_hbm, c1_hbm, c2_hbm, c3_hbm,
                       o_ref, wko_ref, wvo_ref, cbuf_ref, sbuf_ref, perm_ref, sem_ref,
                       *, ns, lp, lreal, plen, rows_pp, n_pages, wlen, nslc, nsel, nb):
    b = pl.program_id(0)
    slot = b % 2
    caches = (c0_hbm, c1_hbm, c2_hbm, c3_hbm)
    G = NSA_KV
    R = NSA_REP
    T = R * lp
    segs_pp = rows_pp // CMP_SEG_ROWS
    PAGE_PITCH = rows_pp + SUBLANE

    def copies(step, sl):
        out = []
        for q in range(ns):
            for p in range(n_pages):
                page = pt_ref[step * ns + q, p]
                for w in range(2):
                    out.append(pltpu.make_async_copy(
                        caches[w].at[page], cbuf_ref.at[sl, q, w, pl.ds(p * PAGE_PITCH, rows_pp)],
                        sem_ref.at[sl, w]))
                    out.append(pltpu.make_async_copy(
                        caches[2 + w].at[page], sbuf_ref.at[sl, q, w, pl.ds(p * rows_pp, rows_pp)],
                        sem_ref.at[sl, 2 + w]))
        return out

    @pl.when(b == 0)
    def _():
        for cp in copies(0, 0):
            cp.start()

    @pl.when(b + 1 < pl.num_programs(0))
    def _():
        for cp in copies(b + 1, 1 - slot):
            cp.start()

    for w in range(len(caches)):
        for q in range(ns):
            whole = sbuf_ref.at[slot, q, 0]
            pltpu.make_async_copy(whole, whole, sem_ref.at[slot, w]).wait()

    scale = NSA_HD ** -0.5
    nseg = plen // CMP_STRIDE
    tok = lax.broadcasted_iota(jnp.int32, (lp, 1), 0)
    qpos_tok = plen + tok
    qpos4 = _tile_rows(qpos_tok, R)
    lane_n = lax.broadcasted_iota(jnp.int32, (T, KCHUNK), 1)
    new_pos = jnp.where(lane_n < lreal, plen + lane_n, qpos4 + 1)
    past_pos = lax.broadcasted_iota(jnp.int32, (T, plen), 1)
    win_pos = plen - wlen + lax.broadcasted_iota(jnp.int32, (T, wlen), 1)
    win_pos = jnp.where(win_pos > qpos4 - WINDOW, win_pos, qpos4 + 1)
    win_pos = jnp.where(win_pos >= 0, win_pos, qpos4 + 1)
    new_win_pos = jnp.where(new_pos > qpos4 - WINDOW, new_pos, qpos4 + 1)
    zpad_f = jnp.zeros((KCHUNK - lp, NSA_HD), F32)
    pad_new = lambda ref, q, ls: jnp.concatenate([ref[q, :, ls], zpad_f], axis=0).astype(BF16)
    gls = lambda g: slice(g * NSA_HD, (g + 1) * NSA_HD)
    pairs = [(q, g) for q in range(ns) for g in range(G)]
    q4 = [(jnp.concatenate([q_ref[q, :, (g * R + r) * NSA_HD:(g * R + r + 1) * NSA_HD]
                            for r in range(R)], axis=0) * scale).astype(BF16) for q, g in pairs]

    def seg_rows(w, j):
        return jnp.concatenate(
            [cbuf_ref[slot, q, w, pl.ds(s * CMP_SEG_ROWS + G * j + g, n_pages, stride=PAGE_PITCH), :]
             for q, g in pairs for s in range(segs_pp)], axis=0)

    def halves(w, acc):
        for i in range(len(pairs)):
            for s in range(segs_pp):
                r0 = i * nseg + s * n_pages
                for hh in range(2):
                    perm_ref[w, hh, pl.ds(i * nseg + s, n_pages, stride=segs_pp), :] = (
                        acc[r0:r0 + n_pages, hh * NSA_HD:(hh + 1) * NSA_HD])
        return perm_ref[w, 0], perm_ref[w, 1]

    cmp = _compress_groups([functools.partial(seg_rows, w) for w in range(2)],
                           [w1_ref.at[w] for w in range(2)],
                           [w2_ref[w].astype(BF16) for w in range(2)], len(pairs) * nseg, halves)
    prow = lambda i: slice(i * nseg, (i + 1) * nseg)
    branch = _cmp_branch_many(q4, [cmp[0][prow(i)] for i in range(len(pairs))],
                              [cmp[1][prow(i)] for i in range(len(pairs))], qpos4, nb)
    pgs = []
    for i in range(len(pairs)):
        p = branch[i][0]
        pg = p[0:lp]
        for r in range(1, R):
            pg = pg + p[r * lp:(r + 1) * lp]
        pgs.append(pg)
    sel = _select(jnp.concatenate(pgs, axis=0), mm_ref[...], _tile_rows(qpos_tok, len(pairs)), nslc, nsel)
    selm_all = _dot(sel.astype(BF16), e_ref[...])

    sel_jobs, win_jobs = [], []
    for i, (q, g) in enumerate(pairs):
        selm = _tile_rows(selm_all[i * lp:(i + 1) * lp], R)
        k_past = sbuf_ref[slot, q, 0, pl.ds(g, plen, stride=G), :].astype(BF16)
        v_past = sbuf_ref[slot, q, 1, pl.ds(g, plen, stride=G), :].astype(BF16)
        m_past = jnp.where(selm[:, :plen] > 0.5, past_pos, qpos4 + 1) <= qpos4
        m_new = jnp.where(selm[:, plen:] > 0.5, new_pos, qpos4 + 1) <= qpos4
        sel_jobs.append((q4[i], [k_past, pad_new(ksn_ref, q, gls(g))],
                         [v_past, pad_new(vsn_ref, q, gls(g))], [m_past, m_new]))
        k_win = wk_ref[q, pl.ds(g, wlen, stride=G), :].astype(BF16)
        v_win = wv_ref[q, pl.ds(g, wlen, stride=G), :].astype(BF16)
        win_jobs.append((q4[i], [k_win, pad_new(kwn_ref, q, gls(g))],
                         [v_win, pad_new(vwn_ref, q, gls(g))], [win_pos <= qpos4, new_win_pos <= qpos4]))
    outs = _attend_many(sel_jobs + win_jobs)
    o_s, o_w = outs[:len(pairs)], outs[len(pairs):]

    for i, (q, g) in enumerate(pairs):
        gts = jax.nn.sigmoid(sm_ref[q])
        for r in range(R):
            c0 = GATE0 + g * R + r
            rs = slice(r * lp, (r + 1) * lp)
            o = (gts[:, c0:c0 + 1] * branch[i][1][rs]
                 + gts[:, c0 + NSA_HEADS:c0 + NSA_HEADS + 1] * o_s[i][rs]
                 + gts[:, c0 + 2 * NSA_HEADS:c0 + 2 * NSA_HEADS + 1] * o_w[i][rs])
            hq = (g * R + r) * NSA_HD
            o_ref[q, :, hq:hq + NSA_HD] = o.astype(o_ref.dtype)

    wrows = wlen * G
    shift = lreal * G
    for q in range(ns):
        for src, new, dst in ((wk_ref, kwn_ref, wko_ref), (wv_ref, vwn_ref, wvo_ref)):
            dst[q, 0:wrows - shift, :] = src[q, shift:wrows, :]
            for t in range(lreal):
                for g in range(G):
                    row = wrows - shift + t * G + g
                    dst[q, row:row + 1, :] = new[q, t:t + 1, g * NSA_HD:(g + 1) * NSA_HD]


def nsa_decode(proj3, small3, caches, page_table, win_k, win_v, w1cat, w2s, emat, mmat, lreal, ns):
    B, lp, _ = proj3.shape
    G = NSA_KV
    n_pages = page_table.shape[1]
    rows_pp = caches[0].shape[1]
    plen = n_pages * rows_pp // G
    wlen = win_k.shape[1] // G
    assert plen % CMP_STRIDE == 0 and lreal < CMP_STRIDE and lreal <= lp and B % ns == 0
    nb = (plen + lreal) // CMP_STRIDE - 1
    nslc = -(-(plen + lreal) // SLC_LEN)
    nsel = min(N_SELECT, nslc)
    new_spec = lambda c0: pl.BlockSpec((ns, lp, KVW), lambda b, pt: (b, 0, c0 // KVW))
    const = lambda shape: pl.BlockSpec(shape, lambda b, pt: (0,) * len(shape))
    hbm = pl.BlockSpec(memory_space=pl.ANY)
    kern = functools.partial(_nsa_decode_kernel, ns=ns, lp=lp, lreal=lreal, plen=plen, rows_pp=rows_pp,
                             n_pages=n_pages, wlen=wlen, nslc=nslc, nsel=nsel, nb=nb)
    grid_spec = pltpu.PrefetchScalarGridSpec(
        num_scalar_prefetch=1,
        grid=(B // ns,),
        in_specs=[pl.BlockSpec((ns, lp, NSA_HEADS * NSA_HD), lambda b, pt: (b, 0, NQ0 // (NSA_HEADS * NSA_HD))),
                  pl.BlockSpec((ns, lp, LANE), lambda b, pt: (b, 0, 0)),
                  new_spec(KS0), new_spec(VS0), new_spec(KW0), new_spec(VW0),
                  pl.BlockSpec((ns, wlen * G, NSA_HD), lambda b, pt: (b, 0, 0)),
                  pl.BlockSpec((ns, wlen * G, NSA_HD), lambda b, pt: (b, 0, 0)),
                  const((2, CMP_STRIDE, NSA_HD, 2 * NSA_HD)),
                  const((2, NSA_HD, NSA_HD)),
                  const((LANE, plen + KCHUNK)),
                  const((LANE, LANE)),
                  hbm, hbm, hbm, hbm],
        out_specs=[pl.BlockSpec((ns, lp, NSA_HEADS * NSA_HD), lambda b, pt: (b, 0, 0)),
                   pl.BlockSpec((ns, wlen * G, NSA_HD), lambda b, pt: (b, 0, 0)),
                   pl.BlockSpec((ns, wlen * G, NSA_HD), lambda b, pt: (b, 0, 0))],
        scratch_shapes=[pltpu.VMEM((2, ns, 2, n_pages * (rows_pp + SUBLANE), NSA_HD), F32),
                        pltpu.VMEM((2, ns, 2, n_pages * rows_pp, NSA_HD), F32),
                        pltpu.VMEM((2, 2, ns * G * (plen // CMP_STRIDE), NSA_HD), F32),
                        pltpu.SemaphoreType.DMA((2, len(caches)))])
    return pl.pallas_call(
        kern,
        grid_spec=grid_spec,
        out_shape=[jax.ShapeDtypeStruct((B, lp, NSA_HEADS * NSA_HD), BF16),
                   jax.ShapeDtypeStruct(win_k.shape, F32), jax.ShapeDtypeStruct(win_v.shape, F32)],
        compiler_params=_cp(("arbitrary",)),
        name="nsa_decode")(page_table, proj3, small3, proj3, proj3, proj3, proj3, win_k, win_v,
                           w1cat, w2s, emat, mmat, caches[0], caches[1], caches[2], caches[3])


def _out_proj_kernel(og_ref, on_ref, x_ref, w0_ref, w1_ref, g_ref, h_ref, hn_ref):
    h = x_ref[...] + _dot(og_ref[...], w0_ref[...]) + _dot(on_ref[...], w1_ref[...])
    h_ref[...] = h
    hn_ref[...] = _rms(h, g_ref[...]).astype(BF16)


def out_proj(og, on, x2, w_out, g, tm):
    M, Dm = x2.shape
    Kh = og.shape[1]
    return pl.pallas_call(
        _out_proj_kernel,
        grid=(M // tm,),
        in_specs=[pl.BlockSpec((tm, Kh), lambda i: (i, 0)),
                  pl.BlockSpec((tm, Kh), lambda i: (i, 0)),
                  pl.BlockSpec((tm, Dm), lambda i: (i, 0)),
                  pl.BlockSpec((Kh, Dm), lambda i: (0, 0)),
                  pl.BlockSpec((Kh, Dm), lambda i: (1, 0)),
                  pl.BlockSpec((1, Dm), lambda i: (0, 0))],
        out_specs=[pl.BlockSpec((tm, Dm), lambda i: (i, 0)),
                   pl.BlockSpec((tm, Dm), lambda i: (i, 0))],
        out_shape=[jax.ShapeDtypeStruct((M, Dm), F32), jax.ShapeDtypeStruct((M, Dm), BF16)],
        compiler_params=_cp(("parallel",)),
        name="out_proj")(og, on, x2, w_out, w_out, g)


def _ffn_up_seq_kernel(hn_ref, wa_ref, wu_ref, cw_ref, cb_ref, pre_ref, hm_ref, tail_ref, aext_ref,
                       *, tm, tiles_per_seq):
    m = pl.program_id(1)

    @pl.when(m % tiles_per_seq == 0)
    def _():
        aext_ref[0:SUBLANE, :] = pre_ref[0]

    hn = hn_ref[...]
    a = _dot(hn, wa_ref[...])
    u = _dot(hn, wu_ref[...])
    aext_ref[SUBLANE:SUBLANE + tm, :] = a
    p1 = aext_ref[pl.ds(SUBLANE - 1, tm), :]
    p2 = aext_ref[pl.ds(SUBLANE - 2, tm), :]
    c = p2 * cw_ref[0:1, :] + p1 * cw_ref[1:2, :] + a * cw_ref[2:3, :] + cb_ref[...]
    hm_ref[...] = (jax.nn.gelu(c) * u).astype(hm_ref.dtype)
    tail = a[tm - SUBLANE:tm, :]
    tail_ref[0] = tail
    aext_ref[0:SUBLANE, :] = tail


def ffn_up_seq(hn, w_a, w_u, conv_w, conv_b, prefix, seq_len, tm, tf):
    M, Dm = hn.shape
    Fp = conv_w.shape[1]
    nf = Fp // tf
    B = M // seq_len
    tps = seq_len // tm
    kern = functools.partial(_ffn_up_seq_kernel, tm=tm, tiles_per_seq=tps)
    return pl.pallas_call(
        kern,
        grid=(nf, M // tm),
        in_specs=[pl.BlockSpec((tm, Dm), lambda f, m: (m, 0)),
                  pl.BlockSpec((Dm, tf), lambda f, m: (0, f)),
                  pl.BlockSpec((Dm, tf), lambda f, m: (0, f)),
                  pl.BlockSpec((SUBLANE, tf), lambda f, m: (0, f)),
                  pl.BlockSpec((1, tf), lambda f, m: (0, f)),
                  pl.BlockSpec((1, SUBLANE, tf), lambda f, m: (m // tps, 0, f))],
        out_specs=[pl.BlockSpec((tm, tf), lambda f, m: (m, f)),
                   pl.BlockSpec((1, SUBLANE, tf), lambda f, m: (m // tps, 0, f))],
        out_shape=[jax.ShapeDtypeStruct((M, Fp), BF16), jax.ShapeDtypeStruct((B, SUBLANE, Fp), F32)],
        scratch_shapes=[pltpu.VMEM((SUBLANE + tm, tf), F32)],
        compiler_params=_cp(("parallel", "arbitrary")),
        name="ffn_up_seq")(hn, w_a, w_u, conv_w, conv_b, prefix)


def _ffn_up_tm_kernel(hn_ref, wa_ref, wu_ref, cw_ref, cb_ref, pre_ref, hm_ref, tail_ref, *, steps, nb):
    hn = hn_ref[...]
    a = _dot(hn, wa_ref[...])
    u = _dot(hn, wu_ref[...])
    slabs = [pre_ref[i] for i in range(CONV_W - 1)] + [a[t * nb:(t + 1) * nb] for t in range(steps)]
    for t in range(steps):
        c = (slabs[t] * cw_ref[0:1, :] + slabs[t + 1] * cw_ref[1:2, :] + slabs[t + 2] * cw_ref[2:3, :]
             + cb_ref[...])
        hm_ref[t * nb:(t + 1) * nb, :] = (jax.nn.gelu(c) * u[t * nb:(t + 1) * nb]).astype(hm_ref.dtype)
    for i in range(CONV_W - 1):
        tail_ref[i] = slabs[steps + i]


def ffn_up_tm(hn, w_a, w_u, conv_w, conv_b, prefix, steps, tf):
    M, Dm = hn.shape
    Fp = conv_w.shape[1]
    nf = Fp // tf
    nb = M // steps
    kern = functools.partial(_ffn_up_tm_kernel, steps=steps, nb=nb)
    return pl.pallas_call(
        kern,
        grid=(nf,),
        in_specs=[pl.BlockSpec((M, Dm), lambda f: (0, 0)),
                  pl.BlockSpec((Dm, tf), lambda f: (0, f)),
                  pl.BlockSpec((Dm, tf), lambda f: (0, f)),
                  pl.BlockSpec((SUBLANE, tf), lambda f: (0, f)),
                  pl.BlockSpec((1, tf), lambda f: (0, f)),
                  pl.BlockSpec((CONV_W - 1, nb, tf), lambda f: (0, 0, f))],
        out_specs=[pl.BlockSpec((M, tf), lambda f: (0, f)),
                   pl.BlockSpec((CONV_W - 1, nb, tf), lambda f: (0, 0, f))],
        out_shape=[jax.ShapeDtypeStruct((M, Fp), BF16), jax.ShapeDtypeStruct((CONV_W - 1, nb, Fp), F32)],
        compiler_params=_cp(("parallel",)),
        name="ffn_up_tm")(hn, w_a, w_u, conv_w, conv_b, prefix)


def _ffn_down_kernel(hm_ref, w_ref, h_ref, o_ref):
    o_ref[...] = h_ref[...] + _dot(hm_ref[...], w_ref[...])


def ffn_down(hmid, w_down, h1, tm, tn):
    M = hmid.shape[0]
    F, Dm = w_down.shape
    assert F % LANE == 0 and F <= hmid.shape[1]
    return pl.pallas_call(
        _ffn_down_kernel,
        grid=(M // tm, Dm // tn),
        in_specs=[pl.BlockSpec((tm, F), lambda i, j: (i, 0)),
                  pl.BlockSpec((F, tn), lambda i, j: (0, j)),
                  pl.BlockSpec((tm, tn), lambda i, j: (i, j))],
        out_specs=pl.BlockSpec((tm, tn), lambda i, j: (i, j)),
        out_shape=jax.ShapeDtypeStruct((M, Dm), F32),
        compiler_params=_cp(("parallel", "arbitrary")),
        name="ffn_down")(hmid, w_down, h1)


def _ple_final_kernel(h_ref, p_ref, wg_ref, bg_ref, wp_ref, g_ref, o_ref):
    h = h_ref[...]
    gate = jax.nn.sigmoid(_dot(h.astype(BF16), wg_ref[...]) + bg_ref[...])
    h = h + gate * _dot(p_ref[...].astype(BF16), wp_ref[...])
    o_ref[...] = _rms(h, g_ref[...])


def ple_final(h2, p2, w_gate, b_gate, w_proj, g, tm):
    M, Dm = h2.shape
    Pd = p2.shape[1]
    return pl.pallas_call(
        _ple_final_kernel,
        grid=(M // tm,),
        in_specs=[pl.BlockSpec((tm, Dm), lambda i: (i, 0)),
                  pl.BlockSpec((tm, Pd), lambda i: (i, 0)),
                  pl.BlockSpec((Dm, Dm), lambda i: (0, 0)),
                  pl.BlockSpec((1, Dm), lambda i: (0, 0)),
                  pl.BlockSpec((Pd, Dm), lambda i: (0, 0)),
                  pl.BlockSpec((1, Dm), lambda i: (0, 0))],
        out_specs=pl.BlockSpec((tm, Dm), lambda i: (i, 0)),
        out_shape=jax.ShapeDtypeStruct((M, Dm), F32),
        compiler_params=_cp(("parallel",)),
        name="ple_final")(h2, p2, w_gate, b_gate, w_proj, g)


def _pick_tile(n, prefs):
    for t in prefs:
        if n % t == 0:
            return t
    raise ValueError(f"no tile for {n}")


def _block_matrices(total_keys, nb):
    e = np.zeros((LANE, total_keys), np.float32)
    t = np.arange(total_keys)
    e[t // SLC_LEN, t] = 1.0
    ratio = SLC_LEN // CMP_STRIDE
    m = np.zeros((LANE, LANE), np.float32)
    for n in range(nb):
        for s in range(LANE):
            m[n, s] = float(ratio * s <= n <= ratio * s + ratio - 1) + float(ratio * s - 1 <= n <= ratio * s + ratio - 2)
    return jnp.asarray(e, BF16), jnp.asarray(m, BF16)


def _prep_weights(i, attn_norm_g, w_in, gla_w_a2, gla_b_a, gla_norm_g, cmp_wk1, cmp_wk2, cmp_wv1, cmp_wv2,
                  w_out, ffn_norm_g, ffn_w_up, ffn_conv_w, ffn_conv_b, ffn_w_down, ple_w_proj, ple_w_gate,
                  ple_b_gate):
    Dm = w_in.shape[1]
    F = ffn_w_down.shape[1]
    Fp = -(-F // 512) * 512
    win = w_in[i]
    ga0 = NQ0
    ng0 = ga0 + GLA_RANK + (NMAIN - NQ0)
    ngw = 3 * NSA_HEADS
    before_ga = lax.broadcasted_iota(jnp.int32, (1, NMAIN), 1) < ga0
    w_main = jnp.where(before_ga, win[:, :NMAIN], win[:, GLA_RANK:GLA_RANK + NMAIN]).astype(BF16)
    w_small = jnp.concatenate([win[:, ga0:ga0 + GLA_RANK], win[:, ng0:ng0 + ngw],
                               jnp.zeros((Dm, LANE - GLA_RANK - ngw), F32)], axis=1).astype(BF16)
    wa_pad = jnp.concatenate([gla_w_a2[i], jnp.zeros((LANE - GLA_RANK, GLA_HEADS * GLA_DK), F32)],
                             axis=0).astype(BF16)
    cat = lambda w1: jnp.concatenate([w1[:CMP_STRIDE], w1[CMP_STRIDE:]], axis=-1)
    w1cat = jnp.stack([cat(cmp_wk1[i]), cat(cmp_wv1[i])]).astype(BF16)
    w2s = jnp.stack([cmp_wk2[i], cmp_wv2[i]])
    wup = ffn_w_up[i]
    w_up_a = jnp.pad(wup[:, :F], ((0, 0), (0, Fp - F))).astype(BF16)
    w_up_u = jnp.pad(wup[:, F:], ((0, 0), (0, Fp - F))).astype(BF16)
    conv_w = jnp.pad(ffn_conv_w[i], ((0, SUBLANE - CONV_W), (0, Fp - F)))
    conv_b = jnp.pad(ffn_conv_b[i], (0, Fp - F)).reshape(1, Fp)
    w_down = ffn_w_down[i].astype(BF16)
    return dict(
        g_attn=attn_norm_g[i].reshape(1, Dm), w_main=w_main, w_small=w_small, wa_pad=wa_pad,
        b_a=gla_b_a[i].reshape(1, -1), gnorm=gla_norm_g[i].reshape(1, -1), w1cat=w1cat, w2s=w2s,
        w_out=w_out[i].astype(BF16), g_ffn=ffn_norm_g[i].reshape(1, Dm), w_up_a=w_up_a, w_up_u=w_up_u,
        conv_w=conv_w,
        conv_b=conv_b, w_down=w_down, w_proj=ple_w_proj[i].astype(BF16), w_gate=ple_w_gate[i].astype(BF16),
        b_gate=ple_b_gate[i].reshape(1, Dm), F=F, Fp=Fp)


def _kv_rows(kv, B, rows):
    t = kv.reshape(kv.shape[0], B, -1, NSA_KV, NSA_HD)[:, :, rows]
    return [t[i] for i in range(kv.shape[0])]


def _prompt_layer(h2d, B, L, W):
    Dm = h2d.shape[1]
    M = B * L
    proj, small, kv = norm_proj(h2d, W["g_attn"], W["w_main"], W["w_small"],
                                _pick_tile(M, (1024, 512, 256, 128)), 512)
    proj3 = proj.reshape(B, L, NMAIN)
    small3 = small.reshape(B, L, LANE)
    cb = _pick_tile(L, (128, 64, 32, 16))
    s0 = jnp.zeros((B, GLA_HEADS, GLA_DK, GLA_DV), F32)
    og, gla_state = gla(proj3, small3, W["wa_pad"], W["b_a"], W["gnorm"], s0, cb, GLA_SUB, cb,
                        _pick_tile(B, (4, 2, 1)))
    cmp = compress(proj3, W["w1cat"], W["w2s"])
    _, mmat = _block_matrices(L, L // CMP_STRIDE - 1)
    on = nsa_prompt(proj3, small3, cmp, mmat.T)
    h1, hn = out_proj(og.reshape(M, -1), on.reshape(M, -1), h2d, W["w_out"], W["g_ffn"],
                      _pick_tile(M, (512, 256, 128)))
    prefix = jnp.zeros((B, SUBLANE, W["Fp"]), F32)
    hmid, tail = ffn_up_seq(hn, W["w_up_a"], W["w_up_u"], W["conv_w"], W["conv_b"], prefix, L,
                            _pick_tile(L, (1024, 512, 256, 128)), 512)
    h2 = ffn_down(hmid, W["w_down"], h1, _pick_tile(M, (1024, 512, 256, 128)), 512)
    keep = min(WINDOW, L)
    outs = tuple(_kv_rows(kv[:4], B, slice(0, L)) + _kv_rows(kv[4:], B, slice(L - keep, L))
                 + [gla_state, tail[:, SUBLANE - (CONV_W - 1):, :W["F"]]])
    return h2, outs


def _decode_layer(h3d, B, L, lp, caches, page_table, win_k, win_v, gla_s, conv_s, W):
    Dm = h3d.shape[2]
    Mp = B * lp
    x2 = h3d.reshape(Mp, Dm)
    proj, small, kv = norm_proj(x2, W["g_attn"], W["w_main"], W["w_small"],
                                _pick_tile(Mp, (1024, 512, 256, 128, 64, 32, 16, 8)), 512)
    proj3 = proj.reshape(B, lp, NMAIN)
    small3 = small.reshape(B, lp, LANE)
    og, gla_state = gla(proj3, small3, W["wa_pad"], W["b_a"], W["gnorm"], gla_s, lp, lp, L,
                        _pick_tile(B, (4, 2, 1)))
    plen = page_table.shape[1] * caches[0].shape[1] // NSA_KV
    emat, mmat = _block_matrices(plen + KCHUNK, (plen + L) // CMP_STRIDE - 1)
    wlen = win_k.shape[1]
    on, wk_new, wv_new = nsa_decode(proj3, small3, caches, page_table,
                                    win_k.reshape(B, wlen * NSA_KV, NSA_HD),
                                    win_v.reshape(B, wlen * NSA_KV, NSA_HD),
                                    W["w1cat"], W["w2s"], emat, mmat, L, _pick_tile(B, (2, 1)))
    h1, hn = out_proj(og.reshape(Mp, -1), on.reshape(Mp, -1), x2, W["w_out"], W["g_ffn"],
                      _pick_tile(Mp, (256, 128, 64, 32, 16, 8)))
    to_tm = lambda t: t.reshape(B, lp, Dm)[:, :L].transpose(1, 0, 2).reshape(L * B, Dm)
    h1_tm, hn_tm = to_tm(h1), to_tm(hn)
    prefix = jnp.pad(conv_s.transpose(1, 0, 2), ((0, 0), (0, 0), (0, W["Fp"] - W["F"])))
    hmid, tail = ffn_up_tm(hn_tm, W["w_up_a"], W["w_up_u"], W["conv_w"], W["conv_b"], prefix, L, 512)
    M = L * B
    h2 = ffn_down(hmid, W["w_down"], h1_tm, _pick_tile(M, (512, 256, 128, 64, 32, 16, 8)), 512)
    outs = tuple(_kv_rows(kv[:4], B, slice(0, L))
                 + [wk_new.reshape(win_k.shape), wv_new.reshape(win_v.shape),
                    gla_state, tail[:, :, :W["F"]].transpose(1, 0, 2)])
    return h2, outs


def kernel(x_prompt, x_sample, p_prompt, p_sample, cache_cmp_k, cache_cmp_v, cache_slc_k, cache_slc_v,
           page_table, state_win_k, state_win_v, state_gla, state_ffn_conv, attn_norm_g, w_in, gla_w_a2,
           gla_b_a, gla_norm_g, cmp_wk1, cmp_wk2, cmp_wv1, cmp_wv2, w_out, ffn_norm_g, ffn_w_up,
           ffn_conv_w, ffn_conv_b, ffn_w_down, ple_w_proj, ple_w_gate, ple_b_gate, final_norm_g):
    depth = w_in.shape[0]
    assert depth == 1, "layers are chained through HBM one at a time; only depth 1 is wired"
    Bp, Lp, Dm = x_prompt.shape
    Bs, Ls, _ = x_sample.shape
    W = _prep_weights(0, attn_norm_g, w_in, gla_w_a2, gla_b_a, gla_norm_g, cmp_wk1, cmp_wk2, cmp_wv1,
                      cmp_wv2, w_out, ffn_norm_g, ffn_w_up, ffn_conv_w, ffn_conv_b, ffn_w_down,
                      ple_w_proj, ple_w_gate, ple_b_gate)
    g_final = final_norm_g.reshape(1, Dm)

    Mp = Bp * Lp
    h2, outs_p = _prompt_layer(x_prompt.reshape(Mp, Dm), Bp, Lp, W)
    y_prompt = ple_final(h2, p_prompt[0].reshape(Mp, -1), W["w_gate"], W["b_gate"], W["w_proj"], g_final,
                         _pick_tile(Mp, (512, 256, 128))).reshape(Bp, Lp, Dm)

    lp = -(-Ls // SUBLANE) * SUBLANE
    xs = jnp.pad(x_sample, ((0, 0), (0, lp - Ls), (0, 0)))
    n_pool, psz = cache_cmp_k.shape[1], cache_cmp_k.shape[2]
    caches = [c[0].reshape(n_pool, psz * NSA_KV, NSA_HD)
              for c in (cache_cmp_k, cache_cmp_v, cache_slc_k, cache_slc_v)]
    h2s, outs_s = _decode_layer(xs, Bs, Ls, lp, caches, page_table, state_win_k[0], state_win_v[0],
                                state_gla[0], state_ffn_conv[0], W)
    p_tm = p_sample[0].transpose(1, 0, 2).reshape(Ls * Bs, -1)
    Ms = Ls * Bs
    y_tm = ple_final(h2s, p_tm, W["w_gate"], W["b_gate"], W["w_proj"], g_final,
                     _pick_tile(Ms, (256, 128, 64, 32, 16, 8)))
    y_sample = y_tm.reshape(Ls, Bs, Dm).transpose(1, 0, 2)

    lead = lambda t: t[None]
    return (y_prompt, y_sample) + tuple(lead(t) for t in outs_p) + tuple(lead(t) for t in outs_s)
```

```python
import functools

import numpy as np
import jax
import jax.numpy as jnp
from jax import lax
from jax.experimental import pallas as pl
from jax.experimental.pallas import tpu as pltpu

F32 = jnp.float32
BF16 = jnp.bfloat16

GLA_HEADS = 4
GLA_DK = 128
GLA_DV = 256
GLA_RANK = 16
GLA_TAU = 16.0
GLA_SUB = 16
NSA_HEADS = 8
NSA_KV = 2
NSA_REP = NSA_HEADS // NSA_KV
NSA_HD = 128
CMP_LEN = 32
CMP_STRIDE = 16
SLC_LEN = 64
N_SELECT = 16
WINDOW = 512
CONV_W = 3
EPS = 1e-6
NEG = -1e30
M_FLOOR = 0.1 * NEG
FORCED = 1e6

LANE = 128
SUBLANE = 8
KCHUNK = 128
CMP_SEG_ROWS = NSA_KV * CMP_STRIDE
CMP_PITCH = CMP_SEG_ROWS + SUBLANE
VMEM_LIMIT = 56 * 1024 * 1024

GQ0 = 0
GK0 = GQ0 + GLA_HEADS * GLA_DK
GV0 = GK0 + GLA_HEADS * GLA_DK
GR0 = GV0 + GLA_HEADS * GLA_DV
NQ0 = GR0 + GLA_HEADS * GLA_DV
KVW = NSA_KV * NSA_HD
KC0 = NQ0 + NSA_HEADS * NSA_HD
VC0 = KC0 + KVW
KS0 = VC0 + KVW
VS0 = KS0 + KVW
KW0 = VS0 + KVW
VW0 = KW0 + KVW
NMAIN = VW0 + KVW
GATE0 = GLA_RANK


def _cp(sem):
    return pltpu.CompilerParams(dimension_semantics=sem, vmem_limit_bytes=VMEM_LIMIT)


def _dot(a, b):
    return jnp.dot(a, b, preferred_element_type=F32)


def _dot_nt(a, b):
    return lax.dot_general(a, b, (((1,), (1,)), ((), ())), preferred_element_type=F32)


def _rms(x, g):
    return x * lax.rsqrt(jnp.mean(x * x, axis=-1, keepdims=True) + EPS) * g


def _norm_proj_kernel(x_ref, g_ref, wm_ref, ws_ref, om_ref, os_ref, kv_ref, xn_ref, *, j_kv0):
    j = pl.program_id(1)

    @pl.when(j == 0)
    def _():
        xn = _rms(x_ref[...], g_ref[...]).astype(BF16)
        xn_ref[...] = xn
        os_ref[...] = _dot(xn, ws_ref[...])

    tile = _dot(xn_ref[...], wm_ref[...])
    om_ref[...] = tile

    @pl.when(j >= j_kv0)
    def _():
        tm = tile.shape[0]
        for a in range(tile.shape[1] // KVW):
            for g in range(NSA_KV):
                c0 = a * KVW + g * NSA_HD
                kv_ref[a, pl.ds(g, tm, stride=NSA_KV), :] = tile[:, c0:c0 + NSA_HD]


def norm_proj(x2, g, w_main, w_small, tm, tn):
    M, Dm = x2.shape
    N = w_main.shape[1]
    assert KC0 % tn == 0 and tn % KVW == 0
    j_kv0 = KC0 // tn
    per_tile = tn // KVW
    n_kv = (N - KC0) // KVW
    return pl.pallas_call(
        functools.partial(_norm_proj_kernel, j_kv0=j_kv0),
        grid=(M // tm, N // tn),
        in_specs=[pl.BlockSpec((tm, Dm), lambda i, j: (i, 0)),
                  pl.BlockSpec((1, Dm), lambda i, j: (0, 0)),
                  pl.BlockSpec((Dm, tn), lambda i, j: (0, j)),
                  pl.BlockSpec((Dm, LANE), lambda i, j: (0, 0))],
        out_specs=[pl.BlockSpec((tm, tn), lambda i, j: (i, j)),
                   pl.BlockSpec((tm, LANE), lambda i, j: (i, 0)),
                   pl.BlockSpec((per_tile, NSA_KV * tm, NSA_HD),
                                lambda i, j: (jnp.maximum(j - j_kv0, 0), i, 0))],
        out_shape=[jax.ShapeDtypeStruct((M, N), F32), jax.ShapeDtypeStruct((M, LANE), F32),
                   jax.ShapeDtypeStruct((n_kv, NSA_KV * M, NSA_HD), F32)],
        scratch_shapes=[pltpu.VMEM((tm, Dm), BF16)],
        compiler_params=_cp(("parallel", "arbitrary")),
        name="norm_proj")(x2, g, w_main, w_small)


def _gla_kernel(q_ref, k_ref, v_ref, r_ref, sm_ref, wa_ref, ba_ref, gn_ref, s0_ref,
                o_ref, s_ref, st_ref, *, cb, sub, valid, nbb):
    c = pl.program_id(1)

    @pl.when(c == 0)
    def _():
        st_ref[...] = s0_ref[...]

    GH = GLA_HEADS
    H = nbb * GH
    side = lambda ref: ref[0] if nbb == 1 else jnp.concatenate([ref[s] for s in range(nbb)], axis=1)
    hk = lambda h: slice(h * GLA_DK, (h + 1) * GLA_DK)
    hv = lambda h: slice(h * GLA_DV, (h + 1) * GLA_DV)
    ri = lax.broadcasted_iota(jnp.int32, (cb, cb), 0)
    ci = lax.broadcasted_iota(jnp.int32, (cb, cb), 1)
    tri = jnp.where(ci <= ri, 1.0, 0.0)
    rowv = lax.broadcasted_iota(jnp.int32, (cb, 1), 0)
    q = side(q_ref) * (GLA_DK ** -0.5)
    k = side(k_ref)
    v = side(v_ref)
    pre = [_dot(sm_ref[s].astype(BF16), wa_ref[...]) + ba_ref[...] for s in range(nbb)]
    pre = pre[0] if nbb == 1 else jnp.concatenate(pre, axis=1)
    log_a = jax.nn.log_sigmoid(pre) / GLA_TAU
    if valid < cb:
        log_a = jnp.where(rowv < valid, log_a, 0.0)
        k = jnp.where(rowv < valid, k, 0.0)
        v = jnp.where(rowv < valid, v, 0.0)
    hi = log_a.astype(BF16).astype(F32)
    lo = (log_a - hi).astype(BF16).astype(F32)
    b = _dot(tri, hi) + _dot(tri, lo)
    b_last = b[cb - 1:cb, :]
    S = [st_ref[h // GH, h % GH] for h in range(H)]
    vb = v.astype(BF16)

    qe = (q * jnp.exp(b)).astype(BF16)
    o = [_dot(qe[:, hk(h)], S[h].astype(BF16)) for h in range(H)]
    a_rows = [[] for _ in range(H)]
    cc = lax.broadcasted_iota(jnp.int32, (sub, cb), 1)
    for blk in range(cb // sub):
        lo_r, hi_r = blk * sub, (blk + 1) * sub
        r_dec = jnp.zeros((1, H * GLA_DK), F32) if blk == 0 else b[lo_r - 1:lo_r, :]
        q_i = q[lo_r:hi_r] * jnp.exp(b[lo_r:hi_r] - r_dec)
        k_i = k * jnp.exp(jnp.where(rowv < hi_r, r_dec - b, 0.0))
        rr = lax.broadcasted_iota(jnp.int32, (sub, cb), 0) + lo_r
        for h in range(H):
            a = _dot_nt(q_i[:, hk(h)], k_i[:, hk(h)])
            a_rows[h].append(jnp.where(cc <= rr, a, 0.0))
    for h in range(H):
        a = a_rows[h][0] if len(a_rows[h]) == 1 else jnp.concatenate(a_rows[h], axis=0)
        o[h] = o[h] + _dot(a, v[:, hv(h)])

    kh = k * jnp.exp(b_last - b)
    dec = jnp.exp(b_last)
    if cb < LANE:
        kh = jnp.concatenate([kh, jnp.zeros((LANE - cb, H * GLA_DK), F32)], axis=0)
        vpad = jnp.concatenate([vb, jnp.zeros((LANE - cb, H * GLA_DV), BF16)], axis=0)
    else:
        vpad = vb
    for h in range(H):
        dec_col = jnp.transpose(jnp.broadcast_to(dec[:, hk(h)], (GLA_DK, GLA_DK)))[:, 0:1]
        s_new = dec_col * S[h] + _dot(jnp.transpose(kh[:, hk(h)]).astype(BF16), vpad[:, hv(h)])
        st_ref[h // GH, h % GH] = s_new
        s_ref[h // GH, h % GH] = s_new

    r = side(r_ref)
    gate = r * jax.nn.sigmoid(r)
    for h in range(H):
        o_ref[h // GH, :, hv(h % GH)] = (_rms(o[h], gn_ref[...]) * gate[:, hv(h)]).astype(o_ref.dtype)


def gla(proj3, small3, wa_pad, b_a, gnorm, s0, cb, sub, valid, nbb):
    B, L, _ = proj3.shape
    H = GLA_HEADS
    qw, vw = H * GLA_DK, H * GLA_DV
    kern = functools.partial(_gla_kernel, cb=cb, sub=sub, valid=valid, nbb=nbb)
    return pl.pallas_call(
        kern,
        grid=(B // nbb, L // cb),
        in_specs=[pl.BlockSpec((nbb, cb, qw), lambda b, c: (b, c, GQ0 // qw)),
                  pl.BlockSpec((nbb, cb, qw), lambda b, c: (b, c, GK0 // qw)),
                  pl.BlockSpec((nbb, cb, vw), lambda b, c: (b, c, GV0 // vw)),
                  pl.BlockSpec((nbb, cb, vw), lambda b, c: (b, c, GR0 // vw)),
                  pl.BlockSpec((nbb, cb, LANE), lambda b, c: (b, c, 0)),
                  pl.BlockSpec((LANE, qw), lambda b, c: (0, 0)),
                  pl.BlockSpec((1, qw), lambda b, c: (0, 0)),
                  pl.BlockSpec((1, GLA_DV), lambda b, c: (0, 0)),
                  pl.BlockSpec((nbb, H, GLA_DK, GLA_DV), lambda b, c: (b, 0, 0, 0))],
        out_specs=[pl.BlockSpec((nbb, cb, vw), lambda b, c: (b, c, 0)),
                   pl.BlockSpec((nbb, H, GLA_DK, GLA_DV), lambda b, c: (b, 0, 0, 0))],
        out_shape=[jax.ShapeDtypeStruct((B, L, vw), BF16),
                   jax.ShapeDtypeStruct((B, H, GLA_DK, GLA_DV), F32)],
        scratch_shapes=[pltpu.VMEM((nbb, H, GLA_DK, GLA_DV), F32)],
        compiler_params=_cp(("parallel", "arbitrary")),
        name="gla")(proj3, proj3, proj3, proj3, small3, wa_pad, b_a, gnorm, s0)


def _pair_rows(load_rows, j):
    return jnp.concatenate([load_rows(j), load_rows(j + 1)], axis=1).astype(BF16)


def _pair_weights(w1_ref, j):
    w = w1_ref[pl.ds(j, 2)]
    return w.reshape(2 * NSA_HD, 2 * NSA_HD)


def _compress_group(load_rows, w1_ref, w2, nseg):
    acc = jnp.zeros((nseg, 2 * NSA_HD), F32)
    for j in range(0, CMP_STRIDE, 2):
        acc = acc + _dot(_pair_rows(load_rows, j), _pair_weights(w1_ref, j))
    first = acc[:, :NSA_HD]
    second = acc[:, NSA_HD:]
    h = jax.nn.gelu(first + pltpu.roll(second, nseg - 1, axis=0))
    return _dot(h.astype(BF16), w2)


def _compress_groups(loads, w1_refs, w2s, nrow):
    n = len(loads)
    acc = [jnp.zeros((nrow, 2 * NSA_HD), F32) for _ in range(n)]
    for j in range(0, CMP_STRIDE, 2):
        for i in range(n):
            acc[i] = acc[i] + _dot(_pair_rows(loads[i], j), _pair_weights(w1_refs[i], j))
    hs = [jax.nn.gelu(a[:, :NSA_HD] + pltpu.roll(a[:, NSA_HD:], nrow - 1, axis=0)) for a in acc]
    return [_dot(h.astype(BF16), w2) for h, w2 in zip(hs, w2s)]


def _cmp_branch_many(q4s, cks, cvs, qpos4, nb):
    s = [_dot_nt(q4, ck.astype(BF16)) for q4, ck in zip(q4s, cks)]
    n = lax.broadcasted_iota(jnp.int32, s[0].shape, 1)
    mask = jnp.where(n * CMP_STRIDE + (CMP_LEN - 1) <= qpos4, n, nb) < nb
    s = [jnp.where(mask, si, NEG) for si in s]
    m = [jnp.max(si, axis=-1, keepdims=True) for si in s]
    e = [jnp.where(mask, jnp.exp(si - mi), 0.0) for si, mi in zip(s, m)]
    p = [ei / jnp.maximum(jnp.sum(ei, axis=-1, keepdims=True), 1e-30) for ei in e]
    o = [_dot(pi.astype(BF16), cv.astype(BF16)) for pi, cv in zip(p, cvs)]
    return list(zip(p, o))


def _select(pg, mmat, qpos_tok, nslc, nsel):
    hi = pg.astype(BF16)
    lo = (pg - hi.astype(F32)).astype(BF16)
    imp = _dot(hi, mmat) + _dot(lo, mmat)
    blk = lax.broadcasted_iota(jnp.int32, pg.shape, 1)
    cur = qpos_tok // SLC_LEN
    forced = jnp.where(blk == 0, 1, 0) + jnp.where(blk == cur, 1, 0) + jnp.where(blk == cur - 1, 1, 0)
    score = jnp.where(blk <= cur, jnp.where(forced > 0, FORCED, imp), -1.0)
    score = jnp.where(blk < nslc, score, -2.0)
    rank = jnp.zeros(pg.shape, F32)
    for j in range(nslc):
        sj = score[:, j:j + 1]
        tie = jnp.where(blk > j, sj, NEG)
        rank = rank + jnp.where(sj > score, 1.0, 0.0) + jnp.where(tie == score, 1.0, 0.0)
    return jnp.where(rank < nsel, jnp.where(blk < nslc, 1.0, 0.0), 0.0)


def _tile_rows(x, reps):
    return jnp.concatenate([x] * reps, axis=0)


def _attend_many(jobs):
    s = [[jnp.where(mk, _dot_nt(q4, k), NEG) for k, mk in zip(keys, masks)] for q4, keys, _, masks in jobs]
    m = []
    for sj in s:
        mj = sj[0].max(axis=-1, keepdims=True)
        for si in sj[1:]:
            mj = jnp.maximum(mj, si.max(axis=-1, keepdims=True))
        m.append(mj)
    e = [[jnp.where(mk, jnp.exp(si - mj), 0.0) for si, mk in zip(sj, job[3])]
         for sj, mj, job in zip(s, m, jobs)]
    l = [sum(ei.sum(axis=-1, keepdims=True) for ei in ej) for ej in e]
    o = [sum(_dot(ei.astype(BF16), v) for ei, v in zip(ej, job[2])) for ej, job in zip(e, jobs)]
    return [oj / jnp.maximum(lj, 1e-30) for oj, lj in zip(o, l)]


def _compress_kernel(x_ref, w1_ref, w2_ref, o_ref, *, nseg):
    load = lambda j: x_ref[0, pl.ds(j, nseg, stride=CMP_STRIDE), :]
    o_ref[0, 0, 0] = _compress_group(load, w1_ref.at[0], w2_ref[0].astype(BF16), nseg)


def compress(proj3, w1cat, w2s):
    B, L, _ = proj3.shape
    nseg = L // CMP_STRIDE
    G = NSA_KV
    return pl.pallas_call(
        functools.partial(_compress_kernel, nseg=nseg),
        grid=(B, 2, G),
        in_specs=[pl.BlockSpec((1, L, NSA_HD), lambda b, w, g: (b, 0, KC0 // NSA_HD + w * G + g)),
                  pl.BlockSpec((1, CMP_STRIDE, NSA_HD, 2 * NSA_HD), lambda b, w, g: (w, 0, 0, 0)),
                  pl.BlockSpec((1, NSA_HD, NSA_HD), lambda b, w, g: (w, 0, 0))],
        out_specs=pl.BlockSpec((1, 1, 1, nseg, NSA_HD), lambda b, w, g: (w, b, g, 0, 0)),
        out_shape=jax.ShapeDtypeStruct((2, B, G, nseg, NSA_HD), F32),
        compiler_params=_cp(("parallel", "parallel", "parallel")),
        name="compress")(proj3, w1cat, w2s)


def _select_t(pg_t, mm_t, qpos_row, nslc, nsel):
    nsp = -(-nslc // SUBLANE) * SUBLANE
    hi = pg_t.astype(BF16)
    lo = (pg_t - hi.astype(F32)).astype(BF16)
    imp = (_dot(mm_t, hi) + _dot(mm_t, lo))[:nsp]
    blk = lax.broadcasted_iota(jnp.int32, imp.shape, 0)
    cur = qpos_row // SLC_LEN
    forced = jnp.where(blk == 0, 1, 0) + jnp.where(blk == cur, 1, 0) + jnp.where(blk == cur - 1, 1, 0)
    score = jnp.where(blk <= cur, jnp.where(forced > 0, FORCED, imp), -1.0)
    score = jnp.where(blk < nslc, score, -2.0)
    rank = jnp.zeros(imp.shape, F32)
    for j in range(nslc):
        sj = score[j:j + 1, :]
        tie = jnp.where(blk > j, sj, NEG)
        rank = rank + jnp.where(sj > score, 1.0, 0.0) + jnp.where(tie == score, 1.0, 0.0)
    return jnp.where(rank < nsel, jnp.where(blk < nslc, 1.0, 0.0), 0.0)


def _flash_steps_t(q_t, jobs):
    s = [_dot(k, q_t) + bias for k, _, bias, _ in jobs]
    m_old = [st[0][...] for _, _, _, st in jobs]
    m_new = [jnp.maximum(mo, jnp.max(si, axis=0, keepdims=True)) for mo, si in zip(m_old, s)]
    alpha = [jnp.exp(mo - mn) for mo, mn in zip(m_old, m_new)]
    p = [jnp.exp(si - mn) for si, mn in zip(s, m_new)]
    pv = [_dot(v_t, pi.astype(BF16)) for (_, v_t, _, _), pi in zip(jobs, p)]
    for (_, _, _, (m_ref, l_ref, acc_ref)), mn, al, pi, pvi in zip(jobs, m_new, alpha, p, pv):
        l_ref[...] = al * l_ref[...] + jnp.sum(pi, axis=0, keepdims=True)
        acc_ref[...] = al * acc_ref[...] + pvi
        m_ref[...] = mn


def _nsa_prompt_kernel(q_ref, sm_ref, ck_ref, cv_ref, ks_ref, vs_ref, kw_ref, vw_ref, mm_ref,
                       o_ref, vst_ref, vwt_ref, sel_ref, m_ref, l_ref, acc_ref, mw_ref, lw_ref, accw_ref,
                       *, tq, kstep, nslc, nsel, nb):
    g = pl.program_id(1)
    qb = pl.program_id(2)
    R = NSA_REP
    T = R * tq
    L = ks_ref.shape[1]
    scale = NSA_HD ** -0.5

    @pl.when(qb == 0)
    def _():
        for c in range(L // KCHUNK):
            cs = slice(c * KCHUNK, (c + 1) * KCHUNK)
            vst_ref[:, cs] = jnp.transpose(vs_ref[0, cs, :]).astype(BF16)
            vwt_ref[:, cs] = jnp.transpose(vw_ref[0, cs, :]).astype(BF16)

    q_t = (jnp.concatenate([jnp.transpose(q_ref[0, :, r * NSA_HD:(r + 1) * NSA_HD]) for r in range(R)],
                           axis=1) * scale).astype(BF16)
    qpos_row = qb * tq + lax.broadcasted_iota(jnp.int32, (1, tq), 1)
    qpos4 = jnp.concatenate([qpos_row] * R, axis=1)

    key = lax.broadcasted_iota(jnp.int32, (kstep, T), 0)
    hi_step = (qb * tq + tq - 1) // kstep + 1
    lo_step = jnp.maximum(qb * tq - (WINDOW - 1), 0) // kstep
    last = hi_step - 1
    sel_state = (m_ref, l_ref, acc_ref)
    win_state = (mw_ref, lw_ref, accw_ref)

    def sel_step(c, causal):
        off = pl.multiple_of(c * kstep, kstep)
        k = ks_ref[0, pl.ds(off, kstep), :].astype(BF16)
        v_t = vst_ref[:, pl.ds(off, kstep)]
        blocks = [jnp.broadcast_to(sel_ref[pl.ds((kstep // SLC_LEN) * c + i, 1), :], (SLC_LEN, tq))
                  for i in range(kstep // SLC_LEN)]
        bias = jnp.concatenate(blocks, axis=0)
        bias = jnp.concatenate([bias] * R, axis=1)
        if causal:
            bias = jnp.where(off + key <= qpos4, bias, NEG)
        return k, v_t, bias, sel_state

    def win_step(c, causal):
        off = pl.multiple_of(c * kstep, kstep)
        k = kw_ref[0, pl.ds(off, kstep), :].astype(BF16)
        v_t = vwt_ref[:, pl.ds(off, kstep)]
        kpos = off + key
        if causal:
            bias = jnp.where(kpos <= qpos4, 0.0, NEG)
        else:
            bias = jnp.where(kpos > qpos4 - WINDOW, 0.0, NEG)
        return k, v_t, bias, win_state

    k_d, vt_d, bias_d, _ = win_step(last, True)
    s = _dot(ck_ref[0, 0, 0].astype(BF16), q_t)
    s_d = _dot(k_d, q_t) + bias_d
    n = lax.broadcasted_iota(jnp.int32, s.shape, 0)
    mask = jnp.where(n * CMP_STRIDE + (CMP_LEN - 1) <= qpos4, n, nb) < nb
    s = jnp.where(mask, s, NEG)
    m_c = jnp.max(s, axis=0, keepdims=True)
    m_d = jnp.maximum(jnp.max(s_d, axis=0, keepdims=True), M_FLOOR)
    e = jnp.where(mask, jnp.exp(s - m_c), 0.0)
    p_d = jnp.exp(s_d - m_d)
    p = e / jnp.maximum(jnp.sum(e, axis=0, keepdims=True), 1e-30)
    lw_ref[...] = jnp.sum(p_d, axis=0, keepdims=True)
    o_c = _dot(jnp.transpose(cv_ref[0, 0, 0]).astype(BF16), p.astype(BF16))
    accw_ref[...] = _dot(vt_d, p_d.astype(BF16))
    mw_ref[...] = m_d
    pg = p[:, 0:tq]
    for r in range(1, R):
        pg = pg + p[:, r * tq:(r + 1) * tq]
    sel_ref[...] = (_select_t(pg, mm_ref[...], qpos_row, nslc, nsel) - 1.0) * (-NEG)

    m_ref[...] = jnp.full(m_ref.shape, M_FLOOR, F32)
    l_ref[...] = jnp.zeros(l_ref.shape, F32)
    acc_ref[...] = jnp.zeros(acc_ref.shape, F32)

    def body_sel(c, carry):
        _flash_steps_t(q_t, [sel_step(c, False)])
        return carry

    def body_both(c, carry):
        _flash_steps_t(q_t, [sel_step(c, False), win_step(c, False)])
        return carry

    lax.fori_loop(0, jnp.minimum(lo_step, last), body_sel, 0)
    lax.fori_loop(lo_step, last, body_both, 0)
    _flash_steps_t(q_t, [sel_step(last, True)])
    o_s = acc_ref[...] / jnp.maximum(l_ref[...], 1e-30)
    o_w = accw_ref[...] / jnp.maximum(lw_ref[...], 1e-30)

    g_t = jnp.transpose(jax.nn.sigmoid(sm_ref[0]))
    for r in range(R):
        def gate(br, r=r):
            c0 = GATE0 + br * NSA_HEADS + r
            return jnp.where(g == 0, g_t[c0:c0 + 1, :], g_t[c0 + R:c0 + R + 1, :])
        cs = slice(r * tq, (r + 1) * tq)
        o = gate(0) * o_c[:, cs] + gate(1) * o_s[:, cs] + gate(2) * o_w[:, cs]
        o_ref[0, :, r * NSA_HD:(r + 1) * NSA_HD] = jnp.transpose(o).astype(o_ref.dtype)


def nsa_prompt(proj3, small3, cmp, mmat_t):
    B, L, _ = proj3.shape
    tq = _pick_tile(L, (2 * KCHUNK, KCHUNK))
    nseg = L // CMP_STRIDE
    nb = nseg - 1
    nslc = -(-L // SLC_LEN)
    nsel = min(N_SELECT, nslc)
    nsp = -(-nslc // SUBLANE) * SUBLANE
    G, R = NSA_KV, NSA_REP
    kstep = _pick_tile(L, (2 * KCHUNK, KCHUNK))
    assert nseg == LANE and L % KCHUNK == 0 and kstep % SLC_LEN == 0
    kv_spec = lambda c0: pl.BlockSpec((1, L, NSA_HD), lambda b, g, qb: (b, 0, c0 // NSA_HD + g))
    kern = functools.partial(_nsa_prompt_kernel, tq=tq, kstep=kstep, nslc=nslc, nsel=nsel, nb=nb)
    return pl.pallas_call(
        kern,
        grid=(B, G, L // tq),
        in_specs=[pl.BlockSpec((1, tq, R * NSA_HD), lambda b, g, qb: (b, qb, NQ0 // (R * NSA_HD) + g)),
                  pl.BlockSpec((1, tq, LANE), lambda b, g, qb: (b, qb, 0)),
                  pl.BlockSpec((1, 1, 1, nseg, NSA_HD), lambda b, g, qb: (0, b, g, 0, 0)),
                  pl.BlockSpec((1, 1, 1, nseg, NSA_HD), lambda b, g, qb: (1, b, g, 0, 0)),
                  kv_spec(KS0), kv_spec(VS0), kv_spec(KW0), kv_spec(VW0),
                  pl.BlockSpec((LANE, LANE), lambda b, g, qb: (0, 0))],
        out_specs=pl.BlockSpec((1, tq, R * NSA_HD), lambda b, g, qb: (b, qb, g)),
        out_shape=jax.ShapeDtypeStruct((B, L, NSA_HEADS * NSA_HD), BF16),
        scratch_shapes=[pltpu.VMEM((NSA_HD, L), BF16),
                        pltpu.VMEM((NSA_HD, L), BF16),
                        pltpu.VMEM((nsp, tq), F32)]
                       + 2 * [pltpu.VMEM((1, R * tq), F32),
                              pltpu.VMEM((1, R * tq), F32),
                              pltpu.VMEM((NSA_HD, R * tq), F32)],
        compiler_params=_cp(("parallel", "parallel", "arbitrary")),
        name="nsa_prompt")(proj3, small3, cmp, cmp, proj3, proj3, proj3, proj3, mmat_t)


def _nsa_decode_kernel(pt_ref, q_ref, sm_ref, ksn_ref, vsn_ref, kwn_ref, vwn_ref, wk_ref, wv_ref,
                       w1_ref, w2_ref, e_ref, mm_ref, c0_hbm, c1_hbm, c2_hbm, c3_hbm,
                       o_ref, wko_ref, wvo_ref, cbuf_ref, sbuf_ref, sem_ref,
                       *, ns, lp, lreal, plen, rows_pp, n_pages, wlen, nslc, nsel, nb):
    b = pl.program_id(0)
    slot = b % 2
    caches = (c0_hbm, c1_hbm, c2_hbm, c3_hbm)
    G = NSA_KV
    R = NSA_REP
    T = R * lp
    segs_pp = rows_pp // CMP_SEG_ROWS

    def copies(step, sl):
        out = []
        for q in range(ns):
            for p in range(n_pages):
                page = pt_ref[step * ns + q, p]
                for w in range(2):
                    for s in range(segs_pp):
                        out.append(pltpu.make_async_copy(
                            caches[w].at[page, pl.ds(s * CMP_SEG_ROWS, CMP_SEG_ROWS)],
                            cbuf_ref.at[sl, q, w, pl.ds((p * segs_pp + s) * CMP_PITCH, CMP_SEG_ROWS)],
                            sem_ref.at[sl, w]))
                    out.append(pltpu.make_async_copy(
                        caches[2 + w].at[page], sbuf_ref.at[sl, q, w, pl.ds(p * rows_pp, rows_pp)],
                        sem_ref.at[sl, 2 + w]))
        return out

    @pl.when(b == 0)
    def _():
        for cp in copies(0, 0):
            cp.start()

    @pl.when(b + 1 < pl.num_programs(0))
    def _():
        for cp in copies(b + 1, 1 - slot):
            cp.start()

    for w in range(len(caches)):
        for q in range(ns):
            whole = sbuf_ref.at[slot, q, 0]
            pltpu.make_async_copy(whole, whole, sem_ref.at[slot, w]).wait()

    scale = NSA_HD ** -0.5
    nseg = plen // CMP_STRIDE
    tok = lax.broadcasted_iota(jnp.int32, (lp, 1), 0)
    qpos_tok = plen + tok
    qpos4 = _tile_rows(qpos_tok, R)
    lane_n = lax.broadcasted_iota(jnp.int32, (T, KCHUNK), 1)
    new_pos = jnp.where(lane_n < lreal, plen + lane_n, qpos4 + 1)
    past_pos = lax.broadcasted_iota(jnp.int32, (T, plen), 1)
    win_pos = plen - wlen + lax.broadcasted_iota(jnp.int32, (T, wlen), 1)
    win_pos = jnp.where(win_pos > qpos4 - WINDOW, win_pos, qpos4 + 1)
    win_pos = jnp.where(win_pos >= 0, win_pos, qpos4 + 1)
    new_win_pos = jnp.where(new_pos > qpos4 - WINDOW, new_pos, qpos4 + 1)
    zpad_f = jnp.zeros((KCHUNK - lp, NSA_HD), F32)
    pad_new = lambda ref, q, ls: jnp.concatenate([ref[q, :, ls], zpad_f], axis=0).astype(BF16)
    gls = lambda g: slice(g * NSA_HD, (g + 1) * NSA_HD)
    pairs = [(q, g) for q in range(ns) for g in range(G)]
    q4 = [(jnp.concatenate([q_ref[q, :, (g * R + r) * NSA_HD:(g * R + r + 1) * NSA_HD]
                            for r in range(R)], axis=0) * scale).astype(BF16) for q, g in pairs]

    def seg_rows(w, j):
        return jnp.concatenate([cbuf_ref[slot, q, w, pl.ds(G * j + g, nseg, stride=CMP_PITCH), :]
                                for q, g in pairs], axis=0)
    cmp = _compress_groups([functools.partial(seg_rows, w) for w in range(2)],
                           [w1_ref.at[w] for w in range(2)],
                           [w2_ref[w].astype(BF16) for w in range(2)], len(pairs) * nseg)
    prow = lambda i: slice(i * nseg, (i + 1) * nseg)
    branch = _cmp_branch_many(q4, [cmp[0][prow(i)] for i in range(len(pairs))],
                              [cmp[1][prow(i)] for i in range(len(pairs))], qpos4, nb)
    pgs = []
    for i in range(len(pairs)):
        p = branch[i][0]
        pg = p[0:lp]
        for r in range(1, R):
            pg = pg + p[r * lp:(r + 1) * lp]
        pgs.append(pg)
    sel = _select(jnp.concatenate(pgs, axis=0), mm_ref[...], _tile_rows(qpos_tok, len(pairs)), nslc, nsel)
    selm_all = _dot(sel.astype(BF16), e_ref[...])

    sel_jobs, win_jobs = [], []
    for i, (q, g) in enumerate(pairs):
        selm = _tile_rows(selm_all[i * lp:(i + 1) * lp], R)
        k_past = sbuf_ref[slot, q, 0, pl.ds(g, plen, stride=G), :].astype(BF16)
        v_past = sbuf_ref[slot, q, 1, pl.ds(g, plen, stride=G), :].astype(BF16)
        m_past = jnp.where(selm[:, :plen] > 0.5, past_pos, qpos4 + 1) <= qpos4
        m_new = jnp.where(selm[:, plen:] > 0.5, new_pos, qpos4 + 1) <= qpos4
        sel_jobs.append((q4[i], [k_past, pad_new(ksn_ref, q, gls(g))],
                         [v_past, pad_new(vsn_ref, q, gls(g))], [m_past, m_new]))
        k_win = wk_ref[q, pl.ds(g, wlen, stride=G), :].astype(BF16)
        v_win = wv_ref[q, pl.ds(g, wlen, stride=G), :].astype(BF16)
        win_jobs.append((q4[i], [k_win, pad_new(kwn_ref, q, gls(g))],
                         [v_win, pad_new(vwn_ref, q, gls(g))], [win_pos <= qpos4, new_win_pos <= qpos4]))
    outs = _attend_many(sel_jobs + win_jobs)
    o_s, o_w = outs[:len(pairs)], outs[len(pairs):]

    for i, (q, g) in enumerate(pairs):
        gts = jax.nn.sigmoid(sm_ref[q])
        for r in range(R):
            c0 = GATE0 + g * R + r
            rs = slice(r * lp, (r + 1) * lp)
            o = (gts[:, c0:c0 + 1] * branch[i][1][rs]
                 + gts[:, c0 + NSA_HEADS:c0 + NSA_HEADS + 1] * o_s[i][rs]
                 + gts[:, c0 + 2 * NSA_HEADS:c0 + 2 * NSA_HEADS + 1] * o_w[i][rs])
            hq = (g * R + r) * NSA_HD
            o_ref[q, :, hq:hq + NSA_HD] = o.astype(o_ref.dtype)

    wrows = wlen * G
    shift = lreal * G
    for q in range(ns):
        for src, new, dst in ((wk_ref, kwn_ref, wko_ref), (wv_ref, vwn_ref, wvo_ref)):
            dst[q, 0:wrows - shift, :] = src[q, shift:wrows, :]
            for t in range(lreal):
                for g in range(G):
                    row = wrows - shift + t * G + g
                    dst[q, row:row + 1, :] = new[q, t:t + 1, g * NSA_HD:(g + 1) * NSA_HD]


def nsa_decode(proj3, small3, caches, page_table, win_k, win_v, w1cat, w2s, emat, mmat, lreal, ns):
    B, lp, _ = proj3.shape
    G = NSA_KV
    n_pages = page_table.shape[1]
    rows_pp = caches[0].shape[1]
    plen = n_pages * rows_pp // G
    wlen = win_k.shape[1] // G
    assert plen % CMP_STRIDE == 0 and lreal < CMP_STRIDE and lreal <= lp and B % ns == 0
    nb = (plen + lreal) // CMP_STRIDE - 1
    nslc = -(-(plen + lreal) // SLC_LEN)
    nsel = min(N_SELECT, nslc)
    new_spec = lambda c0: pl.BlockSpec((ns, lp, KVW), lambda b, pt: (b, 0, c0 // KVW))
    const = lambda shape: pl.BlockSpec(shape, lambda b, pt: (0,) * len(shape))
    hbm = pl.BlockSpec(memory_space=pl.ANY)
    kern = functools.partial(_nsa_decode_kernel, ns=ns, lp=lp, lreal=lreal, plen=plen, rows_pp=rows_pp,
                             n_pages=n_pages, wlen=wlen, nslc=nslc, nsel=nsel, nb=nb)
    grid_spec = pltpu.PrefetchScalarGridSpec(
        num_scalar_prefetch=1,
        grid=(B // ns,),
        in_specs=[pl.BlockSpec((ns, lp, NSA_HEADS * NSA_HD), lambda b, pt: (b, 0, NQ0 // (NSA_HEADS * NSA_HD))),
                  pl.BlockSpec((ns, lp, LANE), lambda b, pt: (b, 0, 0)),
                  new_spec(KS0), new_spec(VS0), new_spec(KW0), new_spec(VW0),
                  pl.BlockSpec((ns, wlen * G, NSA_HD), lambda b, pt: (b, 0, 0)),
                  pl.BlockSpec((ns, wlen * G, NSA_HD), lambda b, pt: (b, 0, 0)),
                  const((2, CMP_STRIDE, NSA_HD, 2 * NSA_HD)),
                  const((2, NSA_HD, NSA_HD)),
                  const((LANE, plen + KCHUNK)),
                  const((LANE, LANE)),
                  hbm, hbm, hbm, hbm],
        out_specs=[pl.BlockSpec((ns, lp, NSA_HEADS * NSA_HD), lambda b, pt: (b, 0, 0)),
                   pl.BlockSpec((ns, wlen * G, NSA_HD), lambda b, pt: (b, 0, 0)),
                   pl.BlockSpec((ns, wlen * G, NSA_HD), lambda b, pt: (b, 0, 0))],
        scratch_shapes=[pltpu.VMEM((2, ns, 2, n_pages * rows_pp // CMP_SEG_ROWS * CMP_PITCH, NSA_HD), F32),
                        pltpu.VMEM((2, ns, 2, n_pages * rows_pp, NSA_HD), F32),
                        pltpu.SemaphoreType.DMA((2, len(caches)))])
    return pl.pallas_call(
        kern,
        grid_spec=grid_spec,
        out_shape=[jax.ShapeDtypeStruct((B, lp, NSA_HEADS * NSA_HD), BF16),
                   jax.ShapeDtypeStruct(win_k.shape, F32), jax.ShapeDtypeStruct(win_v.shape, F32)],
        compiler_params=_cp(("arbitrary",)),
        name="nsa_decode")(page_table, proj3, small3, proj3, proj3, proj3, proj3, win_k, win_v,
                           w1cat, w2s, emat, mmat, caches[0], caches[1], caches[2], caches[3])


def _out_proj_kernel(og_ref, on_ref, x_ref, w0_ref, w1_ref, g_ref, h_ref, hn_ref):
    h = x_ref[...] + _dot(og_ref[...], w0_ref[...]) + _dot(on_ref[...], w1_ref[...])
    h_ref[...] = h
    hn_ref[...] = _rms(h, g_ref[...]).astype(BF16)


def out_proj(og, on, x2, w_out, g, tm):
    M, Dm = x2.shape
    Kh = og.shape[1]
    return pl.pallas_call(
        _out_proj_kernel,
        grid=(M // tm,),
        in_specs=[pl.BlockSpec((tm, Kh), lambda i: (i, 0)),
                  pl.BlockSpec((tm, Kh), lambda i: (i, 0)),
                  pl.BlockSpec((tm, Dm), lambda i: (i, 0)),
                  pl.BlockSpec((Kh, Dm), lambda i: (0, 0)),
                  pl.BlockSpec((Kh, Dm), lambda i: (1, 0)),
                  pl.BlockSpec((1, Dm), lambda i: (0, 0))],
        out_specs=[pl.BlockSpec((tm, Dm), lambda i: (i, 0)),
                   pl.BlockSpec((tm, Dm), lambda i: (i, 0))],
        out_shape=[jax.ShapeDtypeStruct((M, Dm), F32), jax.ShapeDtypeStruct((M, Dm), BF16)],
        compiler_params=_cp(("parallel",)),
        name="out_proj")(og, on, x2, w_out, w_out, g)


def _ffn_up_seq_kernel(hn_ref, wa_ref, wu_ref, cw_ref, cb_ref, pre_ref, hm_ref, tail_ref, aext_ref,
                       *, tm, tiles_per_seq):
    m = pl.program_id(1)

    @pl.when(m % tiles_per_seq == 0)
    def _():
        aext_ref[0:SUBLANE, :] = pre_ref[0]

    hn = hn_ref[...]
    a = _dot(hn, wa_ref[...])
    u = _dot(hn, wu_ref[...])
    aext_ref[SUBLANE:SUBLANE + tm, :] = a
    p1 = aext_ref[pl.ds(SUBLANE - 1, tm), :]
    p2 = aext_ref[pl.ds(SUBLANE - 2, tm), :]
    c = p2 * cw_ref[0:1, :] + p1 * cw_ref[1:2, :] + a * cw_ref[2:3, :] + cb_ref[...]
    hm_ref[...] = (jax.nn.gelu(c) * u).astype(hm_ref.dtype)
    tail = a[tm - SUBLANE:tm, :]
    tail_ref[0] = tail
    aext_ref[0:SUBLANE, :] = tail


def ffn_up_seq(hn, w_a, w_u, conv_w, conv_b, prefix, seq_len, tm, tf):
    M, Dm = hn.shape
    Fp = conv_w.shape[1]
    nf = Fp // tf
    B = M // seq_len
    tps = seq_len // tm
    kern = functools.partial(_ffn_up_seq_kernel, tm=tm, tiles_per_seq=tps)
    return pl.pallas_call(
        kern,
        grid=(nf, M // tm),
        in_specs=[pl.BlockSpec((tm, Dm), lambda f, m: (m, 0)),
                  pl.BlockSpec((Dm, tf), lambda f, m: (0, f)),
                  pl.BlockSpec((Dm, tf), lambda f, m: (0, f)),
                  pl.BlockSpec((SUBLANE, tf), lambda f, m: (0, f)),
                  pl.BlockSpec((1, tf), lambda f, m: (0, f)),
                  pl.BlockSpec((1, SUBLANE, tf), lambda f, m: (m // tps, 0, f))],
        out_specs=[pl.BlockSpec((tm, tf), lambda f, m: (m, f)),
                   pl.BlockSpec((1, SUBLANE, tf), lambda f, m: (m // tps, 0, f))],
        out_shape=[jax.ShapeDtypeStruct((M, Fp), BF16), jax.ShapeDtypeStruct((B, SUBLANE, Fp), F32)],
        scratch_shapes=[pltpu.VMEM((SUBLANE + tm, tf), F32)],
        compiler_params=_cp(("parallel", "arbitrary")),
        name="ffn_up_seq")(hn, w_a, w_u, conv_w, conv_b, prefix)


def _ffn_up_tm_kernel(hn_ref, wa_ref, wu_ref, cw_ref, cb_ref, pre_ref, hm_ref, tail_ref, *, steps, nb):
    hn = hn_ref[...]
    a = _dot(hn, wa_ref[...])
    u = _dot(hn, wu_ref[...])
    slabs = [pre_ref[i] for i in range(CONV_W - 1)] + [a[t * nb:(t + 1) * nb] for t in range(steps)]
    for t in range(steps):
        c = (slabs[t] * cw_ref[0:1, :] + slabs[t + 1] * cw_ref[1:2, :] + slabs[t + 2] * cw_ref[2:3, :]
             + cb_ref[...])
        hm_ref[t * nb:(t + 1) * nb, :] = (jax.nn.gelu(c) * u[t * nb:(t + 1) * nb]).astype(hm_ref.dtype)
    for i in range(CONV_W - 1):
        tail_ref[i] = slabs[steps + i]


def ffn_up_tm(hn, w_a, w_u, conv_w, conv_b, prefix, steps, tf):
    M, Dm = hn.shape
    Fp = conv_w.shape[1]
    nf = Fp // tf
    nb = M // steps
    kern = functools.partial(_ffn_up_tm_kernel, steps=steps, nb=nb)
    return pl.pallas_call(
        kern,
        grid=(nf,),
        in_specs=[pl.BlockSpec((M, Dm), lambda f: (0, 0)),
                  pl.BlockSpec((Dm, tf), lambda f: (0, f)),
                  pl.BlockSpec((Dm, tf), lambda f: (0, f)),
                  pl.BlockSpec((SUBLANE, tf), lambda f: (0, f)),
                  pl.BlockSpec((1, tf), lambda f: (0, f)),
                  pl.BlockSpec((CONV_W - 1, nb, tf), lambda f: (0, 0, f))],
        out_specs=[pl.BlockSpec((M, tf), lambda f: (0, f)),
                   pl.BlockSpec((CONV_W - 1, nb, tf), lambda f: (0, 0, f))],
        out_shape=[jax.ShapeDtypeStruct((M, Fp), BF16), jax.ShapeDtypeStruct((CONV_W - 1, nb, Fp), F32)],
        compiler_params=_cp(("parallel",)),
        name="ffn_up_tm")(hn, w_a, w_u, conv_w, conv_b, prefix)


def _ffn_down_kernel(hm_ref, w_ref, h_ref, o_ref):
    o_ref[...] = h_ref[...] + _dot(hm_ref[...], w_ref[...])


def ffn_down(hmid, w_down, h1, tm, tn):
    M = hmid.shape[0]
    F, Dm = w_down.shape
    assert F % LANE == 0 and F <= hmid.shape[1]
    return pl.pallas_call(
        _ffn_down_kernel,
        grid=(M // tm, Dm // tn),
        in_specs=[pl.BlockSpec((tm, F), lambda i, j: (i, 0)),
                  pl.BlockSpec((F, tn), lambda i, j: (0, j)),
                  pl.BlockSpec((tm, tn), lambda i, j: (i, j))],
        out_specs=pl.BlockSpec((tm, tn), lambda i, j: (i, j)),
        out_shape=jax.ShapeDtypeStruct((M, Dm), F32),
        compiler_params=_cp(("parallel", "arbitrary")),
        name="ffn_down")(hmid, w_down, h1)


def _ple_final_kernel(h_ref, p_ref, wg_ref, bg_ref, wp_ref, g_ref, o_ref):
    h = h_ref[...]
    gate = jax.nn.sigmoid(_dot(h.astype(BF16), wg_ref[...]) + bg_ref[...])
    h = h + gate * _dot(p_ref[...].astype(BF16), wp_ref[...])
    o_ref[...] = _rms(h, g_ref[...])


def ple_final(h2, p2, w_gate, b_gate, w_proj, g, tm):
    M, Dm = h2.shape
    Pd = p2.shape[1]
    return pl.pallas_call(
        _ple_final_kernel,
        grid=(M // tm,),
        in_specs=[pl.BlockSpec((tm, Dm), lambda i: (i, 0)),
                  pl.BlockSpec((tm, Pd), lambda i: (i, 0)),
                  pl.BlockSpec((Dm, Dm), lambda i: (0, 0)),
                  pl.BlockSpec((1, Dm), lambda i: (0, 0)),
                  pl.BlockSpec((Pd, Dm), lambda i: (0, 0)),
                  pl.BlockSpec((1, Dm), lambda i: (0, 0))],
        out_specs=pl.BlockSpec((tm, Dm), lambda i: (i, 0)),
        out_shape=jax.ShapeDtypeStruct((M, Dm), F32),
        compiler_params=_cp(("parallel",)),
        name="ple_final")(h2, p2, w_gate, b_gate, w_proj, g)


def _pick_tile(n, prefs):
    for t in prefs:
        if n % t == 0:
            return t
    raise ValueError(f"no tile for {n}")


def _block_matrices(total_keys, nb):
    e = np.zeros((LANE, total_keys), np.float32)
    t = np.arange(total_keys)
    e[t // SLC_LEN, t] = 1.0
    ratio = SLC_LEN // CMP_STRIDE
    m = np.zeros((LANE, LANE), np.float32)
    for n in range(nb):
        for s in range(LANE):
            m[n, s] = float(ratio * s <= n <= ratio * s + ratio - 1) + float(ratio * s - 1 <= n <= ratio * s + ratio - 2)
    return jnp.asarray(e, BF16), jnp.asarray(m, BF16)


def _prep_weights(i, attn_norm_g, w_in, gla_w_a2, gla_b_a, gla_norm_g, cmp_wk1, cmp_wk2, cmp_wv1, cmp_wv2,
                  w_out, ffn_norm_g, ffn_w_up, ffn_conv_w, ffn_conv_b, ffn_w_down, ple_w_proj, ple_w_gate,
                  ple_b_gate):
    Dm = w_in.shape[1]
    F = ffn_w_down.shape[1]
    Fp = -(-F // 512) * 512
    win = w_in[i]
    ga0 = NQ0
    ng0 = ga0 + GLA_RANK + (NMAIN - NQ0)
    ngw = 3 * NSA_HEADS
    before_ga = lax.broadcasted_iota(jnp.int32, (1, NMAIN), 1) < ga0
    w_main = jnp.where(before_ga, win[:, :NMAIN], win[:, GLA_RANK:GLA_RANK + NMAIN]).astype(BF16)
    w_small = jnp.concatenate([win[:, ga0:ga0 + GLA_RANK], win[:, ng0:ng0 + ngw],
                               jnp.zeros((Dm, LANE - GLA_RANK - ngw), F32)], axis=1).astype(BF16)
    wa_pad = jnp.concatenate([gla_w_a2[i], jnp.zeros((LANE - GLA_RANK, GLA_HEADS * GLA_DK), F32)],
                             axis=0).astype(BF16)
    cat = lambda w1: jnp.concatenate([w1[:CMP_STRIDE], w1[CMP_STRIDE:]], axis=-1)
    w1cat = jnp.stack([cat(cmp_wk1[i]), cat(cmp_wv1[i])]).astype(BF16)
    w2s = jnp.stack([cmp_wk2[i], cmp_wv2[i]])
    wup = ffn_w_up[i]
    w_up_a = jnp.pad(wup[:, :F], ((0, 0), (0, Fp - F))).astype(BF16)
    w_up_u = jnp.pad(wup[:, F:], ((0, 0), (0, Fp - F))).astype(BF16)
    conv_w = jnp.pad(ffn_conv_w[i], ((0, SUBLANE - CONV_W), (0, Fp - F)))
    conv_b = jnp.pad(ffn_conv_b[i], (0, Fp - F)).reshape(1, Fp)
    w_down = ffn_w_down[i].astype(BF16)
    return dict(
        g_attn=attn_norm_g[i].reshape(1, Dm), w_main=w_main, w_small=w_small, wa_pad=wa_pad,
        b_a=gla_b_a[i].reshape(1, -1), gnorm=gla_norm_g[i].reshape(1, -1), w1cat=w1cat, w2s=w2s,
        w_out=w_out[i].astype(BF16), g_ffn=ffn_norm_g[i].reshape(1, Dm), w_up_a=w_up_a, w_up_u=w_up_u,
        conv_w=conv_w,
        conv_b=conv_b, w_down=w_down, w_proj=ple_w_proj[i].astype(BF16), w_gate=ple_w_gate[i].astype(BF16),
        b_gate=ple_b_gate[i].reshape(1, Dm), F=F, Fp=Fp)


def _kv_rows(kv, B, rows):
    t = kv.reshape(kv.shape[0], B, -1, NSA_KV, NSA_HD)[:, :, rows]
    return [t[i] for i in range(kv.shape[0])]


def _prompt_layer(h2d, B, L, W):
    Dm = h2d.shape[1]
    M = B * L
    proj, small, kv = norm_proj(h2d, W["g_attn"], W["w_main"], W["w_small"],
                                _pick_tile(M, (1024, 512, 256, 128)), 512)
    proj3 = proj.reshape(B, L, NMAIN)
    small3 = small.reshape(B, L, LANE)
    cb = _pick_tile(L, (128, 64, 32, 16))
    s0 = jnp.zeros((B, GLA_HEADS, GLA_DK, GLA_DV), F32)
    og, gla_state = gla(proj3, small3, W["wa_pad"], W["b_a"], W["gnorm"], s0, cb, GLA_SUB, cb,
                        _pick_tile(B, (4, 2, 1)))
    cmp = compress(proj3, W["w1cat"], W["w2s"])
    _, mmat = _block_matrices(L, L // CMP_STRIDE - 1)
    on = nsa_prompt(proj3, small3, cmp, mmat.T)
    h1, hn = out_proj(og.reshape(M, -1), on.reshape(M, -1), h2d, W["w_out"], W["g_ffn"],
                      _pick_tile(M, (512, 256, 128)))
    prefix = jnp.zeros((B, SUBLANE, W["Fp"]), F32)
    hmid, tail = ffn_up_seq(hn, W["w_up_a"], W["w_up_u"], W["conv_w"], W["conv_b"], prefix, L,
                            _pick_tile(L, (1024, 512, 256, 128)), 512)
    h2 = ffn_down(hmid, W["w_down"], h1, _pick_tile(M, (1024, 512, 256, 128)), 512)
    keep = min(WINDOW, L)
    outs = tuple(_kv_rows(kv[:4], B, slice(0, L)) + _kv_rows(kv[4:], B, slice(L - keep, L))
                 + [gla_state, tail[:, SUBLANE - (CONV_W - 1):, :W["F"]]])
    return h2, outs


def _decode_layer(h3d, B, L, lp, caches, page_table, win_k, win_v, gla_s, conv_s, W):
    Dm = h3d.shape[2]
    Mp = B * lp
    x2 = h3d.reshape(Mp, Dm)
    proj, small, kv = norm_proj(x2, W["g_attn"], W["w_main"], W["w_small"],
                                _pick_tile(Mp, (1024, 512, 256, 128, 64, 32, 16, 8)), 512)
    proj3 = proj.reshape(B, lp, NMAIN)
    small3 = small.reshape(B, lp, LANE)
    og, gla_state = gla(proj3, small3, W["wa_pad"], W["b_a"], W["gnorm"], gla_s, lp, lp, L,
                        _pick_tile(B, (8, 4, 2, 1)))
    plen = page_table.shape[1] * caches[0].shape[1] // NSA_KV
    emat, mmat = _block_matrices(plen + KCHUNK, (plen + L) // CMP_STRIDE - 1)
    wlen = win_k.shape[1]
    on, wk_new, wv_new = nsa_decode(proj3, small3, caches, page_table,
                                    win_k.reshape(B, wlen * NSA_KV, NSA_HD),
                                    win_v.reshape(B, wlen * NSA_KV, NSA_HD),
                                    W["w1cat"], W["w2s"], emat, mmat, L, _pick_tile(B, (2, 1)))
    h1, hn = out_proj(og.reshape(Mp, -1), on.reshape(Mp, -1), x2, W["w_out"], W["g_ffn"],
                      _pick_tile(Mp, (256, 128, 64, 32, 16, 8)))
    to_tm = lambda t: t.reshape(B, lp, Dm)[:, :L].transpose(1, 0, 2).reshape(L * B, Dm)
    h1_tm, hn_tm = to_tm(h1), to_tm(hn)
    prefix = jnp.pad(conv_s.transpose(1, 0, 2), ((0, 0), (0, 0), (0, W["Fp"] - W["F"])))
    hmid, tail = ffn_up_tm(hn_tm, W["w_up_a"], W["w_up_u"], W["conv_w"], W["conv_b"], prefix, L, 512)
    M = L * B
    h2 = ffn_down(hmid, W["w_down"], h1_tm, _pick_tile(M, (512, 256, 128, 64, 32, 16, 8)), 512)
    outs = tuple(_kv_rows(kv[:4], B, slice(0, L))
                 + [wk_new.reshape(win_k.shape), wv_new.reshape(win_v.shape),
                    gla_state, tail[:, :, :W["F"]].transpose(1, 0, 2)])
    return h2, outs


def kernel(x_prompt, x_sample, p_prompt, p_sample, cache_cmp_k, cache_cmp_v, cache_slc_k, cache_slc_v,
           page_table, state_win_k, state_win_v, state_gla, state_ffn_conv, attn_norm_g, w_in, gla_w_a2,
           gla_b_a, gla_norm_g, cmp_wk1, cmp_wk2, cmp_wv1, cmp_wv2, w_out, ffn_norm_g, ffn_w_up,
           ffn_conv_w, ffn_conv_b, ffn_w_down, ple_w_proj, ple_w_gate, ple_b_gate, final_norm_g):
    depth = w_in.shape[0]
    assert depth == 1, "layers are chained through HBM one at a time; only depth 1 is wired"
    Bp, Lp, Dm = x_prompt.shape
    Bs, Ls, _ = x_sample.shape
    W = _prep_weights(0, attn_norm_g, w_in, gla_w_a2, gla_b_a, gla_norm_g, cmp_wk1, cmp_wk2, cmp_wv1,
                      cmp_wv2, w_out, ffn_norm_g, ffn_w_up, ffn_conv_w, ffn_conv_b, ffn_w_down,
                      ple_w_proj, ple_w_gate, ple_b_gate)
    g_final = final_norm_g.reshape(1, Dm)

    Mp = Bp * Lp
    h2, outs_p = _prompt_layer(x_prompt.reshape(Mp, Dm), Bp, Lp, W)
    y_prompt = ple_final(h2, p_prompt[0].reshape(Mp, -1), W["w_gate"], W["b_gate"], W["w_proj"], g_final,
                         _pick_tile(Mp, (512, 256, 128))).reshape(Bp, Lp, Dm)

    lp = -(-Ls // SUBLANE) * SUBLANE
    xs = jnp.pad(x_sample, ((0, 0), (0, lp - Ls), (0, 0)))
    n_pool, psz = cache_cmp_k.shape[1], cache_cmp_k.shape[2]
    caches = [c[0].reshape(n_pool, psz * NSA_KV, NSA_HD)
              for c in (cache_cmp_k, cache_cmp_v, cache_slc_k, cache_slc_v)]
    h2s, outs_s = _decode_layer(xs, Bs, Ls, lp, caches, page_table, state_win_k[0], state_win_v[0],
                                state_gla[0], state_ffn_conv[0], W)
    p_tm = p_sample[0].transpose(1, 0, 2).reshape(Ls * Bs, -1)
    Ms = Ls * Bs
    y_tm = ple_final(h2s, p_tm, W["w_gate"], W["b_gate"], W["w_proj"], g_final,
                     _pick_tile(Ms, (256, 128, 64, 32, 16, 8)))
    y_sample = y_tm.reshape(Ls, Bs, Dm).transpose(1, 0, 2)

    lead = lambda t: t[None]
    return (y_prompt, y_sample) + tuple(lead(t) for t in outs_p) + tuple(lead(t) for t in outs_s)
```

```python
import functools

import numpy as np
import jax
import jax.numpy as jnp
from jax import lax
from jax.experimental import pallas as pl
from jax.experimental.pallas import tpu as pltpu

F32 = jnp.float32
BF16 = jnp.bfloat16

GLA_HEADS = 4
GLA_DK = 128
GLA_DV = 256
GLA_RANK = 16
GLA_TAU = 16.0
GLA_SUB = 16
NSA_HEADS = 8
NSA_KV = 2
NSA_REP = NSA_HEADS // NSA_KV
NSA_HD = 128
CMP_LEN = 32
CMP_STRIDE = 16
SLC_LEN = 64
N_SELECT = 16
WINDOW = 512
CONV_W = 3
EPS = 1e-6
NEG = -1e30
M_FLOOR = 0.1 * NEG
FORCED = 1e6

LANE = 128
SUBLANE = 8
KCHUNK = 128
CMP_SEG_ROWS = NSA_KV * CMP_STRIDE
CMP_PITCH = CMP_SEG_ROWS + SUBLANE
VMEM_LIMIT = 56 * 1024 * 1024

GQ0 = 0
GK0 = GQ0 + GLA_HEADS * GLA_DK
GV0 = GK0 + GLA_HEADS * GLA_DK
GR0 = GV0 + GLA_HEADS * GLA_DV
NQ0 = GR0 + GLA_HEADS * GLA_DV
KVW = NSA_KV * NSA_HD
KC0 = NQ0 + NSA_HEADS * NSA_HD
VC0 = KC0 + KVW
KS0 = VC0 + KVW
VS0 = KS0 + KVW
KW0 = VS0 + KVW
VW0 = KW0 + KVW
NMAIN = VW0 + KVW
GATE0 = GLA_RANK


def _cp(sem):
    return pltpu.CompilerParams(dimension_semantics=sem, vmem_limit_bytes=VMEM_LIMIT)


def _dot(a, b):
    return jnp.dot(a, b, preferred_element_type=F32)


def _dot_nt(a, b):
    return lax.dot_general(a, b, (((1,), (1,)), ((), ())), preferred_element_type=F32)


def _rms(x, g):
    return x * lax.rsqrt(jnp.mean(x * x, axis=-1, keepdims=True) + EPS) * g


def _norm_proj_kernel(x_ref, g_ref, wm_ref, ws_ref, om_ref, os_ref, kv_ref, xn_ref, *, j_kv0):
    j = pl.program_id(1)

    @pl.when(j == 0)
    def _():
        xn = _rms(x_ref[...], g_ref[...]).astype(BF16)
        xn_ref[...] = xn
        os_ref[...] = _dot(xn, ws_ref[...])

    tile = _dot(xn_ref[...], wm_ref[...])
    om_ref[...] = tile

    @pl.when(j >= j_kv0)
    def _():
        tm = tile.shape[0]
        for a in range(tile.shape[1] // KVW):
            for g in range(NSA_KV):
                c0 = a * KVW + g * NSA_HD
                kv_ref[a, pl.ds(g, tm, stride=NSA_KV), :] = tile[:, c0:c0 + NSA_HD]


def norm_proj(x2, g, w_main, w_small, tm, tn):
    M, Dm = x2.shape
    N = w_main.shape[1]
    assert KC0 % tn == 0 and tn % KVW == 0
    j_kv0 = KC0 // tn
    per_tile = tn // KVW
    n_kv = (N - KC0) // KVW
    return pl.pallas_call(
        functools.partial(_norm_proj_kernel, j_kv0=j_kv0),
        grid=(M // tm, N // tn),
        in_specs=[pl.BlockSpec((tm, Dm), lambda i, j: (i, 0)),
                  pl.BlockSpec((1, Dm), lambda i, j: (0, 0)),
                  pl.BlockSpec((Dm, tn), lambda i, j: (0, j)),
                  pl.BlockSpec((Dm, LANE), lambda i, j: (0, 0))],
        out_specs=[pl.BlockSpec((tm, tn), lambda i, j: (i, j)),
                   pl.BlockSpec((tm, LANE), lambda i, j: (i, 0)),
                   pl.BlockSpec((per_tile, NSA_KV * tm, NSA_HD),
                                lambda i, j: (jnp.maximum(j - j_kv0, 0), i, 0))],
        out_shape=[jax.ShapeDtypeStruct((M, N), F32), jax.ShapeDtypeStruct((M, LANE), F32),
                   jax.ShapeDtypeStruct((n_kv, NSA_KV * M, NSA_HD), F32)],
        scratch_shapes=[pltpu.VMEM((tm, Dm), BF16)],
        compiler_params=_cp(("parallel", "arbitrary")),
        name="norm_proj")(x2, g, w_main, w_small)


def _gla_kernel(q_ref, k_ref, v_ref, r_ref, sm_ref, wa_ref, ba_ref, gn_ref, s0_ref,
                o_ref, s_ref, st_ref, *, cb, sub, valid, nbb):
    c = pl.program_id(1)

    @pl.when(c == 0)
    def _():
        st_ref[...] = s0_ref[...]

    GH = GLA_HEADS
    H = nbb * GH
    side = lambda ref: ref[0] if nbb == 1 else jnp.concatenate([ref[s] for s in range(nbb)], axis=1)
    hk = lambda h: slice(h * GLA_DK, (h + 1) * GLA_DK)
    hv = lambda h: slice(h * GLA_DV, (h + 1) * GLA_DV)
    ri = lax.broadcasted_iota(jnp.int32, (cb, cb), 0)
    ci = lax.broadcasted_iota(jnp.int32, (cb, cb), 1)
    tri = jnp.where(ci <= ri, 1.0, 0.0)
    rowv = lax.broadcasted_iota(jnp.int32, (cb, 1), 0)
    q = side(q_ref) * (GLA_DK ** -0.5)
    k = side(k_ref)
    v = side(v_ref)
    pre = [_dot(sm_ref[s].astype(BF16), wa_ref[...]) + ba_ref[...] for s in range(nbb)]
    pre = pre[0] if nbb == 1 else jnp.concatenate(pre, axis=1)
    log_a = jax.nn.log_sigmoid(pre) / GLA_TAU
    if valid < cb:
        log_a = jnp.where(rowv < valid, log_a, 0.0)
        k = jnp.where(rowv < valid, k, 0.0)
        v = jnp.where(rowv < valid, v, 0.0)
    hi = log_a.astype(BF16).astype(F32)
    lo = (log_a - hi).astype(BF16).astype(F32)
    b = _dot(tri, hi) + _dot(tri, lo)
    b_last = b[cb - 1:cb, :]
    S = [st_ref[h // GH, h % GH] for h in range(H)]
    vb = v.astype(BF16)

    qe = (q * jnp.exp(b)).astype(BF16)
    o = [_dot(qe[:, hk(h)], S[h].astype(BF16)) for h in range(H)]
    a_rows = [[] for _ in range(H)]
    cc = lax.broadcasted_iota(jnp.int32, (sub, cb), 1)
    for blk in range(cb // sub):
        lo_r, hi_r = blk * sub, (blk + 1) * sub
        r_dec = jnp.zeros((1, H * GLA_DK), F32) if blk == 0 else b[lo_r - 1:lo_r, :]
        q_i = q[lo_r:hi_r] * jnp.exp(b[lo_r:hi_r] - r_dec)
        k_i = k * jnp.exp(jnp.where(rowv < hi_r, r_dec - b, 0.0))
        rr = lax.broadcasted_iota(jnp.int32, (sub, cb), 0) + lo_r
        for h in range(H):
            a = _dot_nt(q_i[:, hk(h)], k_i[:, hk(h)])
            a_rows[h].append(jnp.where(cc <= rr, a, 0.0))
    for h in range(H):
        a = a_rows[h][0] if len(a_rows[h]) == 1 else jnp.concatenate(a_rows[h], axis=0)
        o[h] = o[h] + _dot(a, v[:, hv(h)])

    kh = k * jnp.exp(b_last - b)
    dec = jnp.exp(b_last)
    if cb < LANE:
        kh = jnp.concatenate([kh, jnp.zeros((LANE - cb, H * GLA_DK), F32)], axis=0)
        vpad = jnp.concatenate([vb, jnp.zeros((LANE - cb, H * GLA_DV), BF16)], axis=0)
    else:
        vpad = vb
    for h in range(H):
        dec_col = jnp.transpose(jnp.broadcast_to(dec[:, hk(h)], (GLA_DK, GLA_DK)))[:, 0:1]
        s_new = dec_col * S[h] + _dot(jnp.transpose(kh[:, hk(h)]).astype(BF16), vpad[:, hv(h)])
        st_ref[h // GH, h % GH] = s_new
        s_ref[h // GH, h % GH] = s_new

    r = side(r_ref)
    gate = r * jax.nn.sigmoid(r)
    for h in range(H):
        o_ref[h // GH, :, hv(h % GH)] = (_rms(o[h], gn_ref[...]) * gate[:, hv(h)]).astype(o_ref.dtype)


def gla(proj3, small3, wa_pad, b_a, gnorm, s0, cb, sub, valid, nbb):
    B, L, _ = proj3.shape
    H = GLA_HEADS
    qw, vw = H * GLA_DK, H * GLA_DV
    kern = functools.partial(_gla_kernel, cb=cb, sub=sub, valid=valid, nbb=nbb)
    return pl.pallas_call(
        kern,
        grid=(B // nbb, L // cb),
        in_specs=[pl.BlockSpec((nbb, cb, qw), lambda b, c: (b, c, GQ0 // qw)),
                  pl.BlockSpec((nbb, cb, qw), lambda b, c: (b, c, GK0 // qw)),
                  pl.BlockSpec((nbb, cb, vw), lambda b, c: (b, c, GV0 // vw)),
                  pl.BlockSpec((nbb, cb, vw), lambda b, c: (b, c, GR0 // vw)),
                  pl.BlockSpec((nbb, cb, LANE), lambda b, c: (b, c, 0)),
                  pl.BlockSpec((LANE, qw), lambda b, c: (0, 0)),
                  pl.BlockSpec((1, qw), lambda b, c: (0, 0)),
                  pl.BlockSpec((1, GLA_DV), lambda b, c: (0, 0)),
                  pl.BlockSpec((nbb, H, GLA_DK, GLA_DV), lambda b, c: (b, 0, 0, 0))],
        out_specs=[pl.BlockSpec((nbb, cb, vw), lambda b, c: (b, c, 0)),
                   pl.BlockSpec((nbb, H, GLA_DK, GLA_DV), lambda b, c: (b, 0, 0, 0))],
        out_shape=[jax.ShapeDtypeStruct((B, L, vw), BF16),
                   jax.ShapeDtypeStruct((B, H, GLA_DK, GLA_DV), F32)],
        scratch_shapes=[pltpu.VMEM((nbb, H, GLA_DK, GLA_DV), F32)],
        compiler_params=_cp(("parallel", "arbitrary")),
        name="gla")(proj3, proj3, proj3, proj3, small3, wa_pad, b_a, gnorm, s0)


def _pair_rows(load_rows, j):
    return jnp.concatenate([load_rows(j), load_rows(j + 1)], axis=1).astype(BF16)


def _pair_weights(w1_ref, j):
    w = w1_ref[pl.ds(j, 2)]
    return w.reshape(2 * NSA_HD, 2 * NSA_HD)


def _compress_group(load_rows, w1_ref, w2, nseg):
    acc = jnp.zeros((nseg, 2 * NSA_HD), F32)
    for j in range(0, CMP_STRIDE, 2):
        acc = acc + _dot(_pair_rows(load_rows, j), _pair_weights(w1_ref, j))
    first = acc[:, :NSA_HD]
    second = acc[:, NSA_HD:]
    h = jax.nn.gelu(first + pltpu.roll(second, nseg - 1, axis=0))
    return _dot(h.astype(BF16), w2)


def _compress_groups(loads, w1_refs, w2s, nrow):
    n = len(loads)
    acc = [jnp.zeros((nrow, 2 * NSA_HD), F32) for _ in range(n)]
    for j in range(0, CMP_STRIDE, 2):
        for i in range(n):
            acc[i] = acc[i] + _dot(_pair_rows(loads[i], j), _pair_weights(w1_refs[i], j))
    hs = [jax.nn.gelu(a[:, :NSA_HD] + pltpu.roll(a[:, NSA_HD:], nrow - 1, axis=0)) for a in acc]
    return [_dot(h.astype(BF16), w2) for h, w2 in zip(hs, w2s)]


def _cmp_branch_many(q4s, cks, cvs, qpos4, nb):
    s = [_dot_nt(q4, ck.astype(BF16)) for q4, ck in zip(q4s, cks)]
    n = lax.broadcasted_iota(jnp.int32, s[0].shape, 1)
    mask = jnp.where(n * CMP_STRIDE + (CMP_LEN - 1) <= qpos4, n, nb) < nb
    s = [jnp.where(mask, si, NEG) for si in s]
    m = [jnp.max(si, axis=-1, keepdims=True) for si in s]
    e = [jnp.where(mask, jnp.exp(si - mi), 0.0) for si, mi in zip(s, m)]
    p = [ei / jnp.maximum(jnp.sum(ei, axis=-1, keepdims=True), 1e-30) for ei in e]
    o = [_dot(pi.astype(BF16), cv.astype(BF16)) for pi, cv in zip(p, cvs)]
    return list(zip(p, o))


def _select(pg, mmat, qpos_tok, nslc, nsel):
    hi = pg.astype(BF16)
    lo = (pg - hi.astype(F32)).astype(BF16)
    imp = _dot(hi, mmat) + _dot(lo, mmat)
    blk = lax.broadcasted_iota(jnp.int32, pg.shape, 1)
    cur = qpos_tok // SLC_LEN
    forced = jnp.where(blk == 0, 1, 0) + jnp.where(blk == cur, 1, 0) + jnp.where(blk == cur - 1, 1, 0)
    score = jnp.where(blk <= cur, jnp.where(forced > 0, FORCED, imp), -1.0)
    score = jnp.where(blk < nslc, score, -2.0)
    rank = jnp.zeros(pg.shape, F32)
    for j in range(nslc):
        sj = score[:, j:j + 1]
        tie = jnp.where(blk > j, sj, NEG)
        rank = rank + jnp.where(sj > score, 1.0, 0.0) + jnp.where(tie == score, 1.0, 0.0)
    return jnp.where(rank < nsel, jnp.where(blk < nslc, 1.0, 0.0), 0.0)


def _tile_rows(x, reps):
    return jnp.concatenate([x] * reps, axis=0)


def _attend_many(jobs):
    s = [[jnp.where(mk, _dot_nt(q4, k), NEG) for k, mk in zip(keys, masks)] for q4, keys, _, masks in jobs]
    m = []
    for sj in s:
        mj = sj[0].max(axis=-1, keepdims=True)
        for si in sj[1:]:
            mj = jnp.maximum(mj, si.max(axis=-1, keepdims=True))
        m.append(mj)
    e = [[jnp.where(mk, jnp.exp(si - mj), 0.0) for si, mk in zip(sj, job[3])]
         for sj, mj, job in zip(s, m, jobs)]
    l = [sum(ei.sum(axis=-1, keepdims=True) for ei in ej) for ej in e]
    o = [sum(_dot(ei.astype(BF16), v) for ei, v in zip(ej, job[2])) for ej, job in zip(e, jobs)]
    return [oj / jnp.maximum(lj, 1e-30) for oj, lj in zip(o, l)]


def _compress_kernel(x_ref, w1_ref, w2_ref, o_ref, *, nseg):
    load = lambda j: x_ref[0, pl.ds(j, nseg, stride=CMP_STRIDE), :]
    o_ref[0, 0, 0] = _compress_group(load, w1_ref.at[0], w2_ref[0].astype(BF16), nseg)


def compress(proj3, w1cat, w2s):
    B, L, _ = proj3.shape
    nseg = L // CMP_STRIDE
    G = NSA_KV
    return pl.pallas_call(
        functools.partial(_compress_kernel, nseg=nseg),
        grid=(B, 2, G),
        in_specs=[pl.BlockSpec((1, L, NSA_HD), lambda b, w, g: (b, 0, KC0 // NSA_HD + w * G + g)),
                  pl.BlockSpec((1, CMP_STRIDE, NSA_HD, 2 * NSA_HD), lambda b, w, g: (w, 0, 0, 0)),
                  pl.BlockSpec((1, NSA_HD, NSA_HD), lambda b, w, g: (w, 0, 0))],
        out_specs=pl.BlockSpec((1, 1, 1, nseg, NSA_HD), lambda b, w, g: (w, b, g, 0, 0)),
        out_shape=jax.ShapeDtypeStruct((2, B, G, nseg, NSA_HD), F32),
        compiler_params=_cp(("parallel", "parallel", "parallel")),
        name="compress")(proj3, w1cat, w2s)


def _select_t(pg_t, mm_t, qpos_row, nslc, nsel):
    nsp = -(-nslc // SUBLANE) * SUBLANE
    hi = pg_t.astype(BF16)
    lo = (pg_t - hi.astype(F32)).astype(BF16)
    imp = (_dot(mm_t, hi) + _dot(mm_t, lo))[:nsp]
    blk = lax.broadcasted_iota(jnp.int32, imp.shape, 0)
    cur = qpos_row // SLC_LEN
    forced = jnp.where(blk == 0, 1, 0) + jnp.where(blk == cur, 1, 0) + jnp.where(blk == cur - 1, 1, 0)
    score = jnp.where(blk <= cur, jnp.where(forced > 0, FORCED, imp), -1.0)
    score = jnp.where(blk < nslc, score, -2.0)
    rank = jnp.zeros(imp.shape, F32)
    for j in range(nslc):
        sj = score[j:j + 1, :]
        tie = jnp.where(blk > j, sj, NEG)
        rank = rank + jnp.where(sj > score, 1.0, 0.0) + jnp.where(tie == score, 1.0, 0.0)
    return jnp.where(rank < nsel, jnp.where(blk < nslc, 1.0, 0.0), 0.0)


def _flash_steps_t(q_t, jobs):
    s = [_dot(k, q_t) + bias for k, _, bias, _ in jobs]
    m_old = [st[0][...] for _, _, _, st in jobs]
    m_new = [jnp.maximum(mo, jnp.max(si, axis=0, keepdims=True)) for mo, si in zip(m_old, s)]
    alpha = [jnp.exp(mo - mn) for mo, mn in zip(m_old, m_new)]
    p = [jnp.exp(si - mn) for si, mn in zip(s, m_new)]
    pv = [_dot(v_t, pi.astype(BF16)) for (_, v_t, _, _), pi in zip(jobs, p)]
    for (_, _, _, (m_ref, l_ref, acc_ref)), mn, al, pi, pvi in zip(jobs, m_new, alpha, p, pv):
        l_ref[...] = al * l_ref[...] + jnp.sum(pi, axis=0, keepdims=True)
        acc_ref[...] = al * acc_ref[...] + pvi
        m_ref[...] = mn


def _nsa_prompt_kernel(q_ref, sm_ref, ck_ref, cv_ref, ks_ref, vs_ref, kw_ref, vw_ref, mm_ref,
                       o_ref, vst_ref, vwt_ref, sel_ref, m_ref, l_ref, acc_ref, mw_ref, lw_ref, accw_ref,
                       *, tq, kstep, nslc, nsel, nb):
    g = pl.program_id(1)
    qb = pl.program_id(2)
    R = NSA_REP
    T = R * tq
    L = ks_ref.shape[1]
    scale = NSA_HD ** -0.5

    @pl.when(qb == 0)
    def _():
        for c in range(L // KCHUNK):
            cs = slice(c * KCHUNK, (c + 1) * KCHUNK)
            vst_ref[:, cs] = jnp.transpose(vs_ref[0, cs, :]).astype(BF16)
            vwt_ref[:, cs] = jnp.transpose(vw_ref[0, cs, :]).astype(BF16)

    q_t = (jnp.concatenate([jnp.transpose(q_ref[0, :, r * NSA_HD:(r + 1) * NSA_HD]) for r in range(R)],
                           axis=1) * scale).astype(BF16)
    qpos_row = qb * tq + lax.broadcasted_iota(jnp.int32, (1, tq), 1)
    qpos4 = jnp.concatenate([qpos_row] * R, axis=1)

    key = lax.broadcasted_iota(jnp.int32, (kstep, T), 0)
    hi_step = (qb * tq + tq - 1) // kstep + 1
    lo_step = jnp.maximum(qb * tq - (WINDOW - 1), 0) // kstep
    last = hi_step - 1
    sel_state = (m_ref, l_ref, acc_ref)
    win_state = (mw_ref, lw_ref, accw_ref)

    def sel_step(c, causal):
        off = pl.multiple_of(c * kstep, kstep)
        k = ks_ref[0, pl.ds(off, kstep), :].astype(BF16)
        v_t = vst_ref[:, pl.ds(off, kstep)]
        blocks = [jnp.broadcast_to(sel_ref[pl.ds((kstep // SLC_LEN) * c + i, 1), :], (SLC_LEN, tq))
                  for i in range(kstep // SLC_LEN)]
        bias = jnp.concatenate(blocks, axis=0)
        bias = jnp.concatenate([bias] * R, axis=1)
        if causal:
            bias = jnp.where(off + key <= qpos4, bias, NEG)
        return k, v_t, bias, sel_state

    def win_step(c, causal):
        off = pl.multiple_of(c * kstep, kstep)
        k = kw_ref[0, pl.ds(off, kstep), :].astype(BF16)
        v_t = vwt_ref[:, pl.ds(off, kstep)]
        kpos = off + key
        if causal:
            bias = jnp.where(kpos <= qpos4, 0.0, NEG)
        else:
            bias = jnp.where(kpos > qpos4 - WINDOW, 0.0, NEG)
        return k, v_t, bias, win_state

    k_d, vt_d, bias_d, _ = win_step(last, True)
    s = _dot(ck_ref[0, 0, 0].astype(BF16), q_t)
    s_d = _dot(k_d, q_t) + bias_d
    n = lax.broadcasted_iota(jnp.int32, s.shape, 0)
    mask = jnp.where(n * CMP_STRIDE + (CMP_LEN - 1) <= qpos4, n, nb) < nb
    s = jnp.where(mask, s, NEG)
    m_c = jnp.max(s, axis=0, keepdims=True)
    m_d = jnp.maximum(jnp.max(s_d, axis=0, keepdims=True), M_FLOOR)
    e = jnp.where(mask, jnp.exp(s - m_c), 0.0)
    p_d = jnp.exp(s_d - m_d)
    p = e / jnp.maximum(jnp.sum(e, axis=0, keepdims=True), 1e-30)
    lw_ref[...] = jnp.sum(p_d, axis=0, keepdims=True)
    o_c = _dot(jnp.transpose(cv_ref[0, 0, 0]).astype(BF16), p.astype(BF16))
    accw_ref[...] = _dot(vt_d, p_d.astype(BF16))
    mw_ref[...] = m_d
    pg = p[:, 0:tq]
    for r in range(1, R):
        pg = pg + p[:, r * tq:(r + 1) * tq]
    sel_ref[...] = (_select_t(pg, mm_ref[...], qpos_row, nslc, nsel) - 1.0) * (-NEG)

    m_ref[...] = jnp.full(m_ref.shape, M_FLOOR, F32)
    l_ref[...] = jnp.zeros(l_ref.shape, F32)
    acc_ref[...] = jnp.zeros(acc_ref.shape, F32)

    def body_sel(c, carry):
        _flash_steps_t(q_t, [sel_step(c, False)])
        return carry

    def body_both(c, carry):
        _flash_steps_t(q_t, [sel_step(c, False), win_step(c, False)])
        return carry

    lax.fori_loop(0, jnp.minimum(lo_step, last), body_sel, 0)
    lax.fori_loop(lo_step, last, body_both, 0)
    _flash_steps_t(q_t, [sel_step(last, True)])
    o_s = acc_ref[...] / jnp.maximum(l_ref[...], 1e-30)
    o_w = accw_ref[...] / jnp.maximum(lw_ref[...], 1e-30)

    g_t = jnp.transpose(jax.nn.sigmoid(sm_ref[0]))
    for r in range(R):
        def gate(br, r=r):
            c0 = GATE0 + br * NSA_HEADS + r
            return jnp.where(g == 0, g_t[c0:c0 + 1, :], g_t[c0 + R:c0 + R + 1, :])
        cs = slice(r * tq, (r + 1) * tq)
        o = gate(0) * o_c[:, cs] + gate(1) * o_s[:, cs] + gate(2) * o_w[:, cs]
        o_ref[0, :, r * NSA_HD:(r + 1) * NSA_HD] = jnp.transpose(o).astype(o_ref.dtype)


def nsa_prompt(proj3, small3, cmp, mmat_t):
    B, L, _ = proj3.shape
    tq = _pick_tile(L, (2 * KCHUNK, KCHUNK))
    nseg = L // CMP_STRIDE
    nb = nseg - 1
    nslc = -(-L // SLC_LEN)
    nsel = min(N_SELECT, nslc)
    nsp = -(-nslc // SUBLANE) * SUBLANE
    G, R = NSA_KV, NSA_REP
    kstep = _pick_tile(L, (2 * KCHUNK, KCHUNK))
    assert nseg == LANE and L % KCHUNK == 0 and kstep % SLC_LEN == 0
    kv_spec = lambda c0: pl.BlockSpec((1, L, NSA_HD), lambda b, g, qb: (b, 0, c0 // NSA_HD + g))
    kern = functools.partial(_nsa_prompt_kernel, tq=tq, kstep=kstep, nslc=nslc, nsel=nsel, nb=nb)
    return pl.pallas_call(
        kern,
        grid=(B, G, L // tq),
        in_specs=[pl.BlockSpec((1, tq, R * NSA_HD), lambda b, g, qb: (b, qb, NQ0 // (R * NSA_HD) + g)),
                  pl.BlockSpec((1, tq, LANE), lambda b, g, qb: (b, qb, 0)),
                  pl.BlockSpec((1, 1, 1, nseg, NSA_HD), lambda b, g, qb: (0, b, g, 0, 0)),
                  pl.BlockSpec((1, 1, 1, nseg, NSA_HD), lambda b, g, qb: (1, b, g, 0, 0)),
                  kv_spec(KS0), kv_spec(VS0), kv_spec(KW0), kv_spec(VW0),
                  pl.BlockSpec((LANE, LANE), lambda b, g, qb: (0, 0))],
        out_specs=pl.BlockSpec((1, tq, R * NSA_HD), lambda b, g, qb: (b, qb, g)),
        out_shape=jax.ShapeDtypeStruct((B, L, NSA_HEADS * NSA_HD), BF16),
        scratch_shapes=[pltpu.VMEM((NSA_HD, L), BF16),
                        pltpu.VMEM((NSA_HD, L), BF16),
                        pltpu.VMEM((nsp, tq), F32)]
                       + 2 * [pltpu.VMEM((1, R * tq), F32),
                              pltpu.VMEM((1, R * tq), F32),
                              pltpu.VMEM((NSA_HD, R * tq), F32)],
        compiler_params=_cp(("parallel", "parallel", "arbitrary")),
        name="nsa_prompt")(proj3, small3, cmp, cmp, proj3, proj3, proj3, proj3, mmat_t)


def _nsa_decode_kernel(pt_ref, q_ref, sm_ref, ksn_ref, vsn_ref, kwn_ref, vwn_ref, wk_ref, wv_ref,
                       w1_ref, w2_ref, e_ref, mm_ref, c0_hbm, c1_hbm, c2_hbm, c3_hbm,
                       o_ref, wko_ref, wvo_ref, cbuf_ref, sbuf_ref, sem_ref,
                       *, ns, lp, lreal, plen, rows_pp, n_pages, wlen, nslc, nsel, nb):
    b = pl.program_id(0)
    slot = b % 2
    caches = (c0_hbm, c1_hbm, c2_hbm, c3_hbm)
    G = NSA_KV
    R = NSA_REP
    T = R * lp
    segs_pp = rows_pp // CMP_SEG_ROWS

    def copies(step, sl):
        out = []
        for q in range(ns):
            for p in range(n_pages):
                page = pt_ref[step * ns + q, p]
                for w in range(2):
                    for s in range(segs_pp):
                        out.append(pltpu.make_async_copy(
                            caches[w].at[page, pl.ds(s * CMP_SEG_ROWS, CMP_SEG_ROWS)],
                            cbuf_ref.at[sl, q, w, pl.ds((p * segs_pp + s) * CMP_PITCH, CMP_SEG_ROWS)],
                            sem_ref.at[sl, w]))
                    out.append(pltpu.make_async_copy(
                        caches[2 + w].at[page], sbuf_ref.at[sl, q, w, pl.ds(p * rows_pp, rows_pp)],
                        sem_ref.at[sl, 2 + w]))
        return out

    @pl.when(b == 0)
    def _():
        for i, cp in enumerate(copies(0, 0)):
            cp.start(priority=i % 2)

    @pl.when(b + 1 < pl.num_programs(0))
    def _():
        for i, cp in enumerate(copies(b + 1, 1 - slot)):
            cp.start(priority=i % 2)

    for w in range(len(caches)):
        for q in range(ns):
            whole = sbuf_ref.at[slot, q, 0]
            pltpu.make_async_copy(whole, whole, sem_ref.at[slot, w]).wait()

    scale = NSA_HD ** -0.5
    nseg = plen // CMP_STRIDE
    tok = lax.broadcasted_iota(jnp.int32, (lp, 1), 0)
    qpos_tok = plen + tok
    qpos4 = _tile_rows(qpos_tok, R)
    lane_n = lax.broadcasted_iota(jnp.int32, (T, KCHUNK), 1)
    new_pos = jnp.where(lane_n < lreal, plen + lane_n, qpos4 + 1)
    past_pos = lax.broadcasted_iota(jnp.int32, (T, plen), 1)
    win_pos = plen - wlen + lax.broadcasted_iota(jnp.int32, (T, wlen), 1)
    win_pos = jnp.where(win_pos > qpos4 - WINDOW, win_pos, qpos4 + 1)
    win_pos = jnp.where(win_pos >= 0, win_pos, qpos4 + 1)
    new_win_pos = jnp.where(new_pos > qpos4 - WINDOW, new_pos, qpos4 + 1)
    zpad_f = jnp.zeros((KCHUNK - lp, NSA_HD), F32)
    pad_new = lambda ref, q, ls: jnp.concatenate([ref[q, :, ls], zpad_f], axis=0).astype(BF16)
    gls = lambda g: slice(g * NSA_HD, (g + 1) * NSA_HD)
    pairs = [(q, g) for q in range(ns) for g in range(G)]
    q4 = [(jnp.concatenate([q_ref[q, :, (g * R + r) * NSA_HD:(g * R + r + 1) * NSA_HD]
                            for r in range(R)], axis=0) * scale).astype(BF16) for q, g in pairs]

    def seg_rows(w, j):
        return jnp.concatenate([cbuf_ref[slot, q, w, pl.ds(G * j + g, nseg, stride=CMP_PITCH), :]
                                for q, g in pairs], axis=0)
    cmp = _compress_groups([functools.partial(seg_rows, w) for w in range(2)],
                           [w1_ref.at[w] for w in range(2)],
                           [w2_ref[w].astype(BF16) for w in range(2)], len(pairs) * nseg)
    prow = lambda i: slice(i * nseg, (i + 1) * nseg)
    branch = _cmp_branch_many(q4, [cmp[0][prow(i)] for i in range(len(pairs))],
                              [cmp[1][prow(i)] for i in range(len(pairs))], qpos4, nb)
    pgs = []
    for i in range(len(pairs)):
        p = branch[i][0]
        pg = p[0:lp]
        for r in range(1, R):
            pg = pg + p[r * lp:(r + 1) * lp]
        pgs.append(pg)
    sel = _select(jnp.concatenate(pgs, axis=0), mm_ref[...], _tile_rows(qpos_tok, len(pairs)), nslc, nsel)
    selm_all = _dot(sel.astype(BF16), e_ref[...])

    sel_jobs, win_jobs = [], []
    for i, (q, g) in enumerate(pairs):
        selm = _tile_rows(selm_all[i * lp:(i + 1) * lp], R)
        k_past = sbuf_ref[slot, q, 0, pl.ds(g, plen, stride=G), :].astype(BF16)
        v_past = sbuf_ref[slot, q, 1, pl.ds(g, plen, stride=G), :].astype(BF16)
        m_past = jnp.where(selm[:, :plen] > 0.5, past_pos, qpos4 + 1) <= qpos4
        m_new = jnp.where(selm[:, plen:] > 0.5, new_pos, qpos4 + 1) <= qpos4
        sel_jobs.append((q4[i], [k_past, pad_new(ksn_ref, q, gls(g))],
                         [v_past, pad_new(vsn_ref, q, gls(g))], [m_past, m_new]))
        k_win = wk_ref[q, pl.ds(g, wlen, stride=G), :].astype(BF16)
        v_win = wv_ref[q, pl.ds(g, wlen, stride=G), :].astype(BF16)
        win_jobs.append((q4[i], [k_win, pad_new(kwn_ref, q, gls(g))],
                         [v_win, pad_new(vwn_ref, q, gls(g))], [win_pos <= qpos4, new_win_pos <= qpos4]))
    outs = _attend_many(sel_jobs + win_jobs)
    o_s, o_w = outs[:len(pairs)], outs[len(pairs):]

    for i, (q, g) in enumerate(pairs):
        gts = jax.nn.sigmoid(sm_ref[q])
        for r in range(R):
            c0 = GATE0 + g * R + r
            rs = slice(r * lp, (r + 1) * lp)
            o = (gts[:, c0:c0 + 1] * branch[i][1][rs]
                 + gts[:, c0 + NSA_HEADS:c0 + NSA_HEADS + 1] * o_s[i][rs]
                 + gts[:, c0 + 2 * NSA_HEADS:c0 + 2 * NSA_HEADS + 1] * o_w[i][rs])
            hq = (g * R + r) * NSA_HD
            o_ref[q, :, hq:hq + NSA_HD] = o.astype(o_ref.dtype)

    wrows = wlen * G
    shift = lreal * G
    for q in range(ns):
        for src, new, dst in ((wk_ref, kwn_ref, wko_ref), (wv_ref, vwn_ref, wvo_ref)):
            dst[q, 0:wrows - shift, :] = src[q, shift:wrows, :]
            for t in range(lreal):
                for g in range(G):
                    row = wrows - shift + t * G + g
                    dst[q, row:row + 1, :] = new[q, t:t + 1, g * NSA_HD:(g + 1) * NSA_HD]


def nsa_decode(proj3, small3, caches, page_table, win_k, win_v, w1cat, w2s, emat, mmat, lreal, ns):
    B, lp, _ = proj3.shape
    G = NSA_KV
    n_pages = page_table.shape[1]
    rows_pp = caches[0].shape[1]
    plen = n_pages * rows_pp // G
    wlen = win_k.shape[1] // G
    assert plen % CMP_STRIDE == 0 and lreal < CMP_STRIDE and lreal <= lp and B % ns == 0
    nb = (plen + lreal) // CMP_STRIDE - 1
    nslc = -(-(plen + lreal) // SLC_LEN)
    nsel = min(N_SELECT, nslc)
    new_spec = lambda c0: pl.BlockSpec((ns, lp, KVW), lambda b, pt: (b, 0, c0 // KVW))
    const = lambda shape: pl.BlockSpec(shape, lambda b, pt: (0,) * len(shape))
    hbm = pl.BlockSpec(memory_space=pl.ANY)
    kern = functools.partial(_nsa_decode_kernel, ns=ns, lp=lp, lreal=lreal, plen=plen, rows_pp=rows_pp,
                             n_pages=n_pages, wlen=wlen, nslc=nslc, nsel=nsel, nb=nb)
    grid_spec = pltpu.PrefetchScalarGridSpec(
        num_scalar_prefetch=1,
        grid=(B // ns,),
        in_specs=[pl.BlockSpec((ns, lp, NSA_HEADS * NSA_HD), lambda b, pt: (b, 0, NQ0 // (NSA_HEADS * NSA_HD))),
                  pl.BlockSpec((ns, lp, LANE), lambda b, pt: (b, 0, 0)),
                  new_spec(KS0), new_spec(VS0), new_spec(KW0), new_spec(VW0),
                  pl.BlockSpec((ns, wlen * G, NSA_HD), lambda b, pt: (b, 0, 0)),
                  pl.BlockSpec((ns, wlen * G, NSA_HD), lambda b, pt: (b, 0, 0)),
                  const((2, CMP_STRIDE, NSA_HD, 2 * NSA_HD)),
                  const((2, NSA_HD, NSA_HD)),
                  const((LANE, plen + KCHUNK)),
                  const((LANE, LANE)),
                  hbm, hbm, hbm, hbm],
        out_specs=[pl.BlockSpec((ns, lp, NSA_HEADS * NSA_HD), lambda b, pt: (b, 0, 0)),
                   pl.BlockSpec((ns, wlen * G, NSA_HD), lambda b, pt: (b, 0, 0)),
                   pl.BlockSpec((ns, wlen * G, NSA_HD), lambda b, pt: (b, 0, 0))],
        scratch_shapes=[pltpu.VMEM((2, ns, 2, n_pages * rows_pp // CMP_SEG_ROWS * CMP_PITCH, NSA_HD), F32),
                        pltpu.VMEM((2, ns, 2, n_pages * rows_pp, NSA_HD), F32),
                        pltpu.SemaphoreType.DMA((2, len(caches)))])
    return pl.pallas_call(
        kern,
        grid_spec=grid_spec,
        out_shape=[jax.ShapeDtypeStruct((B, lp, NSA_HEADS * NSA_HD), BF16),
                   jax.ShapeDtypeStruct(win_k.shape, F32), jax.ShapeDtypeStruct(win_v.shape, F32)],
        compiler_params=_cp(("arbitrary",)),
        name="nsa_decode")(page_table, proj3, small3, proj3, proj3, proj3, proj3, win_k, win_v,
                           w1cat, w2s, emat, mmat, caches[0], caches[1], caches[2], caches[3])


def _out_proj_kernel(og_ref, on_ref, x_ref, w0_ref, w1_ref, g_ref, h_ref, hn_ref):
    h = x_ref[...] + _dot(og_ref[...], w0_ref[...]) + _dot(on_ref[...], w1_ref[...])
    h_ref[...] = h
    hn_ref[...] = _rms(h, g_ref[...]).astype(BF16)


def out_proj(og, on, x2, w_out, g, tm):
    M, Dm = x2.shape
    Kh = og.shape[1]
    return pl.pallas_call(
        _out_proj_kernel,
        grid=(M // tm,),
        in_specs=[pl.BlockSpec((tm, Kh), lambda i: (i, 0)),
                  pl.BlockSpec((tm, Kh), lambda i: (i, 0)),
                  pl.BlockSpec((tm, Dm), lambda i: (i, 0)),
                  pl.BlockSpec((Kh, Dm), lambda i: (0, 0)),
                  pl.BlockSpec((Kh, Dm), lambda i: (1, 0)),
                  pl.BlockSpec((1, Dm), lambda i: (0, 0))],
        out_specs=[pl.BlockSpec((tm, Dm), lambda i: (i, 0)),
                   pl.BlockSpec((tm, Dm), lambda i: (i, 0))],
        out_shape=[jax.ShapeDtypeStruct((M, Dm), F32), jax.ShapeDtypeStruct((M, Dm), BF16)],
        compiler_params=_cp(("parallel",)),
        name="out_proj")(og, on, x2, w_out, w_out, g)


def _ffn_up_seq_kernel(hn_ref, wa_ref, wu_ref, cw_ref, cb_ref, pre_ref, hm_ref, tail_ref, aext_ref,
                       *, tm, tiles_per_seq):
    m = pl.program_id(1)

    @pl.when(m % tiles_per_seq == 0)
    def _():
        aext_ref[0:SUBLANE, :] = pre_ref[0]

    hn = hn_ref[...]
    a = _dot(hn, wa_ref[...])
    u = _dot(hn, wu_ref[...])
    aext_ref[SUBLANE:SUBLANE + tm, :] = a
    p1 = aext_ref[pl.ds(SUBLANE - 1, tm), :]
    p2 = aext_ref[pl.ds(SUBLANE - 2, tm), :]
    c = p2 * cw_ref[0:1, :] + p1 * cw_ref[1:2, :] + a * cw_ref[2:3, :] + cb_ref[...]
    hm_ref[...] = (jax.nn.gelu(c) * u).astype(hm_ref.dtype)
    tail = a[tm - SUBLANE:tm, :]
    tail_ref[0] = tail
    aext_ref[0:SUBLANE, :] = tail


def ffn_up_seq(hn, w_a, w_u, conv_w, conv_b, prefix, seq_len, tm, tf):
    M, Dm = hn.shape
    Fp = conv_w.shape[1]
    nf = Fp // tf
    B = M // seq_len
    tps = seq_len // tm
    kern = functools.partial(_ffn_up_seq_kernel, tm=tm, tiles_per_seq=tps)
    return pl.pallas_call(
        kern,
        grid=(nf, M // tm),
        in_specs=[pl.BlockSpec((tm, Dm), lambda f, m: (m, 0)),
                  pl.BlockSpec((Dm, tf), lambda f, m: (0, f)),
                  pl.BlockSpec((Dm, tf), lambda f, m: (0, f)),
                  pl.BlockSpec((SUBLANE, tf), lambda f, m: (0, f)),
                  pl.BlockSpec((1, tf), lambda f, m: (0, f)),
                  pl.BlockSpec((1, SUBLANE, tf), lambda f, m: (m // tps, 0, f))],
        out_specs=[pl.BlockSpec((tm, tf), lambda f, m: (m, f)),
                   pl.BlockSpec((1, SUBLANE, tf), lambda f, m: (m // tps, 0, f))],
        out_shape=[jax.ShapeDtypeStruct((M, Fp), BF16), jax.ShapeDtypeStruct((B, SUBLANE, Fp), F32)],
        scratch_shapes=[pltpu.VMEM((SUBLANE + tm, tf), F32)],
        compiler_params=_cp(("parallel", "arbitrary")),
        name="ffn_up_seq")(hn, w_a, w_u, conv_w, conv_b, prefix)


def _ffn_up_tm_kernel(hn_ref, wa_ref, wu_ref, cw_ref, cb_ref, pre_ref, hm_ref, tail_ref, *, steps, nb):
    hn = hn_ref[...]
    a = _dot(hn, wa_ref[...])
    u = _dot(hn, wu_ref[...])
    slabs = [pre_ref[i] for i in range(CONV_W - 1)] + [a[t * nb:(t + 1) * nb] for t in range(steps)]
    for t in range(steps):
        c = (slabs[t] * cw_ref[0:1, :] + slabs[t + 1] * cw_ref[1:2, :] + slabs[t + 2] * cw_ref[2:3, :]
             + cb_ref[...])
        hm_ref[t * nb:(t + 1) * nb, :] = (jax.nn.gelu(c) * u[t * nb:(t + 1) * nb]).astype(hm_ref.dtype)
    for i in range(CONV_W - 1):
        tail_ref[i] = slabs[steps + i]


def ffn_up_tm(hn, w_a, w_u, conv_w, conv_b, prefix, steps, tf):
    M, Dm = hn.shape
    Fp = conv_w.shape[1]
    nf = Fp // tf
    nb = M // steps
    kern = functools.partial(_ffn_up_tm_kernel, steps=steps, nb=nb)
    return pl.pallas_call(
        kern,
        grid=(nf,),
        in_specs=[pl.BlockSpec((M, Dm), lambda f: (0, 0)),
                  pl.BlockSpec((Dm, tf), lambda f: (0, f)),
                  pl.BlockSpec((Dm, tf), lambda f: (0, f)),
                  pl.BlockSpec((SUBLANE, tf), lambda f: (0, f)),
                  pl.BlockSpec((1, tf), lambda f: (0, f)),
                  pl.BlockSpec((CONV_W - 1, nb, tf), lambda f: (0, 0, f))],
        out_specs=[pl.BlockSpec((M, tf), lambda f: (0, f)),
                   pl.BlockSpec((CONV_W - 1, nb, tf), lambda f: (0, 0, f))],
        out_shape=[jax.ShapeDtypeStruct((M, Fp), BF16), jax.ShapeDtypeStruct((CONV_W - 1, nb, Fp), F32)],
        compiler_params=_cp(("parallel",)),
        name="ffn_up_tm")(hn, w_a, w_u, conv_w, conv_b, prefix)


def _ffn_down_kernel(hm_ref, w_ref, h_ref, o_ref):
    o_ref[...] = h_ref[...] + _dot(hm_ref[...], w_ref[...])


def ffn_down(hmid, w_down, h1, tm, tn):
    M = hmid.shape[0]
    F, Dm = w_down.shape
    assert F % LANE == 0 and F <= hmid.shape[1]
    return pl.pallas_call(
        _ffn_down_kernel,
        grid=(M // tm, Dm // tn),
        in_specs=[pl.BlockSpec((tm, F), lambda i, j: (i, 0)),
                  pl.BlockSpec((F, tn), lambda i, j: (0, j)),
                  pl.BlockSpec((tm, tn), lambda i, j: (i, j))],
        out_specs=pl.BlockSpec((tm, tn), lambda i, j: (i, j)),
        out_shape=jax.ShapeDtypeStruct((M, Dm), F32),
        compiler_params=_cp(("parallel", "arbitrary")),
        name="ffn_down")(hmid, w_down, h1)


def _ple_final_kernel(h_ref, p_ref, wg_ref, bg_ref, wp_ref, g_ref, o_ref):
    h = h_ref[...]
    gate = jax.nn.sigmoid(_dot(h.astype(BF16), wg_ref[...]) + bg_ref[...])
    h = h + gate * _dot(p_ref[...].astype(BF16), wp_ref[...])
    o_ref[...] = _rms(h, g_ref[...])


def ple_final(h2, p2, w_gate, b_gate, w_proj, g, tm):
    M, Dm = h2.shape
    Pd = p2.shape[1]
    return pl.pallas_call(
        _ple_final_kernel,
        grid=(M // tm,),
        in_specs=[pl.BlockSpec((tm, Dm), lambda i: (i, 0)),
                  pl.BlockSpec((tm, Pd), lambda i: (i, 0)),
                  pl.BlockSpec((Dm, Dm), lambda i: (0, 0)),
                  pl.BlockSpec((1, Dm), lambda i: (0, 0)),
                  pl.BlockSpec((Pd, Dm), lambda i: (0, 0)),
                  pl.BlockSpec((1, Dm), lambda i: (0, 0))],
        out_specs=pl.BlockSpec((tm, Dm), lambda i: (i, 0)),
        out_shape=jax.ShapeDtypeStruct((M, Dm), F32),
        compiler_params=_cp(("parallel",)),
        name="ple_final")(h2, p2, w_gate, b_gate, w_proj, g)


def _pick_tile(n, prefs):
    for t in prefs:
        if n % t == 0:
            return t
    raise ValueError(f"no tile for {n}")


def _block_matrices(total_keys, nb):
    e = np.zeros((LANE, total_keys), np.float32)
    t = np.arange(total_keys)
    e[t // SLC_LEN, t] = 1.0
    ratio = SLC_LEN // CMP_STRIDE
    m = np.zeros((LANE, LANE), np.float32)
    for n in range(nb):
        for s in range(LANE):
            m[n, s] = float(ratio * s <= n <= ratio * s + ratio - 1) + float(ratio * s - 1 <= n <= ratio * s + ratio - 2)
    return jnp.asarray(e, BF16), jnp.asarray(m, BF16)


def _prep_weights(i, attn_norm_g, w_in, gla_w_a2, gla_b_a, gla_norm_g, cmp_wk1, cmp_wk2, cmp_wv1, cmp_wv2,
                  w_out, ffn_norm_g, ffn_w_up, ffn_conv_w, ffn_conv_b, ffn_w_down, ple_w_proj, ple_w_gate,
                  ple_b_gate):
    Dm = w_in.shape[1]
    F = ffn_w_down.shape[1]
    Fp = -(-F // 512) * 512
    win = w_in[i]
    ga0 = NQ0
    ng0 = ga0 + GLA_RANK + (NMAIN - NQ0)
    ngw = 3 * NSA_HEADS
    before_ga = lax.broadcasted_iota(jnp.int32, (1, NMAIN), 1) < ga0
    w_main = jnp.where(before_ga, win[:, :NMAIN], win[:, GLA_RANK:GLA_RANK + NMAIN]).astype(BF16)
    w_small = jnp.concatenate([win[:, ga0:ga0 + GLA_RANK], win[:, ng0:ng0 + ngw],
                               jnp.zeros((Dm, LANE - GLA_RANK - ngw), F32)], axis=1).astype(BF16)
    wa_pad = jnp.concatenate([gla_w_a2[i], jnp.zeros((LANE - GLA_RANK, GLA_HEADS * GLA_DK), F32)],
                             axis=0).astype(BF16)
    cat = lambda w1: jnp.concatenate([w1[:CMP_STRIDE], w1[CMP_STRIDE:]], axis=-1)
    w1cat = jnp.stack([cat(cmp_wk1[i]), cat(cmp_wv1[i])]).astype(BF16)
    w2s = jnp.stack([cmp_wk2[i], cmp_wv2[i]])
    wup = ffn_w_up[i]
    w_up_a = jnp.pad(wup[:, :F], ((0, 0), (0, Fp - F))).astype(BF16)
    w_up_u = jnp.pad(wup[:, F:], ((0, 0), (0, Fp - F))).astype(BF16)
    conv_w = jnp.pad(ffn_conv_w[i], ((0, SUBLANE - CONV_W), (0, Fp - F)))
    conv_b = jnp.pad(ffn_conv_b[i], (0, Fp - F)).reshape(1, Fp)
    w_down = ffn_w_down[i].astype(BF16)
    return dict(
        g_attn=attn_norm_g[i].reshape(1, Dm), w_main=w_main, w_small=w_small, wa_pad=wa_pad,
        b_a=gla_b_a[i].reshape(1, -1), gnorm=gla_norm_g[i].reshape(1, -1), w1cat=w1cat, w2s=w2s,
        w_out=w_out[i].astype(BF16), g_ffn=ffn_norm_g[i].reshape(1, Dm), w_up_a=w_up_a, w_up_u=w_up_u,
        conv_w=conv_w,
        conv_b=conv_b, w_down=w_down, w_proj=ple_w_proj[i].astype(BF16), w_gate=ple_w_gate[i].astype(BF16),
        b_gate=ple_b_gate[i].reshape(1, Dm), F=F, Fp=Fp)


def _kv_rows(kv, B, rows):
    t = kv.reshape(kv.shape[0], B, -1, NSA_KV, NSA_HD)[:, :, rows]
    return [t[i] for i in range(kv.shape[0])]


def _prompt_layer(h2d, B, L, W):
    Dm = h2d.shape[1]
    M = B * L
    proj, small, kv = norm_proj(h2d, W["g_attn"], W["w_main"], W["w_small"],
                                _pick_tile(M, (1024, 512, 256, 128)), 512)
    proj3 = proj.reshape(B, L, NMAIN)
    small3 = small.reshape(B, L, LANE)
    cb = _pick_tile(L, (128, 64, 32, 16))
    s0 = jnp.zeros((B, GLA_HEADS, GLA_DK, GLA_DV), F32)
    og, gla_state = gla(proj3, small3, W["wa_pad"], W["b_a"], W["gnorm"], s0, cb, GLA_SUB, cb,
                        _pick_tile(B, (4, 2, 1)))
    cmp = compress(proj3, W["w1cat"], W["w2s"])
    _, mmat = _block_matrices(L, L // CMP_STRIDE - 1)
    on = nsa_prompt(proj3, small3, cmp, mmat.T)
    h1, hn = out_proj(og.reshape(M, -1), on.reshape(M, -1), h2d, W["w_out"], W["g_ffn"],
                      _pick_tile(M, (512, 256, 128)))
    prefix = jnp.zeros((B, SUBLANE, W["Fp"]), F32)
    hmid, tail = ffn_up_seq(hn, W["w_up_a"], W["w_up_u"], W["conv_w"], W["conv_b"], prefix, L,
                            _pick_tile(L, (1024, 512, 256, 128)), 512)
    h2 = ffn_down(hmid, W["w_down"], h1, _pick_tile(M, (1024, 512, 256, 128)), 512)
    keep = min(WINDOW, L)
    outs = tuple(_kv_rows(kv[:4], B, slice(0, L)) + _kv_rows(kv[4:], B, slice(L - keep, L))
                 + [gla_state, tail[:, SUBLANE - (CONV_W - 1):, :W["F"]]])
    return h2, outs


def _decode_layer(h3d, B, L, lp, caches, page_table, win_k, win_v, gla_s, conv_s, W):
    Dm = h3d.shape[2]
    Mp = B * lp
    x2 = h3d.reshape(Mp, Dm)
    proj, small, kv = norm_proj(x2, W["g_attn"], W["w_main"], W["w_small"],
                                _pick_tile(Mp, (1024, 512, 256, 128, 64, 32, 16, 8)), 512)
    proj3 = proj.reshape(B, lp, NMAIN)
    small3 = small.reshape(B, lp, LANE)
    og, gla_state = gla(proj3, small3, W["wa_pad"], W["b_a"], W["gnorm"], gla_s, lp, lp, L,
                        _pick_tile(B, (8, 4, 2, 1)))
    plen = page_table.shape[1] * caches[0].shape[1] // NSA_KV
    emat, mmat = _block_matrices(plen + KCHUNK, (plen + L) // CMP_STRIDE - 1)
    wlen = win_k.shape[1]
    on, wk_new, wv_new = nsa_decode(proj3, small3, caches, page_table,
                                    win_k.reshape(B, wlen * NSA_KV, NSA_HD),
                                    win_v.reshape(B, wlen * NSA_KV, NSA_HD),
                                    W["w1cat"], W["w2s"], emat, mmat, L, _pick_tile(B, (2, 1)))
    h1, hn = out_proj(og.reshape(Mp, -1), on.reshape(Mp, -1), x2, W["w_out"], W["g_ffn"],
                      _pick_tile(Mp, (256, 128, 64, 32, 16, 8)))
    to_tm = lambda t: t.reshape(B, lp, Dm)[:, :L].transpose(1, 0, 2).reshape(L * B, Dm)
    h1_tm, hn_tm = to_tm(h1), to_tm(hn)
    prefix = jnp.pad(conv_s.transpose(1, 0, 2), ((0, 0), (0, 0), (0, W["Fp"] - W["F"])))
    hmid, tail = ffn_up_tm(hn_tm, W["w_up_a"], W["w_up_u"], W["conv_w"], W["conv_b"], prefix, L, 512)
    M = L * B
    h2 = ffn_down(hmid, W["w_down"], h1_tm, _pick_tile(M, (512, 256, 128, 64, 32, 16, 8)), 512)
    outs = tuple(_kv_rows(kv[:4], B, slice(0, L))
                 + [wk_new.reshape(win_k.shape), wv_new.reshape(win_v.shape),
                    gla_state, tail[:, :, :W["F"]].transpose(1, 0, 2)])
    return h2, outs


def kernel(x_prompt, x_sample, p_prompt, p_sample, cache_cmp_k, cache_cmp_v, cache_slc_k, cache_slc_v,
           page_table, state_win_k, state_win_v, state_gla, state_ffn_conv, attn_norm_g, w_in, gla_w_a2,
           gla_b_a, gla_norm_g, cmp_wk1, cmp_wk2, cmp_wv1, cmp_wv2, w_out, ffn_norm_g, ffn_w_up,
           ffn_conv_w, ffn_conv_b, ffn_w_down, ple_w_proj, ple_w_gate, ple_b_gate, final_norm_g):
    depth = w_in.shape[0]
    assert depth == 1, "layers are chained through HBM one at a time; only depth 1 is wired"
    Bp, Lp, Dm = x_prompt.shape
    Bs, Ls, _ = x_sample.shape
    W = _prep_weights(0, attn_norm_g, w_in, gla_w_a2, gla_b_a, gla_norm_g, cmp_wk1, cmp_wk2, cmp_wv1,
                      cmp_wv2, w_out, ffn_norm_g, ffn_w_up, ffn_conv_w, ffn_conv_b, ffn_w_down,
                      ple_w_proj, ple_w_gate, ple_b_gate)
    g_final = final_norm_g.reshape(1, Dm)

    Mp = Bp * Lp
    h2, outs_p = _prompt_layer(x_prompt.reshape(Mp, Dm), Bp, Lp, W)
    y_prompt = ple_final(h2, p_prompt[0].reshape(Mp, -1), W["w_gate"], W["b_gate"], W["w_proj"], g_final,
                         _pick_tile(Mp, (512, 256, 128))).reshape(Bp, Lp, Dm)

    lp = -(-Ls // SUBLANE) * SUBLANE
    xs = jnp.pad(x_sample, ((0, 0), (0, lp - Ls), (0, 0)))
    n_pool, psz = cache_cmp_k.shape[1], cache_cmp_k.shape[2]
    caches = [c[0].reshape(n_pool, psz * NSA_KV, NSA_HD)
              for c in (cache_cmp_k, cache_cmp_v, cache_slc_k, cache_slc_v)]
    h2s, outs_s = _decode_layer(xs, Bs, Ls, lp, caches, page_table, state_win_k[0], state_win_v[0],
                                state_gla[0], state_ffn_conv[0], W)
    p_tm = p_sample[0].transpose(1, 0, 2).reshape(Ls * Bs, -1)
    Ms = Ls * Bs
    y_tm = ple_final(h2s, p_tm, W["w_gate"], W["b_gate"], W["w_proj"], g_final,
                     _pick_tile(Ms, (256, 128, 64, 32, 16, 8)))
    y_sample = y_tm.reshape(Ls, Bs, Dm).transpose(1, 0, 2)

    lead = lambda t: t[None]
    return (y_prompt, y_sample) + tuple(lead(t) for t in outs_p) + tuple(lead(t) for t in outs_s)
```
